```python
import math
import jax, jax.numpy as jnp
from jax import lax
import numpy as np

D_MODEL = 2048
BATCH = 1
SEQ = 8192
DEPTH = 1
DEC_BATCH = 128
DEC_SEQ = 1
PAST_LEN = 8192
PAGE_SIZE = 128

HEAD_DIM = 64
N_HEADS = D_MODEL // HEAD_DIM
N_KV_HEADS = 4
GROUP = N_HEADS // N_KV_HEADS
Q_DIM = N_HEADS * HEAD_DIM
KV_DIM = N_KV_HEADS * HEAD_DIM
WINDOW = 128
ROPE_THETA = 10000.0
POOL_WINDOWS = (2, 4, 8, 16)
POOL_GROUPS = len(POOL_WINDOWS)
POOL_WIDTH = D_MODEL // 2
POOL_GROUP_DIM = POOL_WIDTH // POOL_GROUPS
POOL_HIST = max(POOL_WINDOWS) - 1
D_FF = 5632
CONV_W = 3
IN_DIM = Q_DIM + 2 * KV_DIM + POOL_WIDTH + 2 * D_MODEL
LN_EPS = 1e-5
NEG_INF = -1e30

kernel_name = "hybrid_pool_swa_sink_convffn_deepnorm_step"


def layer_norm(x, g, b):
    xf = x.astype(jnp.float32)
    mu = jnp.mean(xf, axis=-1, keepdims=True)
    var = jnp.mean(jnp.square(xf - mu), axis=-1, keepdims=True)
    return ((xf - mu) * lax.rsqrt(var + LN_EPS) * g.astype(jnp.float32) + b.astype(jnp.float32)).astype(x.dtype)


def rope(x, pos):
    half = HEAD_DIM // 2
    inv = ROPE_THETA ** (-jnp.arange(half, dtype=jnp.float32) / half)
    ang = pos.astype(jnp.float32)[:, None] * inv[None, :]
    cos = jnp.cos(ang)[:, None, :]
    sin = jnp.sin(ang)[:, None, :]
    x1 = x[..., :half].astype(jnp.float32)
    x2 = x[..., half:].astype(jnp.float32)
    return jnp.concatenate([x1 * cos - x2 * sin, x2 * cos + x1 * sin], axis=-1).astype(x.dtype)


def window_attention(q, k_ext, v_ext, pos0, sinks):
    N, T = q.shape[0], q.shape[1]
    blk = WINDOW if T % WINDOW == 0 else T
    nblk = T // blk
    qb = q.reshape(N, nblk, blk, N_KV_HEADS, GROUP, HEAD_DIM)
    if blk == WINDOW:
        kb = k_ext.reshape(N, nblk + 1, WINDOW, N_KV_HEADS, HEAD_DIM)
        vb = v_ext.reshape(N, nblk + 1, WINDOW, N_KV_HEADS, HEAD_DIM)
        kb = jnp.concatenate([kb[:, :-1], kb[:, 1:]], axis=2)
        vb = jnp.concatenate([vb[:, :-1], vb[:, 1:]], axis=2)
    else:
        kb = k_ext[:, None]
        vb = v_ext[:, None]
    q_pos = pos0 + jnp.arange(T).reshape(nblk, blk)
    k_pos = pos0 - WINDOW + jnp.arange(nblk)[:, None] * blk + jnp.arange(WINDOW + blk)[None, :]
    diff = q_pos[:, :, None] - k_pos[:, None, :]
    visible = (diff >= 0) & (diff < WINDOW) & (k_pos[:, None, :] >= 0)
    s = jnp.einsum('nbqkgd,nbskd->nbkgqs', qb, kb, preferred_element_type=jnp.float32) * (HEAD_DIM ** -0.5)
    s = jnp.where(visible[None, :, None, None], s, NEG_INF)
    sink = sinks.astype(jnp.float32).reshape(1, 1, N_KV_HEADS, GROUP, 1, 1)
    m = jnp.maximum(jnp.max(s, axis=-1, keepdims=True), sink)
    e = jnp.exp(s - m)
    p = e / (jnp.sum(e, axis=-1, keepdims=True) + jnp.exp(sink - m))
    o = jnp.einsum('nbkgqs,nbskd->nbqkgd', p.astype(v_ext.dtype), vb)
    return o.reshape(N, T, Q_DIM)


def pool_mix(u, hist, pos0, w_pool_mix, pool_scale):
    N, T = u.shape[0], u.shape[1]
    ext = jnp.concatenate([hist, u], axis=1).astype(jnp.float32)
    cs = jnp.pad(jnp.cumsum(ext, axis=1), ((0, 0), (1, 0), (0, 0)))
    pos = pos0 + jnp.arange(T)
    groups = []
    for g, w in enumerate(POOL_WINDOWS):
        sl = slice(g * POOL_GROUP_DIM, (g + 1) * POOL_GROUP_DIM)
        tot = cs[:, POOL_HIST + 1:POOL_HIST + 1 + T, sl] - cs[:, POOL_HIST + 1 - w:POOL_HIST + 1 - w + T, sl]
        cnt = jnp.minimum(w, pos + 1).astype(jnp.float32)[:, None]
        groups.append(tot / cnt - u[..., sl].astype(jnp.float32))
    d = jnp.stack(groups, axis=2).astype(u.dtype)
    y = jnp.einsum('ntgc,gcd->ntgd', d, w_pool_mix).reshape(N, T, POOL_WIDTH)
    return y * pool_scale


def causal_dwconv(g, hist, conv_w, conv_b):
    T = g.shape[1]
    ext = jnp.concatenate([hist, g], axis=1)
    y = conv_b + ext[:, 0:T] * conv_w[0]
    for j in range(1, CONV_W):
        y = y + ext[:, j:j + T] * conv_w[j]
    return y, ext[:, -(CONV_W - 1):]


def decoder_layer(x, hist_k, hist_v, hist_pool, hist_conv, pos0,
                  w_in, attn_sinks, w_pool_mix, pool_scale, w_attn_branch, w_pool_branch, w_out,
                  ln1_g, ln1_b, w_up, w_gate, conv_w, conv_b, w_down, ln2_g, ln2_b):
    N, T, _ = x.shape
    alpha = (2.0 * DEPTH) ** 0.25
    pos = pos0 + jnp.arange(T)
    proj = x @ w_in
    cuts = [Q_DIM, Q_DIM + KV_DIM, Q_DIM + 2 * KV_DIM, Q_DIM + 2 * KV_DIM + POOL_WIDTH,
            Q_DIM + 2 * KV_DIM + POOL_WIDTH + D_MODEL]
    q, k, v, u, gate_pool, gate_attn = jnp.split(proj, cuts, axis=-1)
    q = rope(q.reshape(N, T, N_HEADS, HEAD_DIM), pos)
    k = rope(k.reshape(N, T, N_KV_HEADS, HEAD_DIM), pos)
    v = v.reshape(N, T, N_KV_HEADS, HEAD_DIM)
    k_ext = jnp.concatenate([hist_k, k], axis=1)
    v_ext = jnp.concatenate([hist_v, v], axis=1)
    attn = window_attention(q, k_ext, v_ext, pos0, attn_sinks)
    pooled = pool_mix(u, hist_pool, pos0, w_pool_mix, pool_scale)
    new_pool = jnp.concatenate([hist_pool, u], axis=1)[:, -POOL_HIST:]
    merged = (jax.nn.sigmoid(gate_pool) * (pooled @ w_pool_branch)
              + jax.nn.sigmoid(gate_attn) * (attn @ w_attn_branch))
    x1 = layer_norm(alpha * x + merged @ w_out, ln1_g, ln1_b)
    gc, new_conv = causal_dwconv(x1 @ w_gate, hist_conv, conv_w, conv_b)
    ffn = (jax.nn.gelu(gc) * (x1 @ w_up)) @ w_down
    x2 = layer_norm(alpha * x1 + ffn, ln2_g, ln2_b)
    return x2, k_ext[:, -WINDOW:], v_ext[:, -WINDOW:], new_pool, new_conv


def setup_inputs(seed: int = 0) -> dict:
    key = jax.random.key(seed)
    ks = jax.random.split(key, 24)
    f32 = jnp.float32
    beta = (8.0 * DEPTH) ** -0.25
    nrm = lambda k, shape, scale: jax.random.normal(k, shape, f32) * scale
    col_scale = jnp.concatenate([jnp.ones((Q_DIM + KV_DIM,), f32), jnp.full((KV_DIM,), beta, f32),
                                 jnp.ones((POOL_WIDTH + 2 * D_MODEL,), f32)])
    return {
        'x_prompt': nrm(ks[0], (BATCH, SEQ, D_MODEL), 1.0),
        'x_sample': nrm(ks[1], (DEC_BATCH, DEC_SEQ, D_MODEL), 1.0),
        'cache_k': nrm(ks[2], (DEPTH, DEC_BATCH, WINDOW, N_KV_HEADS, HEAD_DIM), 1.0),
        'cache_v': nrm(ks[3], (DEPTH, DEC_BATCH, WINDOW, N_KV_HEADS, HEAD_DIM), beta),
        'state_pool': nrm(ks[4], (DEPTH, DEC_BATCH, POOL_HIST, POOL_WIDTH), 1.0),
        'state_conv': nrm(ks[5], (DEPTH, DEC_BATCH, CONV_W - 1, D_FF), 1.0),
        'w_in': nrm(ks[6], (DEPTH, D_MODEL, IN_DIM), D_MODEL ** -0.5) * col_scale,
        'attn_sinks': nrm(ks[7], (DEPTH, N_HEADS), 1.0),
        'w_pool_mix': nrm(ks[8], (DEPTH, POOL_GROUPS, POOL_GROUP_DIM, POOL_GROUP_DIM), POOL_GROUP_DIM ** -0.5),
        'pool_scale': 1.0 + nrm(ks[9], (DEPTH, POOL_WIDTH), 0.1),
        'w_attn_branch': nrm(ks[10], (DEPTH, Q_DIM, D_MODEL), Q_DIM ** -0.5),
        'w_pool_branch': nrm(ks[11], (DEPTH, POOL_WIDTH, D_MODEL), POOL_WIDTH ** -0.5),
        'w_out': nrm(ks[12], (DEPTH, D_MODEL, D_MODEL), beta * D_MODEL ** -0.5),
        'ln1_g': 1.0 + nrm(ks[13], (DEPTH, D_MODEL), 0.02),
        'ln1_b': nrm(ks[14], (DEPTH, D_MODEL), 0.02),
        'w_up': nrm(ks[15], (DEPTH, D_MODEL, D_FF), D_MODEL ** -0.5),
        'w_gate': nrm(ks[16], (DEPTH, D_MODEL, D_FF), D_MODEL ** -0.5),
        'conv_w': nrm(ks[17], (DEPTH, CONV_W, D_FF), CONV_W ** -0.5),
        'conv_b': nrm(ks[18], (DEPTH, D_FF), 0.01),
        'w_down': nrm(ks[19], (DEPTH, D_FF, D_MODEL), beta * D_FF ** -0.5),
        'ln2_g': 1.0 + nrm(ks[20], (DEPTH, D_MODEL), 0.02),
        'ln2_b': nrm(ks[21], (DEPTH, D_MODEL), 0.02),
    }


def reference(x_prompt, x_sample, cache_k, cache_v, state_pool, state_conv,
              w_in, attn_sinks, w_pool_mix, pool_scale, w_attn_branch, w_pool_branch, w_out,
              ln1_g, ln1_b, w_up, w_gate, conv_w, conv_b, w_down, ln2_g, ln2_b):
    B = x_prompt.shape[0]
    dt = x_prompt.dtype
    hp, hs = x_prompt, x_sample
    kp_l, vp_l, pp_l, cp_l, ks_l, vs_l, ps_l, cs_l = [], [], [], [], [], [], [], []
    for l in range(DEPTH):
        weights = (w_in[l], attn_sinks[l], w_pool_mix[l], pool_scale[l], w_attn_branch[l], w_pool_branch[l],
                   w_out[l], ln1_g[l], ln1_b[l], w_up[l], w_gate[l], conv_w[l], conv_b[l], w_down[l],
                   ln2_g[l], ln2_b[l])
        hp, kp, vp, pp, cp = decoder_layer(
            hp,
            jnp.zeros((B, WINDOW, N_KV_HEADS, HEAD_DIM), dt),
            jnp.zeros((B, WINDOW, N_KV_HEADS, HEAD_DIM), dt),
            jnp.zeros((B, POOL_HIST, POOL_WIDTH), dt),
            jnp.zeros((B, CONV_W - 1, D_FF), dt),
            0, *weights)
        hs, ksm, vsm, psm, csm = decoder_layer(
            hs, cache_k[l], cache_v[l], state_pool[l], state_conv[l], PAST_LEN, *weights)
        kp_l.append(kp); vp_l.append(vp); pp_l.append(pp); cp_l.append(cp)
        ks_l.append(ksm); vs_l.append(vsm); ps_l.append(psm); cs_l.append(csm)
    return (hp, hs,
            jnp.stack(kp_l), jnp.stack(vp_l), jnp.stack(pp_l), jnp.stack(cp_l),
            jnp.stack(ks_l), jnp.stack(vs_l), jnp.stack(ps_l), jnp.stack(cs_l))
```

```python
import functools

import jax
import jax.numpy as jnp
from jax import lax
from jax.experimental import pallas as pl
from jax.experimental.pallas import tpu as pltpu

F32 = jnp.float32
BF16 = jnp.bfloat16

D_MODEL = 2048
SEQ = 8192
N_SAMPLE = 128
PAST_LEN = 8192
HEAD_DIM = 64
N_HEADS = 32
N_KV_HEADS = 4
GROUP = N_HEADS // N_KV_HEADS
KV_DIM = N_KV_HEADS * HEAD_DIM
WINDOW = 128
ROPE_THETA = 10000.0
POOL_WINDOWS = (2, 4, 8, 16)
POOL_WIDTH = 1024
POOL_GROUP_DIM = 256
POOL_HIST = 15
D_FF = 5632
CONV_W = 3
IN_DIM = 7680
LN_EPS = 1e-5
NEG_INF = -1e30
ALPHA = 2.0 ** 0.25

LANES = 128
TN = 512
VMEM_LIMIT = 56 * 1024 * 1024


def _params(ndim):
    return pltpu.CompilerParams(dimension_semantics=("arbitrary",) * ndim,
                                vmem_limit_bytes=VMEM_LIMIT)


def _dot(a, b):
    return jnp.dot(a, b, preferred_element_type=F32)


def _layer_norm(z, g, b):
    mu = jnp.mean(z, axis=-1, keepdims=True)
    d = z - mu
    var = jnp.mean(d * d, axis=-1, keepdims=True)
    return d * lax.rsqrt(var + LN_EPS) * g + b


def _rope(x, cos, sin_signed):
    lane = lax.broadcasted_iota(jnp.int32, (1, LANES), 1)
    low_half = (lane % HEAD_DIM) < (HEAD_DIM // 2)
    outs = []
    for c in range(x.shape[1] // LANES):
        xc = x[:, c * LANES:(c + 1) * LANES]
        up = pltpu.roll(xc, LANES - HEAD_DIM // 2, 1)
        down = pltpu.roll(xc, HEAD_DIM // 2, 1)
        outs.append(xc * cos + jnp.where(low_half, up, down) * sin_signed)
    return jnp.concatenate(outs, axis=1)


IN_TM = 1024
_NQ = D_MODEL // TN
_N_KV = _NQ
_N_U0 = _N_KV + 1
_N_G0 = _N_U0 + POOL_WIDTH // TN


def _in_proj_kernel(x_ref, xs_ref, w_ref, cos_ref, sin_ref, coss_ref, sins_ref,
                    q_ref, kv_ref, u_ref, g_ref, qs_ref, kvs_ref, us_ref, gs_ref):
    m = pl.program_id(0)
    n = pl.program_id(1)
    w = w_ref[...]

    def epilogue(acc, cos, sin, q_o, kv_o, u_o, g_o):
        @pl.when(n < _NQ)
        def _():
            q_o[...] = (_rope(acc, cos, sin) * (HEAD_DIM ** -0.5)).astype(BF16)

        @pl.when(n == _N_KV)
        def _():
            kv_o[:, :KV_DIM] = _rope(acc[:, :KV_DIM], cos, sin)
            kv_o[:, KV_DIM:] = acc[:, KV_DIM:]

        @pl.when((n >= _N_U0) & (n < _N_G0))
        def _():
            u_o[...] = acc

        @pl.when(n >= _N_G0)
        def _():
            g_o[...] = jax.nn.sigmoid(acc).astype(BF16)

    epilogue(_dot(x_ref[...], w), cos_ref[...], sin_ref[...], q_ref, kv_ref, u_ref, g_ref)

    @pl.when(m == pl.num_programs(0) - 1)
    def _():
        epilogue(_dot(xs_ref[...], w), coss_ref[0:1, :], sins_ref[0:1, :],
                 qs_ref, kvs_ref, us_ref, gs_ref)


def _in_proj(xb, xsb, w_in_b, cos, sin, cos_s, sin_s):
    nm = SEQ // IN_TM
    nn = IN_DIM // TN
    last = nm - 1

    def clamp(n, lo, hi):
        return jnp.clip(n - lo, 0, hi - lo)

    def smp(fn):
        return lambda m, n: (0, jnp.where(m == last, fn(n), 0))

    ng = 2 * D_MODEL // TN
    return pl.pallas_call(
        _in_proj_kernel,
        grid=(nm, nn),
        in_specs=[
            pl.BlockSpec((IN_TM, D_MODEL), lambda m, n: (m, 0)),
            pl.BlockSpec((N_SAMPLE, D_MODEL), lambda m, n: (0, 0)),
            pl.BlockSpec((D_MODEL, TN), lambda m, n: (0, n)),
            pl.BlockSpec((IN_TM, LANES), lambda m, n: (m, 0)),
            pl.BlockSpec((IN_TM, LANES), lambda m, n: (m, 0)),
            pl.BlockSpec((8, LANES), lambda m, n: (0, 0)),
            pl.BlockSpec((8, LANES), lambda m, n: (0, 0)),
        ],
        out_specs=[
            pl.BlockSpec((IN_TM, TN), lambda m, n: (m, clamp(n, 0, _NQ - 1))),
            pl.BlockSpec((IN_TM, TN), lambda m, n: (m, 0)),
            pl.BlockSpec((IN_TM, TN), lambda m, n: (m, clamp(n, _N_U0, _N_G0 - 1))),
            pl.BlockSpec((IN_TM, TN), lambda m, n: (m, clamp(n, _N_G0, _N_G0 + ng - 1))),
            pl.BlockSpec((N_SAMPLE, TN), smp(lambda n: clamp(n, 0, _NQ - 1))),
            pl.BlockSpec((N_SAMPLE, TN), smp(lambda n: 0)),
            pl.BlockSpec((N_SAMPLE, TN), smp(lambda n: clamp(n, _N_U0, _N_G0 - 1))),
            pl.BlockSpec((N_SAMPLE, TN), smp(lambda n: clamp(n, _N_G0, _N_G0 + ng - 1))),
        ],
        out_shape=[
            jax.ShapeDtypeStruct((SEQ, D_MODEL), BF16),
            jax.ShapeDtypeStruct((SEQ, 2 * KV_DIM), F32),
            jax.ShapeDtypeStruct((SEQ, POOL_WIDTH), F32),
            jax.ShapeDtypeStruct((SEQ, 2 * D_MODEL), BF16),
            jax.ShapeDtypeStruct((N_SAMPLE, D_MODEL), BF16),
            jax.ShapeDtypeStruct((N_SAMPLE, 2 * KV_DIM), F32),
            jax.ShapeDtypeStruct((N_SAMPLE, POOL_WIDTH), F32),
            jax.ShapeDtypeStruct((N_SAMPLE, 2 * D_MODEL), BF16),
        ],
        compiler_params=_params(2),
        name="in_proj",
    )(xb, xsb, w_in_b, cos, sin, cos_s, sin_s)


def _softmax_weights(s, sink):
    mx = jnp.maximum(jnp.max(s, axis=-1, keepdims=True), sink)
    e = jnp.exp(s - mx)
    den = jnp.sum(e, axis=-1, keepdims=True) + jnp.exp(sink - mx)
    return e, 1.0 / den


def _attn_kernel(sink_ref, q_ref, kp_ref, kc_ref, vp_ref, vc_ref, o_ref):
    i = pl.program_id(0)
    kk = jnp.concatenate([kp_ref[...], kc_ref[...]], axis=0).astype(BF16)
    vv = jnp.concatenate([vp_ref[...], vc_ref[...]], axis=0).astype(BF16)
    r = lax.broadcasted_iota(jnp.int32, (WINDOW, 2 * WINDOW), 0)
    c = lax.broadcasted_iota(jnp.int32, (WINDOW, 2 * WINDOW), 1)
    visible = (c > r) & (c <= r + WINDOW) & ((i > 0) | (c >= WINDOW))
    for g in range(N_KV_HEADS):
        kg = kk[:, g * HEAD_DIM:(g + 1) * HEAD_DIM]
        vg = vv[:, g * HEAD_DIM:(g + 1) * HEAD_DIM]
        for j in range(GROUP):
            h = g * GROUP + j
            qh = q_ref[:, h * HEAD_DIM:(h + 1) * HEAD_DIM]
            s = lax.dot_general(qh, kg, (((1,), (1,)), ((), ())), preferred_element_type=F32)
            s = jnp.where(visible, s, NEG_INF)
            e, inv = _softmax_weights(s, sink_ref[h])
            o_ref[:, h * HEAD_DIM:(h + 1) * HEAD_DIM] = (_dot(e.astype(BF16), vg) * inv).astype(BF16)


def _attention(sinks, q, kv):
    nb = SEQ // WINDOW
    prev = lambda i: jnp.maximum(i - 1, 0)
    return pl.pallas_call(
        _attn_kernel,
        grid=(nb,),
        in_specs=[
            pl.BlockSpec(memory_space=pltpu.SMEM),
            pl.BlockSpec((WINDOW, D_MODEL), lambda i: (i, 0)),
            pl.BlockSpec((WINDOW, KV_DIM), lambda i: (prev(i), 0)),
            pl.BlockSpec((WINDOW, KV_DIM), lambda i: (i, 0)),
            pl.BlockSpec((WINDOW, KV_DIM), lambda i: (prev(i), 1)),
            pl.BlockSpec((WINDOW, KV_DIM), lambda i: (i, 1)),
        ],
        out_specs=pl.BlockSpec((WINDOW, D_MODEL), lambda i: (i, 0)),
        out_shape=jax.ShapeDtypeStruct((SEQ, D_MODEL), BF16),
        compiler_params=_params(1),
        name="attn_prompt",
    )(sinks, q, kv, kv, kv, kv)


SEQS_PER_STEP = 8


def _attn_sample_kernel(sink_ref, q_ref, kvn_ref, ck_ref, cv_ref, o_ref, nk_ref, nv_ref):
    head = lax.broadcasted_iota(jnp.int32, (N_HEADS, KV_DIM), 0) // GROUP
    col_group = lax.broadcasted_iota(jnp.int32, (N_HEADS, KV_DIM), 1) // HEAD_DIM
    own = head == col_group
    sink = sink_ref[...]
    for b in range(SEQS_PER_STEP):
        nk_ref[b, 0:WINDOW - 1, :] = ck_ref[b, 1:WINDOW, :]
        nk_ref[b, WINDOW - 1:WINDOW, :] = kvn_ref[b:b + 1, 0:KV_DIM]
        nv_ref[b, 0:WINDOW - 1, :] = cv_ref[b, 1:WINDOW, :]
        nv_ref[b, WINDOW - 1:WINDOW, :] = kvn_ref[b:b + 1, KV_DIM:2 * KV_DIM]
        kb = nk_ref[b].astype(BF16)
        vb = nv_ref[b].astype(BF16)
        qb = q_ref[b]
        qe = jnp.where(own, jnp.concatenate([qb] * N_KV_HEADS, axis=1), jnp.zeros((), BF16))
        s = lax.dot_general(qe, kb, (((1,), (1,)), ((), ())), preferred_element_type=F32)
        e, inv = _softmax_weights(s, sink)
        of = jnp.where(own, _dot(e.astype(BF16), vb), 0.0)
        o = of[:, 0:HEAD_DIM]
        for g in range(1, N_KV_HEADS):
            o = o + of[:, g * HEAD_DIM:(g + 1) * HEAD_DIM]
        o_ref[b] = (o * inv).astype(BF16)


def _attention_sample(sink_col, q3, kvn, ck, cv):
    nb = N_SAMPLE // SEQS_PER_STEP
    cache_spec = pl.BlockSpec((SEQS_PER_STEP, WINDOW, KV_DIM), lambda i: (i, 0, 0))
    return pl.pallas_call(
        _attn_sample_kernel,
        grid=(nb,),
        in_specs=[
            pl.BlockSpec((N_HEADS, 1), lambda i: (0, 0)),
            pl.BlockSpec((SEQS_PER_STEP, N_HEADS, HEAD_DIM), lambda i: (i, 0, 0)),
            pl.BlockSpec((SEQS_PER_STEP, 2 * KV_DIM), lambda i: (i, 0)),
            cache_spec, cache_spec,
        ],
        out_specs=[
            pl.BlockSpec((SEQS_PER_STEP, N_HEADS, HEAD_DIM), lambda i: (i, 0, 0)),
            cache_spec, cache_spec,
        ],
        out_shape=[
            jax.ShapeDtypeStruct((N_SAMPLE, N_HEADS, HEAD_DIM), BF16),
            jax.ShapeDtypeStruct((N_SAMPLE, WINDOW, KV_DIM), F32),
            jax.ShapeDtypeStruct((N_SAMPLE, WINDOW, KV_DIM), F32),
        ],
        compiler_params=_params(1),
        name="attn_sample",
    )(sink_col, q3, kvn, ck, cv)


POOL_TM = 1024
POOL_PAD = 16


def _pool_kernel(u_ref, us_ref, sp_ref, wmix_ref, scale_ref, p_ref, ps_ref, ext):
    m = pl.program_id(0)

    @pl.when(m == 0)
    def _():
        ext[0:POOL_PAD, :] = jnp.zeros((POOL_PAD, POOL_WIDTH), F32)

    ext[POOL_PAD:POOL_PAD + POOL_TM, :] = u_ref[...]
    pos = m * POOL_TM + lax.broadcasted_iota(jnp.int32, (POOL_TM, 1), 0)
    for g, w in enumerate(POOL_WINDOWS):
        cols = slice(g * POOL_GROUP_DIM, (g + 1) * POOL_GROUP_DIM)
        tot = ext[POOL_PAD:POOL_PAD + POOL_TM, cols]
        for j in range(1, w):
            tot = tot + ext[POOL_PAD - j:POOL_PAD - j + POOL_TM, cols]
        inv_cnt = 1.0 / jnp.minimum(w, pos + 1).astype(F32)
        d = tot * inv_cnt - u_ref[:, cols]
        y = _dot(d.astype(BF16), wmix_ref[g])
        p_ref[:, cols] = (y * scale_ref[:, cols]).astype(BF16)
    ext[0:POOL_PAD, :] = ext[POOL_TM:POOL_TM + POOL_PAD, :]

    @pl.when(m == pl.num_programs(0) - 1)
    def _():
        for g, w in enumerate(POOL_WINDOWS):
            cols = slice(g * POOL_GROUP_DIM, (g + 1) * POOL_GROUP_DIM)
            tot = us_ref[:, cols]
            for j in range(1, w):
                base = (POOL_HIST - j) * POOL_WIDTH + g * POOL_GROUP_DIM
                tot = tot + sp_ref[:, base:base + POOL_GROUP_DIM]
            d = tot * (1.0 / w) - us_ref[:, cols]
            y = _dot(d.astype(BF16), wmix_ref[g])
            ps_ref[:, cols] = (y * scale_ref[:, cols]).astype(BF16)


def _pool(u, us, sp2, wmix_b, scale):
    nm = SEQ // POOL_TM
    full = lambda shape: pl.BlockSpec(shape, lambda m: (0,) * len(shape))
    return pl.pallas_call(
        _pool_kernel,
        grid=(nm,),
        in_specs=[
            pl.BlockSpec((POOL_TM, POOL_WIDTH), lambda m: (m, 0)),
            full((N_SAMPLE, POOL_WIDTH)),
            full((N_SAMPLE, POOL_HIST * POOL_WIDTH)),
            full((len(POOL_WINDOWS), POOL_GROUP_DIM, POOL_GROUP_DIM)),
            full((1, POOL_WIDTH)),
        ],
        out_specs=[
            pl.BlockSpec((POOL_TM, POOL_WIDTH), lambda m: (m, 0)),
            full((N_SAMPLE, POOL_WIDTH)),
        ],
        out_shape=[
            jax.ShapeDtypeStruct((SEQ, POOL_WIDTH), BF16),
            jax.ShapeDtypeStruct((N_SAMPLE, POOL_WIDTH), BF16),
        ],
        scratch_shapes=[pltpu.VMEM((POOL_PAD + POOL_TM, POOL_WIDTH), F32)],
        compiler_params=_params(1),
        name="pool_mix",
    )(u, us, sp2, wmix_b, scale)


MG_TM = 512
_NC = D_MODEL // TN


def _merge_kernel(pool_ref, attn_ref, gp_ref, ga_ref, x_ref,
                  pools_ref, attns_ref, gps_ref, gas_ref, xs_ref,
                  wp_ref, wa_ref, wo_ref, lng_ref, lnb_ref,
                  x1_ref, x1b_ref, x1s_ref, x1bs_ref,
                  mrg, y, mrgs, ys):
    m = pl.program_id(0)
    n = pl.program_id(1)
    last_m = m == pl.num_programs(0) - 1

    @pl.when(n < _NC)
    def _():
        def merged(pool, attn, gp, ga):
            a = _dot(pool, wp_ref[...])
            b = _dot(attn, wa_ref[...])
            return (gp.astype(F32) * a + ga.astype(F32) * b).astype(BF16)

        mrg[n] = merged(pool_ref[...], attn_ref[...], gp_ref[...], ga_ref[...])

        @pl.when(last_m)
        def _():
            mrgs[n] = merged(pools_ref[...], attns_ref[...], gps_ref[...], gas_ref[...])

    @pl.when(n >= _NC)
    def _():
        def out_proj(src):
            acc = _dot(src[0], wo_ref[0:TN, :])
            for k in range(1, _NC):
                acc = acc + _dot(src[k], wo_ref[k * TN:(k + 1) * TN, :])
            return acc

        y[n - _NC] = out_proj(mrg)

        @pl.when(last_m)
        def _():
            ys[n - _NC] = out_proj(mrgs)

    @pl.when(n == 2 * _NC - 1)
    def _():
        def finish(x_in, ysrc, o_f32, o_bf16):
            z = jnp.concatenate([ysrc[k] for k in range(_NC)], axis=1) + ALPHA * x_in[...]
            x1 = _layer_norm(z, lng_ref[...], lnb_ref[...])
            o_f32[...] = x1
            o_bf16[...] = x1.astype(BF16)

        finish(x_ref, y, x1_ref, x1b_ref)

        @pl.when(last_m)
        def _():
            finish(xs_ref, ys, x1s_ref, x1bs_ref)


def _merge(pooled, attn, gates, x, pooled_s, attn_s, gates_s, xs, wp_b, wa_b, wo_b, ln_g, ln_b):
    nm = SEQ // MG_TM
    row = lambda width: pl.BlockSpec((MG_TM, width), lambda m, n: (m, 0))
    full = lambda shape: pl.BlockSpec(shape, lambda m, n: (0, 0))
    lo = lambda n: jnp.minimum(n, _NC - 1)
    hi = lambda n: jnp.maximum(n - _NC, 0)
    return pl.pallas_call(
        _merge_kernel,
        grid=(nm, 2 * _NC),
        in_specs=[
            row(POOL_WIDTH), row(D_MODEL),
            pl.BlockSpec((MG_TM, TN), lambda m, n: (m, lo(n))),
            pl.BlockSpec((MG_TM, TN), lambda m, n: (m, _NC + lo(n))),
            row(D_MODEL),
            full((N_SAMPLE, POOL_WIDTH)), full((N_SAMPLE, D_MODEL)),
            pl.BlockSpec((N_SAMPLE, TN), lambda m, n: (0, lo(n))),
            pl.BlockSpec((N_SAMPLE, TN), lambda m, n: (0, _NC + lo(n))),
            full((N_SAMPLE, D_MODEL)),
            pl.BlockSpec((POOL_WIDTH, TN), lambda m, n: (0, lo(n))),
            pl.BlockSpec((D_MODEL, TN), lambda m, n: (0, lo(n))),
            pl.BlockSpec((D_MODEL, TN), lambda m, n: (0, hi(n))),
            full((1, D_MODEL)), full((1, D_MODEL)),
        ],
        out_specs=[row(D_MODEL), row(D_MODEL), full((N_SAMPLE, D_MODEL)), full((N_SAMPLE, D_MODEL))],
        out_shape=[
            jax.ShapeDtypeStruct((SEQ, D_MODEL), F32),
            jax.ShapeDtypeStruct((SEQ, D_MODEL), BF16),
            jax.ShapeDtypeStruct((N_SAMPLE, D_MODEL), F32),
            jax.ShapeDtypeStruct((N_SAMPLE, D_MODEL), BF16),
        ],
        scratch_shapes=[
            pltpu.VMEM((_NC, MG_TM, TN), BF16), pltpu.VMEM((_NC, MG_TM, TN), F32),
            pltpu.VMEM((_NC, N_SAMPLE, TN), BF16), pltpu.VMEM((_NC, N_SAMPLE, TN), F32),
        ],
        compiler_params=_params(2),
        name="merge_out_ln1",
    )(pooled, attn, gates, gates, x, pooled_s, attn_s, gates_s, gates_s, xs,
      wp_b, wa_b, wo_b, ln_g, ln_b)


FF_TM = 512
TF = 512
CARRY = 8


def _ffn_kernel(x1b_ref, x1_ref, x1bs_ref, x1s_ref, sc0_ref, sc1_ref,
                wg_ref, wu_ref, wd_ref, cw_ref, cb_ref, lng_ref, lnb_ref,
                o_ref, os_ref, glast_ref, gs_ref,
                acc, accs, gext, carry):
    m = pl.program_id(0)
    f = pl.program_id(1)
    last_m = m == pl.num_programs(0) - 1
    last_f = f == pl.num_programs(1) - 1
    w0 = cw_ref[0:1, :]
    w1 = cw_ref[1:2, :]
    w2 = cw_ref[2:3, :]
    cb = cb_ref[...]

    @pl.when(f == 0)
    def _():
        acc[...] = jnp.zeros_like(acc)

    g = _dot(x1b_ref[...], wg_ref[...])
    up = _dot(x1b_ref[...], wu_ref[...])

    @pl.when(m == 0)
    def _():
        gext[0:CARRY, :] = jnp.zeros((CARRY, TF), F32)

    @pl.when(m > 0)
    def _():
        gext[0:CARRY, :] = carry[f]

    gext[CARRY:CARRY + FF_TM, :] = g
    carry[f] = g[FF_TM - CARRY:FF_TM, :]
    glast_ref[...] = g[FF_TM - CARRY:FF_TM, :]
    gc = (cb + w0 * gext[CARRY - 2:CARRY - 2 + FF_TM, :]
          + w1 * gext[CARRY - 1:CARRY - 1 + FF_TM, :] + w2 * g)
    h = (jax.nn.gelu(gc, approximate=True) * up).astype(BF16)
    acc[...] += _dot(h, wd_ref[...])

    @pl.when(last_f)
    def _():
        o_ref[...] = _layer_norm(ALPHA * x1_ref[...] + acc[...], lng_ref[...], lnb_ref[...])

    @pl.when(last_m)
    def _():
        @pl.when(f == 0)
        def _():
            accs[...] = jnp.zeros_like(accs)

        gs = _dot(x1bs_ref[...], wg_ref[...])
        ups = _dot(x1bs_ref[...], wu_ref[...])
        gs_ref[...] = gs
        gcs = cb + w0 * sc0_ref[...] + w1 * sc1_ref[...] + w2 * gs
        hs = (jax.nn.gelu(gcs, approximate=True) * ups).astype(BF16)
        accs[...] += _dot(hs, wd_ref[...])

        @pl.when(last_f)
        def _():
            os_ref[...] = _layer_norm(ALPHA * x1s_ref[...] + accs[...], lng_ref[...], lnb_ref[...])


def _ffn(x1b, x1, x1bs, x1s, sc2, wg_b, wu_b, wd_b, conv_w, conv_b, ln_g, ln_b):
    nm = SEQ // FF_TM
    nf = D_FF // TF
    last = nm - 1
    row = lambda width: pl.BlockSpec((FF_TM, width), lambda m, f: (m, 0))
    full = lambda shape: pl.BlockSpec(shape, lambda m, f: (0, 0))
    return pl.pallas_call(
        _ffn_kernel,
        grid=(nm, nf),
        in_specs=[
            row(D_MODEL), row(D_MODEL),
            full((N_SAMPLE, D_MODEL)), full((N_SAMPLE, D_MODEL)),
            pl.BlockSpec((N_SAMPLE, TF), lambda m, f: (0, f)),
            pl.BlockSpec((N_SAMPLE, TF), lambda m, f: (0, nf + f)),
            pl.BlockSpec((D_MODEL, TF), lambda m, f: (0, f)),
            pl.BlockSpec((D_MODEL, TF), lambda m, f: (0, f)),
            pl.BlockSpec((TF, D_MODEL), lambda m, f: (f, 0)),
            pl.BlockSpec((CONV_W, TF), lambda m, f: (0, f)),
            pl.BlockSpec((1, TF), lambda m, f: (0, f)),
            full((1, D_MODEL)), full((1, D_MODEL)),
        ],
        out_specs=[
            row(D_MODEL),
            full((N_SAMPLE, D_MODEL)),
            pl.BlockSpec((CARRY, TF), lambda m, f: (0, f)),
            pl.BlockSpec((N_SAMPLE, TF), lambda m, f: (0, jnp.where(m == last, f, 0))),
        ],
        out_shape=[
            jax.ShapeDtypeStruct((SEQ, D_MODEL), F32),
            jax.ShapeDtypeStruct((N_SAMPLE, D_MODEL), F32),
            jax.ShapeDtypeStruct((CARRY, D_FF), F32),
            jax.ShapeDtypeStruct((N_SAMPLE, D_FF), F32),
        ],
        scratch_shapes=[
            pltpu.VMEM((FF_TM, D_MODEL), F32), pltpu.VMEM((N_SAMPLE, D_MODEL), F32),
            pltpu.VMEM((CARRY + FF_TM, TF), F32), pltpu.VMEM((nf, CARRY, TF), F32),
        ],
        compiler_params=_params(2),
        name="convffn_ln2",
    )(x1b, x1, x1bs, x1s, sc2, sc2, wg_b, wu_b, wd_b, conv_w, conv_b, ln_g, ln_b)


def _rope_tables(pos):
    half = HEAD_DIM // 2
    inv = ROPE_THETA ** (-jnp.arange(half, dtype=F32) / half)
    ang = pos.astype(F32)[:, None] * inv[None, :]
    cos = jnp.cos(ang)
    sin = jnp.sin(ang)
    reps = LANES // HEAD_DIM
    return (jnp.concatenate([cos, cos] * reps, axis=1),
            jnp.concatenate([-sin, sin] * reps, axis=1))


def kernel(x_prompt, x_sample, cache_k, cache_v, state_pool, state_conv, w_in, attn_sinks, w_pool_mix,
           pool_scale, w_attn_branch, w_pool_branch, w_out, ln1_g, ln1_b, w_up, w_gate, conv_w, conv_b,
           w_down, ln2_g, ln2_b):
    x = x_prompt.reshape(SEQ, D_MODEL)
    xs = x_sample.reshape(N_SAMPLE, D_MODEL)
    cos, sin = _rope_tables(jnp.arange(SEQ))
    cos_s, sin_s = _rope_tables(jnp.full((8,), PAST_LEN))

    q, kv, u, gates, q_s, kv_s, u_s, gates_s = _in_proj(
        x.astype(BF16), xs.astype(BF16), w_in[0].astype(BF16), cos, sin, cos_s, sin_s)

    attn = _attention(attn_sinks[0], q, kv)
    attn_s, new_k, new_v = _attention_sample(
        attn_sinks[0].reshape(N_HEADS, 1), q_s.reshape(N_SAMPLE, N_HEADS, HEAD_DIM), kv_s,
        cache_k[0].reshape(N_SAMPLE, WINDOW, KV_DIM), cache_v[0].reshape(N_SAMPLE, WINDOW, KV_DIM))

    pooled, pooled_s = _pool(u, u_s, state_pool[0].reshape(N_SAMPLE, POOL_HIST * POOL_WIDTH),
                             w_pool_mix[0].astype(BF16), pool_scale)

    x1, x1b, x1_s, x1b_s = _merge(
        pooled, attn, gates, x, pooled_s, attn_s.reshape(N_SAMPLE, D_MODEL), gates_s, xs,
        w_pool_branch[0].astype(BF16), w_attn_branch[0].astype(BF16), w_out[0].astype(BF16),
        ln1_g, ln1_b)

    y, y_s, g_last, g_s = _ffn(
        x1b, x1, x1b_s, x1_s, state_conv[0].reshape(N_SAMPLE, (CONV_W - 1) * D_FF),
        w_gate[0].astype(BF16), w_up[0].astype(BF16), w_down[0].astype(BF16),
        conv_w[0], conv_b, ln2_g, ln2_b)

    cache_shape = (1, N_SAMPLE, WINDOW, N_KV_HEADS, HEAD_DIM)
    return (
        y.reshape(1, SEQ, D_MODEL),
        y_s.reshape(N_SAMPLE, 1, D_MODEL),
        kv[SEQ - WINDOW:, :KV_DIM].reshape(1, 1, WINDOW, N_KV_HEADS, HEAD_DIM),
        kv[SEQ - WINDOW:, KV_DIM:].reshape(1, 1, WINDOW, N_KV_HEADS, HEAD_DIM),
        u[SEQ - POOL_HIST:].reshape(1, 1, POOL_HIST, POOL_WIDTH),
        g_last[CARRY - (CONV_W - 1):].reshape(1, 1, CONV_W - 1, D_FF),
        new_k.reshape(cache_shape),
        new_v.reshape(cache_shape),
        jnp.concatenate([state_pool[0, :, 1:], u_s[:, None, :]], axis=1)[None],
        jnp.stack([state_conv[0, :, 1], g_s], axis=1)[None],
    )
```

```python
import functools

import jax
import jax.numpy as jnp
from jax import lax
from jax.experimental import pallas as pl
from jax.experimental.pallas import tpu as pltpu

F32 = jnp.float32
BF16 = jnp.bfloat16

D_MODEL = 2048
SEQ = 8192
N_SAMPLE = 128
PAST_LEN = 8192
HEAD_DIM = 64
N_HEADS = 32
N_KV_HEADS = 4
GROUP = N_HEADS // N_KV_HEADS
KV_DIM = N_KV_HEADS * HEAD_DIM
WINDOW = 128
ROPE_THETA = 10000.0
POOL_WINDOWS = (2, 4, 8, 16)
POOL_WIDTH = 1024
POOL_GROUP_DIM = 256
POOL_HIST = 15
D_FF = 5632
CONV_W = 3
IN_DIM = 7680
LN_EPS = 1e-5
NEG_INF = -1e30
ALPHA = 2.0 ** 0.25
LOG2E = 1.4426950408889634

LANES = 128
TN = 512
VMEM_LIMIT = 56 * 1024 * 1024


def _params(ndim):
    return pltpu.CompilerParams(dimension_semantics=("arbitrary",) * ndim,
                                vmem_limit_bytes=VMEM_LIMIT)


def _dot(a, b):
    return jnp.dot(a, b, preferred_element_type=F32)


def _layer_norm(z, g, b):
    mu = jnp.mean(z, axis=-1, keepdims=True)
    d = z - mu
    var = jnp.mean(d * d, axis=-1, keepdims=True)
    return d * lax.rsqrt(var + LN_EPS) * g + b


def _rope(x, cos, sin_signed):
    lane = lax.broadcasted_iota(jnp.int32, (1, LANES), 1)
    low_half = (lane % HEAD_DIM) < (HEAD_DIM // 2)
    outs = []
    for c in range(x.shape[1] // LANES):
        xc = x[:, c * LANES:(c + 1) * LANES]
        up = pltpu.roll(xc, LANES - HEAD_DIM // 2, 1)
        down = pltpu.roll(xc, HEAD_DIM // 2, 1)
        outs.append(xc * cos + jnp.where(low_half, up, down) * sin_signed)
    return jnp.concatenate(outs, axis=1)


IN_TM = 1024
_NQ = D_MODEL // TN
_N_KV = _NQ
_N_U0 = _N_KV + 1
_N_G0 = _N_U0 + POOL_WIDTH // TN


def _in_proj_kernel(x_ref, xs_ref, w_ref, cos_ref, sin_ref, coss_ref, sins_ref,
                    q_ref, kv_ref, u_ref, g_ref, qs_ref, kvs_ref, us_ref, gs_ref):
    m = pl.program_id(0)
    n = pl.program_id(1)
    w = w_ref[...]

    def epilogue(acc, cos, sin, q_o, kv_o, u_o, g_o):
        @pl.when(n < _NQ)
        def _():
            q_o[...] = (_rope(acc, cos, sin) * (HEAD_DIM ** -0.5 * LOG2E)).astype(BF16)

        @pl.when(n == _N_KV)
        def _():
            kv_o[:, :KV_DIM] = _rope(acc[:, :KV_DIM], cos, sin)
            kv_o[:, KV_DIM:] = acc[:, KV_DIM:]

        @pl.when((n >= _N_U0) & (n < _N_G0))
        def _():
            u_o[...] = acc

        @pl.when(n >= _N_G0)
        def _():
            g_o[...] = jax.nn.sigmoid(acc).astype(BF16)

    epilogue(_dot(x_ref[...], w), cos_ref[...], sin_ref[...], q_ref, kv_ref, u_ref, g_ref)

    @pl.when(m == pl.num_programs(0) - 1)
    def _():
        epilogue(_dot(xs_ref[...], w), coss_ref[0:1, :], sins_ref[0:1, :],
                 qs_ref, kvs_ref, us_ref, gs_ref)


def _in_proj(xb, xsb, w_in_b, cos, sin, cos_s, sin_s):
    nm = SEQ // IN_TM
    nn = IN_DIM // TN
    last = nm - 1

    def clamp(n, lo, hi):
        return jnp.clip(n - lo, 0, hi - lo)

    def smp(fn):
        return lambda m, n: (0, jnp.where(m == last, fn(n), 0))

    ng = 2 * D_MODEL // TN
    return pl.pallas_call(
        _in_proj_kernel,
        grid=(nm, nn),
        in_specs=[
            pl.BlockSpec((IN_TM, D_MODEL), lambda m, n: (m, 0)),
            pl.BlockSpec((N_SAMPLE, D_MODEL), lambda m, n: (0, 0)),
            pl.BlockSpec((D_MODEL, TN), lambda m, n: (0, n)),
            pl.BlockSpec((IN_TM, LANES), lambda m, n: (m, 0)),
            pl.BlockSpec((IN_TM, LANES), lambda m, n: (m, 0)),
            pl.BlockSpec((8, LANES), lambda m, n: (0, 0)),
            pl.BlockSpec((8, LANES), lambda m, n: (0, 0)),
        ],
        out_specs=[
            pl.BlockSpec((IN_TM, TN), lambda m, n: (m, clamp(n, 0, _NQ - 1))),
            pl.BlockSpec((IN_TM, TN), lambda m, n: (m, 0)),
            pl.BlockSpec((IN_TM, TN), lambda m, n: (m, clamp(n, _N_U0, _N_G0 - 1))),
            pl.BlockSpec((IN_TM, TN), lambda m, n: (m, clamp(n, _N_G0, _N_G0 + ng - 1))),
            pl.BlockSpec((N_SAMPLE, TN), smp(lambda n: clamp(n, 0, _NQ - 1))),
            pl.BlockSpec((N_SAMPLE, TN), smp(lambda n: 0)),
            pl.BlockSpec((N_SAMPLE, TN), smp(lambda n: clamp(n, _N_U0, _N_G0 - 1))),
            pl.BlockSpec((N_SAMPLE, TN), smp(lambda n: clamp(n, _N_G0, _N_G0 + ng - 1))),
        ],
        out_shape=[
            jax.ShapeDtypeStruct((SEQ, D_MODEL), BF16),
            jax.ShapeDtypeStruct((SEQ, 2 * KV_DIM), F32),
            jax.ShapeDtypeStruct((SEQ, POOL_WIDTH), F32),
            jax.ShapeDtypeStruct((SEQ, 2 * D_MODEL), BF16),
            jax.ShapeDtypeStruct((N_SAMPLE, D_MODEL), BF16),
            jax.ShapeDtypeStruct((N_SAMPLE, 2 * KV_DIM), F32),
            jax.ShapeDtypeStruct((N_SAMPLE, POOL_WIDTH), F32),
            jax.ShapeDtypeStruct((N_SAMPLE, 2 * D_MODEL), BF16),
        ],
        compiler_params=_params(2),
        name="in_proj",
    )(xb, xsb, w_in_b, cos, sin, cos_s, sin_s)


PAIRS = GROUP // 2


def _attn_kernel(sink_ref, q_ref, kp_ref, kc_ref, vp_ref, vc_ref, o_ref):
    i = pl.program_id(0)
    kk = jnp.concatenate([kp_ref[...], kc_ref[...]], axis=0)
    vv = jnp.concatenate([vp_ref[...], vc_ref[...]], axis=0)
    low = lax.broadcasted_iota(jnp.int32, (1, LANES), 1) < HEAD_DIM
    rows = PAIRS * WINDOW
    r = lax.broadcasted_iota(jnp.int32, (rows, 2 * WINDOW), 0) % WINDOW
    c = lax.broadcasted_iota(jnp.int32, (rows, 2 * WINDOW), 1)
    visible = (c > r) & (c <= r + WINDOW) & ((i > 0) | (c >= WINDOW))
    chunk = lax.broadcasted_iota(jnp.int32, (rows, 1), 0) // WINDOW

    def sink_column(first_head):
        col = jnp.full((rows, 1), sink_ref[first_head], F32)
        for t in range(1, PAIRS):
            col = jnp.where(chunk == t, sink_ref[first_head + 2 * t], col)
        return col

    for pair in range(N_KV_HEADS // 2):
        k_pair = kk[:, pair * LANES:(pair + 1) * LANES]
        v_pair = vv[:, pair * LANES:(pair + 1) * LANES]
        k_swap = pltpu.roll(k_pair, HEAD_DIM, 1)
        v_swap = pltpu.roll(v_pair, HEAD_DIM, 1)
        for second in range(2):
            g = 2 * pair + second
            k_in_low, k_in_high = (k_swap, k_pair) if second else (k_pair, k_swap)
            v_in_low, v_in_high = (v_swap, v_pair) if second else (v_pair, v_swap)
            x = jnp.concatenate(
                [q_ref[:, (g * PAIRS + t) * LANES:(g * PAIRS + t + 1) * LANES] for t in range(PAIRS)], axis=0)

            def half(k_pad, v_pad, sink):
                s = lax.dot_general(x, k_pad.astype(BF16), (((1,), (1,)), ((), ())),
                                    preferred_element_type=F32)
                s = jnp.where(visible, s, NEG_INF)
                mx = jnp.maximum(jnp.max(s, axis=-1, keepdims=True), sink)
                e = jnp.exp2(s - mx).astype(BF16)
                return _dot(e, v_pad.astype(BF16)), jnp.exp2(sink - mx)

            a, sink_a = half(jnp.where(low, k_in_low, 0.0), jnp.where(low, v_in_low, 1.0),
                             sink_column(g * GROUP))
            b, sink_b = half(jnp.where(low, 0.0, k_in_high), jnp.where(low, 1.0, v_in_high),
                             sink_column(g * GROUP + 1))
            den = pltpu.roll(jnp.where(low, b, a), HEAD_DIM, 1) + jnp.where(low, sink_a, sink_b)
            out = (jnp.where(low, a, b) / den).astype(BF16)
            for t in range(PAIRS):
                o_ref[:, (g * PAIRS + t) * LANES:(g * PAIRS + t + 1) * LANES] = out[t * WINDOW:(t + 1) * WINDOW]


def _attention(sinks, q, kv):
    nb = SEQ // WINDOW
    prev = lambda i: jnp.maximum(i - 1, 0)
    return pl.pallas_call(
        _attn_kernel,
        grid=(nb,),
        in_specs=[
            pl.BlockSpec(memory_space=pltpu.SMEM),
            pl.BlockSpec((WINDOW, D_MODEL), lambda i: (i, 0)),
            pl.BlockSpec((WINDOW, KV_DIM), lambda i: (prev(i), 0)),
            pl.BlockSpec((WINDOW, KV_DIM), lambda i: (i, 0)),
            pl.BlockSpec((WINDOW, KV_DIM), lambda i: (prev(i), 1)),
            pl.BlockSpec((WINDOW, KV_DIM), lambda i: (i, 1)),
        ],
        out_specs=pl.BlockSpec((WINDOW, D_MODEL), lambda i: (i, 0)),
        out_shape=jax.ShapeDtypeStruct((SEQ, D_MODEL), BF16),
        compiler_params=_params(1),
        name="attn_prompt",
    )(sinks, q, kv, kv, kv, kv)


SEQS_PER_STEP = 8


def _attn_sample_kernel(sink_ref, q_ref, kvn_ref, ck_ref, cv_ref, o_ref, nk_ref, nv_ref):
    head = lax.broadcasted_iota(jnp.int32, (N_HEADS, KV_DIM), 0) // GROUP
    col_group = lax.broadcasted_iota(jnp.int32, (N_HEADS, KV_DIM), 1) // HEAD_DIM
    own = head == col_group
    newest = lax.broadcasted_iota(jnp.int32, (1, WINDOW), 1) == WINDOW - 1
    scores = []
    for b in range(SEQS_PER_STEP):
        nk_ref[b] = jnp.where(newest, kvn_ref[0, 0:KV_DIM, b:b + 1], pltpu.roll(ck_ref[b], WINDOW - 1, 1))
        nv_ref[b] = jnp.where(newest, kvn_ref[0, KV_DIM:2 * KV_DIM, b:b + 1],
                              pltpu.roll(cv_ref[b], WINDOW - 1, 1))
        qb = q_ref[b]
        qe = jnp.where(own, jnp.concatenate([qb] * N_KV_HEADS, axis=1), jnp.zeros((), BF16))
        scores.append(_dot(qe, nk_ref[b].astype(BF16)))
    s = jnp.concatenate(scores, axis=0)
    sink = jnp.concatenate([sink_ref[...]] * SEQS_PER_STEP, axis=0)
    mx = jnp.maximum(jnp.max(s, axis=-1, keepdims=True), sink)
    e = jnp.exp2(s - mx)
    inv = 1.0 / (jnp.sum(e, axis=-1, keepdims=True) + jnp.exp2(sink - mx))
    e = e.astype(BF16)
    for b in range(SEQS_PER_STEP):
        rows = slice(b * N_HEADS, (b + 1) * N_HEADS)
        of = lax.dot_general(e[rows], nv_ref[b].astype(BF16), (((1,), (1,)), ((), ())),
                             preferred_element_type=F32)
        of = jnp.where(own, of, 0.0)
        o = of[:, 0:HEAD_DIM]
        for g in range(1, N_KV_HEADS):
            o = o + of[:, g * HEAD_DIM:(g + 1) * HEAD_DIM]
        o_ref[b] = (o * inv[rows]).astype(BF16)


def _attention_sample(sink_col, q3, kvn_t, ck, cv):
    nb = N_SAMPLE // SEQS_PER_STEP
    cache_spec = pl.BlockSpec((SEQS_PER_STEP, KV_DIM, WINDOW), lambda i: (i, 0, 0))
    return pl.pallas_call(
        _attn_sample_kernel,
        grid=(nb,),
        in_specs=[
            pl.BlockSpec((N_HEADS, 1), lambda i: (0, 0)),
            pl.BlockSpec((SEQS_PER_STEP, N_HEADS, HEAD_DIM), lambda i: (i, 0, 0)),
            pl.BlockSpec((1, 2 * KV_DIM, SEQS_PER_STEP), lambda i: (i, 0, 0)),
            cache_spec, cache_spec,
        ],
        out_specs=[
            pl.BlockSpec((SEQS_PER_STEP, N_HEADS, HEAD_DIM), lambda i: (i, 0, 0)),
            cache_spec, cache_spec,
        ],
        out_shape=[
            jax.ShapeDtypeStruct((N_SAMPLE, N_HEADS, HEAD_DIM), BF16),
            jax.ShapeDtypeStruct((N_SAMPLE, KV_DIM, WINDOW), F32),
            jax.ShapeDtypeStruct((N_SAMPLE, KV_DIM, WINDOW), F32),
        ],
        compiler_params=_params(1),
        name="attn_sample",
    )(sink_col, q3, kvn_t, ck, cv)


POOL_TM = 1024
POOL_PAD = 16


def _pool_kernel(u_ref, us_ref, sp_ref, wmix_ref, scale_ref, p_ref, ps_ref, ext):
    m = pl.program_id(0)

    @pl.when(m == 0)
    def _():
        ext[0:POOL_PAD, :] = jnp.zeros((POOL_PAD, POOL_WIDTH), F32)

    ext[POOL_PAD:POOL_PAD + POOL_TM, :] = u_ref[...]
    pos = m * POOL_TM + lax.broadcasted_iota(jnp.int32, (POOL_TM, 1), 0)
    for g, w in enumerate(POOL_WINDOWS):
        cols = slice(g * POOL_GROUP_DIM, (g + 1) * POOL_GROUP_DIM)
        tot = ext[POOL_PAD:POOL_PAD + POOL_TM, cols]
        for j in range(1, w):
            tot = tot + ext[POOL_PAD - j:POOL_PAD - j + POOL_TM, cols]
        inv_cnt = 1.0 / jnp.minimum(w, pos + 1).astype(F32)
        d = tot * inv_cnt - u_ref[:, cols]
        y = _dot(d.astype(BF16), wmix_ref[g])
        p_ref[:, cols] = (y * scale_ref[:, cols]).astype(BF16)
    ext[0:POOL_PAD, :] = ext[POOL_TM:POOL_TM + POOL_PAD, :]

    @pl.when(m == pl.num_programs(0) - 1)
    def _():
        for g, w in enumerate(POOL_WINDOWS):
            cols = slice(g * POOL_GROUP_DIM, (g + 1) * POOL_GROUP_DIM)
            tot = us_ref[:, cols]
            for j in range(1, w):
                tot = tot + sp_ref[POOL_HIST - j, :, cols]
            d = tot * (1.0 / w) - us_ref[:, cols]
            y = _dot(d.astype(BF16), wmix_ref[g])
            ps_ref[:, cols] = (y * scale_ref[:, cols]).astype(BF16)


def _pool(u, us, sp2, wmix_b, scale):
    nm = SEQ // POOL_TM
    full = lambda shape: pl.BlockSpec(shape, lambda m: (0,) * len(shape))
    return pl.pallas_call(
        _pool_kernel,
        grid=(nm,),
        in_specs=[
            pl.BlockSpec((POOL_TM, POOL_WIDTH), lambda m: (m, 0)),
            full((N_SAMPLE, POOL_WIDTH)),
            full((POOL_HIST, N_SAMPLE, POOL_WIDTH)),
            full((len(POOL_WINDOWS), POOL_GROUP_DIM, POOL_GROUP_DIM)),
            full((1, POOL_WIDTH)),
        ],
        out_specs=[
            pl.BlockSpec((POOL_TM, POOL_WIDTH), lambda m: (m, 0)),
            full((N_SAMPLE, POOL_WIDTH)),
        ],
        out_shape=[
            jax.ShapeDtypeStruct((SEQ, POOL_WIDTH), BF16),
            jax.ShapeDtypeStruct((N_SAMPLE, POOL_WIDTH), BF16),
        ],
        scratch_shapes=[pltpu.VMEM((POOL_PAD + POOL_TM, POOL_WIDTH), F32)],
        compiler_params=_params(1),
        name="pool_mix",
    )(u, us, sp2, wmix_b, scale)


MG_TM = 512
_NC = D_MODEL // TN


def _merge_kernel(pool_ref, attn_ref, gp_ref, ga_ref, x_ref,
                  pools_ref, attns_ref, gps_ref, gas_ref, xs_ref,
                  wp_ref, wa_ref, wo_ref, lng_ref, lnb_ref,
                  x1_ref, x1b_ref, x1s_ref, x1bs_ref,
                  mrg, y, mrgs, ys):
    m = pl.program_id(0)
    n = pl.program_id(1)
    last_m = m == pl.num_programs(0) - 1

    @pl.when(n < _NC)
    def _():
        def merged(pool, attn, gp, ga):
            a = _dot(pool, wp_ref[...])
            b = _dot(attn, wa_ref[...])
            return (gp.astype(F32) * a + ga.astype(F32) * b).astype(BF16)

        mrg[n] = merged(pool_ref[...], attn_ref[...], gp_ref[...], ga_ref[...])

        @pl.when(last_m)
        def _():
            mrgs[n] = merged(pools_ref[...], attns_ref[...], gps_ref[...], gas_ref[...])

    @pl.when(n >= _NC)
    def _():
        def out_proj(src):
            acc = _dot(src[0], wo_ref[0:TN, :])
            for k in range(1, _NC):
                acc = acc + _dot(src[k], wo_ref[k * TN:(k + 1) * TN, :])
            return acc

        y[n - _NC] = out_proj(mrg)

        @pl.when(last_m)
        def _():
            ys[n - _NC] = out_proj(mrgs)

    @pl.when(n == 2 * _NC - 1)
    def _():
        def finish(x_in, ysrc, o_f32, o_bf16):
            z = jnp.concatenate([ysrc[k] for k in range(_NC)], axis=1) + ALPHA * x_in[...]
            x1 = _layer_norm(z, lng_ref[...], lnb_ref[...])
            o_f32[...] = x1
            o_bf16[...] = x1.astype(BF16)

        finish(x_ref, y, x1_ref, x1b_ref)

        @pl.when(last_m)
        def _():
            finish(xs_ref, ys, x1s_ref, x1bs_ref)


def _merge(pooled, attn, gates, x, pooled_s, attn_s, gates_s, xs, wp_b, wa_b, wo_b, ln_g, ln_b):
    nm = SEQ // MG_TM
    row = lambda width: pl.BlockSpec((MG_TM, width), lambda m, n: (m, 0))
    full = lambda shape: pl.BlockSpec(shape, lambda m, n: (0, 0))
    lo = lambda n: jnp.minimum(n, _NC - 1)
    hi = lambda n: jnp.maximum(n - _NC, 0)
    return pl.pallas_call(
        _merge_kernel,
        grid=(nm, 2 * _NC),
        in_specs=[
            row(POOL_WIDTH), row(D_MODEL),
            pl.BlockSpec((MG_TM, TN), lambda m, n: (m, lo(n))),
            pl.BlockSpec((MG_TM, TN), lambda m, n: (m, _NC + lo(n))),
            row(D_MODEL),
            full((N_SAMPLE, POOL_WIDTH)), full((N_SAMPLE, D_MODEL)),
            pl.BlockSpec((N_SAMPLE, TN), lambda m, n: (0, lo(n))),
            pl.BlockSpec((N_SAMPLE, TN), lambda m, n: (0, _NC + lo(n))),
            full((N_SAMPLE, D_MODEL)),
            pl.BlockSpec((POOL_WIDTH, TN), lambda m, n: (0, lo(n))),
            pl.BlockSpec((D_MODEL, TN), lambda m, n: (0, lo(n))),
            pl.BlockSpec((D_MODEL, TN), lambda m, n: (0, hi(n))),
            full((1, D_MODEL)), full((1, D_MODEL)),
        ],
        out_specs=[row(D_MODEL), row(D_MODEL), full((N_SAMPLE, D_MODEL)), full((N_SAMPLE, D_MODEL))],
        out_shape=[
            jax.ShapeDtypeStruct((SEQ, D_MODEL), F32),
            jax.ShapeDtypeStruct((SEQ, D_MODEL), BF16),
            jax.ShapeDtypeStruct((N_SAMPLE, D_MODEL), F32),
            jax.ShapeDtypeStruct((N_SAMPLE, D_MODEL), BF16),
        ],
        scratch_shapes=[
            pltpu.VMEM((_NC, MG_TM, TN), BF16), pltpu.VMEM((_NC, MG_TM, TN), F32),
            pltpu.VMEM((_NC, N_SAMPLE, TN), BF16), pltpu.VMEM((_NC, N_SAMPLE, TN), F32),
        ],
        compiler_params=_params(2),
        name="merge_out_ln1",
    )(pooled, attn, gates, gates, x, pooled_s, attn_s, gates_s, gates_s, xs,
      wp_b, wa_b, wo_b, ln_g, ln_b)


FF_TM = 512
TF = 512
CARRY = 8


def _ffn_kernel(x1b_ref, x1_ref, x1bs_ref, x1s_ref, sc_ref,
                wg_ref, wu_ref, wd_ref, cw_ref, cb_ref, lng_ref, lnb_ref,
                o_ref, os_ref, glast_ref, scn_ref,
                acc, accs, gext, carry):
    m = pl.program_id(0)
    f = pl.program_id(1)
    last_m = m == pl.num_programs(0) - 1
    last_f = f == pl.num_programs(1) - 1
    w0 = cw_ref[0:1, :]
    w1 = cw_ref[1:2, :]
    w2 = cw_ref[2:3, :]
    cb = cb_ref[...]

    @pl.when(f == 0)
    def _():
        acc[...] = jnp.zeros_like(acc)

    g = _dot(x1b_ref[...], wg_ref[...])
    up = _dot(x1b_ref[...], wu_ref[...])

    @pl.when(m == 0)
    def _():
        gext[0:CARRY, :] = jnp.zeros((CARRY, TF), F32)

    @pl.when(m > 0)
    def _():
        gext[0:CARRY, :] = carry[f]

    gext[CARRY:CARRY + FF_TM, :] = g
    carry[f] = g[FF_TM - CARRY:FF_TM, :]
    gc =(cb + w0 * gext[CARRY - 2:CARRY - 2 + FF_TM, :]
          + w1 * gext[CARRY - 1:CARRY - 1 + FF_TM, :] + w2 * g)
    h = (jax.nn.gelu(gc, approximate=True) * up).astype(BF16)
    acc[...] += _dot(h, wd_ref[...])

    @pl.when(last_f)
    def _():
        o_ref[...] = _layer_norm(ALPHA * x1_ref[...] + acc[...], lng_ref[...], lnb_ref[...])

    @pl.when(last_m)
    def _():
        @pl.when(f == 0)
        def _():
            accs[...] = jnp.zeros_like(accs)

        glast_ref[...] = g[FF_TM - CARRY:FF_TM, :]
        gs = _dot(x1bs_ref[...], wg_ref[...])
        ups = _dot(x1bs_ref[...], wu_ref[...])
        older = sc_ref[:, 0, :]
        newer = sc_ref[:, 1, :]
        scn_ref[:, 0, :] = newer
        scn_ref[:, 1, :] = gs
        gcs = cb + w0 * older + w1 * newer + w2 * gs
        hs = (jax.nn.gelu(gcs, approximate=True) * ups).astype(BF16)
        accs[...] += _dot(hs, wd_ref[...])

        @pl.when(last_f)
        def _():
            os_ref[...] = _layer_norm(ALPHA * x1s_ref[...] + accs[...], lng_ref[...], lnb_ref[...])


def _ffn(x1b, x1, x1bs, x1s, sc, wg_b, wu_b, wd_b, conv_w, conv_b, ln_g, ln_b):
    nm = SEQ // FF_TM
    nf = D_FF // TF
    last = nm - 1
    row = lambda width: pl.BlockSpec((FF_TM, width), lambda m, f: (m, 0))
    full = lambda shape: pl.BlockSpec(shape, lambda m, f: (0, 0))
    only_last = lambda f, m: jnp.where(m == last, f, 0)
    return pl.pallas_call(
        _ffn_kernel,
        grid=(nm, nf),
        in_specs=[
            row(D_MODEL), row(D_MODEL),
            full((N_SAMPLE, D_MODEL)), full((N_SAMPLE, D_MODEL)),
            pl.BlockSpec((N_SAMPLE, CONV_W - 1, TF), lambda m, f: (0, 0, only_last(f, m))),
            pl.BlockSpec((D_MODEL, TF), lambda m, f: (0, f)),
            pl.BlockSpec((D_MODEL, TF), lambda m, f: (0, f)),
            pl.BlockSpec((TF, D_MODEL), lambda m, f: (f, 0)),
            pl.BlockSpec((CONV_W, TF), lambda m, f: (0, f)),
            pl.BlockSpec((1, TF), lambda m, f: (0, f)),
            full((1, D_MODEL)), full((1, D_MODEL)),
        ],
        out_specs=[
            row(D_MODEL),
            full((N_SAMPLE, D_MODEL)),
            pl.BlockSpec((CARRY, TF), lambda m, f: (0, only_last(f, m))),
            pl.BlockSpec((N_SAMPLE, CONV_W - 1, TF), lambda m, f: (0, 0, only_last(f, m))),
        ],
        out_shape=[
            jax.ShapeDtypeStruct((SEQ, D_MODEL), F32),
            jax.ShapeDtypeStruct((N_SAMPLE, D_MODEL), F32),
            jax.ShapeDtypeStruct((CARRY, D_FF), F32),
            jax.ShapeDtypeStruct((N_SAMPLE, CONV_W - 1, D_FF), F32),
        ],
        scratch_shapes=[
            pltpu.VMEM((FF_TM, D_MODEL), F32), pltpu.VMEM((N_SAMPLE, D_MODEL), F32),
            pltpu.VMEM((CARRY + FF_TM, TF), F32), pltpu.VMEM((nf, CARRY, TF), F32),
        ],
        compiler_params=_params(2),
        name="convffn_ln2",
    )(x1b, x1, x1bs, x1s, sc, wg_b, wu_b, wd_b, conv_w, conv_b, ln_g, ln_b)


def _rope_tables(pos):
    half = HEAD_DIM // 2
    inv = ROPE_THETA ** (-jnp.arange(half, dtype=F32) / half)
    ang = pos.astype(F32)[:, None] * inv[None, :]
    cos = jnp.cos(ang)
    sin = jnp.sin(ang)
    reps = LANES // HEAD_DIM
    return (jnp.concatenate([cos, cos] * reps, axis=1),
            jnp.concatenate([-sin, sin] * reps, axis=1))


def kernel(x_prompt, x_sample, cache_k, cache_v, state_pool, state_conv, w_in, attn_sinks, w_pool_mix,
           pool_scale, w_attn_branch, w_pool_branch, w_out, ln1_g, ln1_b, w_up, w_gate, conv_w, conv_b,
           w_down, ln2_g, ln2_b):
    x = x_prompt.reshape(SEQ, D_MODEL)
    xs = x_sample.reshape(N_SAMPLE, D_MODEL)
    cos, sin = _rope_tables(jnp.arange(SEQ))
    cos_s, sin_s = _rope_tables(jnp.full((8,), PAST_LEN))

    q, kv, u, gates, q_s, kv_s, u_s, gates_s = _in_proj(
        x.astype(BF16), xs.astype(BF16), w_in[0].astype(BF16), cos, sin, cos_s, sin_s)

    to_dsw = lambda c: jnp.transpose(c[0], (0, 2, 3, 1)).reshape(N_SAMPLE, KV_DIM, WINDOW)
    steps = N_SAMPLE // SEQS_PER_STEP
    kvn_t = jnp.transpose(kv_s.reshape(steps, SEQS_PER_STEP, 2 * KV_DIM), (0, 2, 1))
    sinks = attn_sinks[0] * LOG2E
    attn = _attention(sinks, q, kv)
    attn_s, new_k, new_v = _attention_sample(
        sinks.reshape(N_HEADS, 1), q_s.reshape(N_SAMPLE, N_HEADS, HEAD_DIM), kvn_t,
        to_dsw(cache_k), to_dsw(cache_v))

    state_pool_t = jnp.transpose(state_pool[0], (1, 0, 2))
    pooled, pooled_s = _pool(u, u_s, state_pool_t, w_pool_mix[0].astype(BF16), pool_scale)

    x1, x1b, x1_s, x1b_s = _merge(
        pooled, attn, gates, x, pooled_s, attn_s.reshape(N_SAMPLE, D_MODEL), gates_s, xs,
        w_pool_branch[0].astype(BF16), w_attn_branch[0].astype(BF16), w_out[0].astype(BF16),
        ln1_g, ln1_b)

    y, y_s, g_last, new_conv_s = _ffn(
        x1b, x1, x1b_s, x1_s, state_conv[0],
        w_gate[0].astype(BF16), w_up[0].astype(BF16), w_down[0].astype(BF16),
        conv_w[0], conv_b, ln2_g, ln2_b)

    from_dsw = lambda c: jnp.transpose(
        c.reshape(N_SAMPLE, N_KV_HEADS, HEAD_DIM, WINDOW), (0, 3, 1, 2))[None]
    return (
        y.reshape(1, SEQ, D_MODEL),
        y_s.reshape(N_SAMPLE, 1, D_MODEL),
        kv[SEQ - WINDOW:, :KV_DIM].reshape(1, 1, WINDOW, N_KV_HEADS, HEAD_DIM),
        kv[SEQ - WINDOW:, KV_DIM:].reshape(1, 1, WINDOW, N_KV_HEADS, HEAD_DIM),
        u[SEQ - POOL_HIST:].reshape(1, 1, POOL_HIST, POOL_WIDTH),
        g_last[CARRY - (CONV_W - 1):].reshape(1, 1, CONV_W - 1, D_FF),
        from_dsw(new_k),
        from_dsw(new_v),
        jnp.transpose(jnp.concatenate([state_pool_t[1:], u_s[None]], axis=0), (1, 0, 2))[None],
        new_conv_s[None],
    )
```

```python
import jax
import jax.numpy as jnp
from jax import lax
from jax.experimental import pallas as pl
from jax.experimental.pallas import tpu as pltpu

F32 = jnp.float32
BF16 = jnp.bfloat16

D_MODEL = 2048
SEQ = 8192
N_SAMPLE = 128
PAST_LEN = 8192
HEAD_DIM = 64
N_HEADS = 32
N_KV_HEADS = 4
GROUP = N_HEADS // N_KV_HEADS
KV_DIM = N_KV_HEADS * HEAD_DIM
WINDOW = 128
ROPE_THETA = 10000.0
POOL_WINDOWS = (2, 4, 8, 16)
POOL_WIDTH = 1024
POOL_GROUP_DIM = 256
POOL_HIST = 15
D_FF = 5632
CONV_W = 3
IN_DIM = 7680
LN_EPS = 1e-5
NEG_INF = -1e30
ALPHA = 2.0 ** 0.25
LOG2E = 1.4426950408889634

LANES = 128
VMEM_LIMIT = 60000 * 1024

ROWS = SEQ + N_SAMPLE
TM = 1040
NT = ROWS // TM
SPLIT = SEQ - (NT - 1) * TM
HALF = 528
TN = 512
assert NT * TM == ROWS and SPLIT + N_SAMPLE == TM and SPLIT % 16 == 0 and HALF % 16 == 0


def _params(ndim):
    return pltpu.CompilerParams(dimension_semantics=("arbitrary",) * ndim,
                                vmem_limit_bytes=VMEM_LIMIT)


def _dot(a, b):
    return jnp.dot(a, b, preferred_element_type=F32)


def _layer_norm(z, g, b):
    mu = jnp.mean(z, axis=-1, keepdims=True)
    d = z - mu
    var = jnp.mean(d * d, axis=-1, keepdims=True)
    return d * lax.rsqrt(var + LN_EPS) * g + b


def _rope(x, cos, sin_signed):
    lane = lax.broadcasted_iota(jnp.int32, (1, LANES), 1)
    low_half = (lane % HEAD_DIM) < (HEAD_DIM // 2)
    outs = []
    for c in range(x.shape[1] // LANES):
        xc = x[:, c * LANES:(c + 1) * LANES]
        up = pltpu.roll(xc, LANES - HEAD_DIM // 2, 1)
        down = pltpu.roll(xc, HEAD_DIM // 2, 1)
        outs.append(xc * cos + jnp.where(low_half, up, down) * sin_signed)
    return jnp.concatenate(outs, axis=1)


_NQ = D_MODEL // TN
_N_KV = _NQ
_N_U0 = _N_KV + 1
_N_G0 = _N_U0 + POOL_WIDTH // TN


def _in_proj_kernel(x_ref, xs_ref, w_ref, cos_ref, sin_ref, q_ref, kv_ref, u_ref, g_ref, xb):
    m = pl.program_id(0)
    n = pl.program_id(1)

    @pl.when(n == 0)
    def _():
        @pl.when(m < NT - 1)
        def _():
            xb[...] = x_ref[...].astype(BF16)

        @pl.when(m == NT - 1)
        def _():
            xb[0:SPLIT, :] = x_ref[0:SPLIT, :].astype(BF16)
            xb[SPLIT:TM, :] = xs_ref[...].astype(BF16)

    wb = w_ref[...].astype(BF16)
    halves = ((0, HALF), (HALF, TM))

    def project(epilogue):
        accs = [_dot(xb[lo:hi, :], wb) for lo, hi in halves]
        for (lo, hi), acc in zip(halves, accs):
            epilogue(acc, lo, hi)

    def q_epilogue(acc, lo, hi):
        roped = _rope(acc, cos_ref[lo:hi, :], sin_ref[lo:hi, :])
        q_ref[lo:hi, :] = (roped * (HEAD_DIM ** -0.5 * LOG2E)).astype(BF16)

    def kv_epilogue(acc, lo, hi):
        kv_ref[lo:hi, :KV_DIM] = _rope(acc[:, :KV_DIM], cos_ref[lo:hi, :], sin_ref[lo:hi, :])
        kv_ref[lo:hi, KV_DIM:] = acc[:, KV_DIM:]

    def u_epilogue(acc, lo, hi):
        u_ref[lo:hi, :] = acc

    def gate_epilogue(acc, lo, hi):
        g_ref[lo:hi, :] = jax.nn.sigmoid(acc).astype(BF16)

    pl.when(n < _NQ)(lambda: project(q_epilogue))
    pl.when(n == _N_KV)(lambda: project(kv_epilogue))
    pl.when((n >= _N_U0) & (n < _N_G0))(lambda: project(u_epilogue))
    pl.when(n >= _N_G0)(lambda: project(gate_epilogue))


def _in_proj(x, xs, w_in, cos, sin):
    nn = IN_DIM // TN
    ng = 2 * D_MODEL // TN

    def clamp(n, lo, hi):
        return jnp.clip(n - lo, 0, hi - lo)

    return pl.pallas_call(
        _in_proj_kernel,
        grid=(NT, nn),
        in_specs=[
            pl.BlockSpec((TM, D_MODEL), lambda m, n: (m, 0)),
            pl.BlockSpec((N_SAMPLE, D_MODEL), lambda m, n: (0, 0)),
            pl.BlockSpec((D_MODEL, TN), lambda m, n: (0, n)),
            pl.BlockSpec((TM, LANES), lambda m, n: (m, 0)),
            pl.BlockSpec((TM, LANES), lambda m, n: (m, 0)),
        ],
        out_specs=[
            pl.BlockSpec((TM, TN), lambda m, n: (m, clamp(n, 0, _NQ - 1))),
            pl.BlockSpec((TM, TN), lambda m, n: (m, 0)),
            pl.BlockSpec((TM, TN), lambda m, n: (m, clamp(n, _N_U0, _N_G0 - 1))),
            pl.BlockSpec((TM, TN), lambda m, n: (m, clamp(n, _N_G0, _N_G0 + ng - 1))),
        ],
        out_shape=[
            jax.ShapeDtypeStruct((ROWS, D_MODEL), BF16),
            jax.ShapeDtypeStruct((ROWS, 2 * KV_DIM), F32),
            jax.ShapeDtypeStruct((ROWS, POOL_WIDTH), F32),
            jax.ShapeDtypeStruct((ROWS, 2 * D_MODEL), BF16),
        ],
        scratch_shapes=[pltpu.VMEM((TM, D_MODEL), BF16)],
        compiler_params=_params(2),
        name="in_proj",
    )(x, xs, w_in, cos, sin)


PAIRS = GROUP // 2


def _attn_kernel(sink_ref, q_ref, kp_ref, kc_ref, vp_ref, vc_ref, os_ref, o_ref):
    i = pl.program_id(0)
    last = pl.num_programs(0) - 1

    @pl.when(i < last)
    def _():
        _attn_block(i, sink_ref, q_ref, kp_ref, kc_ref, vp_ref, vc_ref, o_ref)

    @pl.when(i == last)
    def _():
        o_ref[...] = os_ref[...]


def _attn_block(i, sink_ref, q_ref, kp_ref, kc_ref, vp_ref, vc_ref, o_ref):
    kk =jnp.concatenate([kp_ref[...], kc_ref[...]], axis=0)
    vv = jnp.concatenate([vp_ref[...], vc_ref[...]], axis=0)
    low = lax.broadcasted_iota(jnp.int32, (1, LANES), 1) < HEAD_DIM
    rows = PAIRS * WINDOW
    r = lax.broadcasted_iota(jnp.int32, (rows, 2 * WINDOW), 0) % WINDOW
    c = lax.broadcasted_iota(jnp.int32, (rows, 2 * WINDOW), 1)
    visible = (c > r) & (c <= r + WINDOW) & ((i > 0) | (c >= WINDOW))
    chunk = lax.broadcasted_iota(jnp.int32, (rows, 1), 0) // WINDOW

    def sink_column(first_head):
        col = jnp.full((rows, 1), sink_ref[first_head], F32)
        for t in range(1, PAIRS):
            col = jnp.where(chunk == t, sink_ref[first_head + 2 * t], col)
        return col

    for pair in range(N_KV_HEADS // 2):
        k_pair = kk[:, pair * LANES:(pair + 1) * LANES]
        v_pair = vv[:, pair * LANES:(pair + 1) * LANES]
        k_swap = pltpu.roll(k_pair, HEAD_DIM, 1)
        v_swap = pltpu.roll(v_pair, HEAD_DIM, 1)
        for second in range(2):
            g = 2 * pair + second
            k_in_low, k_in_high = (k_swap, k_pair) if second else (k_pair, k_swap)
            v_in_low, v_in_high = (v_swap, v_pair) if second else (v_pair, v_swap)
            x = jnp.concatenate(
                [q_ref[:, (g * PAIRS + t) * LANES:(g * PAIRS + t + 1) * LANES] for t in range(PAIRS)], axis=0)

            def half(k_pad, v_pad, sink):
                s = lax.dot_general(x, k_pad.astype(BF16), (((1,), (1,)), ((), ())),
                                    preferred_element_type=F32)
                s = jnp.where(visible, s, NEG_INF)
                mx = jnp.maximum(jnp.max(s, axis=-1, keepdims=True), sink)
                e = jnp.exp2(s - mx).astype(BF16)
                return _dot(e, v_pad.astype(BF16)), jnp.exp2(sink - mx)

            a, sink_a = half(jnp.where(low, k_in_low, 0.0), jnp.where(low, v_in_low, 1.0),
                             sink_column(g * GROUP))
            b, sink_b = half(jnp.where(low, 0.0, k_in_high), jnp.where(low, 1.0, v_in_high),
                             sink_column(g * GROUP + 1))
            den = pltpu.roll(jnp.where(low, b, a), HEAD_DIM, 1) + jnp.where(low, sink_a, sink_b)
            out = (jnp.where(low, a, b) / den).astype(BF16)
            for t in range(PAIRS):
                o_ref[:, (g * PAIRS + t) * LANES:(g * PAIRS + t + 1) * LANES] = out[t * WINDOW:(t + 1) * WINDOW]


def _attention(sinks, q, kv, attn_s):
    nb = SEQ // WINDOW
    cur = lambda i: jnp.minimum(i, nb - 1)
    prev = lambda i: jnp.clip(i - 1, 0, nb - 1)
    return pl.pallas_call(
        _attn_kernel,
        grid=(nb + 1,),
        in_specs=[
            pl.BlockSpec(memory_space=pltpu.SMEM),
            pl.BlockSpec((WINDOW, D_MODEL), lambda i: (cur(i), 0)),
            pl.BlockSpec((WINDOW, KV_DIM), lambda i: (prev(i), 0)),
            pl.BlockSpec((WINDOW, KV_DIM), lambda i: (cur(i), 0)),
            pl.BlockSpec((WINDOW, KV_DIM), lambda i: (prev(i), 1)),
            pl.BlockSpec((WINDOW, KV_DIM), lambda i: (cur(i), 1)),
            pl.BlockSpec((N_SAMPLE, D_MODEL), lambda i: (0, 0)),
        ],
        out_specs=pl.BlockSpec((WINDOW, D_MODEL), lambda i: (i, 0)),
        out_shape=jax.ShapeDtypeStruct((ROWS, D_MODEL), BF16),
        compiler_params=_params(1),
        name="attn_prompt",
    )(sinks, q, kv, kv, kv, kv, attn_s)


SEQS_PER_STEP = 8


def _attn_sample_kernel(sink_ref, q_ref, kvn_ref, ck_ref, cv_ref, o_ref, nk_ref, nv_ref):
    head = lax.broadcasted_iota(jnp.int32, (N_HEADS, KV_DIM), 0) // GROUP
    col_group = lax.broadcasted_iota(jnp.int32, (N_HEADS, KV_DIM), 1) // HEAD_DIM
    own = head == col_group
    newest = lax.broadcasted_iota(jnp.int32, (1, WINDOW), 1) == WINDOW - 1
    scores = []
    for b in range(SEQS_PER_STEP):
        nk_ref[b] = jnp.where(newest, kvn_ref[0, 0:KV_DIM, b:b + 1], pltpu.roll(ck_ref[b], WINDOW - 1, 1))
        nv_ref[b] = jnp.where(newest, kvn_ref[0, KV_DIM:2 * KV_DIM, b:b + 1],
                              pltpu.roll(cv_ref[b], WINDOW - 1, 1))
        qb = q_ref[b]
        qe = jnp.where(own, jnp.concatenate([qb] * N_KV_HEADS, axis=1), jnp.zeros((), BF16))
        scores.append(_dot(qe, nk_ref[b].astype(BF16)))
    s = jnp.concatenate(scores, axis=0)
    sink = jnp.concatenate([sink_ref[...]] * SEQS_PER_STEP, axis=0)
    mx = jnp.maximum(jnp.max(s, axis=-1, keepdims=True), sink)
    e = jnp.exp2(s - mx)
    inv = 1.0 / (jnp.sum(e, axis=-1, keepdims=True) + jnp.exp2(sink - mx))
    e = e.astype(BF16)
    for b in range(SEQS_PER_STEP):
        rows = slice(b * N_HEADS, (b + 1) * N_HEADS)
        of = lax.dot_general(e[rows], nv_ref[b].astype(BF16), (((1,), (1,)), ((), ())),
                             preferred_element_type=F32)
        of = jnp.where(own, of, 0.0)
        o = of[:, 0:HEAD_DIM]
        for g in range(1, N_KV_HEADS):
            o = o + of[:, g * HEAD_DIM:(g + 1) * HEAD_DIM]
        o_ref[b] = (o * inv[rows]).astype(BF16)


def _attention_sample(sink_col, q3, kvn_t, ck, cv):
    nb = N_SAMPLE // SEQS_PER_STEP
    cache_spec = pl.BlockSpec((SEQS_PER_STEP, KV_DIM, WINDOW), lambda i: (i, 0, 0))
    return pl.pallas_call(
        _attn_sample_kernel,
        grid=(nb,),
        in_specs=[
            pl.BlockSpec((N_HEADS, 1), lambda i: (0, 0)),
            pl.BlockSpec((SEQS_PER_STEP, N_HEADS, HEAD_DIM), lambda i: (i, 0, 0)),
            pl.BlockSpec((1, 2 * KV_DIM, SEQS_PER_STEP), lambda i: (i, 0, 0)),
            cache_spec, cache_spec,
        ],
        out_specs=[
            pl.BlockSpec((SEQS_PER_STEP, N_HEADS, HEAD_DIM), lambda i: (i, 0, 0)),
            cache_spec, cache_spec,
        ],
        out_shape=[
            jax.ShapeDtypeStruct((N_SAMPLE, N_HEADS, HEAD_DIM), BF16),
            jax.ShapeDtypeStruct((N_SAMPLE, KV_DIM, WINDOW), F32),
            jax.ShapeDtypeStruct((N_SAMPLE, KV_DIM, WINDOW), F32),
        ],
        compiler_params=_params(1),
        name="attn_sample",
    )(sink_col, q3, kvn_t, ck, cv)


POOL_PAD = 16


def _pool_kernel(u_ref, sp_ref, wmix_ref, scale_ref, p_ref, ext):
    m = pl.program_id(0)

    @pl.when(m == 0)
    def _():
        ext[0:POOL_PAD, :] = jnp.zeros((POOL_PAD, POOL_WIDTH), F32)

    ext[POOL_PAD:POOL_PAD + TM, :] = u_ref[...]
    pos = m * TM + lax.broadcasted_iota(jnp.int32, (TM, 1), 0)
    for g, w in enumerate(POOL_WINDOWS):
        cols = slice(g * POOL_GROUP_DIM, (g + 1) * POOL_GROUP_DIM)
        tot = ext[POOL_PAD:POOL_PAD + TM, cols]
        for j in range(1, w):
            tot = tot + ext[POOL_PAD - j:POOL_PAD - j + TM, cols]
        inv_cnt = 1.0 / jnp.minimum(w, pos + 1).astype(F32)
        d = tot * inv_cnt - u_ref[:, cols]
        y = _dot(d.astype(BF16), wmix_ref[g].astype(BF16))
        p_ref[:, cols] = (y * scale_ref[:, cols]).astype(BF16)
    ext[0:POOL_PAD, :] = ext[TM:TM + POOL_PAD, :]

    @pl.when(m == NT - 1)
    def _():
        for g, w in enumerate(POOL_WINDOWS):
            cols = slice(g * POOL_GROUP_DIM, (g + 1) * POOL_GROUP_DIM)
            us = u_ref[SPLIT:TM, cols]
            tot = us
            for j in range(1, w):
                tot = tot + sp_ref[POOL_HIST - j, :, cols]
            d = tot * (1.0 / w) - us
            y = _dot(d.astype(BF16), wmix_ref[g].astype(BF16))
            p_ref[SPLIT:TM, cols] = (y * scale_ref[:, cols]).astype(BF16)


def _pool(u, sp_t, wmix, scale):
    full = lambda shape: pl.BlockSpec(shape, lambda m: (0,) * len(shape))
    return pl.pallas_call(
        _pool_kernel,
        grid=(NT,),
        in_specs=[
            pl.BlockSpec((TM, POOL_WIDTH), lambda m: (m, 0)),
            full((POOL_HIST, N_SAMPLE, POOL_WIDTH)),
            full((len(POOL_WINDOWS), POOL_GROUP_DIM, POOL_GROUP_DIM)),
            full((1, POOL_WIDTH)),
        ],
        out_specs=pl.BlockSpec((TM, POOL_WIDTH), lambda m: (m, 0)),
        out_shape=jax.ShapeDtypeStruct((ROWS, POOL_WIDTH), BF16),
        scratch_shapes=[pltpu.VMEM((POOL_PAD + TM, POOL_WIDTH), F32)],
        compiler_params=_params(1),
        name="pool_mix",
    )(u, sp_t, wmix, scale)


_NC = D_MODEL // TN


def _merge_kernel(pool_ref, attn_ref, gp_ref, ga_ref, wp_ref, wa_ref, o_ref, wpb, wab):
    @pl.when(pl.program_id(1) == 0)
    def _():
        wpb[...] = wp_ref[...].astype(BF16)
        wab[...] = wa_ref[...].astype(BF16)

    halves = ((0, HALF), (HALF, TM))
    accs = [(_dot(pool_ref[lo:hi, :], wpb[...]), _dot(attn_ref[lo:hi, :], wab[...])) for lo, hi in halves]
    for (lo, hi), (a, b) in zip(halves, accs):
        o_ref[lo:hi, :] = (gp_ref[lo:hi, :].astype(F32) * a + ga_ref[lo:hi, :].astype(F32) * b).astype(BF16)


def _merge(pooled, attn, gates, wp, wa):
    return pl.pallas_call(
        _merge_kernel,
        grid=(_NC, NT),
        in_specs=[
            pl.BlockSpec((TM, POOL_WIDTH), lambda n, m: (m, 0)),
            pl.BlockSpec((TM, D_MODEL), lambda n, m: (m, 0)),
            pl.BlockSpec((TM, TN), lambda n, m: (m, n)),
            pl.BlockSpec((TM, TN), lambda n, m: (m, _NC + n)),
            pl.BlockSpec((POOL_WIDTH, TN), lambda n, m: (0, n)),
            pl.BlockSpec((D_MODEL, TN), lambda n, m: (0, n)),
        ],
        out_specs=pl.BlockSpec((TM, TN), lambda n, m: (m, n)),
        out_shape=jax.ShapeDtypeStruct((ROWS, D_MODEL), BF16),
        scratch_shapes=[pltpu.VMEM((POOL_WIDTH, TN), BF16), pltpu.VMEM((D_MODEL, TN), BF16)],
        compiler_params=_params(2),
        name="branch_merge",
    )(pooled, attn, gates, gates, wp, wa)


def _out_ln_kernel(mrg_ref, wo_ref, x_ref, xs_ref, lng_ref, lnb_ref, x1_ref, x1b_ref):
    m = pl.program_id(0)
    n = pl.program_id(1)
    y = _dot(mrg_ref[...], wo_ref[...].astype(BF16))

    for k in range(_NC):
        cols = slice(k * TN, (k + 1) * TN)

        @pl.when((n == k) & (m < NT - 1))
        def _():
            x1_ref[:, cols] = y + ALPHA * x_ref[...]

        @pl.when((n == k) & (m == NT - 1))
        def _():
            x1_ref[0:SPLIT, cols] = y[0:SPLIT] + ALPHA * x_ref[0:SPLIT, :]
            x1_ref[SPLIT:TM, cols] = y[SPLIT:TM] + ALPHA * xs_ref[...]

    @pl.when(n == _NC - 1)
    def _():
        x1 = _layer_norm(x1_ref[...], lng_ref[...], lnb_ref[...])
        x1_ref[...] = x1
        x1b_ref[...] = x1.astype(BF16)


def _out_ln(merged, wo, x, xs, ln_g, ln_b):
    full = lambda shape: pl.BlockSpec(shape, lambda m, n: (0, 0))
    row = pl.BlockSpec((TM, D_MODEL), lambda m, n: (m, 0))
    return pl.pallas_call(
        _out_ln_kernel,
        grid=(NT, _NC),
        in_specs=[
            row,
            pl.BlockSpec((D_MODEL, TN), lambda m, n: (0, n)),
            pl.BlockSpec((TM, TN), lambda m, n: (m, n)),
            pl.BlockSpec((N_SAMPLE, TN), lambda m, n: (0, n)),
            full((1, D_MODEL)), full((1, D_MODEL)),
        ],
        out_specs=[row, row],
        out_shape=[jax.ShapeDtypeStruct((ROWS, D_MODEL), F32), jax.ShapeDtypeStruct((ROWS, D_MODEL), BF16)],
        compiler_params=_params(2),
        name="out_proj_ln1",
    )(merged, wo, x, xs, ln_g, ln_b)


TF = 256
NF = D_FF // TF
CARRY = 8


def _ffn_kernel(x1b_ref, x1_hbm, sc_ref, wg_ref, wu_ref, wd_ref, cw_ref, cb_ref, lng_ref, lnb_ref,
                o_ref, os_ref, glast_ref, scn_ref,
                hbuf, gext, carry, x1res, sem):
    m = pl.program_id(0)
    f = pl.program_id(1)
    last_m = m == NT - 1
    x1_copy = pltpu.make_async_copy(x1_hbm.at[pl.ds(m * TM, TM), :], x1res, sem)

    def up_and_gate(with_down):
        x = x1b_ref[...]
        g = _dot(x, wg_ref[...].astype(BF16))
        up = _dot(x, wu_ref[...].astype(BF16))
        if with_down:
            o_ref[...] += _dot(hbuf[(f + 1) % 2], wd_ref[...].astype(BF16))
        gext[0:CARRY, :] = carry[f]
        gext[CARRY:CARRY + TM, :] = g
        carry[f] = g[TM - CARRY:TM, :]
        w0 = cw_ref[0:1, :]
        w1 = cw_ref[1:2, :]
        w2 = cw_ref[2:3, :]
        cb = cb_ref[...]
        older = gext[CARRY - 2:CARRY - 2 + TM, :]
        newer = gext[CARRY - 1:CARRY - 1 + TM, :]
        gc_prompt = cb + w0 * older[0:SPLIT] + w1 * newer[0:SPLIT] + w2 * g[0:SPLIT]
        older_s = jnp.where(last_m, sc_ref[:, 0, :], older[SPLIT:TM])
        newer_s = jnp.where(last_m, sc_ref[:, 1, :], newer[SPLIT:TM])
        gc_tail = cb + w0 * older_s + w1 * newer_s + w2 * g[SPLIT:TM]
        gc = jnp.concatenate([gc_prompt, gc_tail], axis=0)
        hbuf[f % 2] = (jax.nn.gelu(gc, approximate=True) * up).astype(BF16)

        @pl.when(last_m)
        def _():
            glast_ref[...] = g[SPLIT - CARRY:SPLIT, :]
            scn_ref[:, 0, :] = sc_ref[:, 1, :]
            scn_ref[:, 1, :] = g[SPLIT:TM, :]

    @pl.when(f == 0)
    def _():
        @pl.when(m == 0)
        def _():
            carry[...] = jnp.zeros_like(carry)

        x1_copy.start()
        o_ref[...] = jnp.zeros_like(o_ref)
        up_and_gate(False)

    @pl.when((f > 0) & (f < NF))
    def _():
        up_and_gate(True)

    @pl.when(f == NF)
    def _():
        acc = o_ref[...] + _dot(hbuf[(NF - 1) % 2], wd_ref[...].astype(BF16))
        x1_copy.wait()
        y = _layer_norm(ALPHA * x1res[...] + acc, lng_ref[...], lnb_ref[...])
        o_ref[...] = y

        @pl.when(last_m)
        def _():
            os_ref[...] = y[SPLIT:TM, :]


def _ffn(x1b, x1, sc, wg, wu, wd, conv_w, conv_b, ln_g, ln_b):
    full = lambda shape: pl.BlockSpec(shape, lambda m, f: (0, 0))
    up_tile = lambda f: jnp.minimum(f, NF - 1)
    only_last = lambda m, f: jnp.where(m == NT - 1, up_tile(f), 0)
    row = pl.BlockSpec((TM, D_MODEL), lambda m, f: (m, 0))
    return pl.pallas_call(
        _ffn_kernel,
        grid=(NT, NF + 1),
        in_specs=[
            row,
            pl.BlockSpec(memory_space=pl.ANY),
            pl.BlockSpec((N_SAMPLE, CONV_W - 1, TF), lambda m, f: (0, 0, only_last(m, f))),
            pl.BlockSpec((D_MODEL, TF), lambda m, f: (0, up_tile(f))),
            pl.BlockSpec((D_MODEL, TF), lambda m, f: (0, up_tile(f))),
            pl.BlockSpec((TF, D_MODEL), lambda m, f: (jnp.maximum(f - 1, 0), 0)),
            pl.BlockSpec((CONV_W, TF), lambda m, f: (0, up_tile(f))),
            pl.BlockSpec((1, TF), lambda m, f: (0, up_tile(f))),
            full((1, D_MODEL)), full((1, D_MODEL)),
        ],
        out_specs=[
            row,
            full((N_SAMPLE, D_MODEL)),
            pl.BlockSpec((CARRY, TF), lambda m, f: (0, only_last(m, f))),
            pl.BlockSpec((N_SAMPLE, CONV_W - 1, TF), lambda m, f: (0, 0, only_last(m, f))),
        ],
        out_shape=[
            jax.ShapeDtypeStruct((SEQ, D_MODEL), F32),
            jax.ShapeDtypeStruct((N_SAMPLE, D_MODEL), F32),
            jax.ShapeDtypeStruct((CARRY, D_FF), F32),
            jax.ShapeDtypeStruct((N_SAMPLE, CONV_W - 1, D_FF), F32),
        ],
        scratch_shapes=[
            pltpu.VMEM((2, TM, TF), BF16),
            pltpu.VMEM((CARRY + TM, TF), F32),
            pltpu.VMEM((NF, CARRY, TF), F32),
            pltpu.VMEM((TM, D_MODEL), F32),
            pltpu.SemaphoreType.DMA(()),
        ],
        compiler_params=_params(2),
        name="convffn_ln2",
    )(x1b, x1, sc, wg, wu, wd, conv_w, conv_b, ln_g, ln_b)


def _rope_tables(pos):
    half = HEAD_DIM // 2
    inv = ROPE_THETA ** (-jnp.arange(half, dtype=F32) / half)
    ang = pos.astype(F32)[:, None] * inv[None, :]
    cos = jnp.cos(ang)
    sin = jnp.sin(ang)
    reps = LANES // HEAD_DIM
    return (jnp.concatenate([cos, cos] * reps, axis=1),
            jnp.concatenate([-sin, sin] * reps, axis=1))


def kernel(x_prompt, x_sample, cache_k, cache_v, state_pool, state_conv, w_in, attn_sinks, w_pool_mix,
           pool_scale, w_attn_branch, w_pool_branch, w_out, ln1_g, ln1_b, w_up, w_gate, conv_w, conv_b,
           w_down, ln2_g, ln2_b):
    x = x_prompt.reshape(SEQ, D_MODEL)
    xs = x_sample.reshape(N_SAMPLE, D_MODEL)
    pos = jnp.concatenate([jnp.arange(SEQ), jnp.full((N_SAMPLE,), PAST_LEN)])
    cos, sin = _rope_tables(pos)

    q, kv, u, gates = _in_proj(x, xs, w_in[0], cos, sin)

    to_dsw = lambda c: jnp.transpose(c[0], (0, 2, 3, 1)).reshape(N_SAMPLE, KV_DIM, WINDOW)
    steps = N_SAMPLE // SEQS_PER_STEP
    kvn_t = jnp.transpose(kv[SEQ:].reshape(steps, SEQS_PER_STEP, 2 * KV_DIM), (0, 2, 1))
    sinks = attn_sinks[0] * LOG2E
    attn_s, new_k, new_v = _attention_sample(
        sinks.reshape(N_HEADS, 1), q[SEQ:].reshape(N_SAMPLE, N_HEADS, HEAD_DIM), kvn_t,
        to_dsw(cache_k), to_dsw(cache_v))
    attn = _attention(sinks, q, kv, attn_s.reshape(N_SAMPLE, D_MODEL))

    state_pool_t = jnp.transpose(state_pool[0], (1, 0, 2))
    pooled = _pool(u, state_pool_t, w_pool_mix[0], pool_scale)

    merged = _merge(pooled, attn, gates, w_pool_branch[0], w_attn_branch[0])
    x1, x1b = _out_ln(merged, w_out[0], x, xs, ln1_g, ln1_b)

    y, y_s, g_last, new_conv_s = _ffn(x1b, x1, state_conv[0], w_gate[0], w_up[0], w_down[0],
                                      conv_w[0], conv_b, ln2_g, ln2_b)

    from_dsw = lambda c: jnp.transpose(
        c.reshape(N_SAMPLE, N_KV_HEADS, HEAD_DIM, WINDOW), (0, 3, 1, 2))[None]
    return (
        y.reshape(1, SEQ, D_MODEL),
        y_s.reshape(N_SAMPLE, 1, D_MODEL),
        kv[SEQ - WINDOW:SEQ, :KV_DIM].reshape(1, 1, WINDOW, N_KV_HEADS, HEAD_DIM),
        kv[SEQ - WINDOW:SEQ, KV_DIM:].reshape(1, 1, WINDOW, N_KV_HEADS, HEAD_DIM),
        u[SEQ - POOL_HIST:SEQ].reshape(1, 1, POOL_HIST, POOL_WIDTH),
        g_last[CARRY - (CONV_W - 1):].reshape(1, 1, CONV_W - 1, D_FF),
        from_dsw(new_k),
        from_dsw(new_v),
        jnp.transpose(jnp.concatenate([state_pool_t[1:], u[SEQ:][None]], axis=0), (1, 0, 2))[None],
        new_conv_s[None],
    )
```

```python
import jax
import jax.numpy as jnp
from jax import lax
from jax.experimental import pallas as pl
from jax.experimental.pallas import tpu as pltpu

F32 = jnp.float32
BF16 = jnp.bfloat16

D_MODEL = 2048
SEQ = 8192
N_SAMPLE = 128
PAST_LEN = 8192
HEAD_DIM = 64
N_HEADS = 32
N_KV_HEADS = 4
GROUP = N_HEADS // N_KV_HEADS
KV_DIM = N_KV_HEADS * HEAD_DIM
WINDOW = 128
ROPE_THETA = 10000.0
POOL_WINDOWS = (2, 4, 8, 16)
POOL_WIDTH = 1024
POOL_GROUP_DIM = 256
POOL_HIST = 15
D_FF = 5632
CONV_W = 3
IN_DIM = 7680
LN_EPS = 1e-5
NEG_INF = -1e30
ALPHA = 2.0 ** 0.25
LOG2E = 1.4426950408889634

LANES = 128
VMEM_LIMIT = 60000 * 1024

ROWS = SEQ + N_SAMPLE
TM = 1040
NT = ROWS // TM
SPLIT = SEQ - (NT - 1) * TM
HALF = 528
TN = 512
assert NT * TM == ROWS and SPLIT + N_SAMPLE == TM and SPLIT % 16 == 0 and HALF % 16 == 0


def _params(ndim, flags=None):
    return pltpu.CompilerParams(dimension_semantics=("arbitrary",) * ndim,
                                vmem_limit_bytes=VMEM_LIMIT, flags=flags)


def _dot(a, b):
    return jnp.dot(a, b, preferred_element_type=F32)


def _layer_norm(z, g, b):
    mu = jnp.mean(z, axis=-1, keepdims=True)
    d = z - mu
    var = jnp.mean(d * d, axis=-1, keepdims=True)
    return d * lax.rsqrt(var + LN_EPS) * g + b


def _rope(x, cos, sin_signed):
    lane = lax.broadcasted_iota(jnp.int32, (1, LANES), 1)
    low_half = (lane % HEAD_DIM) < (HEAD_DIM // 2)
    outs = []
    for c in range(x.shape[1] // LANES):
        xc = x[:, c * LANES:(c + 1) * LANES]
        up = pltpu.roll(xc, LANES - HEAD_DIM // 2, 1)
        down = pltpu.roll(xc, HEAD_DIM // 2, 1)
        outs.append(xc * cos + jnp.where(low_half, up, down) * sin_signed)
    return jnp.concatenate(outs, axis=1)


_NQ = D_MODEL // TN
_N_KV = _NQ
_N_U0 = _N_KV + 1
_N_G0 = _N_U0 + POOL_WIDTH // TN


def _in_proj_kernel(x_ref, xs_ref, w_ref, cos_ref, sin_ref, q_ref, kv_ref, u_ref, g_ref, xb):
    m = pl.program_id(0)
    n = pl.program_id(1)

    @pl.when(n == 0)
    def _():
        @pl.when(m < NT - 1)
        def _():
            xb[...] = x_ref[...].astype(BF16)

        @pl.when(m == NT - 1)
        def _():
            xb[0:SPLIT, :] = x_ref[0:SPLIT, :].astype(BF16)
            xb[SPLIT:TM, :] = xs_ref[...].astype(BF16)

    wb = w_ref[...].astype(BF16)
    halves = ((0, HALF), (HALF, TM))

    def project(epilogue):
        accs = [_dot(xb[lo:hi, :], wb) for lo, hi in halves]
        for (lo, hi), acc in zip(halves, accs):
            epilogue(acc, lo, hi)

    def q_epilogue(acc, lo, hi):
        roped = _rope(acc, cos_ref[lo:hi, :], sin_ref[lo:hi, :])
        q_ref[lo:hi, :] = (roped * (HEAD_DIM ** -0.5 * LOG2E)).astype(BF16)

    def kv_epilogue(acc, lo, hi):
        kv_ref[lo:hi, :KV_DIM] = _rope(acc[:, :KV_DIM], cos_ref[lo:hi, :], sin_ref[lo:hi, :])
        kv_ref[lo:hi, KV_DIM:] = acc[:, KV_DIM:]

    def u_epilogue(acc, lo, hi):
        u_ref[lo:hi, :] = acc

    def gate_epilogue(acc, lo, hi):
        g_ref[lo:hi, :] = jax.nn.sigmoid(acc).astype(BF16)

    pl.when(n < _NQ)(lambda: project(q_epilogue))
    pl.when(n == _N_KV)(lambda: project(kv_epilogue))
    pl.when((n >= _N_U0) & (n < _N_G0))(lambda: project(u_epilogue))
    pl.when(n >= _N_G0)(lambda: project(gate_epilogue))


def _in_proj(x, xs, w_in, cos, sin):
    nn = IN_DIM // TN
    ng = 2 * D_MODEL // TN

    def clamp(n, lo, hi):
        return jnp.clip(n - lo, 0, hi - lo)

    return pl.pallas_call(
        _in_proj_kernel,
        grid=(NT, nn),
        in_specs=[
            pl.BlockSpec((TM, D_MODEL), lambda m, n: (m, 0)),
            pl.BlockSpec((N_SAMPLE, D_MODEL), lambda m, n: (0, 0)),
            pl.BlockSpec((D_MODEL, TN), lambda m, n: (0, n)),
            pl.BlockSpec((TM, LANES), lambda m, n: (m, 0)),
            pl.BlockSpec((TM, LANES), lambda m, n: (m, 0)),
        ],
        out_specs=[
            pl.BlockSpec((TM, TN), lambda m, n: (m, clamp(n, 0, _NQ - 1))),
            pl.BlockSpec((TM, TN), lambda m, n: (m, 0)),
            pl.BlockSpec((TM, TN), lambda m, n: (m, clamp(n, _N_U0, _N_G0 - 1))),
            pl.BlockSpec((TM, TN), lambda m, n: (m, clamp(n, _N_G0, _N_G0 + ng - 1))),
        ],
        out_shape=[
            jax.ShapeDtypeStruct((ROWS, D_MODEL), BF16),
            jax.ShapeDtypeStruct((ROWS, 2 * KV_DIM), F32),
            jax.ShapeDtypeStruct((ROWS, POOL_WIDTH), F32),
            jax.ShapeDtypeStruct((ROWS, 2 * D_MODEL), BF16),
        ],
        scratch_shapes=[pltpu.VMEM((TM, D_MODEL), BF16)],
        compiler_params=_params(2),
        name="in_proj",
    )(x, xs, w_in, cos, sin)


PAIRS = GROUP // 2


def _attn_kernel(sink_ref, q_ref, kp_ref, kc_ref, vp_ref, vc_ref, os_ref, o_ref):
    i = pl.program_id(0)
    last = pl.num_programs(0) - 1

    @pl.when(i < last)
    def _():
        _attn_block(i, sink_ref, q_ref, kp_ref, kc_ref, vp_ref, vc_ref, o_ref)

    @pl.when(i == last)
    def _():
        o_ref[...] = os_ref[...]


def _attn_block(i, sink_ref, q_ref, kp_ref, kc_ref, vp_ref, vc_ref, o_ref):
    kk =jnp.concatenate([kp_ref[...], kc_ref[...]], axis=0)
    vv = jnp.concatenate([vp_ref[...], vc_ref[...]], axis=0)
    low = lax.broadcasted_iota(jnp.int32, (1, LANES), 1) < HEAD_DIM
    rows = PAIRS * WINDOW
    r = lax.broadcasted_iota(jnp.int32, (rows, 2 * WINDOW), 0) % WINDOW
    c = lax.broadcasted_iota(jnp.int32, (rows, 2 * WINDOW), 1)
    visible = (c > r) & (c <= r + WINDOW) & ((i > 0) | (c >= WINDOW))
    chunk = lax.broadcasted_iota(jnp.int32, (rows, 1), 0) // WINDOW

    def sink_column(first_head):
        col = jnp.full((rows, 1), sink_ref[first_head], F32)
        for t in range(1, PAIRS):
            col = jnp.where(chunk == t, sink_ref[first_head + 2 * t], col)
        return col

    for pair in range(N_KV_HEADS // 2):
        k_pair = kk[:, pair * LANES:(pair + 1) * LANES]
        v_pair = vv[:, pair * LANES:(pair + 1) * LANES]
        k_swap = pltpu.roll(k_pair, HEAD_DIM, 1)
        v_swap = pltpu.roll(v_pair, HEAD_DIM, 1)
        for second in range(2):
            g = 2 * pair + second
            k_in_low, k_in_high = (k_swap, k_pair) if second else (k_pair, k_swap)
            v_in_low, v_in_high = (v_swap, v_pair) if second else (v_pair, v_swap)
            x = jnp.concatenate(
                [q_ref[:, (g * PAIRS + t) * LANES:(g * PAIRS + t + 1) * LANES] for t in range(PAIRS)], axis=0)

            def half(k_pad, v_pad, sink):
                s = lax.dot_general(x, k_pad.astype(BF16), (((1,), (1,)), ((), ())),
                                    preferred_element_type=F32)
                s = jnp.where(visible, s, NEG_INF)
                mx = jnp.maximum(jnp.max(s, axis=-1, keepdims=True), sink)
                e = jnp.exp2(s - mx).astype(BF16)
                return _dot(e, v_pad.astype(BF16)), jnp.exp2(sink - mx)

            a, sink_a = half(jnp.where(low, k_in_low, 0.0), jnp.where(low, v_in_low, 1.0),
                             sink_column(g * GROUP))
            b, sink_b = half(jnp.where(low, 0.0, k_in_high), jnp.where(low, 1.0, v_in_high),
                             sink_column(g * GROUP + 1))
            den = pltpu.roll(jnp.where(low, b, a), HEAD_DIM, 1) + jnp.where(low, sink_a, sink_b)
            out = (jnp.where(low, a, b) / den).astype(BF16)
            for t in range(PAIRS):
                o_ref[:, (g * PAIRS + t) * LANES:(g * PAIRS + t + 1) * LANES] = out[t * WINDOW:(t + 1) * WINDOW]


def _attention(sinks, q, kv, attn_s):
    nb = SEQ // WINDOW
    cur = lambda i: jnp.minimum(i, nb - 1)
    prev = lambda i: jnp.clip(i - 1, 0, nb - 1)
    return pl.pallas_call(
        _attn_kernel,
        grid=(nb + 1,),
        in_specs=[
            pl.BlockSpec(memory_space=pltpu.SMEM),
            pl.BlockSpec((WINDOW, D_MODEL), lambda i: (cur(i), 0)),
            pl.BlockSpec((WINDOW, KV_DIM), lambda i: (prev(i), 0)),
            pl.BlockSpec((WINDOW, KV_DIM), lambda i: (cur(i), 0)),
            pl.BlockSpec((WINDOW, KV_DIM), lambda i: (prev(i), 1)),
            pl.BlockSpec((WINDOW, KV_DIM), lambda i: (cur(i), 1)),
            pl.BlockSpec((N_SAMPLE, D_MODEL), lambda i: (0, 0)),
        ],
        out_specs=pl.BlockSpec((WINDOW, D_MODEL), lambda i: (i, 0)),
        out_shape=jax.ShapeDtypeStruct((ROWS, D_MODEL), BF16),
        compiler_params=_params(1),
        name="attn_prompt",
    )(sinks, q, kv, kv, kv, kv, attn_s)


SEQS_PER_STEP = 8


def _attn_sample_kernel(sink_ref, q_ref, kvn_ref, ck_ref, cv_ref, o_ref, nk_ref, nv_ref):
    head = lax.broadcasted_iota(jnp.int32, (N_HEADS, KV_DIM), 0) // GROUP
    col_group = lax.broadcasted_iota(jnp.int32, (N_HEADS, KV_DIM), 1) // HEAD_DIM
    own = head == col_group
    newest = lax.broadcasted_iota(jnp.int32, (1, WINDOW), 1) == WINDOW - 1
    scores = []
    for b in range(SEQS_PER_STEP):
        nk_ref[b] = jnp.where(newest, kvn_ref[0, 0:KV_DIM, b:b + 1], pltpu.roll(ck_ref[b], WINDOW - 1, 1))
        nv_ref[b] = jnp.where(newest, kvn_ref[0, KV_DIM:2 * KV_DIM, b:b + 1],
                              pltpu.roll(cv_ref[b], WINDOW - 1, 1))
        qb = q_ref[b]
        qe = jnp.where(own, jnp.concatenate([qb] * N_KV_HEADS, axis=1), jnp.zeros((), BF16))
        scores.append(_dot(qe, nk_ref[b].astype(BF16)))
    s = jnp.concatenate(scores, axis=0)
    sink = jnp.concatenate([sink_ref[...]] * SEQS_PER_STEP, axis=0)
    mx = jnp.maximum(jnp.max(s, axis=-1, keepdims=True), sink)
    e = jnp.exp2(s - mx)
    inv = 1.0 / (jnp.sum(e, axis=-1, keepdims=True) + jnp.exp2(sink - mx))
    e = e.astype(BF16)
    for b in range(SEQS_PER_STEP):
        rows = slice(b * N_HEADS, (b + 1) * N_HEADS)
        of = lax.dot_general(e[rows], nv_ref[b].astype(BF16), (((1,), (1,)), ((), ())),
                             preferred_element_type=F32)
        of = jnp.where(own, of, 0.0)
        o = of[:, 0:HEAD_DIM]
        for g in range(1, N_KV_HEADS):
            o = o + of[:, g * HEAD_DIM:(g + 1) * HEAD_DIM]
        o_ref[b] = (o * inv[rows]).astype(BF16)


def _attention_sample(sink_col, q3, kvn_t, ck, cv):
    nb = N_SAMPLE // SEQS_PER_STEP
    cache_spec = pl.BlockSpec((SEQS_PER_STEP, KV_DIM, WINDOW), lambda i: (i, 0, 0))
    return pl.pallas_call(
        _attn_sample_kernel,
        grid=(nb,),
        in_specs=[
            pl.BlockSpec((N_HEADS, 1), lambda i: (0, 0)),
            pl.BlockSpec((SEQS_PER_STEP, N_HEADS, HEAD_DIM), lambda i: (i, 0, 0)),
            pl.BlockSpec((1, 2 * KV_DIM, SEQS_PER_STEP), lambda i: (i, 0, 0)),
            cache_spec, cache_spec,
        ],
        out_specs=[
            pl.BlockSpec((SEQS_PER_STEP, N_HEADS, HEAD_DIM), lambda i: (i, 0, 0)),
            cache_spec, cache_spec,
        ],
        out_shape=[
            jax.ShapeDtypeStruct((N_SAMPLE, N_HEADS, HEAD_DIM), BF16),
            jax.ShapeDtypeStruct((N_SAMPLE, KV_DIM, WINDOW), F32),
            jax.ShapeDtypeStruct((N_SAMPLE, KV_DIM, WINDOW), F32),
        ],
        compiler_params=_params(1),
        name="attn_sample",
    )(sink_col, q3, kvn_t, ck, cv)


POOL_PAD = 16


def _pool_kernel(u_ref, sp_ref, wmix_ref, scale_ref, p_ref, ext):
    m = pl.program_id(0)

    @pl.when(m == 0)
    def _():
        ext[0:POOL_PAD, :] = jnp.zeros((POOL_PAD, POOL_WIDTH), F32)

    ext[POOL_PAD:POOL_PAD + TM, :] = u_ref[...]
    pos = m * TM + lax.broadcasted_iota(jnp.int32, (TM, 1), 0)
    for g, w in enumerate(POOL_WINDOWS):
        cols = slice(g * POOL_GROUP_DIM, (g + 1) * POOL_GROUP_DIM)
        tot = ext[POOL_PAD:POOL_PAD + TM, cols]
        for j in range(1, w):
            tot = tot + ext[POOL_PAD - j:POOL_PAD - j + TM, cols]
        inv_cnt = 1.0 / jnp.minimum(w, pos + 1).astype(F32)
        d = tot * inv_cnt - u_ref[:, cols]
        y = _dot(d.astype(BF16), wmix_ref[g].astype(BF16))
        p_ref[:, cols] = (y * scale_ref[:, cols]).astype(BF16)
    ext[0:POOL_PAD, :] = ext[TM:TM + POOL_PAD, :]

    @pl.when(m == NT - 1)
    def _():
        for g, w in enumerate(POOL_WINDOWS):
            cols = slice(g * POOL_GROUP_DIM, (g + 1) * POOL_GROUP_DIM)
            us = u_ref[SPLIT:TM, cols]
            tot = us
            for j in range(1, w):
                tot = tot + sp_ref[POOL_HIST - j, :, cols]
            d = tot * (1.0 / w) - us
            y = _dot(d.astype(BF16), wmix_ref[g].astype(BF16))
            p_ref[SPLIT:TM, cols] = (y * scale_ref[:, cols]).astype(BF16)


def _pool(u, sp_t, wmix, scale):
    full = lambda shape: pl.BlockSpec(shape, lambda m: (0,) * len(shape))
    return pl.pallas_call(
        _pool_kernel,
        grid=(NT,),
        in_specs=[
            pl.BlockSpec((TM, POOL_WIDTH), lambda m: (m, 0)),
            full((POOL_HIST, N_SAMPLE, POOL_WIDTH)),
            full((len(POOL_WINDOWS), POOL_GROUP_DIM, POOL_GROUP_DIM)),
            full((1, POOL_WIDTH)),
        ],
        out_specs=pl.BlockSpec((TM, POOL_WIDTH), lambda m: (m, 0)),
        out_shape=jax.ShapeDtypeStruct((ROWS, POOL_WIDTH), BF16),
        scratch_shapes=[pltpu.VMEM((POOL_PAD + TM, POOL_WIDTH), F32)],
        compiler_params=_params(1),
        name="pool_mix",
    )(u, sp_t, wmix, scale)


_NC = D_MODEL // TN


def _merge_kernel(pool_ref, attn_ref, gp_ref, ga_ref, wp_ref, wa_ref, o_ref, wpb, wab):
    @pl.when(pl.program_id(1) == 0)
    def _():
        wpb[...] = wp_ref[...].astype(BF16)
        wab[...] = wa_ref[...].astype(BF16)

    halves = ((0, HALF), (HALF, TM))
    accs = [(_dot(pool_ref[lo:hi, :], wpb[...]), _dot(attn_ref[lo:hi, :], wab[...])) for lo, hi in halves]
    for (lo, hi), (a, b) in zip(halves, accs):
        o_ref[lo:hi, :] = (gp_ref[lo:hi, :].astype(F32) * a + ga_ref[lo:hi, :].astype(F32) * b).astype(BF16)


def _merge(pooled, attn, gates, wp, wa):
    return pl.pallas_call(
        _merge_kernel,
        grid=(_NC, NT),
        in_specs=[
            pl.BlockSpec((TM, POOL_WIDTH), lambda n, m: (m, 0)),
            pl.BlockSpec((TM, D_MODEL), lambda n, m: (m, 0)),
            pl.BlockSpec((TM, TN), lambda n, m: (m, n)),
            pl.BlockSpec((TM, TN), lambda n, m: (m, _NC + n)),
            pl.BlockSpec((POOL_WIDTH, TN), lambda n, m: (0, n)),
            pl.BlockSpec((D_MODEL, TN), lambda n, m: (0, n)),
        ],
        out_specs=pl.BlockSpec((TM, TN), lambda n, m: (m, n)),
        out_shape=jax.ShapeDtypeStruct((ROWS, D_MODEL), BF16),
        scratch_shapes=[pltpu.VMEM((POOL_WIDTH, TN), BF16), pltpu.VMEM((D_MODEL, TN), BF16)],
        compiler_params=_params(2),
        name="branch_merge",
    )(pooled, attn, gates, gates, wp, wa)


def _out_ln_kernel(mrg_ref, wo_ref, x_ref, xs_ref, lng_ref, lnb_ref, x1_ref, x1b_ref):
    m = pl.program_id(0)
    n = pl.program_id(1)
    y = _dot(mrg_ref[...], wo_ref[...].astype(BF16))

    for k in range(_NC):
        cols = slice(k * TN, (k + 1) * TN)

        @pl.when((n == k) & (m < NT - 1))
        def _():
            x1_ref[:, cols] = y + ALPHA * x_ref[...]

        @pl.when((n == k) & (m == NT - 1))
        def _():
            x1_ref[0:SPLIT, cols] = y[0:SPLIT] + ALPHA * x_ref[0:SPLIT, :]
            x1_ref[SPLIT:TM, cols] = y[SPLIT:TM] + ALPHA * xs_ref[...]

    @pl.when(n == _NC - 1)
    def _():
        x1 = _layer_norm(x1_ref[...], lng_ref[...], lnb_ref[...])
        x1_ref[...] = x1
        x1b_ref[...] = x1.astype(BF16)


def _out_ln(merged, wo, x, xs, ln_g, ln_b):
    full = lambda shape: pl.BlockSpec(shape, lambda m, n: (0, 0))
    row = pl.BlockSpec((TM, D_MODEL), lambda m, n: (m, 0))
    return pl.pallas_call(
        _out_ln_kernel,
        grid=(NT, _NC),
        in_specs=[
            row,
            pl.BlockSpec((D_MODEL, TN), lambda m, n: (0, n)),
            pl.BlockSpec((TM, TN), lambda m, n: (m, n)),
            pl.BlockSpec((N_SAMPLE, TN), lambda m, n: (0, n)),
            full((1, D_MODEL)), full((1, D_MODEL)),
        ],
        out_specs=[row, row],
        out_shape=[jax.ShapeDtypeStruct((ROWS, D_MODEL), F32), jax.ShapeDtypeStruct((ROWS, D_MODEL), BF16)],
        compiler_params=_params(2),
        name="out_proj_ln1",
    )(merged, wo, x, xs, ln_g, ln_b)


TF = 256
NF = D_FF // TF
CARRY = 8


def _ffn_kernel(x1b_ref, x1_hbm, sc_ref, wg_ref, wu_ref, wd_ref, cw_ref, cb_ref, lng_ref, lnb_ref,
                o_ref, os_ref, glast_ref, scn_ref,
                h0, h1, gext, carry, sem):
    m = pl.program_id(0)
    f = pl.program_id(1)
    last_m = m == NT - 1
    h_slots = (h0, h1)
    x1_copy = pltpu.make_async_copy(x1_hbm.at[pl.ds(m * TM, TM), :], o_ref, sem)

    def up_and_gate(slot, with_down):
        halves = ((0, HALF), (HALF, TM))
        wgb = wg_ref[...].astype(BF16)
        wub = wu_ref[...].astype(BF16)
        gext[0:CARRY, :] = carry[f]
        for lo, hi in halves:
            gext[CARRY + lo:CARRY + hi, :] = _dot(x1b_ref[lo:hi, :], wgb)
        ups = [_dot(x1b_ref[lo:hi, :], wub) for lo, hi in halves]
        if with_down:
            o_ref[...] += _dot(h_slots[1 - slot][...], wd_ref[...].astype(BF16))
        w0 = cw_ref[0:1, :]
        w1 = cw_ref[1:2, :]
        w2 = cw_ref[2:3, :]
        cb = cb_ref[...]

        def conv(lo, hi, older=None, newer=None):
            older = gext[CARRY - 2 + lo:CARRY - 2 + hi, :] if older is None else older
            newer = gext[CARRY - 1 + lo:CARRY - 1 + hi, :] if newer is None else newer
            return cb + w0 * older + w1 * newer + w2 * gext[CARRY + lo:CARRY + hi, :]

        for (lo, hi), up in zip(halves, ups):
            if hi <= SPLIT:
                gc = conv(lo, hi)
            else:
                older_s = jnp.where(last_m, sc_ref[:, 0, :], gext[CARRY - 2 + SPLIT:CARRY - 2 + TM, :])
                newer_s = jnp.where(last_m, sc_ref[:, 1, :], gext[CARRY - 1 + SPLIT:CARRY - 1 + TM, :])
                gc = jnp.concatenate([conv(lo, SPLIT), conv(SPLIT, TM, older_s, newer_s)], axis=0)
            h_slots[slot][lo:hi, :] = (jax.nn.gelu(gc, approximate=True) * up).astype(BF16)
        carry[f] = gext[TM:CARRY + TM, :]

        @pl.when(last_m)
        def _():
            glast_ref[...] = gext[SPLIT:CARRY + SPLIT, :]
            scn_ref[:, 0, :] = sc_ref[:, 1, :]
            scn_ref[:, 1, :] = gext[CARRY + SPLIT:CARRY + TM, :]

    @pl.when(f == 0)
    def _():
        @pl.when(m == 0)
        def _():
            carry[...] = jnp.zeros_like(carry)

        x1_copy.start()
        up_and_gate(0, False)
        x1_copy.wait()
        o_ref[...] = ALPHA * o_ref[...]

    for parity in range(2):
        @pl.when((f > 0) & (f < NF) & (f % 2 == parity))
        def _():
            up_and_gate(parity, True)

    @pl.when(f == NF)
    def _():
        acc = o_ref[...] + _dot(h_slots[(NF - 1) % 2][...], wd_ref[...].astype(BF16))
        y = _layer_norm(acc, lng_ref[...], lnb_ref[...])
        o_ref[...] = y

        @pl.when(last_m)
        def _():
            os_ref[...] = y[SPLIT:TM, :]


def _ffn(x1b, x1, sc, wg, wu, wd, conv_w, conv_b, ln_g, ln_b):
    full = lambda shape: pl.BlockSpec(shape, lambda m, f: (0, 0))
    g_tile = lambda f: jnp.minimum(f, NF - 1)
    d_tile = lambda f: jnp.maximum(f - 1, 0)
    only_last = lambda m, f: jnp.where(m == NT - 1, g_tile(f), 0)
    row = pl.BlockSpec((TM, D_MODEL), lambda m, f: (m, 0))
    return pl.pallas_call(
        _ffn_kernel,
        grid=(NT, NF + 1),
        in_specs=[
            row,
            pl.BlockSpec(memory_space=pl.ANY),
            pl.BlockSpec((N_SAMPLE, CONV_W - 1, TF), lambda m, f: (0, 0, only_last(m, f))),
            pl.BlockSpec((D_MODEL, TF), lambda m, f: (0, g_tile(f))),
            pl.BlockSpec((D_MODEL, TF), lambda m, f: (0, g_tile(f))),
            pl.BlockSpec((TF, D_MODEL), lambda m, f: (d_tile(f), 0)),
            pl.BlockSpec((CONV_W, TF), lambda m, f: (0, g_tile(f))),
            pl.BlockSpec((1, TF), lambda m, f: (0, g_tile(f))),
            full((1, D_MODEL)), full((1, D_MODEL)),
        ],
        out_specs=[
            row,
            full((N_SAMPLE, D_MODEL)),
            pl.BlockSpec((CARRY, TF), lambda m, f: (0, only_last(m, f))),
            pl.BlockSpec((N_SAMPLE, CONV_W - 1, TF), lambda m, f: (0, 0, only_last(m, f))),
        ],
        out_shape=[
            jax.ShapeDtypeStruct((SEQ, D_MODEL), F32),
            jax.ShapeDtypeStruct((N_SAMPLE, D_MODEL), F32),
            jax.ShapeDtypeStruct((CARRY, D_FF), F32),
            jax.ShapeDtypeStruct((N_SAMPLE, CONV_W - 1, D_FF), F32),
        ],
        scratch_shapes=[
            pltpu.VMEM((TM, TF), BF16), pltpu.VMEM((TM, TF), BF16),
            pltpu.VMEM((CARRY + TM, TF), F32),
            pltpu.VMEM((NF, CARRY, TF), F32),
            pltpu.SemaphoreType.DMA(()),
        ],
        compiler_params=_params(2),
        name="convffn_ln2",
    )(x1b, x1, sc, wg, wu, wd, conv_w, conv_b, ln_g, ln_b)


def _rope_tables(pos):
    half = HEAD_DIM // 2
    inv = ROPE_THETA ** (-jnp.arange(half, dtype=F32) / half)
    ang = pos.astype(F32)[:, None] * inv[None, :]
    cos = jnp.cos(ang)
    sin = jnp.sin(ang)
    reps = LANES // HEAD_DIM
    return (jnp.concatenate([cos, cos] * reps, axis=1),
            jnp.concatenate([-sin, sin] * reps, axis=1))


def kernel(x_prompt, x_sample, cache_k, cache_v, state_pool, state_conv, w_in, attn_sinks, w_pool_mix,
           pool_scale, w_attn_branch, w_pool_branch, w_out, ln1_g, ln1_b, w_up, w_gate, conv_w, conv_b,
           w_down, ln2_g, ln2_b):
    x = x_prompt.reshape(SEQ, D_MODEL)
    xs = x_sample.reshape(N_SAMPLE, D_MODEL)
    pos = jnp.concatenate([jnp.arange(SEQ), jnp.full((N_SAMPLE,), PAST_LEN)])
    cos, sin = _rope_tables(pos)

    q, kv, u, gates = _in_proj(x, xs, w_in[0], cos, sin)

    to_dsw = lambda c: jnp.transpose(c[0], (0, 2, 3, 1)).reshape(N_SAMPLE, KV_DIM, WINDOW)
    steps = N_SAMPLE // SEQS_PER_STEP
    kvn_t = jnp.transpose(kv[SEQ:].reshape(steps, SEQS_PER_STEP, 2 * KV_DIM), (0, 2, 1))
    sinks = attn_sinks[0] * LOG2E
    attn_s, new_k, new_v = _attention_sample(
        sinks.reshape(N_HEADS, 1), q[SEQ:].reshape(N_SAMPLE, N_HEADS, HEAD_DIM), kvn_t,
        to_dsw(cache_k), to_dsw(cache_v))
    attn = _attention(sinks, q, kv, attn_s.reshape(N_SAMPLE, D_MODEL))

    state_pool_t = jnp.transpose(state_pool[0], (1, 0, 2))
    pooled = _pool(u, state_pool_t, w_pool_mix[0], pool_scale)

    merged = _merge(pooled, attn, gates, w_pool_branch[0], w_attn_branch[0])
    x1, x1b = _out_ln(merged, w_out[0], x, xs, ln1_g, ln1_b)

    y, y_s, g_last, new_conv_s = _ffn(x1b, x1, state_conv[0], w_gate[0], w_up[0], w_down[0],
                                      conv_w[0], conv_b, ln2_g, ln2_b)

    from_dsw = lambda c: jnp.transpose(
        c.reshape(N_SAMPLE, N_KV_HEADS, HEAD_DIM, WINDOW), (0, 3, 1, 2))[None]
    return (
        y.reshape(1, SEQ, D_MODEL),
        y_s.reshape(N_SAMPLE, 1, D_MODEL),
        kv[SEQ - WINDOW:SEQ, :KV_DIM].reshape(1, 1, WINDOW, N_KV_HEADS, HEAD_DIM),
        kv[SEQ - WINDOW:SEQ, KV_DIM:].reshape(1, 1, WINDOW, N_KV_HEADS, HEAD_DIM),
        u[SEQ - POOL_HIST:SEQ].reshape(1, 1, POOL_HIST, POOL_WIDTH),
        g_last[CARRY - (CONV_W - 1):].reshape(1, 1, CONV_W - 1, D_FF),
        from_dsw(new_k),
        from_dsw(new_v),
        jnp.transpose(jnp.concatenate([state_pool_t[1:], u[SEQ:][None]], axis=0), (1, 0, 2))[None],
        new_conv_s[None],
    )
```

```python
import jax
import jax.numpy as jnp
from jax import lax
from jax.experimental import pallas as pl
from jax.experimental.pallas import tpu as pltpu

F32 = jnp.float32
BF16 = jnp.bfloat16

D_MODEL = 2048
SEQ = 8192
N_SAMPLE = 128
PAST_LEN = 8192
HEAD_DIM = 64
N_HEADS = 32
N_KV_HEADS = 4
GROUP = N_HEADS // N_KV_HEADS
KV_DIM = N_KV_HEADS * HEAD_DIM
WINDOW = 128
ROPE_THETA = 10000.0
POOL_WINDOWS = (2, 4, 8, 16)
POOL_WIDTH = 1024
POOL_GROUP_DIM = 256
POOL_HIST = 15
D_FF = 5632
CONV_W = 3
IN_DIM = 7680
LN_EPS = 1e-5
NEG_INF = -1e30
ALPHA = 2.0 ** 0.25
LOG2E = 1.4426950408889634

LANES = 128
VMEM_LIMIT = 60000 * 1024

ROWS = SEQ + N_SAMPLE
TM = 1040
NT = ROWS // TM
SPLIT = SEQ - (NT - 1) * TM
HALF = 528
TN = 512
assert NT * TM == ROWS and SPLIT + N_SAMPLE == TM and SPLIT % 16 == 0 and HALF % 16 == 0


def _params(ndim, flags=None):
    return pltpu.CompilerParams(dimension_semantics=("arbitrary",) * ndim,
                                vmem_limit_bytes=VMEM_LIMIT, flags=flags)


def _dot(a, b):
    return jnp.dot(a, b, preferred_element_type=F32)


def _layer_norm(z, g, b):
    mu = jnp.mean(z, axis=-1, keepdims=True)
    d = z - mu
    var = jnp.mean(d * d, axis=-1, keepdims=True)
    return d * lax.rsqrt(var + LN_EPS) * g + b


def _rope(x, cos, sin_signed):
    lane = lax.broadcasted_iota(jnp.int32, (1, LANES), 1)
    low_half = (lane % HEAD_DIM) < (HEAD_DIM // 2)
    outs = []
    for c in range(x.shape[1] // LANES):
        xc = x[:, c * LANES:(c + 1) * LANES]
        up = pltpu.roll(xc, LANES - HEAD_DIM // 2, 1)
        down = pltpu.roll(xc, HEAD_DIM // 2, 1)
        outs.append(xc * cos + jnp.where(low_half, up, down) * sin_signed)
    return jnp.concatenate(outs, axis=1)


_NQ = D_MODEL // TN
_N_KV = _NQ
_N_U0 = _N_KV + 1
_N_G0 = _N_U0 + POOL_WIDTH // TN


def _in_proj_kernel(x_ref, xs_ref, w_ref, cos_ref, sin_ref, q_ref, kv_ref, u_ref, g_ref, xb):
    m = pl.program_id(0)
    n = pl.program_id(1)

    @pl.when(n == 0)
    def _():
        @pl.when(m < NT - 1)
        def _():
            xb[...] = x_ref[...].astype(BF16)

        @pl.when(m == NT - 1)
        def _():
            xb[0:SPLIT, :] = x_ref[0:SPLIT, :].astype(BF16)
            xb[SPLIT:TM, :] = xs_ref[...].astype(BF16)

    wb = w_ref[...].astype(BF16)
    halves = ((0, HALF), (HALF, TM))

    def project(epilogue):
        accs = [_dot(xb[lo:hi, :], wb) for lo, hi in halves]
        for (lo, hi), acc in zip(halves, accs):
            epilogue(acc, lo, hi)

    def q_epilogue(acc, lo, hi):
        roped = _rope(acc, cos_ref[lo:hi, :], sin_ref[lo:hi, :])
        q_ref[lo:hi, :] = (roped * (HEAD_DIM ** -0.5 * LOG2E)).astype(BF16)

    def kv_epilogue(acc, lo, hi):
        kv_ref[lo:hi, :KV_DIM] = _rope(acc[:, :KV_DIM], cos_ref[lo:hi, :], sin_ref[lo:hi, :])
        kv_ref[lo:hi, KV_DIM:] = acc[:, KV_DIM:]

    def u_epilogue(acc, lo, hi):
        u_ref[lo:hi, :] = acc

    def gate_epilogue(acc, lo, hi):
        g_ref[lo:hi, :] = jax.nn.sigmoid(acc).astype(BF16)

    pl.when(n < _NQ)(lambda: project(q_epilogue))
    pl.when(n == _N_KV)(lambda: project(kv_epilogue))
    pl.when((n >= _N_U0) & (n < _N_G0))(lambda: project(u_epilogue))
    pl.when(n >= _N_G0)(lambda: project(gate_epilogue))


def _in_proj(x, xs, w_in, cos, sin):
    nn = IN_DIM // TN
    ng = 2 * D_MODEL // TN

    def clamp(n, lo, hi):
        return jnp.clip(n - lo, 0, hi - lo)

    return pl.pallas_call(
        _in_proj_kernel,
        grid=(NT, nn),
        in_specs=[
            pl.BlockSpec((TM, D_MODEL), lambda m, n: (m, 0)),
            pl.BlockSpec((N_SAMPLE, D_MODEL), lambda m, n: (0, 0)),
            pl.BlockSpec((D_MODEL, TN), lambda m, n: (0, n)),
            pl.BlockSpec((TM, LANES), lambda m, n: (m, 0)),
            pl.BlockSpec((TM, LANES), lambda m, n: (m, 0)),
        ],
        out_specs=[
            pl.BlockSpec((TM, TN), lambda m, n: (m, clamp(n, 0, _NQ - 1))),
            pl.BlockSpec((TM, TN), lambda m, n: (m, 0)),
            pl.BlockSpec((TM, TN), lambda m, n: (m, clamp(n, _N_U0, _N_G0 - 1))),
            pl.BlockSpec((TM, TN), lambda m, n: (m, clamp(n, _N_G0, _N_G0 + ng - 1))),
        ],
        out_shape=[
            jax.ShapeDtypeStruct((ROWS, D_MODEL), BF16),
            jax.ShapeDtypeStruct((ROWS, 2 * KV_DIM), F32),
            jax.ShapeDtypeStruct((ROWS, POOL_WIDTH), F32),
            jax.ShapeDtypeStruct((ROWS, 2 * D_MODEL), BF16),
        ],
        scratch_shapes=[pltpu.VMEM((TM, D_MODEL), BF16)],
        compiler_params=_params(2),
        name="in_proj",
    )(x, xs, w_in, cos, sin)


PAIRS = GROUP // 2


def _attn_kernel(sink_ref, q_ref, kp_ref, kc_ref, vp_ref, vc_ref, os_ref, *refs):
    n_w = (len(refs) - 1) // 2
    w_refs, o_ref, wb_refs = refs[:n_w], refs[n_w], refs[n_w + 1:]
    i = pl.program_id(0)
    last = pl.num_programs(0) - 1

    @pl.when(i < last)
    def _():
        for w_ref, wb_ref in zip(w_refs, wb_refs):
            wb_ref[...] = w_ref[...].astype(BF16)
        _attn_block(i, sink_ref, q_ref, kp_ref, kc_ref, vp_ref, vc_ref, o_ref)

    @pl.when(i == last)
    def _():
        o_ref[...] = os_ref[...]


def _attn_block(i, sink_ref, q_ref, kp_ref, kc_ref, vp_ref, vc_ref, o_ref):
    kk =jnp.concatenate([kp_ref[...], kc_ref[...]], axis=0)
    vv = jnp.concatenate([vp_ref[...], vc_ref[...]], axis=0)
    low = lax.broadcasted_iota(jnp.int32, (1, LANES), 1) < HEAD_DIM
    rows = PAIRS * WINDOW
    r = lax.broadcasted_iota(jnp.int32, (rows, 2 * WINDOW), 0) % WINDOW
    c = lax.broadcasted_iota(jnp.int32, (rows, 2 * WINDOW), 1)
    visible = (c > r) & (c <= r + WINDOW) & ((i > 0) | (c >= WINDOW))
    chunk = lax.broadcasted_iota(jnp.int32, (rows, 1), 0) // WINDOW

    def sink_column(first_head):
        col = jnp.full((rows, 1), sink_ref[first_head], F32)
        for t in range(1, PAIRS):
            col = jnp.where(chunk == t, sink_ref[first_head + 2 * t], col)
        return col

    for pair in range(N_KV_HEADS // 2):
        k_pair = kk[:, pair * LANES:(pair + 1) * LANES]
        v_pair = vv[:, pair * LANES:(pair + 1) * LANES]
        k_swap = pltpu.roll(k_pair, HEAD_DIM, 1)
        v_swap = pltpu.roll(v_pair, HEAD_DIM, 1)
        for second in range(2):
            g = 2 * pair + second
            k_in_low, k_in_high = (k_swap, k_pair) if second else (k_pair, k_swap)
            v_in_low, v_in_high = (v_swap, v_pair) if second else (v_pair, v_swap)
            x = jnp.concatenate(
                [q_ref[:, (g * PAIRS + t) * LANES:(g * PAIRS + t + 1) * LANES] for t in range(PAIRS)], axis=0)

            def half(k_pad, v_pad, sink):
                s = lax.dot_general(x, k_pad.astype(BF16), (((1,), (1,)), ((), ())),
                                    preferred_element_type=F32)
                s = jnp.where(visible, s, NEG_INF)
                mx = jnp.maximum(jnp.max(s, axis=-1, keepdims=True), sink)
                e = jnp.exp2(s - mx).astype(BF16)
                return _dot(e, v_pad.astype(BF16)), jnp.exp2(sink - mx)

            a, sink_a = half(jnp.where(low, k_in_low, 0.0), jnp.where(low, v_in_low, 1.0),
                             sink_column(g * GROUP))
            b, sink_b = half(jnp.where(low, 0.0, k_in_high), jnp.where(low, 1.0, v_in_high),
                             sink_column(g * GROUP + 1))
            den = pltpu.roll(jnp.where(low, b, a), HEAD_DIM, 1) + jnp.where(low, sink_a, sink_b)
            out = (jnp.where(low, a, b) / den).astype(BF16)
            for t in range(PAIRS):
                o_ref[:, (g * PAIRS + t) * LANES:(g * PAIRS + t + 1) * LANES] = out[t * WINDOW:(t + 1) * WINDOW]


def _cast_chunk_rows(rows, steps):
    assert rows % 16 == 0
    units = rows // 16
    n = max(d for d in range(1, steps + 1) if units % d == 0)
    return rows // n


def _attention(sinks, q, kv, attn_s, weights):
    nb = SEQ // WINDOW
    cur = lambda i: jnp.minimum(i, nb - 1)
    prev = lambda i: jnp.clip(i - 1, 0, nb - 1)
    w_specs = []
    for w in weights:
        chunk = _cast_chunk_rows(w.shape[0], nb)
        w_specs.append(pl.BlockSpec((chunk, w.shape[1]),
                                    lambda i, n=w.shape[0] // chunk: (jnp.minimum(i, n - 1), 0)))
    outs = pl.pallas_call(
        _attn_kernel,
        grid=(nb + 1,),
        in_specs=[
            pl.BlockSpec(memory_space=pltpu.SMEM),
            pl.BlockSpec((WINDOW, D_MODEL), lambda i: (cur(i), 0)),
            pl.BlockSpec((WINDOW, KV_DIM), lambda i: (prev(i), 0)),
            pl.BlockSpec((WINDOW, KV_DIM), lambda i: (cur(i), 0)),
            pl.BlockSpec((WINDOW, KV_DIM), lambda i: (prev(i), 1)),
            pl.BlockSpec((WINDOW, KV_DIM), lambda i: (cur(i), 1)),
            pl.BlockSpec((N_SAMPLE, D_MODEL), lambda i: (0, 0)),
        ] + w_specs,
        out_specs=[pl.BlockSpec((WINDOW, D_MODEL), lambda i: (i, 0))] + w_specs,
        out_shape=[jax.ShapeDtypeStruct((ROWS, D_MODEL), BF16)]
        + [jax.ShapeDtypeStruct(w.shape, BF16) for w in weights],
        compiler_params=_params(1),
        name="attn_prompt",
    )(sinks, q, kv, kv, kv, kv, attn_s, *weights)
    return outs[0], outs[1:]


SEQS_PER_STEP = 8


def _attn_sample_kernel(sink_ref, q_ref, kvn_ref, ck_ref, cv_ref, o_ref, nk_ref, nv_ref):
    head = lax.broadcasted_iota(jnp.int32, (N_HEADS, KV_DIM), 0) // GROUP
    col_group = lax.broadcasted_iota(jnp.int32, (N_HEADS, KV_DIM), 1) // HEAD_DIM
    own = head == col_group
    newest = lax.broadcasted_iota(jnp.int32, (1, WINDOW), 1) == WINDOW - 1
    scores = []
    for b in range(SEQS_PER_STEP):
        nk_ref[b] = jnp.where(newest, kvn_ref[0, 0:KV_DIM, b:b + 1], pltpu.roll(ck_ref[b], WINDOW - 1, 1))
        nv_ref[b] = jnp.where(newest, kvn_ref[0, KV_DIM:2 * KV_DIM, b:b + 1],
                              pltpu.roll(cv_ref[b], WINDOW - 1, 1))
        qb = q_ref[b]
        qe = jnp.where(own, jnp.concatenate([qb] * N_KV_HEADS, axis=1), jnp.zeros((), BF16))
        scores.append(_dot(qe, nk_ref[b].astype(BF16)))
    s = jnp.concatenate(scores, axis=0)
    sink = jnp.concatenate([sink_ref[...]] * SEQS_PER_STEP, axis=0)
    mx = jnp.maximum(jnp.max(s, axis=-1, keepdims=True), sink)
    e = jnp.exp2(s - mx)
    inv = 1.0 / (jnp.sum(e, axis=-1, keepdims=True) + jnp.exp2(sink - mx))
    e = e.astype(BF16)
    for b in range(SEQS_PER_STEP):
        rows = slice(b * N_HEADS, (b + 1) * N_HEADS)
        of = lax.dot_general(e[rows], nv_ref[b].astype(BF16), (((1,), (1,)), ((), ())),
                             preferred_element_type=F32)
        of = jnp.where(own, of, 0.0)
        o = of[:, 0:HEAD_DIM]
        for g in range(1, N_KV_HEADS):
            o = o + of[:, g * HEAD_DIM:(g + 1) * HEAD_DIM]
        o_ref[b] = (o * inv[rows]).astype(BF16)


def _attention_sample(sink_col, q3, kvn_t, ck, cv):
    nb = N_SAMPLE // SEQS_PER_STEP
    cache_spec = pl.BlockSpec((SEQS_PER_STEP, KV_DIM, WINDOW), lambda i: (i, 0, 0))
    return pl.pallas_call(
        _attn_sample_kernel,
        grid=(nb,),
        in_specs=[
            pl.BlockSpec((N_HEADS, 1), lambda i: (0, 0)),
            pl.BlockSpec((SEQS_PER_STEP, N_HEADS, HEAD_DIM), lambda i: (i, 0, 0)),
            pl.BlockSpec((1, 2 * KV_DIM, SEQS_PER_STEP), lambda i: (i, 0, 0)),
            cache_spec, cache_spec,
        ],
        out_specs=[
            pl.BlockSpec((SEQS_PER_STEP, N_HEADS, HEAD_DIM), lambda i: (i, 0, 0)),
            cache_spec, cache_spec,
        ],
        out_shape=[
            jax.ShapeDtypeStruct((N_SAMPLE, N_HEADS, HEAD_DIM), BF16),
            jax.ShapeDtypeStruct((N_SAMPLE, KV_DIM, WINDOW), F32),
            jax.ShapeDtypeStruct((N_SAMPLE, KV_DIM, WINDOW), F32),
        ],
        compiler_params=_params(1),
        name="attn_sample",
    )(sink_col, q3, kvn_t, ck, cv)


POOL_PAD = 16


def _pool_kernel(u_ref, sp_ref, wmix_ref, scale_ref, p_ref, ext):
    m = pl.program_id(0)

    @pl.when(m == 0)
    def _():
        ext[0:POOL_PAD, :] = jnp.zeros((POOL_PAD, POOL_WIDTH), F32)

    ext[POOL_PAD:POOL_PAD + TM, :] = u_ref[...]
    pos = m * TM + lax.broadcasted_iota(jnp.int32, (TM, 1), 0)
    for g, w in enumerate(POOL_WINDOWS):
        cols = slice(g * POOL_GROUP_DIM, (g + 1) * POOL_GROUP_DIM)
        tot = ext[POOL_PAD:POOL_PAD + TM, cols]
        for j in range(1, w):
            tot = tot + ext[POOL_PAD - j:POOL_PAD - j + TM, cols]
        inv_cnt = 1.0 / jnp.minimum(w, pos + 1).astype(F32)
        d = tot * inv_cnt - u_ref[:, cols]
        y = _dot(d.astype(BF16), wmix_ref[g].astype(BF16))
        p_ref[:, cols] = (y * scale_ref[:, cols]).astype(BF16)
    ext[0:POOL_PAD, :] = ext[TM:TM + POOL_PAD, :]

    @pl.when(m == NT - 1)
    def _():
        for g, w in enumerate(POOL_WINDOWS):
            cols = slice(g * POOL_GROUP_DIM, (g + 1) * POOL_GROUP_DIM)
            us = u_ref[SPLIT:TM, cols]
            tot = us
            for j in range(1, w):
                tot = tot + sp_ref[POOL_HIST - j, :, cols]
            d = tot * (1.0 / w) - us
            y = _dot(d.astype(BF16), wmix_ref[g].astype(BF16))
            p_ref[SPLIT:TM, cols] = (y * scale_ref[:, cols]).astype(BF16)


def _pool(u, sp_t, wmix, scale):
    full = lambda shape: pl.BlockSpec(shape, lambda m: (0,) * len(shape))
    return pl.pallas_call(
        _pool_kernel,
        grid=(NT,),
        in_specs=[
            pl.BlockSpec((TM, POOL_WIDTH), lambda m: (m, 0)),
            full((POOL_HIST, N_SAMPLE, POOL_WIDTH)),
            full((len(POOL_WINDOWS), POOL_GROUP_DIM, POOL_GROUP_DIM)),
            full((1, POOL_WIDTH)),
        ],
        out_specs=pl.BlockSpec((TM, POOL_WIDTH), lambda m: (m, 0)),
        out_shape=jax.ShapeDtypeStruct((ROWS, POOL_WIDTH), BF16),
        scratch_shapes=[pltpu.VMEM((POOL_PAD + TM, POOL_WIDTH), F32)],
        compiler_params=_params(1),
        name="pool_mix",
    )(u, sp_t, wmix, scale)


_NC = D_MODEL // TN


def _merge_kernel(pool_ref, attn_ref, gp_ref, ga_ref, wp_ref, wa_ref, o_ref):
    halves = ((0, HALF), (HALF, TM))
    accs = [(_dot(pool_ref[lo:hi, :], wp_ref[...]), _dot(attn_ref[lo:hi, :], wa_ref[...])) for lo, hi in halves]
    for (lo, hi), (a, b) in zip(halves, accs):
        o_ref[lo:hi, :] = (gp_ref[lo:hi, :].astype(F32) * a + ga_ref[lo:hi, :].astype(F32) * b).astype(BF16)


def _merge(pooled, attn, gates, wp, wa):
    return pl.pallas_call(
        _merge_kernel,
        grid=(_NC, NT),
        in_specs=[
            pl.BlockSpec((TM, POOL_WIDTH), lambda n, m: (m, 0)),
            pl.BlockSpec((TM, D_MODEL), lambda n, m: (m, 0)),
            pl.BlockSpec((TM, TN), lambda n, m: (m, n)),
            pl.BlockSpec((TM, TN), lambda n, m: (m, _NC + n)),
            pl.BlockSpec((POOL_WIDTH, TN), lambda n, m: (0, n)),
            pl.BlockSpec((D_MODEL, TN), lambda n, m: (0, n)),
        ],
        out_specs=pl.BlockSpec((TM, TN), lambda n, m: (m, n)),
        out_shape=jax.ShapeDtypeStruct((ROWS, D_MODEL), BF16),
        compiler_params=_params(2),
        name="branch_merge",
    )(pooled, attn, gates, gates, wp, wa)


def _out_ln_kernel(mrg_ref, wo_ref, x_ref, xs_ref, lng_ref, lnb_ref, x1_ref, x1b_ref):
    m = pl.program_id(0)
    n = pl.program_id(1)
    y = _dot(mrg_ref[...], wo_ref[...])

    for k in range(_NC):
        cols = slice(k * TN, (k + 1) * TN)

        @pl.when((n == k) & (m < NT - 1))
        def _():
            x1_ref[:, cols] = y + ALPHA * x_ref[...]

        @pl.when((n == k) & (m == NT - 1))
        def _():
            x1_ref[0:SPLIT, cols] = y[0:SPLIT] + ALPHA * x_ref[0:SPLIT, :]
            x1_ref[SPLIT:TM, cols] = y[SPLIT:TM] + ALPHA * xs_ref[...]

    @pl.when(n == _NC - 1)
    def _():
        x1 = _layer_norm(x1_ref[...], lng_ref[...], lnb_ref[...])
        x1_ref[...] = x1
        x1b_ref[...] = x1.astype(BF16)


def _out_ln(merged, wo, x, xs, ln_g, ln_b):
    full = lambda shape: pl.BlockSpec(shape, lambda m, n: (0, 0))
    row = pl.BlockSpec((TM, D_MODEL), lambda m, n: (m, 0))
    return pl.pallas_call(
        _out_ln_kernel,
        grid=(NT, _NC),
        in_specs=[
            row,
            pl.BlockSpec((D_MODEL, TN), lambda m, n: (0, n)),
            pl.BlockSpec((TM, TN), lambda m, n: (m, n)),
            pl.BlockSpec((N_SAMPLE, TN), lambda m, n: (0, n)),
            full((1, D_MODEL)), full((1, D_MODEL)),
        ],
        out_specs=[row, row],
        out_shape=[jax.ShapeDtypeStruct((ROWS, D_MODEL), F32), jax.ShapeDtypeStruct((ROWS, D_MODEL), BF16)],
        compiler_params=_params(2),
        name="out_proj_ln1",
    )(merged, wo, x, xs, ln_g, ln_b)


TF = 512
NF = D_FF // TF
CARRY = 8


def _ffn_kernel(x1b_ref, x1_hbm, sc_ref, wg_ref, wu_ref, wd_ref, cw_ref, cb_ref, lng_ref, lnb_ref,
                o_ref, os_ref, glast_ref, scn_ref,
                h0, h1, gext, carry, sem):
    m = pl.program_id(0)
    f = pl.program_id(1)
    last_m = m == NT - 1
    h_slots = (h0, h1)
    x1_copy = pltpu.make_async_copy(x1_hbm.at[pl.ds(m * TM, TM), :], o_ref, sem)

    def up_and_gate(slot, with_down):
        halves = ((0, HALF), (HALF, TM))
        gext[0:CARRY, :] = carry[f]
        for lo, hi in halves:
            gext[CARRY + lo:CARRY + hi, :] = _dot(x1b_ref[lo:hi, :], wg_ref[...])
        ups = [_dot(x1b_ref[lo:hi, :], wu_ref[...]) for lo, hi in halves]
        if with_down:
            o_ref[...] += _dot(h_slots[1 - slot][...], wd_ref[...])
        w0 = cw_ref[0:1, :]
        w1 = cw_ref[1:2, :]
        w2 = cw_ref[2:3, :]
        cb = cb_ref[...]

        def conv(lo, hi, older=None, newer=None):
            older = gext[CARRY - 2 + lo:CARRY - 2 + hi, :] if older is None else older
            newer = gext[CARRY - 1 + lo:CARRY - 1 + hi, :] if newer is None else newer
            return cb + w0 * older + w1 * newer + w2 * gext[CARRY + lo:CARRY + hi, :]

        for (lo, hi), up in zip(halves, ups):
            if hi <= SPLIT:
                gc = conv(lo, hi)
            else:
                older_s = jnp.where(last_m, sc_ref[:, 0, :], gext[CARRY - 2 + SPLIT:CARRY - 2 + TM, :])
                newer_s = jnp.where(last_m, sc_ref[:, 1, :], gext[CARRY - 1 + SPLIT:CARRY - 1 + TM, :])
                gc = jnp.concatenate([conv(lo, SPLIT), conv(SPLIT, TM, older_s, newer_s)], axis=0)
            h_slots[slot][lo:hi, :] = (jax.nn.gelu(gc, approximate=True) * up).astype(BF16)
        carry[f] = gext[TM:CARRY + TM, :]

        @pl.when(last_m)
        def _():
            glast_ref[...] = gext[SPLIT:CARRY + SPLIT, :]
            scn_ref[:, 0, :] = sc_ref[:, 1, :]
            scn_ref[:, 1, :] = gext[CARRY + SPLIT:CARRY + TM, :]

    @pl.when(f == 0)
    def _():
        @pl.when(m == 0)
        def _():
            carry[...] = jnp.zeros_like(carry)

        x1_copy.start()
        up_and_gate(0, False)
        x1_copy.wait()
        o_ref[...] = ALPHA * o_ref[...]

    for parity in range(2):
        @pl.when((f > 0) & (f < NF) & (f % 2 == parity))
        def _():
            up_and_gate(parity, True)

    @pl.when(f == NF)
    def _():
        acc = o_ref[...] + _dot(h_slots[(NF - 1) % 2][...], wd_ref[...])
        y = _layer_norm(acc, lng_ref[...], lnb_ref[...])
        o_ref[...] = y

        @pl.when(last_m)
        def _():
            os_ref[...] = y[SPLIT:TM, :]


def _ffn(x1b, x1, sc, wg, wu, wd, conv_w, conv_b, ln_g, ln_b):
    full = lambda shape: pl.BlockSpec(shape, lambda m, f: (0, 0))
    g_tile = lambda f: jnp.minimum(f, NF - 1)
    d_tile = lambda f: jnp.maximum(f - 1, 0)
    only_last = lambda m, f: jnp.where(m == NT - 1, g_tile(f), 0)
    row = pl.BlockSpec((TM, D_MODEL), lambda m, f: (m, 0))
    return pl.pallas_call(
        _ffn_kernel,
        grid=(NT, NF + 1),
        in_specs=[
            row,
            pl.BlockSpec(memory_space=pl.ANY),
            pl.BlockSpec((N_SAMPLE, CONV_W - 1, TF), lambda m, f: (0, 0, only_last(m, f))),
            pl.BlockSpec((D_MODEL, TF), lambda m, f: (0, g_tile(f))),
            pl.BlockSpec((D_MODEL, TF), lambda m, f: (0, g_tile(f))),
            pl.BlockSpec((TF, D_MODEL), lambda m, f: (d_tile(f), 0)),
            pl.BlockSpec((CONV_W, TF), lambda m, f: (0, g_tile(f))),
            pl.BlockSpec((1, TF), lambda m, f: (0, g_tile(f))),
            full((1, D_MODEL)), full((1, D_MODEL)),
        ],
        out_specs=[
            row,
            full((N_SAMPLE, D_MODEL)),
            pl.BlockSpec((CARRY, TF), lambda m, f: (0, only_last(m, f))),
            pl.BlockSpec((N_SAMPLE, CONV_W - 1, TF), lambda m, f: (0, 0, only_last(m, f))),
        ],
        out_shape=[
            jax.ShapeDtypeStruct((SEQ, D_MODEL), F32),
            jax.ShapeDtypeStruct((N_SAMPLE, D_MODEL), F32),
            jax.ShapeDtypeStruct((CARRY, D_FF), F32),
            jax.ShapeDtypeStruct((N_SAMPLE, CONV_W - 1, D_FF), F32),
        ],
        scratch_shapes=[
            pltpu.VMEM((TM, TF), BF16), pltpu.VMEM((TM, TF), BF16),
            pltpu.VMEM((CARRY + TM, TF), F32),
            pltpu.VMEM((NF, CARRY, TF), F32),
            pltpu.SemaphoreType.DMA(()),
        ],
        compiler_params=_params(2),
        name="convffn_ln2",
    )(x1b, x1, sc, wg, wu, wd, conv_w, conv_b, ln_g, ln_b)


def _rope_tables(pos):
    half = HEAD_DIM // 2
    inv = ROPE_THETA ** (-jnp.arange(half, dtype=F32) / half)
    ang = pos.astype(F32)[:, None] * inv[None, :]
    cos = jnp.cos(ang)
    sin = jnp.sin(ang)
    reps = LANES // HEAD_DIM
    return (jnp.concatenate([cos, cos] * reps, axis=1),
            jnp.concatenate([-sin, sin] * reps, axis=1))


def kernel(x_prompt, x_sample, cache_k, cache_v, state_pool, state_conv, w_in, attn_sinks, w_pool_mix,
           pool_scale, w_attn_branch, w_pool_branch, w_out, ln1_g, ln1_b, w_up, w_gate, conv_w, conv_b,
           w_down, ln2_g, ln2_b):
    x = x_prompt.reshape(SEQ, D_MODEL)
    xs = x_sample.reshape(N_SAMPLE, D_MODEL)
    pos = jnp.concatenate([jnp.arange(SEQ), jnp.full((N_SAMPLE,), PAST_LEN)])
    cos, sin = _rope_tables(pos)

    q, kv, u, gates = _in_proj(x, xs, w_in[0], cos, sin)

    to_dsw = lambda c: jnp.transpose(c[0], (0, 2, 3, 1)).reshape(N_SAMPLE, KV_DIM, WINDOW)
    steps = N_SAMPLE // SEQS_PER_STEP
    kvn_t = jnp.transpose(kv[SEQ:].reshape(steps, SEQS_PER_STEP, 2 * KV_DIM), (0, 2, 1))
    sinks = attn_sinks[0] * LOG2E
    attn_s, new_k, new_v = _attention_sample(
        sinks.reshape(N_HEADS, 1), q[SEQ:].reshape(N_SAMPLE, N_HEADS, HEAD_DIM), kvn_t,
        to_dsw(cache_k), to_dsw(cache_v))
    attn, (wp_b, wa_b, wo_b, wg_b, wu_b, wd_b) = _attention(
        sinks, q, kv, attn_s.reshape(N_SAMPLE, D_MODEL),
        [w_pool_branch[0], w_attn_branch[0], w_out[0], w_gate[0], w_up[0], w_down[0]])

    state_pool_t = jnp.transpose(state_pool[0], (1, 0, 2))
    pooled = _pool(u, state_pool_t, w_pool_mix[0], pool_scale)

    merged = _merge(pooled, attn, gates, wp_b, wa_b)
    x1, x1b = _out_ln(merged, wo_b, x, xs, ln1_g, ln1_b)

    y, y_s, g_last, new_conv_s = _ffn(x1b, x1, state_conv[0], wg_b, wu_b, wd_b,
                                      conv_w[0], conv_b, ln2_g, ln2_b)

    from_dsw = lambda c: jnp.transpose(
        c.reshape(N_SAMPLE, N_KV_HEADS, HEAD_DIM, WINDOW), (0, 3, 1, 2))[None]
    return (
        y.reshape(1, SEQ, D_MODEL),
        y_s.reshape(N_SAMPLE, 1, D_MODEL),
        kv[SEQ - WINDOW:SEQ, :KV_DIM].reshape(1, 1, WINDOW, N_KV_HEADS, HEAD_DIM),
        kv[SEQ - WINDOW:SEQ, KV_DIM:].reshape(1, 1, WINDOW, N_KV_HEADS, HEAD_DIM),
        u[SEQ - POOL_HIST:SEQ].reshape(1, 1, POOL_HIST, POOL_WIDTH),
        g_last[CARRY - (CONV_W - 1):].reshape(1, 1, CONV_W - 1, D_FF),
        from_dsw(new_k),
        from_dsw(new_v),
        jnp.transpose(jnp.concatenate([state_pool_t[1:], u[SEQ:][None]], axis=0), (1, 0, 2))[None],
        new_conv_s[None],
    )
```

```python
import jax
import jax.numpy as jnp
from jax import lax
from jax.experimental import pallas as pl
from jax.experimental.pallas import tpu as pltpu

F32 = jnp.float32
BF16 = jnp.bfloat16

D_MODEL = 2048
SEQ = 8192
N_SAMPLE = 128
PAST_LEN = 8192
HEAD_DIM = 64
N_HEADS = 32
N_KV_HEADS = 4
GROUP = N_HEADS // N_KV_HEADS
KV_DIM = N_KV_HEADS * HEAD_DIM
WINDOW = 128
ROPE_THETA = 10000.0
POOL_WINDOWS = (2, 4, 8, 16)
POOL_WIDTH = 1024
POOL_GROUP_DIM = 256
POOL_HIST = 15
D_FF = 5632
CONV_W = 3
IN_DIM = 7680
LN_EPS = 1e-5
NEG_INF = -1e30
ALPHA = 2.0 ** 0.25
LOG2E = 1.4426950408889634

LANES = 128
VMEM_LIMIT = 60000 * 1024

ROWS = SEQ + N_SAMPLE
TM = 1040
NT = ROWS // TM
SPLIT = SEQ - (NT - 1) * TM
HALF = 528
TN = 512
assert NT * TM == ROWS and SPLIT + N_SAMPLE == TM and SPLIT % 16 == 0 and HALF % 16 == 0


def _params(ndim, flags=None):
    return pltpu.CompilerParams(dimension_semantics=("arbitrary",) * ndim,
                                vmem_limit_bytes=VMEM_LIMIT, flags=flags)


def _dot(a, b):
    return jnp.dot(a, b, preferred_element_type=F32)


def _layer_norm(z, g, b):
    mu = jnp.mean(z, axis=-1, keepdims=True)
    d = z - mu
    var = jnp.mean(d * d, axis=-1, keepdims=True)
    return d * lax.rsqrt(var + LN_EPS) * g + b


def _rope(x, cos, sin_signed):
    lane = lax.broadcasted_iota(jnp.int32, (1, LANES), 1)
    low_half = (lane % HEAD_DIM) < (HEAD_DIM // 2)
    outs = []
    for c in range(x.shape[1] // LANES):
        xc = x[:, c * LANES:(c + 1) * LANES]
        up = pltpu.roll(xc, LANES - HEAD_DIM // 2, 1)
        down = pltpu.roll(xc, HEAD_DIM // 2, 1)
        outs.append(xc * cos + jnp.where(low_half, up, down) * sin_signed)
    return jnp.concatenate(outs, axis=1)


_NQ = D_MODEL // TN
_N_KV = _NQ
_N_U0 = _N_KV + 1
_N_G0 = _N_U0 + POOL_WIDTH // TN


def _in_proj_kernel(x_ref, xs_ref, w_ref, cos_ref, sin_ref, q_ref, kv_ref, u_ref, g_ref, xb):
    m = pl.program_id(0)
    n = pl.program_id(1)

    @pl.when(n == 0)
    def _():
        @pl.when(m < NT - 1)
        def _():
            xb[...] = x_ref[...].astype(BF16)

        @pl.when(m == NT - 1)
        def _():
            xb[0:SPLIT, :] = x_ref[0:SPLIT, :].astype(BF16)
            xb[SPLIT:TM, :] = xs_ref[...].astype(BF16)

    wb = w_ref[...].astype(BF16)
    halves = ((0, HALF), (HALF, TM))
    thirds = ((0, 352), (352, 704), (704, TM))

    def project(epilogue, parts):
        accs = [_dot(xb[lo:hi, :], wb) for lo, hi in parts]
        for (lo, hi), acc in zip(parts, accs):
            epilogue(acc, lo, hi)

    def q_epilogue(acc, lo, hi):
        roped = _rope(acc, cos_ref[lo:hi, :], sin_ref[lo:hi, :])
        q_ref[lo:hi, :] = (roped * (HEAD_DIM ** -0.5 * LOG2E)).astype(BF16)

    def kv_epilogue(acc, lo, hi):
        kv_ref[lo:hi, :KV_DIM] = _rope(acc[:, :KV_DIM], cos_ref[lo:hi, :], sin_ref[lo:hi, :])
        kv_ref[lo:hi, KV_DIM:] = acc[:, KV_DIM:]

    def u_epilogue(acc, lo, hi):
        u_ref[lo:hi, :] = acc

    def gate_epilogue(acc, lo, hi):
        g_ref[lo:hi, :] = jax.nn.sigmoid(acc).astype(BF16)

    pl.when(n < _NQ)(lambda: project(q_epilogue, thirds))
    pl.when(n == _N_KV)(lambda: project(kv_epilogue, halves))
    pl.when((n >= _N_U0) & (n < _N_G0))(lambda: project(u_epilogue, halves))
    pl.when(n >= _N_G0)(lambda: project(gate_epilogue, thirds))


def _in_proj(x, xs, w_in, cos, sin):
    nn = IN_DIM // TN
    ng = 2 * D_MODEL // TN

    def clamp(n, lo, hi):
        return jnp.clip(n - lo, 0, hi - lo)

    return pl.pallas_call(
        _in_proj_kernel,
        grid=(NT, nn),
        in_specs=[
            pl.BlockSpec((TM, D_MODEL), lambda m, n: (m, 0)),
            pl.BlockSpec((N_SAMPLE, D_MODEL), lambda m, n: (0, 0)),
            pl.BlockSpec((D_MODEL, TN), lambda m, n: (0, n)),
            pl.BlockSpec((TM, LANES), lambda m, n: (m, 0)),
            pl.BlockSpec((TM, LANES), lambda m, n: (m, 0)),
        ],
        out_specs=[
            pl.BlockSpec((TM, TN), lambda m, n: (m, clamp(n, 0, _NQ - 1))),
            pl.BlockSpec((TM, TN), lambda m, n: (m, 0)),
            pl.BlockSpec((TM, TN), lambda m, n: (m, clamp(n, _N_U0, _N_G0 - 1))),
            pl.BlockSpec((TM, TN), lambda m, n: (m, clamp(n, _N_G0, _N_G0 + ng - 1))),
        ],
        out_shape=[
            jax.ShapeDtypeStruct((ROWS, D_MODEL), BF16),
            jax.ShapeDtypeStruct((ROWS, 2 * KV_DIM), F32),
            jax.ShapeDtypeStruct((ROWS, POOL_WIDTH), F32),
            jax.ShapeDtypeStruct((ROWS, 2 * D_MODEL), BF16),
        ],
        scratch_shapes=[pltpu.VMEM((TM, D_MODEL), BF16)],
        compiler_params=_params(2),
        name="in_proj",
    )(x, xs, w_in, cos, sin)


PAIRS = GROUP // 2


def _attn_kernel(sink_ref, q_ref, kp_ref, kc_ref, vp_ref, vc_ref, os_ref, *refs):
    n_w = (len(refs) - 1) // 2
    w_refs, o_ref, wb_refs = refs[:n_w], refs[n_w], refs[n_w + 1:]
    i = pl.program_id(0)
    last = pl.num_programs(0) - 1

    @pl.when(i < last)
    def _():
        for w_ref, wb_ref in zip(w_refs, wb_refs):
            wb_ref[...] = w_ref[...].astype(BF16)
        _attn_block(i, sink_ref, q_ref, kp_ref, kc_ref, vp_ref, vc_ref, o_ref)

    @pl.when(i == last)
    def _():
        o_ref[...] = os_ref[...]


def _attn_block(i, sink_ref, q_ref, kp_ref, kc_ref, vp_ref, vc_ref, o_ref):
    kk =jnp.concatenate([kp_ref[...], kc_ref[...]], axis=0)
    vv = jnp.concatenate([vp_ref[...], vc_ref[...]], axis=0)
    low = lax.broadcasted_iota(jnp.int32, (1, LANES), 1) < HEAD_DIM
    rows = PAIRS * WINDOW
    r = lax.broadcasted_iota(jnp.int32, (rows, 2 * WINDOW), 0) % WINDOW
    c = lax.broadcasted_iota(jnp.int32, (rows, 2 * WINDOW), 1)
    visible = (c > r) & (c <= r + WINDOW) & ((i > 0) | (c >= WINDOW))
    chunk = lax.broadcasted_iota(jnp.int32, (rows, 1), 0) // WINDOW

    def sink_column(first_head):
        col = jnp.full((rows, 1), sink_ref[first_head], F32)
        for t in range(1, PAIRS):
            col = jnp.where(chunk == t, sink_ref[first_head + 2 * t], col)
        return col

    for pair in range(N_KV_HEADS // 2):
        k_pair = kk[:, pair * LANES:(pair + 1) * LANES]
        v_pair = vv[:, pair * LANES:(pair + 1) * LANES]
        k_swap = pltpu.roll(k_pair, HEAD_DIM, 1)
        v_swap = pltpu.roll(v_pair, HEAD_DIM, 1)
        for second in range(2):
            g = 2 * pair + second
            k_in_low, k_in_high = (k_swap, k_pair) if second else (k_pair, k_swap)
            v_in_low, v_in_high = (v_swap, v_pair) if second else (v_pair, v_swap)
            x = jnp.concatenate(
                [q_ref[:, (g * PAIRS + t) * LANES:(g * PAIRS + t + 1) * LANES] for t in range(PAIRS)], axis=0)

            def half(k_pad, v_pad, sink):
                s = lax.dot_general(x, k_pad.astype(BF16), (((1,), (1,)), ((), ())),
                                    preferred_element_type=F32)
                s = jnp.where(visible, s, NEG_INF)
                mx = jnp.maximum(jnp.max(s, axis=-1, keepdims=True), sink)
                e = jnp.exp2(s - mx).astype(BF16)
                return _dot(e, v_pad.astype(BF16)), jnp.exp2(sink - mx)

            a, sink_a = half(jnp.where(low, k_in_low, 0.0), jnp.where(low, v_in_low, 1.0),
                             sink_column(g * GROUP))
            b, sink_b = half(jnp.where(low, 0.0, k_in_high), jnp.where(low, 1.0, v_in_high),
                             sink_column(g * GROUP + 1))
            den = pltpu.roll(jnp.where(low, b, a), HEAD_DIM, 1) + jnp.where(low, sink_a, sink_b)
            out = (jnp.where(low, a, b) / den).astype(BF16)
            for t in range(PAIRS):
                o_ref[:, (g * PAIRS + t) * LANES:(g * PAIRS + t + 1) * LANES] = out[t * WINDOW:(t + 1) * WINDOW]


def _cast_chunk_rows(rows, steps):
    assert rows % 16 == 0
    units = rows // 16
    n = max(d for d in range(1, steps + 1) if units % d == 0)
    return rows // n


def _attention(sinks, q, kv, attn_s, weights):
    nb = SEQ // WINDOW
    cur = lambda i: jnp.minimum(i, nb - 1)
    prev = lambda i: jnp.clip(i - 1, 0, nb - 1)
    w_specs = []
    for w in weights:
        chunk = _cast_chunk_rows(w.shape[0], nb)
        w_specs.append(pl.BlockSpec((chunk, w.shape[1]),
                                    lambda i, n=w.shape[0] // chunk: (jnp.minimum(i, n - 1), 0)))
    outs = pl.pallas_call(
        _attn_kernel,
        grid=(nb + 1,),
        in_specs=[
            pl.BlockSpec(memory_space=pltpu.SMEM),
            pl.BlockSpec((WINDOW, D_MODEL), lambda i: (cur(i), 0)),
            pl.BlockSpec((WINDOW, KV_DIM), lambda i: (prev(i), 0)),
            pl.BlockSpec((WINDOW, KV_DIM), lambda i: (cur(i), 0)),
            pl.BlockSpec((WINDOW, KV_DIM), lambda i: (prev(i), 1)),
            pl.BlockSpec((WINDOW, KV_DIM), lambda i: (cur(i), 1)),
            pl.BlockSpec((N_SAMPLE, D_MODEL), lambda i: (0, 0)),
        ] + w_specs,
        out_specs=[pl.BlockSpec((WINDOW, D_MODEL), lambda i: (i, 0))] + w_specs,
        out_shape=[jax.ShapeDtypeStruct((ROWS, D_MODEL), BF16)]
        + [jax.ShapeDtypeStruct(w.shape, BF16) for w in weights],
        compiler_params=_params(1),
        name="attn_prompt",
    )(sinks, q, kv, kv, kv, kv, attn_s, *weights)
    return outs[0], outs[1:]


SEQS_PER_STEP = 8


def _attn_sample_kernel(sink_ref, q_ref, kvn_ref, ck_ref, cv_ref, o_ref, nk_ref, nv_ref):
    head = lax.broadcasted_iota(jnp.int32, (N_HEADS, KV_DIM), 0) // GROUP
    col_group = lax.broadcasted_iota(jnp.int32, (N_HEADS, KV_DIM), 1) // HEAD_DIM
    own = head == col_group
    newest = lax.broadcasted_iota(jnp.int32, (1, WINDOW), 1) == WINDOW - 1
    scores = []
    for b in range(SEQS_PER_STEP):
        nk_ref[b] = jnp.where(newest, kvn_ref[0, 0:KV_DIM, b:b + 1], pltpu.roll(ck_ref[b], WINDOW - 1, 1))
        nv_ref[b] = jnp.where(newest, kvn_ref[0, KV_DIM:2 * KV_DIM, b:b + 1],
                              pltpu.roll(cv_ref[b], WINDOW - 1, 1))
        qb = q_ref[b]
        qe = jnp.where(own, jnp.concatenate([qb] * N_KV_HEADS, axis=1), jnp.zeros((), BF16))
        scores.append(_dot(qe, nk_ref[b].astype(BF16)))
    s = jnp.concatenate(scores, axis=0)
    sink = jnp.concatenate([sink_ref[...]] * SEQS_PER_STEP, axis=0)
    mx = jnp.maximum(jnp.max(s, axis=-1, keepdims=True), sink)
    e = jnp.exp2(s - mx)
    inv = 1.0 / (jnp.sum(e, axis=-1, keepdims=True) + jnp.exp2(sink - mx))
    e = e.astype(BF16)
    for b in range(SEQS_PER_STEP):
        rows = slice(b * N_HEADS, (b + 1) * N_HEADS)
        of = lax.dot_general(e[rows], nv_ref[b].astype(BF16), (((1,), (1,)), ((), ())),
                             preferred_element_type=F32)
        of = jnp.where(own, of, 0.0)
        o = of[:, 0:HEAD_DIM]
        for g in range(1, N_KV_HEADS):
            o = o + of[:, g * HEAD_DIM:(g + 1) * HEAD_DIM]
        o_ref[b] = (o * inv[rows]).astype(BF16)


def _attention_sample(sink_col, q3, kvn_t, ck, cv):
    nb = N_SAMPLE // SEQS_PER_STEP
    cache_spec = pl.BlockSpec((SEQS_PER_STEP, KV_DIM, WINDOW), lambda i: (i, 0, 0))
    return pl.pallas_call(
        _attn_sample_kernel,
        grid=(nb,),
        in_specs=[
            pl.BlockSpec((N_HEADS, 1), lambda i: (0, 0)),
            pl.BlockSpec((SEQS_PER_STEP, N_HEADS, HEAD_DIM), lambda i: (i, 0, 0)),
            pl.BlockSpec((1, 2 * KV_DIM, SEQS_PER_STEP), lambda i: (i, 0, 0)),
            cache_spec, cache_spec,
        ],
        out_specs=[
            pl.BlockSpec((SEQS_PER_STEP, N_HEADS, HEAD_DIM), lambda i: (i, 0, 0)),
            cache_spec, cache_spec,
        ],
        out_shape=[
            jax.ShapeDtypeStruct((N_SAMPLE, N_HEADS, HEAD_DIM), BF16),
            jax.ShapeDtypeStruct((N_SAMPLE, KV_DIM, WINDOW), F32),
            jax.ShapeDtypeStruct((N_SAMPLE, KV_DIM, WINDOW), F32),
        ],
        compiler_params=_params(1),
        name="attn_sample",
    )(sink_col, q3, kvn_t, ck, cv)


POOL_PAD = 16


def _pool_kernel(u_ref, sp_ref, wmix_ref, scale_ref, p_ref, spn_ref, ext):
    m = pl.program_id(0)

    @pl.when(m == 0)
    def _():
        ext[0:POOL_PAD, :] = jnp.zeros((POOL_PAD, POOL_WIDTH), F32)

    ext[POOL_PAD:POOL_PAD + TM, :] = u_ref[...]
    pos = m * TM + lax.broadcasted_iota(jnp.int32, (TM, 1), 0)
    for g, w in enumerate(POOL_WINDOWS):
        cols = slice(g * POOL_GROUP_DIM, (g + 1) * POOL_GROUP_DIM)
        tot = ext[POOL_PAD:POOL_PAD + TM, cols]
        for j in range(1, w):
            tot = tot + ext[POOL_PAD - j:POOL_PAD - j + TM, cols]
        inv_cnt = 1.0 / jnp.minimum(w, pos + 1).astype(F32)
        d = tot * inv_cnt - u_ref[:, cols]
        y = _dot(d.astype(BF16), wmix_ref[g].astype(BF16))
        p_ref[:, cols] = (y * scale_ref[:, cols]).astype(BF16)
    ext[0:POOL_PAD, :] = ext[TM:TM + POOL_PAD, :]

    @pl.when(m == NT - 1)
    def _():
        for g, w in enumerate(POOL_WINDOWS):
            cols = slice(g * POOL_GROUP_DIM, (g + 1) * POOL_GROUP_DIM)
            us = u_ref[SPLIT:TM, cols]
            tot = us
            for j in range(1, w):
                tot = tot + sp_ref[POOL_HIST - j, :, cols]
            d = tot * (1.0 / w) - us
            y = _dot(d.astype(BF16), wmix_ref[g].astype(BF16))
            p_ref[SPLIT:TM, cols] = (y * scale_ref[:, cols]).astype(BF16)
        spn_ref[0:POOL_HIST - 1] = sp_ref[1:POOL_HIST]
        spn_ref[POOL_HIST - 1] = u_ref[SPLIT:TM, :]


def _pool(u, sp_t, wmix, scale):
    full = lambda shape: pl.BlockSpec(shape, lambda m: (0,) * len(shape))
    return pl.pallas_call(
        _pool_kernel,
        grid=(NT,),
        in_specs=[
            pl.BlockSpec((TM, POOL_WIDTH), lambda m: (m, 0)),
            full((POOL_HIST, N_SAMPLE, POOL_WIDTH)),
            full((len(POOL_WINDOWS), POOL_GROUP_DIM, POOL_GROUP_DIM)),
            full((1, POOL_WIDTH)),
        ],
        out_specs=[pl.BlockSpec((TM, POOL_WIDTH), lambda m: (m, 0)),
                   full((POOL_HIST, N_SAMPLE, POOL_WIDTH))],
        out_shape=[jax.ShapeDtypeStruct((ROWS, POOL_WIDTH), BF16),
                   jax.ShapeDtypeStruct((POOL_HIST, N_SAMPLE, POOL_WIDTH), F32)],
        scratch_shapes=[pltpu.VMEM((POOL_PAD + TM, POOL_WIDTH), F32)],
        compiler_params=_params(1),
        name="pool_mix",
    )(u, sp_t, wmix, scale)


_NC = D_MODEL // TN


def _merge_kernel(pool_ref, attn_ref, gp_ref, ga_ref, wp_ref, wa_ref, o_ref):
    halves = ((0, HALF), (HALF, TM))
    accs = [(_dot(pool_ref[lo:hi, :], wp_ref[...]), _dot(attn_ref[lo:hi, :], wa_ref[...])) for lo, hi in halves]
    for (lo, hi), (a, b) in zip(halves, accs):
        o_ref[lo:hi, :] = (gp_ref[lo:hi, :].astype(F32) * a + ga_ref[lo:hi, :].astype(F32) * b).astype(BF16)


def _merge(pooled, attn, gates, wp, wa):
    return pl.pallas_call(
        _merge_kernel,
        grid=(_NC, NT),
        in_specs=[
            pl.BlockSpec((TM, POOL_WIDTH), lambda n, m: (m, 0)),
            pl.BlockSpec((TM, D_MODEL), lambda n, m: (m, 0)),
            pl.BlockSpec((TM, TN), lambda n, m: (m, n)),
            pl.BlockSpec((TM, TN), lambda n, m: (m, _NC + n)),
            pl.BlockSpec((POOL_WIDTH, TN), lambda n, m: (0, n)),
            pl.BlockSpec((D_MODEL, TN), lambda n, m: (0, n)),
        ],
        out_specs=pl.BlockSpec((TM, TN), lambda n, m: (m, n)),
        out_shape=jax.ShapeDtypeStruct((ROWS, D_MODEL), BF16),
        compiler_params=_params(2),
        name="branch_merge",
    )(pooled, attn, gates, gates, wp, wa)


def _out_ln_kernel(mrg_ref, wo_ref, x_ref, xs_ref, lng_ref, lnb_ref, x1_ref, x1b_ref):
    m = pl.program_id(0)
    n = pl.program_id(1)
    y = _dot(mrg_ref[...], wo_ref[...])

    for k in range(_NC):
        cols = slice(k * TN, (k + 1) * TN)

        @pl.when((n == k) & (m < NT - 1))
        def _():
            x1_ref[:, cols] = y + ALPHA * x_ref[...]

        @pl.when((n == k) & (m == NT - 1))
        def _():
            x1_ref[0:SPLIT, cols] = y[0:SPLIT] + ALPHA * x_ref[0:SPLIT, :]
            x1_ref[SPLIT:TM, cols] = y[SPLIT:TM] + ALPHA * xs_ref[...]

    @pl.when(n == _NC - 1)
    def _():
        x1 = _layer_norm(x1_ref[...], lng_ref[...], lnb_ref[...])
        x1_ref[...] = x1
        x1b_ref[...] = x1.astype(BF16)


def _out_ln(merged, wo, x, xs, ln_g, ln_b):
    full = lambda shape: pl.BlockSpec(shape, lambda m, n: (0, 0))
    row = pl.BlockSpec((TM, D_MODEL), lambda m, n: (m, 0))
    return pl.pallas_call(
        _out_ln_kernel,
        grid=(NT, _NC),
        in_specs=[
            row,
            pl.BlockSpec((D_MODEL, TN), lambda m, n: (0, n)),
            pl.BlockSpec((TM, TN), lambda m, n: (m, n)),
            pl.BlockSpec((N_SAMPLE, TN), lambda m, n: (0, n)),
            full((1, D_MODEL)), full((1, D_MODEL)),
        ],
        out_specs=[row, row],
        out_shape=[jax.ShapeDtypeStruct((ROWS, D_MODEL), F32), jax.ShapeDtypeStruct((ROWS, D_MODEL), BF16)],
        compiler_params=_params(2),
        name="out_proj_ln1",
    )(merged, wo, x, xs, ln_g, ln_b)


TF = 512
NF = D_FF // TF
CARRY = 8


def _ffn_kernel(x1b_ref, x1_hbm, sc_ref, wg_ref, wu_ref, wd_ref, cw_ref, cb_ref, lng_ref, lnb_ref,
                o_ref, os_ref, glast_ref, scn_ref,
                h0, h1, gext, carry, sem):
    m = pl.program_id(0)
    f = pl.program_id(1)
    last_m = m == NT - 1
    h_slots = (h0, h1)
    x1_copy = pltpu.make_async_copy(x1_hbm.at[pl.ds(m * TM, TM), :], o_ref, sem)

    def up_and_gate(slot, with_down):
        halves = ((0, HALF), (HALF, TM))
        gext[0:CARRY, :] = carry[f]
        for lo, hi in halves:
            gext[CARRY + lo:CARRY + hi, :] = _dot(x1b_ref[lo:hi, :], wg_ref[...])
        ups = [_dot(x1b_ref[lo:hi, :], wu_ref[...]) for lo, hi in halves]
        if with_down:
            o_ref[...] += _dot(h_slots[1 - slot][...], wd_ref[...])
        w0 = cw_ref[0:1, :]
        w1 = cw_ref[1:2, :]
        w2 = cw_ref[2:3, :]
        cb = cb_ref[...]

        def conv(lo, hi, older=None, newer=None):
            older = gext[CARRY - 2 + lo:CARRY - 2 + hi, :] if older is None else older
            newer = gext[CARRY - 1 + lo:CARRY - 1 + hi, :] if newer is None else newer
            return cb + w0 * older + w1 * newer + w2 * gext[CARRY + lo:CARRY + hi, :]

        for (lo, hi), up in zip(halves, ups):
            if hi <= SPLIT:
                gc = conv(lo, hi)
            else:
                older_s = jnp.where(last_m, sc_ref[:, 0, :], gext[CARRY - 2 + SPLIT:CARRY - 2 + TM, :])
                newer_s = jnp.where(last_m, sc_ref[:, 1, :], gext[CARRY - 1 + SPLIT:CARRY - 1 + TM, :])
                gc = jnp.concatenate([conv(lo, SPLIT), conv(SPLIT, TM, older_s, newer_s)], axis=0)
            h_slots[slot][lo:hi, :] = (jax.nn.gelu(gc, approximate=True) * up).astype(BF16)
        carry[f] = gext[TM:CARRY + TM, :]

        @pl.when(last_m)
        def _():
            glast_ref[...] = gext[SPLIT:CARRY + SPLIT, :]
            scn_ref[:, 0, :] = sc_ref[:, 1, :]
            scn_ref[:, 1, :] = gext[CARRY + SPLIT:CARRY + TM, :]

    @pl.when(f == 0)
    def _():
        @pl.when(m == 0)
        def _():
            carry[...] = jnp.zeros_like(carry)

        x1_copy.start()
        up_and_gate(0, False)
        x1_copy.wait()
        o_ref[...] = ALPHA * o_ref[...]

    for parity in range(2):
        @pl.when((f > 0) & (f < NF) & (f % 2 == parity))
        def _():
            up_and_gate(parity, True)

    @pl.when(f == NF)
    def _():
        acc = o_ref[...] + _dot(h_slots[(NF - 1) % 2][...], wd_ref[...])
        y = _layer_norm(acc, lng_ref[...], lnb_ref[...])
        o_ref[...] = y

        @pl.when(last_m)
        def _():
            os_ref[...] = y[SPLIT:TM, :]


def _ffn(x1b, x1, sc, wg, wu, wd, conv_w, conv_b, ln_g, ln_b):
    full = lambda shape: pl.BlockSpec(shape, lambda m, f: (0, 0))
    g_tile = lambda f: jnp.minimum(f, NF - 1)
    d_tile = lambda f: jnp.maximum(f - 1, 0)
    only_last = lambda m, f: jnp.where(m == NT - 1, g_tile(f), 0)
    row = pl.BlockSpec((TM, D_MODEL), lambda m, f: (m, 0))
    return pl.pallas_call(
        _ffn_kernel,
        grid=(NT, NF + 1),
        in_specs=[
            row,
            pl.BlockSpec(memory_space=pl.ANY),
            pl.BlockSpec((N_SAMPLE, CONV_W - 1, TF), lambda m, f: (0, 0, only_last(m, f))),
            pl.BlockSpec((D_MODEL, TF), lambda m, f: (0, g_tile(f))),
            pl.BlockSpec((D_MODEL, TF), lambda m, f: (0, g_tile(f))),
            pl.BlockSpec((TF, D_MODEL), lambda m, f: (d_tile(f), 0)),
            pl.BlockSpec((CONV_W, TF), lambda m, f: (0, g_tile(f))),
            pl.BlockSpec((1, TF), lambda m, f: (0, g_tile(f))),
            full((1, D_MODEL)), full((1, D_MODEL)),
        ],
        out_specs=[
            row,
            full((N_SAMPLE, D_MODEL)),
            pl.BlockSpec((CARRY, TF), lambda m, f: (0, only_last(m, f))),
            pl.BlockSpec((N_SAMPLE, CONV_W - 1, TF), lambda m, f: (0, 0, only_last(m, f))),
        ],
        out_shape=[
            jax.ShapeDtypeStruct((SEQ, D_MODEL), F32),
            jax.ShapeDtypeStruct((N_SAMPLE, D_MODEL), F32),
            jax.ShapeDtypeStruct((CARRY, D_FF), F32),
            jax.ShapeDtypeStruct((N_SAMPLE, CONV_W - 1, D_FF), F32),
        ],
        scratch_shapes=[
            pltpu.VMEM((TM, TF), BF16), pltpu.VMEM((TM, TF), BF16),
            pltpu.VMEM((CARRY + TM, TF), F32),
            pltpu.VMEM((NF, CARRY, TF), F32),
            pltpu.SemaphoreType.DMA(()),
        ],
        compiler_params=_params(2),
        name="convffn_ln2",
    )(x1b, x1, sc, wg, wu, wd, conv_w, conv_b, ln_g, ln_b)


def _rope_tables(pos):
    half = HEAD_DIM // 2
    inv = ROPE_THETA ** (-jnp.arange(half, dtype=F32) / half)
    lane = jnp.arange(LANES)
    inv_lanes = inv[lane % half]
    sign = jnp.where((lane % HEAD_DIM) < half, -1.0, 1.0).astype(F32)
    ang = pos.astype(F32)[:, None] * inv_lanes[None, :]
    return jnp.cos(ang), jnp.sin(ang) * sign[None, :]


def kernel(x_prompt, x_sample, cache_k, cache_v, state_pool, state_conv, w_in, attn_sinks, w_pool_mix,
           pool_scale, w_attn_branch, w_pool_branch, w_out, ln1_g, ln1_b, w_up, w_gate, conv_w, conv_b,
           w_down, ln2_g, ln2_b):
    x = x_prompt.reshape(SEQ, D_MODEL)
    xs = x_sample.reshape(N_SAMPLE, D_MODEL)
    pos = jnp.concatenate([jnp.arange(SEQ), jnp.full((N_SAMPLE,), PAST_LEN)])
    cos, sin = _rope_tables(pos)

    q, kv, u, gates = _in_proj(x, xs, w_in[0], cos, sin)

    to_dsw = lambda c: jnp.transpose(c[0], (0, 2, 3, 1)).reshape(N_SAMPLE, KV_DIM, WINDOW)
    steps = N_SAMPLE // SEQS_PER_STEP
    kvn_t = jnp.transpose(kv[SEQ:].reshape(steps, SEQS_PER_STEP, 2 * KV_DIM), (0, 2, 1))
    sinks = attn_sinks[0] * LOG2E
    attn_s, new_k, new_v = _attention_sample(
        sinks.reshape(N_HEADS, 1), q[SEQ:].reshape(N_SAMPLE, N_HEADS, HEAD_DIM), kvn_t,
        to_dsw(cache_k), to_dsw(cache_v))
    attn, (wp_b, wa_b, wo_b, wg_b, wu_b, wd_b) = _attention(
        sinks, q, kv, attn_s.reshape(N_SAMPLE, D_MODEL),
        [w_pool_branch[0], w_attn_branch[0], w_out[0], w_gate[0], w_up[0], w_down[0]])

    state_pool_t = jnp.transpose(state_pool[0], (1, 0, 2))
    pooled, new_pool_t = _pool(u, state_pool_t, w_pool_mix[0], pool_scale)

    merged = _merge(pooled, attn, gates, wp_b, wa_b)
    x1, x1b = _out_ln(merged, wo_b, x, xs, ln1_g, ln1_b)

    y, y_s, g_last, new_conv_s = _ffn(x1b, x1, state_conv[0], wg_b, wu_b, wd_b,
                                      conv_w[0], conv_b, ln2_g, ln2_b)

    from_dsw = lambda c: jnp.transpose(
        c.reshape(N_SAMPLE, N_KV_HEADS, HEAD_DIM, WINDOW), (0, 3, 1, 2))[None]
    return (
        y.reshape(1, SEQ, D_MODEL),
        y_s.reshape(N_SAMPLE, 1, D_MODEL),
        kv[SEQ - WINDOW:SEQ, :KV_DIM].reshape(1, 1, WINDOW, N_KV_HEADS, HEAD_DIM),
        kv[SEQ - WINDOW:SEQ, KV_DIM:].reshape(1, 1, WINDOW, N_KV_HEADS, HEAD_DIM),
        u[SEQ - POOL_HIST:SEQ].reshape(1, 1, POOL_HIST, POOL_WIDTH),
        g_last[CARRY - (CONV_W - 1):].reshape(1, 1, CONV_W - 1, D_FF),
        from_dsw(new_k),
        from_dsw(new_v),
        jnp.transpose(new_pool_t, (1, 0, 2))[None],
        new_conv_s[None],
    )
```

```python
import jax
import jax.numpy as jnp
from jax import lax
from jax.experimental import pallas as pl
from jax.experimental.pallas import tpu as pltpu

F32 = jnp.float32
BF16 = jnp.bfloat16

D_MODEL = 2048
SEQ = 8192
N_SAMPLE = 128
PAST_LEN = 8192
HEAD_DIM = 64
N_HEADS = 32
N_KV_HEADS = 4
GROUP = N_HEADS // N_KV_HEADS
KV_DIM = N_KV_HEADS * HEAD_DIM
WINDOW = 128
ROPE_THETA = 10000.0
POOL_WINDOWS = (2, 4, 8, 16)
POOL_WIDTH = 1024
POOL_GROUP_DIM = 256
POOL_HIST = 15
D_FF = 5632
CONV_W = 3
IN_DIM = 7680
LN_EPS = 1e-5
NEG_INF = -1e30
ALPHA = 2.0 ** 0.25
LOG2E = 1.4426950408889634

LANES = 128
VMEM_LIMIT = 60000 * 1024

ROWS = SEQ + N_SAMPLE
TM = 1040
NT = ROWS // TM
SPLIT = SEQ - (NT - 1) * TM
HALF = 528
TN = 512
assert NT * TM == ROWS and SPLIT + N_SAMPLE == TM and SPLIT % 16 == 0 and HALF % 16 == 0


def _params(ndim, flags=None):
    return pltpu.CompilerParams(dimension_semantics=("arbitrary",) * ndim,
                                vmem_limit_bytes=VMEM_LIMIT, flags=flags)


def _dot(a, b):
    return jnp.dot(a, b, preferred_element_type=F32)


def _layer_norm(z, g, b):
    mu = jnp.mean(z, axis=-1, keepdims=True)
    d = z - mu
    var = jnp.mean(d * d, axis=-1, keepdims=True)
    return d * lax.rsqrt(var + LN_EPS) * g + b


def _rope(x, cos, sin_signed):
    lane = lax.broadcasted_iota(jnp.int32, (1, LANES), 1)
    low_half = (lane % HEAD_DIM) < (HEAD_DIM // 2)
    outs = []
    for c in range(x.shape[1] // LANES):
        xc = x[:, c * LANES:(c + 1) * LANES]
        up = pltpu.roll(xc, LANES - HEAD_DIM // 2, 1)
        down = pltpu.roll(xc, HEAD_DIM // 2, 1)
        outs.append(xc * cos + jnp.where(low_half, up, down) * sin_signed)
    return jnp.concatenate(outs, axis=1)


_NQ = D_MODEL // TN
_N_KV = _NQ
_N_U0 = _N_KV + 1
_N_G0 = _N_U0 + POOL_WIDTH // TN


def _in_proj_kernel(x_ref, xs_ref, w_ref, cos_ref, sin_ref, q_ref, kv_ref, u_ref, g_ref, xb):
    m = pl.program_id(0)
    n = pl.program_id(1)

    @pl.when(n == 0)
    def _():
        @pl.when(m < NT - 1)
        def _():
            xb[...] = x_ref[...].astype(BF16)

        @pl.when(m == NT - 1)
        def _():
            xb[0:SPLIT, :] = x_ref[0:SPLIT, :].astype(BF16)
            xb[SPLIT:TM, :] = xs_ref[...].astype(BF16)

    wb = w_ref[...].astype(BF16)
    halves = ((0, HALF), (HALF, TM))
    thirds = ((0, 352), (352, 704), (704, TM))

    def project(epilogue, parts):
        accs = [_dot(xb[lo:hi, :], wb) for lo, hi in parts]
        for (lo, hi), acc in zip(parts, accs):
            epilogue(acc, lo, hi)

    def q_epilogue(acc, lo, hi):
        roped = _rope(acc, cos_ref[lo:hi, :], sin_ref[lo:hi, :])
        q_ref[lo:hi, :] = (roped * (HEAD_DIM ** -0.5 * LOG2E)).astype(BF16)

    def kv_epilogue(acc, lo, hi):
        kv_ref[lo:hi, :KV_DIM] = _rope(acc[:, :KV_DIM], cos_ref[lo:hi, :], sin_ref[lo:hi, :])
        kv_ref[lo:hi, KV_DIM:] = acc[:, KV_DIM:]

    def u_epilogue(acc, lo, hi):
        u_ref[lo:hi, :] = acc

    def gate_epilogue(acc, lo, hi):
        g_ref[lo:hi, :] = jax.nn.sigmoid(acc).astype(BF16)

    pl.when(n < _NQ)(lambda: project(q_epilogue, thirds))
    pl.when(n == _N_KV)(lambda: project(kv_epilogue, halves))
    pl.when((n >= _N_U0) & (n < _N_G0))(lambda: project(u_epilogue, halves))
    pl.when(n >= _N_G0)(lambda: project(gate_epilogue, thirds))


def _in_proj(x, xs, w_in, cos, sin):
    nn = IN_DIM // TN
    ng = 2 * D_MODEL // TN

    def clamp(n, lo, hi):
        return jnp.clip(n - lo, 0, hi - lo)

    return pl.pallas_call(
        _in_proj_kernel,
        grid=(NT, nn),
        in_specs=[
            pl.BlockSpec((TM, D_MODEL), lambda m, n: (m, 0)),
            pl.BlockSpec((N_SAMPLE, D_MODEL), lambda m, n: (0, 0)),
            pl.BlockSpec((D_MODEL, TN), lambda m, n: (0, n)),
            pl.BlockSpec((TM, LANES), lambda m, n: (m, 0)),
            pl.BlockSpec((TM, LANES), lambda m, n: (m, 0)),
        ],
        out_specs=[
            pl.BlockSpec((TM, TN), lambda m, n: (m, clamp(n, 0, _NQ - 1))),
            pl.BlockSpec((TM, TN), lambda m, n: (m, 0)),
            pl.BlockSpec((TM, TN), lambda m, n: (m, clamp(n, _N_U0, _N_G0 - 1))),
            pl.BlockSpec((TM, TN), lambda m, n: (m, clamp(n, _N_G0, _N_G0 + ng - 1))),
        ],
        out_shape=[
            jax.ShapeDtypeStruct((ROWS, D_MODEL), BF16),
            jax.ShapeDtypeStruct((ROWS, 2 * KV_DIM), F32),
            jax.ShapeDtypeStruct((ROWS, POOL_WIDTH), F32),
            jax.ShapeDtypeStruct((ROWS, 2 * D_MODEL), BF16),
        ],
        scratch_shapes=[pltpu.VMEM((TM, D_MODEL), BF16)],
        compiler_params=_params(2),
        name="in_proj",
    )(x, xs, w_in, cos, sin)


PAIRS = GROUP // 2


def _attn_kernel(sink_ref, q_ref, kp_ref, kc_ref, vp_ref, vc_ref, os_ref, *refs):
    n_w = (len(refs) - 1) // 2
    w_refs, o_ref, wb_refs = refs[:n_w], refs[n_w], refs[n_w + 1:]
    i = pl.program_id(0)
    last = pl.num_programs(0) - 1

    @pl.when(i < last)
    def _():
        for w_ref, wb_ref in zip(w_refs, wb_refs):
            wb_ref[...] = w_ref[...].astype(BF16)
        _attn_block(i, sink_ref, q_ref, kp_ref, kc_ref, vp_ref, vc_ref, o_ref)

    @pl.when(i == last)
    def _():
        o_ref[...] = os_ref[...]


def _attn_block(i, sink_ref, q_ref, kp_ref, kc_ref, vp_ref, vc_ref, o_ref):
    kk =jnp.concatenate([kp_ref[...], kc_ref[...]], axis=0)
    vv = jnp.concatenate([vp_ref[...], vc_ref[...]], axis=0)
    low = lax.broadcasted_iota(jnp.int32, (1, LANES), 1) < HEAD_DIM
    rows = PAIRS * WINDOW
    r = lax.broadcasted_iota(jnp.int32, (rows, 2 * WINDOW), 0) % WINDOW
    c = lax.broadcasted_iota(jnp.int32, (rows, 2 * WINDOW), 1)
    visible = (c > r) & (c <= r + WINDOW) & ((i > 0) | (c >= WINDOW))
    chunk = lax.broadcasted_iota(jnp.int32, (rows, 1), 0) // WINDOW

    def sink_column(first_head):
        col = jnp.full((rows, 1), sink_ref[first_head], F32)
        for t in range(1, PAIRS):
            col = jnp.where(chunk == t, sink_ref[first_head + 2 * t], col)
        return col

    for pair in range(N_KV_HEADS // 2):
        k_pair = kk[:, pair * LANES:(pair + 1) * LANES]
        v_pair = vv[:, pair * LANES:(pair + 1) * LANES]
        k_swap = pltpu.roll(k_pair, HEAD_DIM, 1)
        v_swap = pltpu.roll(v_pair, HEAD_DIM, 1)
        for second in range(2):
            g = 2 * pair + second
            k_in_low, k_in_high = (k_swap, k_pair) if second else (k_pair, k_swap)
            v_in_low, v_in_high = (v_swap, v_pair) if second else (v_pair, v_swap)
            x = jnp.concatenate(
                [q_ref[:, (g * PAIRS + t) * LANES:(g * PAIRS + t + 1) * LANES] for t in range(PAIRS)], axis=0)

            def half(k_pad, v_pad, sink):
                s = lax.dot_general(x, k_pad.astype(BF16), (((1,), (1,)), ((), ())),
                                    preferred_element_type=F32)
                s = jnp.where(visible, s, NEG_INF)
                mx = jnp.maximum(jnp.max(s, axis=-1, keepdims=True), sink)
                e = jnp.exp2(s - mx).astype(BF16)
                return _dot(e, v_pad.astype(BF16)), jnp.exp2(sink - mx)

            a, sink_a = half(jnp.where(low, k_in_low, 0.0), jnp.where(low, v_in_low, 1.0),
                             sink_column(g * GROUP))
            b, sink_b = half(jnp.where(low, 0.0, k_in_high), jnp.where(low, 1.0, v_in_high),
                             sink_column(g * GROUP + 1))
            den = pltpu.roll(jnp.where(low, b, a), HEAD_DIM, 1) + jnp.where(low, sink_a, sink_b)
            out = (jnp.where(low, a, b) / den).astype(BF16)
            for t in range(PAIRS):
                o_ref[:, (g * PAIRS + t) * LANES:(g * PAIRS + t + 1) * LANES] = out[t * WINDOW:(t + 1) * WINDOW]


def _cast_chunk_rows(rows, steps):
    assert rows % 16 == 0
    units = rows // 16
    n = max(d for d in range(1, steps + 1) if units % d == 0)
    return rows // n


def _attention(sinks, q, kv, attn_s, weights):
    nb = SEQ // WINDOW
    cur = lambda i: jnp.minimum(i, nb - 1)
    prev = lambda i: jnp.clip(i - 1, 0, nb - 1)
    w_specs = []
    for w in weights:
        chunk = _cast_chunk_rows(w.shape[0], nb)
        w_specs.append(pl.BlockSpec((chunk, w.shape[1]),
                                    lambda i, n=w.shape[0] // chunk: (jnp.minimum(i, n - 1), 0)))
    outs = pl.pallas_call(
        _attn_kernel,
        grid=(nb + 1,),
        in_specs=[
            pl.BlockSpec(memory_space=pltpu.SMEM),
            pl.BlockSpec((WINDOW, D_MODEL), lambda i: (cur(i), 0)),
            pl.BlockSpec((WINDOW, KV_DIM), lambda i: (prev(i), 0)),
            pl.BlockSpec((WINDOW, KV_DIM), lambda i: (cur(i), 0)),
            pl.BlockSpec((WINDOW, KV_DIM), lambda i: (prev(i), 1)),
            pl.BlockSpec((WINDOW, KV_DIM), lambda i: (cur(i), 1)),
            pl.BlockSpec((N_SAMPLE, D_MODEL), lambda i: (0, 0)),
        ] + w_specs,
        out_specs=[pl.BlockSpec((WINDOW, D_MODEL), lambda i: (i, 0))] + w_specs,
        out_shape=[jax.ShapeDtypeStruct((ROWS, D_MODEL), BF16)]
        + [jax.ShapeDtypeStruct(w.shape, BF16) for w in weights],
        compiler_params=_params(1),
        name="attn_prompt",
    )(sinks, q, kv, kv, kv, kv, attn_s, *weights)
    return outs[0], outs[1:]


SEQS_PER_STEP = 8


def _attn_sample_kernel(sink_ref, q_ref, kvn_ref, ck_ref, cv_ref, o_ref, nk_ref, nv_ref):
    head = lax.broadcasted_iota(jnp.int32, (N_HEADS, KV_DIM), 0) // GROUP
    col_group = lax.broadcasted_iota(jnp.int32, (N_HEADS, KV_DIM), 1) // HEAD_DIM
    own = head == col_group
    newest = lax.broadcasted_iota(jnp.int32, (1, WINDOW), 1) == WINDOW - 1
    scores = []
    for b in range(SEQS_PER_STEP):
        nk_ref[b] = jnp.where(newest, kvn_ref[0, 0:KV_DIM, b:b + 1], pltpu.roll(ck_ref[b], WINDOW - 1, 1))
        nv_ref[b] = jnp.where(newest, kvn_ref[0, KV_DIM:2 * KV_DIM, b:b + 1],
                              pltpu.roll(cv_ref[b], WINDOW - 1, 1))
        qb = q_ref[b]
        qe = jnp.where(own, jnp.concatenate([qb] * N_KV_HEADS, axis=1), jnp.zeros((), BF16))
        scores.append(_dot(qe, nk_ref[b].astype(BF16)))
    s = jnp.concatenate(scores, axis=0)
    sink = jnp.concatenate([sink_ref[...]] * SEQS_PER_STEP, axis=0)
    mx = jnp.maximum(jnp.max(s, axis=-1, keepdims=True), sink)
    e = jnp.exp2(s - mx)
    inv = 1.0 / (jnp.sum(e, axis=-1, keepdims=True) + jnp.exp2(sink - mx))
    e = e.astype(BF16)
    for b in range(SEQS_PER_STEP):
        rows = slice(b * N_HEADS, (b + 1) * N_HEADS)
        of = lax.dot_general(e[rows], nv_ref[b].astype(BF16), (((1,), (1,)), ((), ())),
                             preferred_element_type=F32)
        of = jnp.where(own, of, 0.0)
        o = of[:, 0:HEAD_DIM]
        for g in range(1, N_KV_HEADS):
            o = o + of[:, g * HEAD_DIM:(g + 1) * HEAD_DIM]
        o_ref[b] = (o * inv[rows]).astype(BF16)


def _attention_sample(sink_col, q3, kvn_t, ck, cv):
    nb = N_SAMPLE // SEQS_PER_STEP
    cache_spec = pl.BlockSpec((SEQS_PER_STEP, KV_DIM, WINDOW), lambda i: (i, 0, 0))
    return pl.pallas_call(
        _attn_sample_kernel,
        grid=(nb,),
        in_specs=[
            pl.BlockSpec((N_HEADS, 1), lambda i: (0, 0)),
            pl.BlockSpec((SEQS_PER_STEP, N_HEADS, HEAD_DIM), lambda i: (i, 0, 0)),
            pl.BlockSpec((1, 2 * KV_DIM, SEQS_PER_STEP), lambda i: (i, 0, 0)),
            cache_spec, cache_spec,
        ],
        out_specs=[
            pl.BlockSpec((SEQS_PER_STEP, N_HEADS, HEAD_DIM), lambda i: (i, 0, 0)),
            cache_spec, cache_spec,
        ],
        out_shape=[
            jax.ShapeDtypeStruct((N_SAMPLE, N_HEADS, HEAD_DIM), BF16),
            jax.ShapeDtypeStruct((N_SAMPLE, KV_DIM, WINDOW), F32),
            jax.ShapeDtypeStruct((N_SAMPLE, KV_DIM, WINDOW), F32),
        ],
        compiler_params=_params(1),
        name="attn_sample",
    )(sink_col, q3, kvn_t, ck, cv)


POOL_PAD = 16


def _pool_kernel(u_ref, sp_ref, wmix_ref, scale_ref, p_ref, spn_ref, ext):
    m = pl.program_id(0)

    @pl.when(m == 0)
    def _():
        ext[0:POOL_PAD, :] = jnp.zeros((POOL_PAD, POOL_WIDTH), F32)

    ext[POOL_PAD:POOL_PAD + TM, :] = u_ref[...]
    pos = m * TM + lax.broadcasted_iota(jnp.int32, (TM, 1), 0)
    for g, w in enumerate(POOL_WINDOWS):
        cols = slice(g * POOL_GROUP_DIM, (g + 1) * POOL_GROUP_DIM)
        tot = ext[POOL_PAD:POOL_PAD + TM, cols]
        for j in range(1, w):
            tot = tot + ext[POOL_PAD - j:POOL_PAD - j + TM, cols]
        inv_cnt = 1.0 / jnp.minimum(w, pos + 1).astype(F32)
        d = tot * inv_cnt - u_ref[:, cols]
        y = _dot(d.astype(BF16), wmix_ref[g].astype(BF16))
        p_ref[:, cols] = (y * scale_ref[:, cols]).astype(BF16)
    ext[0:POOL_PAD, :] = ext[TM:TM + POOL_PAD, :]

    @pl.when(m == NT - 1)
    def _():
        for g, w in enumerate(POOL_WINDOWS):
            cols = slice(g * POOL_GROUP_DIM, (g + 1) * POOL_GROUP_DIM)
            us = u_ref[SPLIT:TM, cols]
            tot = us
            for j in range(1, w):
                tot = tot + sp_ref[POOL_HIST - j, :, cols]
            d = tot * (1.0 / w) - us
            y = _dot(d.astype(BF16), wmix_ref[g].astype(BF16))
            p_ref[SPLIT:TM, cols] = (y * scale_ref[:, cols]).astype(BF16)
        spn_ref[0:POOL_HIST - 1] = sp_ref[1:POOL_HIST]
        spn_ref[POOL_HIST - 1] = u_ref[SPLIT:TM, :]


def _pool(u, sp_t, wmix, scale):
    full = lambda shape: pl.BlockSpec(shape, lambda m: (0,) * len(shape))
    return pl.pallas_call(
        _pool_kernel,
        grid=(NT,),
        in_specs=[
            pl.BlockSpec((TM, POOL_WIDTH), lambda m: (m, 0)),
            full((POOL_HIST, N_SAMPLE, POOL_WIDTH)),
            full((len(POOL_WINDOWS), POOL_GROUP_DIM, POOL_GROUP_DIM)),
            full((1, POOL_WIDTH)),
        ],
        out_specs=[pl.BlockSpec((TM, POOL_WIDTH), lambda m: (m, 0)),
                   full((POOL_HIST, N_SAMPLE, POOL_WIDTH))],
        out_shape=[jax.ShapeDtypeStruct((ROWS, POOL_WIDTH), BF16),
                   jax.ShapeDtypeStruct((POOL_HIST, N_SAMPLE, POOL_WIDTH), F32)],
        scratch_shapes=[pltpu.VMEM((POOL_PAD + TM, POOL_WIDTH), F32)],
        compiler_params=_params(1),
        name="pool_mix",
    )(u, sp_t, wmix, scale)


_NC = D_MODEL // TN


TMH = TM // 2
NTH = ROWS // TMH
SPLIT_H = SEQ - (NTH - 1) * TMH
assert NTH * TMH == ROWS and SPLIT_H + N_SAMPLE == TMH and SPLIT_H % 8 == 0


def _resident(shape):
    return pl.BlockSpec(shape, lambda m: (0,) * len(shape), pipeline_mode=pl.Buffered(1))


def _merge_kernel(pool_ref, attn_ref, g_ref, wp_ref, wa_ref, o_ref):
    for c in range(_NC):
        cols = slice(c * TN, (c + 1) * TN)
        a = _dot(pool_ref[...], wp_ref[:, cols])
        b = _dot(attn_ref[...], wa_ref[:, cols])
        gate_pool = g_ref[:, cols].astype(F32)
        gate_attn = g_ref[:, D_MODEL + c * TN:D_MODEL + (c + 1) * TN].astype(F32)
        o_ref[:, cols] = (gate_pool * a + gate_attn * b).astype(BF16)


def _merge(pooled, attn, gates, wp, wa):
    row = lambda width: pl.BlockSpec((TMH, width), lambda m: (m, 0))
    return pl.pallas_call(
        _merge_kernel,
        grid=(NTH,),
        in_specs=[row(POOL_WIDTH), row(D_MODEL), row(2 * D_MODEL),
                  _resident((POOL_WIDTH, D_MODEL)), _resident((D_MODEL, D_MODEL))],
        out_specs=row(D_MODEL),
        out_shape=jax.ShapeDtypeStruct((ROWS, D_MODEL), BF16),
        compiler_params=_params(1),
        name="branch_merge",
    )(pooled, attn, gates, wp, wa)


def _out_ln_kernel(mrg_ref, wo_ref, x_ref, xs_ref, lng_ref, lnb_ref, x1_ref, x1b_ref):
    m = pl.program_id(0)
    for c in range(_NC):
        cols = slice(c * TN, (c + 1) * TN)
        x1_ref[:, cols] = _dot(mrg_ref[...], wo_ref[:, cols])

    def finish(z):
        x1 = _layer_norm(z, lng_ref[...], lnb_ref[...])
        x1_ref[...] = x1
        x1b_ref[...] = x1.astype(BF16)

    @pl.when(m < NTH - 1)
    def _():
        finish(x1_ref[...] + ALPHA * x_ref[...])

    @pl.when(m == NTH - 1)
    def _():
        finish(jnp.concatenate([x1_ref[0:SPLIT_H, :] + ALPHA * x_ref[0:SPLIT_H, :],
                                x1_ref[SPLIT_H:TMH, :] + ALPHA * xs_ref[...]], axis=0))


def _out_ln(merged, wo, x, xs, ln_g, ln_b):
    row = pl.BlockSpec((TMH, D_MODEL), lambda m: (m, 0))
    return pl.pallas_call(
        _out_ln_kernel,
        grid=(NTH,),
        in_specs=[row, _resident((D_MODEL, D_MODEL)), row, _resident((N_SAMPLE, D_MODEL)),
                  _resident((1, D_MODEL)), _resident((1, D_MODEL))],
        out_specs=[row, row],
        out_shape=[jax.ShapeDtypeStruct((ROWS, D_MODEL), F32), jax.ShapeDtypeStruct((ROWS, D_MODEL), BF16)],
        compiler_params=_params(1),
        name="out_proj_ln1",
    )(merged, wo, x, xs, ln_g, ln_b)


TF = 512
NF = D_FF // TF
CARRY = 8


def _ffn_kernel(x1b_ref, x1_hbm, sc_ref, wg_ref, wu_ref, wd_ref, cw_ref, cb_ref, lng_ref, lnb_ref,
                o_ref, os_ref, glast_ref, scn_ref,
                h0, h1, gext, carry, sem):
    m = pl.program_id(0)
    f = pl.program_id(1)
    last_m = m == NT - 1
    h_slots = (h0, h1)
    x1_copy = pltpu.make_async_copy(x1_hbm.at[pl.ds(m * TM, TM), :], o_ref, sem)

    def up_and_gate(slot, with_down):
        halves = ((0, HALF), (HALF, TM))
        gext[0:CARRY, :] = carry[f]
        for lo, hi in halves:
            gext[CARRY + lo:CARRY + hi, :] = _dot(x1b_ref[lo:hi, :], wg_ref[...])
        ups = [_dot(x1b_ref[lo:hi, :], wu_ref[...]) for lo, hi in halves]
        if with_down:
            o_ref[...] += _dot(h_slots[1 - slot][...], wd_ref[...])
        w0 = cw_ref[0:1, :]
        w1 = cw_ref[1:2, :]
        w2 = cw_ref[2:3, :]
        cb = cb_ref[...]

        def conv(lo, hi, older=None, newer=None):
            older = gext[CARRY - 2 + lo:CARRY - 2 + hi, :] if older is None else older
            newer = gext[CARRY - 1 + lo:CARRY - 1 + hi, :] if newer is None else newer
            return cb + w0 * older + w1 * newer + w2 * gext[CARRY + lo:CARRY + hi, :]

        for (lo, hi), up in zip(halves, ups):
            if hi <= SPLIT:
                gc = conv(lo, hi)
            else:
                older_s = jnp.where(last_m, sc_ref[:, 0, :], gext[CARRY - 2 + SPLIT:CARRY - 2 + TM, :])
                newer_s = jnp.where(last_m, sc_ref[:, 1, :], gext[CARRY - 1 + SPLIT:CARRY - 1 + TM, :])
                gc = jnp.concatenate([conv(lo, SPLIT), conv(SPLIT, TM, older_s, newer_s)], axis=0)
            h_slots[slot][lo:hi, :] = (jax.nn.gelu(gc, approximate=True) * up).astype(BF16)
        carry[f] = gext[TM:CARRY + TM, :]

        @pl.when(last_m)
        def _():
            glast_ref[...] = gext[SPLIT:CARRY + SPLIT, :]
            scn_ref[:, 0, :] = sc_ref[:, 1, :]
            scn_ref[:, 1, :] = gext[CARRY + SPLIT:CARRY + TM, :]

    @pl.when(f == 0)
    def _():
        @pl.when(m == 0)
        def _():
            carry[...] = jnp.zeros_like(carry)

        x1_copy.start()
        up_and_gate(0, False)
        x1_copy.wait()
        o_ref[...] = ALPHA * o_ref[...]

    for parity in range(2):
        @pl.when((f > 0) & (f < NF) & (f % 2 == parity))
        def _():
            up_and_gate(parity, True)

    @pl.when(f == NF)
    def _():
        acc = o_ref[...] + _dot(h_slots[(NF - 1) % 2][...], wd_ref[...])
        y = _layer_norm(acc, lng_ref[...], lnb_ref[...])
        o_ref[...] = y

        @pl.when(last_m)
        def _():
            os_ref[...] = y[SPLIT:TM, :]


def _ffn(x1b, x1, sc, wg, wu, wd, conv_w, conv_b, ln_g, ln_b):
    full = lambda shape: pl.BlockSpec(shape, lambda m, f: (0, 0))
    g_tile = lambda f: jnp.minimum(f, NF - 1)
    d_tile = lambda f: jnp.maximum(f - 1, 0)
    only_last = lambda m, f: jnp.where(m == NT - 1, g_tile(f), 0)
    row = pl.BlockSpec((TM, D_MODEL), lambda m, f: (m, 0))
    return pl.pallas_call(
        _ffn_kernel,
        grid=(NT, NF + 1),
        in_specs=[
            row,
            pl.BlockSpec(memory_space=pl.ANY),
            pl.BlockSpec((N_SAMPLE, CONV_W - 1, TF), lambda m, f: (0, 0, only_last(m, f))),
            pl.BlockSpec((D_MODEL, TF), lambda m, f: (0, g_tile(f))),
            pl.BlockSpec((D_MODEL, TF), lambda m, f: (0, g_tile(f))),
            pl.BlockSpec((TF, D_MODEL), lambda m, f: (d_tile(f), 0)),
            pl.BlockSpec((CONV_W, TF), lambda m, f: (0, g_tile(f))),
            pl.BlockSpec((1, TF), lambda m, f: (0, g_tile(f))),
            full((1, D_MODEL)), full((1, D_MODEL)),
        ],
        out_specs=[
            row,
            full((N_SAMPLE, D_MODEL)),
            pl.BlockSpec((CARRY, TF), lambda m, f: (0, only_last(m, f))),
            pl.BlockSpec((N_SAMPLE, CONV_W - 1, TF), lambda m, f: (0, 0, only_last(m, f))),
        ],
        out_shape=[
            jax.ShapeDtypeStruct((SEQ, D_MODEL), F32),
            jax.ShapeDtypeStruct((N_SAMPLE, D_MODEL), F32),
            jax.ShapeDtypeStruct((CARRY, D_FF), F32),
            jax.ShapeDtypeStruct((N_SAMPLE, CONV_W - 1, D_FF), F32),
        ],
        scratch_shapes=[
            pltpu.VMEM((TM, TF), BF16), pltpu.VMEM((TM, TF), BF16),
            pltpu.VMEM((CARRY + TM, TF), F32),
            pltpu.VMEM((NF, CARRY, TF), F32),
            pltpu.SemaphoreType.DMA(()),
        ],
        compiler_params=_params(2),
        name="convffn_ln2",
    )(x1b, x1, sc, wg, wu, wd, conv_w, conv_b, ln_g, ln_b)


def _rope_tables(pos):
    half = HEAD_DIM // 2
    inv = ROPE_THETA ** (-jnp.arange(half, dtype=F32) / half)
    lane = jnp.arange(LANES)
    inv_lanes = inv[lane % half]
    sign = jnp.where((lane % HEAD_DIM) < half, -1.0, 1.0).astype(F32)
    ang = pos.astype(F32)[:, None] * inv_lanes[None, :]
    return jnp.cos(ang), jnp.sin(ang) * sign[None, :]


def kernel(x_prompt, x_sample, cache_k, cache_v, state_pool, state_conv, w_in, attn_sinks, w_pool_mix,
           pool_scale, w_attn_branch, w_pool_branch, w_out, ln1_g, ln1_b, w_up, w_gate, conv_w, conv_b,
           w_down, ln2_g, ln2_b):
    x = x_prompt.reshape(SEQ, D_MODEL)
    xs = x_sample.reshape(N_SAMPLE, D_MODEL)
    pos = jnp.concatenate([jnp.arange(SEQ), jnp.full((N_SAMPLE,), PAST_LEN)])
    cos, sin = _rope_tables(pos)

    q, kv, u, gates = _in_proj(x, xs, w_in[0], cos, sin)

    to_dsw = lambda c: jnp.transpose(c[0], (0, 2, 3, 1)).reshape(N_SAMPLE, KV_DIM, WINDOW)
    steps = N_SAMPLE // SEQS_PER_STEP
    kvn_t = jnp.transpose(kv[SEQ:].reshape(steps, SEQS_PER_STEP, 2 * KV_DIM), (0, 2, 1))
    sinks = attn_sinks[0] * LOG2E
    attn_s, new_k, new_v = _attention_sample(
        sinks.reshape(N_HEADS, 1), q[SEQ:].reshape(N_SAMPLE, N_HEADS, HEAD_DIM), kvn_t,
        to_dsw(cache_k), to_dsw(cache_v))
    attn, (wp_b, wa_b, wo_b, wg_b, wu_b, wd_b) = _attention(
        sinks, q, kv, attn_s.reshape(N_SAMPLE, D_MODEL),
        [w_pool_branch[0], w_attn_branch[0], w_out[0], w_gate[0], w_up[0], w_down[0]])

    state_pool_t = jnp.transpose(state_pool[0], (1, 0, 2))
    pooled, new_pool_t = _pool(u, state_pool_t, w_pool_mix[0], pool_scale)

    merged = _merge(pooled, attn, gates, wp_b, wa_b)
    x1, x1b = _out_ln(merged, wo_b, x, xs, ln1_g, ln1_b)

    y, y_s, g_last, new_conv_s = _ffn(x1b, x1, state_conv[0], wg_b, wu_b, wd_b,
                                      conv_w[0], conv_b, ln2_g, ln2_b)

    from_dsw = lambda c: jnp.transpose(
        c.reshape(N_SAMPLE, N_KV_HEADS, HEAD_DIM, WINDOW), (0, 3, 1, 2))[None]
    return (
        y.reshape(1, SEQ, D_MODEL),
        y_s.reshape(N_SAMPLE, 1, D_MODEL),
        kv[SEQ - WINDOW:SEQ, :KV_DIM].reshape(1, 1, WINDOW, N_KV_HEADS, HEAD_DIM),
        kv[SEQ - WINDOW:SEQ, KV_DIM:].reshape(1, 1, WINDOW, N_KV_HEADS, HEAD_DIM),
        u[SEQ - POOL_HIST:SEQ].reshape(1, 1, POOL_HIST, POOL_WIDTH),
        g_last[CARRY - (CONV_W - 1):].reshape(1, 1, CONV_W - 1, D_FF),
        from_dsw(new_k),
        from_dsw(new_v),
        jnp.transpose(new_pool_t, (1, 0, 2))[None],
        new_conv_s[None],
    )
```

```python
import jax
import jax.numpy as jnp
from jax import lax
from jax.experimental import pallas as pl
from jax.experimental.pallas import tpu as pltpu

F32 = jnp.float32
BF16 = jnp.bfloat16

D_MODEL = 2048
SEQ = 8192
N_SAMPLE = 128
PAST_LEN = 8192
HEAD_DIM = 64
N_HEADS = 32
N_KV_HEADS = 4
GROUP = N_HEADS // N_KV_HEADS
KV_DIM = N_KV_HEADS * HEAD_DIM
WINDOW = 128
ROPE_THETA = 10000.0
POOL_WINDOWS = (2, 4, 8, 16)
POOL_WIDTH = 1024
POOL_GROUP_DIM = 256
POOL_HIST = 15
D_FF = 5632
CONV_W = 3
IN_DIM = 7680
LN_EPS = 1e-5
NEG_INF = -1e30
ALPHA = 2.0 ** 0.25
LOG2E = 1.4426950408889634

LANES = 128
VMEM_LIMIT = 60000 * 1024

ROWS = SEQ + N_SAMPLE
TM = 1040
NT = ROWS // TM
SPLIT = SEQ - (NT - 1) * TM
HALF = 528
TN = 512
assert NT * TM == ROWS and SPLIT + N_SAMPLE == TM and SPLIT % 16 == 0 and HALF % 16 == 0


def _params(ndim, flags=None):
    return pltpu.CompilerParams(dimension_semantics=("arbitrary",) * ndim,
                                vmem_limit_bytes=VMEM_LIMIT, flags=flags)


def _dot(a, b):
    return jnp.dot(a, b, preferred_element_type=F32)


def _layer_norm(z, g, b):
    mu = jnp.mean(z, axis=-1, keepdims=True)
    d = z - mu
    var = jnp.mean(d * d, axis=-1, keepdims=True)
    return d * lax.rsqrt(var + LN_EPS) * g + b


def _rope(x, cos, sin_signed):
    lane = lax.broadcasted_iota(jnp.int32, (1, LANES), 1)
    low_half = (lane % HEAD_DIM) < (HEAD_DIM // 2)
    outs = []
    for c in range(x.shape[1] // LANES):
        xc = x[:, c * LANES:(c + 1) * LANES]
        up = pltpu.roll(xc, LANES - HEAD_DIM // 2, 1)
        down = pltpu.roll(xc, HEAD_DIM // 2, 1)
        outs.append(xc * cos + jnp.where(low_half, up, down) * sin_signed)
    return jnp.concatenate(outs, axis=1)


_NQ = D_MODEL // TN
_N_KV = _NQ
_N_U0 = _N_KV + 1
_N_G0 = _N_U0 + POOL_WIDTH // TN


def _in_proj_kernel(x_ref, xs_ref, w_ref, cos_ref, sin_ref, q_ref, kv_ref, u_ref, g_ref, xb):
    m = pl.program_id(0)
    n = pl.program_id(1)

    @pl.when(n == 0)
    def _():
        @pl.when(m < NT - 1)
        def _():
            xb[...] = x_ref[...].astype(BF16)

        @pl.when(m == NT - 1)
        def _():
            xb[0:SPLIT, :] = x_ref[0:SPLIT, :].astype(BF16)
            xb[SPLIT:TM, :] = xs_ref[...].astype(BF16)

    wb = w_ref[...].astype(BF16)
    halves = ((0, HALF), (HALF, TM))
    thirds = ((0, 352), (352, 704), (704, TM))

    def project(epilogue, parts):
        accs = [_dot(xb[lo:hi, :], wb) for lo, hi in parts]
        for (lo, hi), acc in zip(parts, accs):
            epilogue(acc, lo, hi)

    def q_epilogue(acc, lo, hi):
        roped = _rope(acc, cos_ref[lo:hi, :], sin_ref[lo:hi, :])
        q_ref[lo:hi, :] = (roped * (HEAD_DIM ** -0.5 * LOG2E)).astype(BF16)

    def kv_epilogue(acc, lo, hi):
        kv_ref[lo:hi, :KV_DIM] = _rope(acc[:, :KV_DIM], cos_ref[lo:hi, :], sin_ref[lo:hi, :])
        kv_ref[lo:hi, KV_DIM:] = acc[:, KV_DIM:]

    def u_epilogue(acc, lo, hi):
        u_ref[lo:hi, :] = acc

    def gate_epilogue(acc, lo, hi):
        g_ref[lo:hi, :] = jax.nn.sigmoid(acc).astype(BF16)

    pl.when(n < _NQ)(lambda: project(q_epilogue, thirds))
    pl.when(n == _N_KV)(lambda: project(kv_epilogue, halves))
    pl.when((n >= _N_U0) & (n < _N_G0))(lambda: project(u_epilogue, halves))
    pl.when(n >= _N_G0)(lambda: project(gate_epilogue, thirds))


def _in_proj(x, xs, w_in, cos, sin):
    nn = IN_DIM // TN
    ng = 2 * D_MODEL // TN

    def clamp(n, lo, hi):
        return jnp.clip(n - lo, 0, hi - lo)

    return pl.pallas_call(
        _in_proj_kernel,
        grid=(NT, nn),
        in_specs=[
            pl.BlockSpec((TM, D_MODEL), lambda m, n: (m, 0)),
            pl.BlockSpec((N_SAMPLE, D_MODEL), lambda m, n: (0, 0)),
            pl.BlockSpec((D_MODEL, TN), lambda m, n: (0, n)),
            pl.BlockSpec((TM, LANES), lambda m, n: (m, 0)),
            pl.BlockSpec((TM, LANES), lambda m, n: (m, 0)),
        ],
        out_specs=[
            pl.BlockSpec((TM, TN), lambda m, n: (m, clamp(n, 0, _NQ - 1))),
            pl.BlockSpec((TM, TN), lambda m, n: (m, 0)),
            pl.BlockSpec((TM, TN), lambda m, n: (m, clamp(n, _N_U0, _N_G0 - 1))),
            pl.BlockSpec((TM, TN), lambda m, n: (m, clamp(n, _N_G0, _N_G0 + ng - 1))),
        ],
        out_shape=[
            jax.ShapeDtypeStruct((ROWS, D_MODEL), BF16),
            jax.ShapeDtypeStruct((ROWS, 2 * KV_DIM), F32),
            jax.ShapeDtypeStruct((ROWS, POOL_WIDTH), F32),
            jax.ShapeDtypeStruct((ROWS, 2 * D_MODEL), BF16),
        ],
        scratch_shapes=[pltpu.VMEM((TM, D_MODEL), BF16)],
        compiler_params=_params(2),
        name="in_proj",
    )(x, xs, w_in, cos, sin)


PAIRS = GROUP // 2


def _attn_kernel(sink_ref, q_ref, kp_ref, kc_ref, vp_ref, vc_ref, os_ref, *refs):
    n_w = (len(refs) - 1) // 2
    w_refs, o_ref, wb_refs = refs[:n_w], refs[n_w], refs[n_w + 1:]
    i = pl.program_id(0)
    last = pl.num_programs(0) - 1

    @pl.when(i < last)
    def _():
        for w_ref, wb_ref in zip(w_refs, wb_refs):
            wb_ref[...] = w_ref[...].astype(BF16)
        _attn_block(i, sink_ref, q_ref, kp_ref, kc_ref, vp_ref, vc_ref, o_ref)

    @pl.when(i == last)
    def _():
        o_ref[...] = os_ref[...]


def _attn_block(i, sink_ref, q_ref, kp_ref, kc_ref, vp_ref, vc_ref, o_ref):
    kk =jnp.concatenate([kp_ref[...], kc_ref[...]], axis=0)
    vv = jnp.concatenate([vp_ref[...], vc_ref[...]], axis=0)
    low = lax.broadcasted_iota(jnp.int32, (1, LANES), 1) < HEAD_DIM
    rows = PAIRS * WINDOW
    r = lax.broadcasted_iota(jnp.int32, (rows, 2 * WINDOW), 0) % WINDOW
    c = lax.broadcasted_iota(jnp.int32, (rows, 2 * WINDOW), 1)
    visible = (c > r) & (c <= r + WINDOW) & ((i > 0) | (c >= WINDOW))
    chunk = lax.broadcasted_iota(jnp.int32, (rows, 1), 0) // WINDOW

    def sink_column(first_head):
        col = jnp.full((rows, 1), sink_ref[first_head], F32)
        for t in range(1, PAIRS):
            col = jnp.where(chunk == t, sink_ref[first_head + 2 * t], col)
        return col

    for pair in range(N_KV_HEADS // 2):
        k_pair = kk[:, pair * LANES:(pair + 1) * LANES]
        v_pair = vv[:, pair * LANES:(pair + 1) * LANES]
        k_swap = pltpu.roll(k_pair, HEAD_DIM, 1)
        v_swap = pltpu.roll(v_pair, HEAD_DIM, 1)
        for second in range(2):
            g = 2 * pair + second
            k_in_low, k_in_high = (k_swap, k_pair) if second else (k_pair, k_swap)
            v_in_low, v_in_high = (v_swap, v_pair) if second else (v_pair, v_swap)
            x = jnp.concatenate(
                [q_ref[:, (g * PAIRS + t) * LANES:(g * PAIRS + t + 1) * LANES] for t in range(PAIRS)], axis=0)

            def half(k_pad, v_pad, sink):
                s = lax.dot_general(x, k_pad.astype(BF16), (((1,), (1,)), ((), ())),
                                    preferred_element_type=F32)
                s = jnp.where(visible, s, NEG_INF)
                mx = jnp.maximum(jnp.max(s, axis=-1, keepdims=True), sink)
                e = jnp.exp2(s - mx).astype(BF16)
                return _dot(e, v_pad.astype(BF16)), jnp.exp2(sink - mx)

            a, sink_a = half(jnp.where(low, k_in_low, 0.0), jnp.where(low, v_in_low, 1.0),
                             sink_column(g * GROUP))
            b, sink_b = half(jnp.where(low, 0.0, k_in_high), jnp.where(low, 1.0, v_in_high),
                             sink_column(g * GROUP + 1))
            den = pltpu.roll(jnp.where(low, b, a), HEAD_DIM, 1) + jnp.where(low, sink_a, sink_b)
            out = (jnp.where(low, a, b) / den).astype(BF16)
            for t in range(PAIRS):
                o_ref[:, (g * PAIRS + t) * LANES:(g * PAIRS + t + 1) * LANES] = out[t * WINDOW:(t + 1) * WINDOW]


def _cast_chunk_rows(rows, steps):
    assert rows % 16 == 0
    units = rows // 16
    n = max(d for d in range(1, steps + 1) if units % d == 0)
    return rows // n


def _attention(sinks, q, kv, attn_s, weights):
    nb = SEQ // WINDOW
    cur = lambda i: jnp.minimum(i, nb - 1)
    prev = lambda i: jnp.clip(i - 1, 0, nb - 1)
    w_specs = []
    for w in weights:
        chunk = _cast_chunk_rows(w.shape[0], nb)
        w_specs.append(pl.BlockSpec((chunk, w.shape[1]),
                                    lambda i, n=w.shape[0] // chunk: (jnp.minimum(i, n - 1), 0)))
    outs = pl.pallas_call(
        _attn_kernel,
        grid=(nb + 1,),
        in_specs=[
            pl.BlockSpec(memory_space=pltpu.SMEM),
            pl.BlockSpec((WINDOW, D_MODEL), lambda i: (cur(i), 0)),
            pl.BlockSpec((WINDOW, KV_DIM), lambda i: (prev(i), 0)),
            pl.BlockSpec((WINDOW, KV_DIM), lambda i: (cur(i), 0)),
            pl.BlockSpec((WINDOW, KV_DIM), lambda i: (prev(i), 1)),
            pl.BlockSpec((WINDOW, KV_DIM), lambda i: (cur(i), 1)),
            pl.BlockSpec((N_SAMPLE, D_MODEL), lambda i: (0, 0)),
        ] + w_specs,
        out_specs=[pl.BlockSpec((WINDOW, D_MODEL), lambda i: (i, 0))] + w_specs,
        out_shape=[jax.ShapeDtypeStruct((ROWS, D_MODEL), BF16)]
        + [jax.ShapeDtypeStruct(w.shape, BF16) for w in weights],
        compiler_params=_params(1),
        name="attn_prompt",
    )(sinks, q, kv, kv, kv, kv, attn_s, *weights)
    return outs[0], outs[1:]


SEQS_PER_STEP = 8


def _attn_sample_kernel(sink_ref, q_ref, kvn_ref, ck_ref, cv_ref, o_ref, nk_ref, nv_ref):
    head = lax.broadcasted_iota(jnp.int32, (N_HEADS, KV_DIM), 0) // GROUP
    col_group = lax.broadcasted_iota(jnp.int32, (N_HEADS, KV_DIM), 1) // HEAD_DIM
    own = head == col_group
    newest = lax.broadcasted_iota(jnp.int32, (1, WINDOW), 1) == WINDOW - 1
    scores = []
    for b in range(SEQS_PER_STEP):
        nk_ref[b] = jnp.where(newest, kvn_ref[0, 0:KV_DIM, b:b + 1], pltpu.roll(ck_ref[b], WINDOW - 1, 1))
        nv_ref[b] = jnp.where(newest, kvn_ref[0, KV_DIM:2 * KV_DIM, b:b + 1],
                              pltpu.roll(cv_ref[b], WINDOW - 1, 1))
        qb = q_ref[b]
        qe = jnp.where(own, jnp.concatenate([qb] * N_KV_HEADS, axis=1), jnp.zeros((), BF16))
        scores.append(_dot(qe, nk_ref[b].astype(BF16)))
    s = jnp.concatenate(scores, axis=0)
    sink = jnp.concatenate([sink_ref[...]] * SEQS_PER_STEP, axis=0)
    mx = jnp.maximum(jnp.max(s, axis=-1, keepdims=True), sink)
    e = jnp.exp2(s - mx)
    inv = 1.0 / (jnp.sum(e, axis=-1, keepdims=True) + jnp.exp2(sink - mx))
    e = e.astype(BF16)
    for b in range(SEQS_PER_STEP):
        rows = slice(b * N_HEADS, (b + 1) * N_HEADS)
        of = lax.dot_general(e[rows], nv_ref[b].astype(BF16), (((1,), (1,)), ((), ())),
                             preferred_element_type=F32)
        of = jnp.where(own, of, 0.0)
        o = of[:, 0:HEAD_DIM]
        for g in range(1, N_KV_HEADS):
            o = o + of[:, g * HEAD_DIM:(g + 1) * HEAD_DIM]
        o_ref[b] = (o * inv[rows]).astype(BF16)


def _attention_sample(sink_col, q3, kvn_t, ck, cv):
    nb = N_SAMPLE // SEQS_PER_STEP
    cache_spec = pl.BlockSpec((SEQS_PER_STEP, KV_DIM, WINDOW), lambda i: (i, 0, 0))
    return pl.pallas_call(
        _attn_sample_kernel,
        grid=(nb,),
        in_specs=[
            pl.BlockSpec((N_HEADS, 1), lambda i: (0, 0)),
            pl.BlockSpec((SEQS_PER_STEP, N_HEADS, HEAD_DIM), lambda i: (i, 0, 0)),
            pl.BlockSpec((1, 2 * KV_DIM, SEQS_PER_STEP), lambda i: (i, 0, 0)),
            cache_spec, cache_spec,
        ],
        out_specs=[
            pl.BlockSpec((SEQS_PER_STEP, N_HEADS, HEAD_DIM), lambda i: (i, 0, 0)),
            cache_spec, cache_spec,
        ],
        out_shape=[
            jax.ShapeDtypeStruct((N_SAMPLE, N_HEADS, HEAD_DIM), BF16),
            jax.ShapeDtypeStruct((N_SAMPLE, KV_DIM, WINDOW), F32),
            jax.ShapeDtypeStruct((N_SAMPLE, KV_DIM, WINDOW), F32),
        ],
        compiler_params=_params(1),
        name="attn_sample",
    )(sink_col, q3, kvn_t, ck, cv)


POOL_PAD = 16


def _pool_kernel(u_ref, sp_ref, wmix_ref, scale_ref, p_ref, spn_ref, ext):
    m = pl.program_id(0)

    @pl.when(m == 0)
    def _():
        ext[0:POOL_PAD, :] = jnp.zeros((POOL_PAD, POOL_WIDTH), F32)

    ext[POOL_PAD:POOL_PAD + TM, :] = u_ref[...]
    pos = m * TM + lax.broadcasted_iota(jnp.int32, (TM, 1), 0)
    for g, w in enumerate(POOL_WINDOWS):
        cols = slice(g * POOL_GROUP_DIM, (g + 1) * POOL_GROUP_DIM)
        run = ext[:, cols]
        shift = 1
        while shift < w:
            run = run + pltpu.roll(run, shift, 0)
            shift *= 2
        tot = run[POOL_PAD:POOL_PAD + TM, :]
        inv_cnt = 1.0 / jnp.minimum(w, pos + 1).astype(F32)
        d = tot * inv_cnt - u_ref[:, cols]
        y = _dot(d.astype(BF16), wmix_ref[g].astype(BF16))
        p_ref[:, cols] = (y * scale_ref[:, cols]).astype(BF16)
    ext[0:POOL_PAD, :] = ext[TM:TM + POOL_PAD, :]

    @pl.when(m == NT - 1)
    def _():
        for g, w in enumerate(POOL_WINDOWS):
            cols = slice(g * POOL_GROUP_DIM, (g + 1) * POOL_GROUP_DIM)
            us = u_ref[SPLIT:TM, cols]
            tot = us
            for j in range(1, w):
                tot = tot + sp_ref[POOL_HIST - j, :, cols]
            d = tot * (1.0 / w) - us
            y = _dot(d.astype(BF16), wmix_ref[g].astype(BF16))
            p_ref[SPLIT:TM, cols] = (y * scale_ref[:, cols]).astype(BF16)
        spn_ref[0:POOL_HIST - 1] = sp_ref[1:POOL_HIST]
        spn_ref[POOL_HIST - 1] = u_ref[SPLIT:TM, :]


def _pool(u, sp_t, wmix, scale):
    full = lambda shape: pl.BlockSpec(shape, lambda m: (0,) * len(shape))
    return pl.pallas_call(
        _pool_kernel,
        grid=(NT,),
        in_specs=[
            pl.BlockSpec((TM, POOL_WIDTH), lambda m: (m, 0)),
            full((POOL_HIST, N_SAMPLE, POOL_WIDTH)),
            full((len(POOL_WINDOWS), POOL_GROUP_DIM, POOL_GROUP_DIM)),
            full((1, POOL_WIDTH)),
        ],
        out_specs=[pl.BlockSpec((TM, POOL_WIDTH), lambda m: (m, 0)),
                   full((POOL_HIST, N_SAMPLE, POOL_WIDTH))],
        out_shape=[jax.ShapeDtypeStruct((ROWS, POOL_WIDTH), BF16),
                   jax.ShapeDtypeStruct((POOL_HIST, N_SAMPLE, POOL_WIDTH), F32)],
        scratch_shapes=[pltpu.VMEM((POOL_PAD + TM, POOL_WIDTH), F32)],
        compiler_params=_params(1),
        name="pool_mix",
    )(u, sp_t, wmix, scale)


_NC = D_MODEL // TN


TMH = TM // 2
NTH = ROWS // TMH
SPLIT_H = SEQ - (NTH - 1) * TMH
assert NTH * TMH == ROWS and SPLIT_H + N_SAMPLE == TMH and SPLIT_H % 8 == 0


def _resident(shape):
    return pl.BlockSpec(shape, lambda m: (0,) * len(shape), pipeline_mode=pl.Buffered(1))


def _merge_kernel(pool_ref, attn_ref, g_ref, wp_ref, wa_ref, o_ref):
    for c in range(_NC):
        cols = slice(c * TN, (c + 1) * TN)
        a = _dot(pool_ref[...], wp_ref[:, cols])
        b = _dot(attn_ref[...], wa_ref[:, cols])
        gate_pool = g_ref[:, cols].astype(F32)
        gate_attn = g_ref[:, D_MODEL + c * TN:D_MODEL + (c + 1) * TN].astype(F32)
        o_ref[:, cols] = (gate_pool * a + gate_attn * b).astype(BF16)


def _merge(pooled, attn, gates, wp, wa):
    row = lambda width: pl.BlockSpec((TMH, width), lambda m: (m, 0))
    return pl.pallas_call(
        _merge_kernel,
        grid=(NTH,),
        in_specs=[row(POOL_WIDTH), row(D_MODEL), row(2 * D_MODEL),
                  _resident((POOL_WIDTH, D_MODEL)), _resident((D_MODEL, D_MODEL))],
        out_specs=row(D_MODEL),
        out_shape=jax.ShapeDtypeStruct((ROWS, D_MODEL), BF16),
        compiler_params=_params(1),
        name="branch_merge",
    )(pooled, attn, gates, wp, wa)


def _out_ln_kernel(mrg_ref, wo_ref, x_ref, xs_ref, lng_ref, lnb_ref, x1_ref, x1b_ref):
    m = pl.program_id(0)
    for c in range(_NC):
        cols = slice(c * TN, (c + 1) * TN)
        x1_ref[:, cols] = _dot(mrg_ref[...], wo_ref[:, cols])

    def finish(z):
        x1 = _layer_norm(z, lng_ref[...], lnb_ref[...])
        x1_ref[...] = ALPHA * x1
        x1b_ref[...] = x1.astype(BF16)

    @pl.when(m < NTH - 1)
    def _():
        finish(x1_ref[...] + ALPHA * x_ref[...])

    @pl.when(m == NTH - 1)
    def _():
        finish(jnp.concatenate([x1_ref[0:SPLIT_H, :] + ALPHA * x_ref[0:SPLIT_H, :],
                                x1_ref[SPLIT_H:TMH, :] + ALPHA * xs_ref[...]], axis=0))


def _out_ln(merged, wo, x, xs, ln_g, ln_b):
    row = pl.BlockSpec((TMH, D_MODEL), lambda m: (m, 0))
    return pl.pallas_call(
        _out_ln_kernel,
        grid=(NTH,),
        in_specs=[row, _resident((D_MODEL, D_MODEL)), row, _resident((N_SAMPLE, D_MODEL)),
                  _resident((1, D_MODEL)), _resident((1, D_MODEL))],
        out_specs=[row, row],
        out_shape=[jax.ShapeDtypeStruct((ROWS, D_MODEL), F32), jax.ShapeDtypeStruct((ROWS, D_MODEL), BF16)],
        compiler_params=_params(1),
        name="out_proj_ln1",
    )(merged, wo, x, xs, ln_g, ln_b)


TF = 512
NF = D_FF // TF
CARRY = 8


def _ffn_kernel(x1b_ref, x1_hbm, sc_ref, wg_ref, wu_ref, wd_ref, cw_ref, cb_ref, lng_ref, lnb_ref,
                o_ref, os_ref, glast_ref, scn_ref,
                h0, h1, gext, carry, sem):
    m = pl.program_id(0)
    f = pl.program_id(1)
    last_m = m == NT - 1
    h_slots = (h0, h1)
    x1_copy = pltpu.make_async_copy(x1_hbm.at[pl.ds(m * TM, TM), :], o_ref, sem)

    def up_and_gate(slot, with_down):
        halves = ((0, HALF), (HALF, TM))
        gext[0:CARRY, :] = carry[f]
        for lo, hi in halves:
            gext[CARRY + lo:CARRY + hi, :] = _dot(x1b_ref[lo:hi, :], wg_ref[...])
        ups = [_dot(x1b_ref[lo:hi, :], wu_ref[...]) for lo, hi in halves]
        if with_down:
            o_ref[...] += _dot(h_slots[1 - slot][...], wd_ref[...])
        w0 = cw_ref[0:1, :]
        w1 = cw_ref[1:2, :]
        w2 = cw_ref[2:3, :]
        cb = cb_ref[...]

        def conv(lo, hi, older=None, newer=None):
            older = gext[CARRY - 2 + lo:CARRY - 2 + hi, :] if older is None else older
            newer = gext[CARRY - 1 + lo:CARRY - 1 + hi, :] if newer is None else newer
            return cb + w0 * older + w1 * newer + w2 * gext[CARRY + lo:CARRY + hi, :]

        for (lo, hi), up in zip(halves, ups):
            if hi <= SPLIT:
                gc = conv(lo, hi)
            else:
                older_s = jnp.where(last_m, sc_ref[:, 0, :], gext[CARRY - 2 + SPLIT:CARRY - 2 + TM, :])
                newer_s = jnp.where(last_m, sc_ref[:, 1, :], gext[CARRY - 1 + SPLIT:CARRY - 1 + TM, :])
                gc = jnp.concatenate([conv(lo, SPLIT), conv(SPLIT, TM, older_s, newer_s)], axis=0)
            h_slots[slot][lo:hi, :] = (jax.nn.gelu(gc, approximate=True) * up).astype(BF16)
        carry[f] = gext[TM:CARRY + TM, :]

        @pl.when(last_m)
        def _():
            glast_ref[...] = gext[SPLIT:CARRY + SPLIT, :]
            scn_ref[:, 0, :] = sc_ref[:, 1, :]
            scn_ref[:, 1, :] = gext[CARRY + SPLIT:CARRY + TM, :]

    @pl.when(f == 0)
    def _():
        @pl.when(m == 0)
        def _():
            carry[...] = jnp.zeros_like(carry)

        x1_copy.start()
        up_and_gate(0, False)
        x1_copy.wait()

    for parity in range(2):
        @pl.when((f > 0) & (f < NF) & (f % 2 == parity))
        def _():
            up_and_gate(parity, True)

    @pl.when(f == NF)
    def _():
        h_last = h_slots[(NF - 1) % 2]
        parts = ((0, 352), (352, 704), (704, TM))
        accs = [o_ref[lo:hi, :] + _dot(h_last[lo:hi, :], wd_ref[...]) for lo, hi in parts]
        for (lo, hi), acc in zip(parts, accs):
            o_ref[lo:hi, :] = _layer_norm(acc, lng_ref[...], lnb_ref[...])

        @pl.when(last_m)
        def _():
            os_ref[...] = o_ref[SPLIT:TM, :]


def _ffn(x1b, x1, sc, wg, wu, wd, conv_w, conv_b, ln_g, ln_b):
    full = lambda shape: pl.BlockSpec(shape, lambda m, f: (0, 0))
    g_tile = lambda f: jnp.minimum(f, NF - 1)
    d_tile = lambda f: jnp.maximum(f - 1, 0)
    only_last = lambda m, f: jnp.where(m == NT - 1, g_tile(f), 0)
    row = pl.BlockSpec((TM, D_MODEL), lambda m, f: (m, 0))
    return pl.pallas_call(
        _ffn_kernel,
        grid=(NT, NF + 1),
        in_specs=[
            row,
            pl.BlockSpec(memory_space=pl.ANY),
            pl.BlockSpec((N_SAMPLE, CONV_W - 1, TF), lambda m, f: (0, 0, only_last(m, f))),
            pl.BlockSpec((D_MODEL, TF), lambda m, f: (0, g_tile(f))),
            pl.BlockSpec((D_MODEL, TF), lambda m, f: (0, g_tile(f))),
            pl.BlockSpec((TF, D_MODEL), lambda m, f: (d_tile(f), 0)),
            pl.BlockSpec((CONV_W, TF), lambda m, f: (0, g_tile(f))),
            pl.BlockSpec((1, TF), lambda m, f: (0, g_tile(f))),
            full((1, D_MODEL)), full((1, D_MODEL)),
        ],
        out_specs=[
            row,
            full((N_SAMPLE, D_MODEL)),
            pl.BlockSpec((CARRY, TF), lambda m, f: (0, only_last(m, f))),
            pl.BlockSpec((N_SAMPLE, CONV_W - 1, TF), lambda m, f: (0, 0, only_last(m, f))),
        ],
        out_shape=[
            jax.ShapeDtypeStruct((SEQ, D_MODEL), F32),
            jax.ShapeDtypeStruct((N_SAMPLE, D_MODEL), F32),
            jax.ShapeDtypeStruct((CARRY, D_FF), F32),
            jax.ShapeDtypeStruct((N_SAMPLE, CONV_W - 1, D_FF), F32),
        ],
        scratch_shapes=[
            pltpu.VMEM((TM, TF), BF16), pltpu.VMEM((TM, TF), BF16),
            pltpu.VMEM((CARRY + TM, TF), F32),
            pltpu.VMEM((NF, CARRY, TF), F32),
            pltpu.SemaphoreType.DMA(()),
        ],
        compiler_params=_params(2),
        name="convffn_ln2",
    )(x1b, x1, sc, wg, wu, wd, conv_w, conv_b, ln_g, ln_b)


def _rope_tables(pos):
    half = HEAD_DIM // 2
    inv = ROPE_THETA ** (-jnp.arange(half, dtype=F32) / half)
    lane = jnp.arange(LANES)
    inv_lanes = inv[lane % half]
    sign = jnp.where((lane % HEAD_DIM) < half, -1.0, 1.0).astype(F32)
    ang = pos.astype(F32)[:, None] * inv_lanes[None, :]
    return jnp.cos(ang), jnp.sin(ang) * sign[None, :]


def kernel(x_prompt, x_sample, cache_k, cache_v, state_pool, state_conv, w_in, attn_sinks, w_pool_mix,
           pool_scale, w_attn_branch, w_pool_branch, w_out, ln1_g, ln1_b, w_up, w_gate, conv_w, conv_b,
           w_down, ln2_g, ln2_b):
    x = x_prompt.reshape(SEQ, D_MODEL)
    xs = x_sample.reshape(N_SAMPLE, D_MODEL)
    pos = jnp.concatenate([jnp.arange(SEQ), jnp.full((N_SAMPLE,), PAST_LEN)])
    cos, sin = _rope_tables(pos)

    q, kv, u, gates = _in_proj(x, xs, w_in[0], cos, sin)

    to_dsw = lambda c: jnp.transpose(c[0], (0, 2, 3, 1)).reshape(N_SAMPLE, KV_DIM, WINDOW)
    steps = N_SAMPLE // SEQS_PER_STEP
    kvn_t = jnp.transpose(kv[SEQ:].reshape(steps, SEQS_PER_STEP, 2 * KV_DIM), (0, 2, 1))
    sinks = attn_sinks[0] * LOG2E
    attn_s, new_k, new_v = _attention_sample(
        sinks.reshape(N_HEADS, 1), q[SEQ:].reshape(N_SAMPLE, N_HEADS, HEAD_DIM), kvn_t,
        to_dsw(cache_k), to_dsw(cache_v))
    attn, (wp_b, wa_b, wo_b, wg_b, wu_b, wd_b) = _attention(
        sinks, q, kv, attn_s.reshape(N_SAMPLE, D_MODEL),
        [w_pool_branch[0], w_attn_branch[0], w_out[0], w_gate[0], w_up[0], w_down[0]])

    state_pool_t = jnp.transpose(state_pool[0], (1, 0, 2))
    pooled, new_pool_t = _pool(u, state_pool_t, w_pool_mix[0], pool_scale)

    merged = _merge(pooled, attn, gates, wp_b, wa_b)
    x1, x1b = _out_ln(merged, wo_b, x, xs, ln1_g, ln1_b)

    y, y_s, g_last, new_conv_s = _ffn(x1b, x1, state_conv[0], wg_b, wu_b, wd_b,
                                      conv_w[0], conv_b, ln2_g, ln2_b)

    from_dsw = lambda c: jnp.transpose(
        c.reshape(N_SAMPLE, N_KV_HEADS, HEAD_DIM, WINDOW), (0, 3, 1, 2))[None]
    return (
        y.reshape(1, SEQ, D_MODEL),
        y_s.reshape(N_SAMPLE, 1, D_MODEL),
        kv[SEQ - WINDOW:SEQ, :KV_DIM].reshape(1, 1, WINDOW, N_KV_HEADS, HEAD_DIM),
        kv[SEQ - WINDOW:SEQ, KV_DIM:].reshape(1, 1, WINDOW, N_KV_HEADS, HEAD_DIM),
        u[SEQ - POOL_HIST:SEQ].reshape(1, 1, POOL_HIST, POOL_WIDTH),
        g_last[CARRY - (CONV_W - 1):].reshape(1, 1, CONV_W - 1, D_FF),
        from_dsw(new_k),
        from_dsw(new_v),
        jnp.transpose(new_pool_t, (1, 0, 2))[None],
        new_conv_s[None],
    )
```

```python
import jax
import jax.numpy as jnp
from jax import lax
from jax.experimental import pallas as pl
from jax.experimental.pallas import tpu as pltpu

F32 = jnp.float32
BF16 = jnp.bfloat16

D_MODEL = 2048
SEQ = 8192
N_SAMPLE = 128
PAST_LEN = 8192
HEAD_DIM = 64
N_HEADS = 32
N_KV_HEADS = 4
GROUP = N_HEADS // N_KV_HEADS
KV_DIM = N_KV_HEADS * HEAD_DIM
WINDOW = 128
ROPE_THETA = 10000.0
POOL_WINDOWS = (2, 4, 8, 16)
POOL_WIDTH = 1024
POOL_GROUP_DIM = 256
POOL_HIST = 15
D_FF = 5632
CONV_W = 3
IN_DIM = 7680
LN_EPS = 1e-5
NEG_INF = -1e30
ALPHA = 2.0 ** 0.25
LOG2E = 1.4426950408889634

LANES = 128
VMEM_LIMIT = 60000 * 1024

ROWS = SEQ + N_SAMPLE
TM = 1040
NT = ROWS // TM
SPLIT = SEQ - (NT - 1) * TM
HALF = 528
TN = 512
assert NT * TM == ROWS and SPLIT + N_SAMPLE == TM and SPLIT % 16 == 0 and HALF % 16 == 0


def _params(ndim, flags=None):
    return pltpu.CompilerParams(dimension_semantics=("arbitrary",) * ndim,
                                vmem_limit_bytes=VMEM_LIMIT, flags=flags)


def _dot(a, b):
    return jnp.dot(a, b, preferred_element_type=F32)


def _layer_norm(z, g, b):
    mu = jnp.mean(z, axis=-1, keepdims=True)
    d = z - mu
    var = jnp.mean(d * d, axis=-1, keepdims=True)
    return d * lax.rsqrt(var + LN_EPS) * g + b


def _rope(x, cos, sin_signed):
    lane = lax.broadcasted_iota(jnp.int32, (1, LANES), 1)
    low_half = (lane % HEAD_DIM) < (HEAD_DIM // 2)
    outs = []
    for c in range(x.shape[1] // LANES):
        xc = x[:, c * LANES:(c + 1) * LANES]
        up = pltpu.roll(xc, LANES - HEAD_DIM // 2, 1)
        down = pltpu.roll(xc, HEAD_DIM // 2, 1)
        outs.append(xc * cos + jnp.where(low_half, up, down) * sin_signed)
    return jnp.concatenate(outs, axis=1)


_NQ = D_MODEL // TN
_N_KV = _NQ
_N_U0 = _N_KV + 1
_N_G0 = _N_U0 + POOL_WIDTH // TN


def _in_proj_kernel(x_ref, xs_ref, w_ref, cos_ref, sin_ref, q_ref, kv_ref, u_ref, xb):
    m = pl.program_id(0)
    n = pl.program_id(1)

    @pl.when(n == 0)
    def _():
        @pl.when(m < NT - 1)
        def _():
            xb[...] = x_ref[...].astype(BF16)

        @pl.when(m == NT - 1)
        def _():
            xb[0:SPLIT, :] = x_ref[0:SPLIT, :].astype(BF16)
            xb[SPLIT:TM, :] = xs_ref[...].astype(BF16)

    wb = w_ref[...].astype(BF16)
    halves = ((0, HALF), (HALF, TM))
    thirds = ((0, 352), (352, 704), (704, TM))

    def project(epilogue, parts):
        accs = [_dot(xb[lo:hi, :], wb) for lo, hi in parts]
        for (lo, hi), acc in zip(parts, accs):
            epilogue(acc, lo, hi)

    def q_epilogue(acc, lo, hi):
        roped = _rope(acc, cos_ref[lo:hi, :], sin_ref[lo:hi, :])
        q_ref[lo:hi, :] = (roped * (HEAD_DIM ** -0.5 * LOG2E)).astype(BF16)

    def kv_epilogue(acc, lo, hi):
        kv_ref[lo:hi, :KV_DIM] = _rope(acc[:, :KV_DIM], cos_ref[lo:hi, :], sin_ref[lo:hi, :])
        kv_ref[lo:hi, KV_DIM:] = acc[:, KV_DIM:]

    def u_epilogue(acc, lo, hi):
        u_ref[lo:hi, :] = acc

    pl.when(n < _NQ)(lambda: project(q_epilogue, thirds))
    pl.when(n == _N_KV)(lambda: project(kv_epilogue, halves))
    pl.when(n >= _N_U0)(lambda: project(u_epilogue, halves))


def _in_proj(x, xs, w_in, cos, sin):
    nn = _N_G0

    def clamp(n, lo, hi):
        return jnp.clip(n - lo, 0, hi - lo)

    return pl.pallas_call(
        _in_proj_kernel,
        grid=(NT, nn),
        in_specs=[
            pl.BlockSpec((TM, D_MODEL), lambda m, n: (m, 0)),
            pl.BlockSpec((N_SAMPLE, D_MODEL), lambda m, n: (0, 0)),
            pl.BlockSpec((D_MODEL, TN), lambda m, n: (0, n)),
            pl.BlockSpec((TM, LANES), lambda m, n: (m, 0)),
            pl.BlockSpec((TM, LANES), lambda m, n: (m, 0)),
        ],
        out_specs=[
            pl.BlockSpec((TM, TN), lambda m, n: (m, clamp(n, 0, _NQ - 1))),
            pl.BlockSpec((TM, TN), lambda m, n: (m, 0)),
            pl.BlockSpec((TM, TN), lambda m, n: (m, clamp(n, _N_U0, _N_G0 - 1))),
            pl.BlockSpec((TM, D_MODEL), lambda m, n: (m, 0)),
        ],
        out_shape=[
            jax.ShapeDtypeStruct((ROWS, D_MODEL), BF16),
            jax.ShapeDtypeStruct((ROWS, 2 * KV_DIM), F32),
            jax.ShapeDtypeStruct((ROWS, POOL_WIDTH), F32),
            jax.ShapeDtypeStruct((ROWS, D_MODEL), BF16),
        ],
        compiler_params=_params(2),
        name="in_proj",
    )(x, xs, w_in, cos, sin)


PAIRS = GROUP // 2


def _attn_kernel(sink_ref, q_ref, kp_ref, kc_ref, vp_ref, vc_ref, os_ref, *refs):
    n_w = (len(refs) - 1) // 2
    w_refs, o_ref, wb_refs = refs[:n_w], refs[n_w], refs[n_w + 1:]
    i = pl.program_id(0)
    last = pl.num_programs(0) - 1

    @pl.when(i < last)
    def _():
        for w_ref, wb_ref in zip(w_refs, wb_refs):
            wb_ref[...] = w_ref[...].astype(BF16)
        _attn_block(i, sink_ref, q_ref, kp_ref, kc_ref, vp_ref, vc_ref, o_ref)

    @pl.when(i == last)
    def _():
        o_ref[...] = os_ref[...]


def _attn_block(i, sink_ref, q_ref, kp_ref, kc_ref, vp_ref, vc_ref, o_ref):
    kk =jnp.concatenate([kp_ref[...], kc_ref[...]], axis=0)
    vv = jnp.concatenate([vp_ref[...], vc_ref[...]], axis=0)
    low = lax.broadcasted_iota(jnp.int32, (1, LANES), 1) < HEAD_DIM
    rows = PAIRS * WINDOW
    r = lax.broadcasted_iota(jnp.int32, (rows, 2 * WINDOW), 0) % WINDOW
    c = lax.broadcasted_iota(jnp.int32, (rows, 2 * WINDOW), 1)
    visible = (c > r) & (c <= r + WINDOW) & ((i > 0) | (c >= WINDOW))
    chunk = lax.broadcasted_iota(jnp.int32, (rows, 1), 0) // WINDOW

    def sink_column(first_head):
        col = jnp.full((rows, 1), sink_ref[first_head], F32)
        for t in range(1, PAIRS):
            col = jnp.where(chunk == t, sink_ref[first_head + 2 * t], col)
        return col

    for pair in range(N_KV_HEADS // 2):
        k_pair = kk[:, pair * LANES:(pair + 1) * LANES]
        v_pair = vv[:, pair * LANES:(pair + 1) * LANES]
        k_swap = pltpu.roll(k_pair, HEAD_DIM, 1)
        v_swap = pltpu.roll(v_pair, HEAD_DIM, 1)
        for second in range(2):
            g = 2 * pair + second
            k_in_low, k_in_high = (k_swap, k_pair) if second else (k_pair, k_swap)
            v_in_low, v_in_high = (v_swap, v_pair) if second else (v_pair, v_swap)
            x = jnp.concatenate(
                [q_ref[:, (g * PAIRS + t) * LANES:(g * PAIRS + t + 1) * LANES] for t in range(PAIRS)], axis=0)

            def half(k_pad, v_pad, sink):
                s = lax.dot_general(x, k_pad.astype(BF16), (((1,), (1,)), ((), ())),
                                    preferred_element_type=F32)
                s = jnp.where(visible, s, NEG_INF)
                mx = jnp.maximum(jnp.max(s, axis=-1, keepdims=True), sink)
                e = jnp.exp2(s - mx).astype(BF16)
                return _dot(e, v_pad.astype(BF16)), jnp.exp2(sink - mx)

            a, sink_a = half(jnp.where(low, k_in_low, 0.0), jnp.where(low, v_in_low, 1.0),
                             sink_column(g * GROUP))
            b, sink_b = half(jnp.where(low, 0.0, k_in_high), jnp.where(low, 1.0, v_in_high),
                             sink_column(g * GROUP + 1))
            den = pltpu.roll(jnp.where(low, b, a), HEAD_DIM, 1) + jnp.where(low, sink_a, sink_b)
            out = (jnp.where(low, a, b) / den).astype(BF16)
            for t in range(PAIRS):
                o_ref[:, (g * PAIRS + t) * LANES:(g * PAIRS + t + 1) * LANES] = out[t * WINDOW:(t + 1) * WINDOW]


def _cast_chunk_rows(rows, steps):
    assert rows % 16 == 0
    units = rows // 16
    n = max(d for d in range(1, steps + 1) if units % d == 0)
    return rows // n


def _attention(sinks, q, kv, attn_s, weights):
    nb = SEQ // WINDOW
    cur = lambda i: jnp.minimum(i, nb - 1)
    prev = lambda i: jnp.clip(i - 1, 0, nb - 1)
    w_in_specs, w_out_specs, w_out_shapes = [], [], []
    for w, col, width in weights:
        chunk = _cast_chunk_rows(w.shape[0], nb)
        n = w.shape[0] // chunk
        w_in_specs.append(pl.BlockSpec((chunk, width), lambda i, n=n, col=col: (jnp.minimum(i, n - 1), col)))
        w_out_specs.append(pl.BlockSpec((chunk, width), lambda i, n=n: (jnp.minimum(i, n - 1), 0)))
        w_out_shapes.append(jax.ShapeDtypeStruct((w.shape[0], width), BF16))
    outs = pl.pallas_call(
        _attn_kernel,
        grid=(nb + 1,),
        in_specs=[
            pl.BlockSpec(memory_space=pltpu.SMEM),
            pl.BlockSpec((WINDOW, D_MODEL), lambda i: (cur(i), 0)),
            pl.BlockSpec((WINDOW, KV_DIM), lambda i: (prev(i), 0)),
            pl.BlockSpec((WINDOW, KV_DIM), lambda i: (cur(i), 0)),
            pl.BlockSpec((WINDOW, KV_DIM), lambda i: (prev(i), 1)),
            pl.BlockSpec((WINDOW, KV_DIM), lambda i: (cur(i), 1)),
            pl.BlockSpec((N_SAMPLE, D_MODEL), lambda i: (0, 0)),
        ] + w_in_specs,
        out_specs=[pl.BlockSpec((WINDOW, D_MODEL), lambda i: (i, 0))] + w_out_specs,
        out_shape=[jax.ShapeDtypeStruct((ROWS, D_MODEL), BF16)] + w_out_shapes,
        compiler_params=_params(1),
        name="attn_prompt",
    )(sinks, q, kv, kv, kv, kv, attn_s, *[w for w, _, _ in weights])
    return outs[0], outs[1:]


SEQS_PER_STEP = 8


def _attn_sample_kernel(sink_ref, q_ref, kvn_ref, ck_ref, cv_ref, o_ref, nk_ref, nv_ref):
    head = lax.broadcasted_iota(jnp.int32, (N_HEADS, KV_DIM), 0) // GROUP
    col_group = lax.broadcasted_iota(jnp.int32, (N_HEADS, KV_DIM), 1) // HEAD_DIM
    own = head == col_group
    newest = lax.broadcasted_iota(jnp.int32, (1, WINDOW), 1) == WINDOW - 1
    scores = []
    for b in range(SEQS_PER_STEP):
        nk_ref[b] = jnp.where(newest, kvn_ref[0, 0:KV_DIM, b:b + 1], pltpu.roll(ck_ref[b], WINDOW - 1, 1))
        nv_ref[b] = jnp.where(newest, kvn_ref[0, KV_DIM:2 * KV_DIM, b:b + 1],
                              pltpu.roll(cv_ref[b], WINDOW - 1, 1))
        qb = q_ref[b]
        qe = jnp.where(own, jnp.concatenate([qb] * N_KV_HEADS, axis=1), jnp.zeros((), BF16))
        scores.append(_dot(qe, nk_ref[b].astype(BF16)))
    s = jnp.concatenate(scores, axis=0)
    sink = jnp.concatenate([sink_ref[...]] * SEQS_PER_STEP, axis=0)
    mx = jnp.maximum(jnp.max(s, axis=-1, keepdims=True), sink)
    e = jnp.exp2(s - mx)
    inv = 1.0 / (jnp.sum(e, axis=-1, keepdims=True) + jnp.exp2(sink - mx))
    e = e.astype(BF16)
    for b in range(SEQS_PER_STEP):
        rows = slice(b * N_HEADS, (b + 1) * N_HEADS)
        of = lax.dot_general(e[rows], nv_ref[b].astype(BF16), (((1,), (1,)), ((), ())),
                             preferred_element_type=F32)
        of = jnp.where(own, of, 0.0)
        o = of[:, 0:HEAD_DIM]
        for g in range(1, N_KV_HEADS):
            o = o + of[:, g * HEAD_DIM:(g + 1) * HEAD_DIM]
        o_ref[b] = (o * inv[rows]).astype(BF16)


def _attention_sample(sink_col, q3, kvn_t, ck, cv):
    nb = N_SAMPLE // SEQS_PER_STEP
    cache_spec = pl.BlockSpec((SEQS_PER_STEP, KV_DIM, WINDOW), lambda i: (i, 0, 0))
    return pl.pallas_call(
        _attn_sample_kernel,
        grid=(nb,),
        in_specs=[
            pl.BlockSpec((N_HEADS, 1), lambda i: (0, 0)),
            pl.BlockSpec((SEQS_PER_STEP, N_HEADS, HEAD_DIM), lambda i: (i, 0, 0)),
            pl.BlockSpec((1, 2 * KV_DIM, SEQS_PER_STEP), lambda i: (i, 0, 0)),
            cache_spec, cache_spec,
        ],
        out_specs=[
            pl.BlockSpec((SEQS_PER_STEP, N_HEADS, HEAD_DIM), lambda i: (i, 0, 0)),
            cache_spec, cache_spec,
        ],
        out_shape=[
            jax.ShapeDtypeStruct((N_SAMPLE, N_HEADS, HEAD_DIM), BF16),
            jax.ShapeDtypeStruct((N_SAMPLE, KV_DIM, WINDOW), F32),
            jax.ShapeDtypeStruct((N_SAMPLE, KV_DIM, WINDOW), F32),
        ],
        compiler_params=_params(1),
        name="attn_sample",
    )(sink_col, q3, kvn_t, ck, cv)


POOL_PAD = 16


def _pool_kernel(u_ref, sp_ref, wmix_ref, scale_ref, p_ref, spn_ref, ext):
    m = pl.program_id(0)

    @pl.when(m == 0)
    def _():
        ext[0:POOL_PAD, :] = jnp.zeros((POOL_PAD, POOL_WIDTH), F32)

    ext[POOL_PAD:POOL_PAD + TM, :] = u_ref[...]
    pos = m * TM + lax.broadcasted_iota(jnp.int32, (TM, 1), 0)
    for g, w in enumerate(POOL_WINDOWS):
        cols = slice(g * POOL_GROUP_DIM, (g + 1) * POOL_GROUP_DIM)
        run = ext[:, cols]
        shift = 1
        while shift < w:
            run = run + pltpu.roll(run, shift, 0)
            shift *= 2
        tot = run[POOL_PAD:POOL_PAD + TM, :]
        inv_cnt = 1.0 / jnp.minimum(w, pos + 1).astype(F32)
        d = tot * inv_cnt - u_ref[:, cols]
        y = _dot(d.astype(BF16), wmix_ref[g].astype(BF16))
        p_ref[:, cols] = (y * scale_ref[:, cols]).astype(BF16)
    ext[0:POOL_PAD, :] = ext[TM:TM + POOL_PAD, :]

    @pl.when(m == NT - 1)
    def _():
        for g, w in enumerate(POOL_WINDOWS):
            cols = slice(g * POOL_GROUP_DIM, (g + 1) * POOL_GROUP_DIM)
            us = u_ref[SPLIT:TM, cols]
            tot = us
            for j in range(1, w):
                tot = tot + sp_ref[POOL_HIST - j, :, cols]
            d = tot * (1.0 / w) - us
            y = _dot(d.astype(BF16), wmix_ref[g].astype(BF16))
            p_ref[SPLIT:TM, cols] = (y * scale_ref[:, cols]).astype(BF16)
        spn_ref[0:POOL_HIST - 1] = sp_ref[1:POOL_HIST]
        spn_ref[POOL_HIST - 1] = u_ref[SPLIT:TM, :]


def _pool(u, sp_t, wmix, scale):
    full = lambda shape: pl.BlockSpec(shape, lambda m: (0,) * len(shape))
    return pl.pallas_call(
        _pool_kernel,
        grid=(NT,),
        in_specs=[
            pl.BlockSpec((TM, POOL_WIDTH), lambda m: (m, 0)),
            full((POOL_HIST, N_SAMPLE, POOL_WIDTH)),
            full((len(POOL_WINDOWS), POOL_GROUP_DIM, POOL_GROUP_DIM)),
            full((1, POOL_WIDTH)),
        ],
        out_specs=[pl.BlockSpec((TM, POOL_WIDTH), lambda m: (m, 0)),
                   full((POOL_HIST, N_SAMPLE, POOL_WIDTH))],
        out_shape=[jax.ShapeDtypeStruct((ROWS, POOL_WIDTH), BF16),
                   jax.ShapeDtypeStruct((POOL_HIST, N_SAMPLE, POOL_WIDTH), F32)],
        scratch_shapes=[pltpu.VMEM((POOL_PAD + TM, POOL_WIDTH), F32)],
        compiler_params=_params(1),
        name="pool_mix",
    )(u, sp_t, wmix, scale)


_NC = D_MODEL // TN


TMH = TM // 2
NTH = ROWS // TMH
SPLIT_H = SEQ - (NTH - 1) * TMH
assert NTH * TMH == ROWS and SPLIT_H + N_SAMPLE == TMH and SPLIT_H % 8 == 0


def _resident(shape):
    return pl.BlockSpec(shape, lambda m: (0,) * len(shape), pipeline_mode=pl.Buffered(1))


def _merge_kernel(xb_ref, pool_ref, attn_ref, wp_ref, wa_ref, *refs):
    gate_w, o_ref = refs[:2 * _NC], refs[2 * _NC]
    for c in range(_NC):
        cols = slice(c * TN, (c + 1) * TN)
        gate_pool = jax.nn.sigmoid(_dot(xb_ref[...], gate_w[c][...]))
        gate_attn = jax.nn.sigmoid(_dot(xb_ref[...], gate_w[_NC + c][...]))
        a = _dot(pool_ref[...], wp_ref[:, cols])
        b = _dot(attn_ref[...], wa_ref[:, cols])
        o_ref[:, cols] = (gate_pool * a + gate_attn * b).astype(BF16)


def _merge(xb, pooled, attn, wp, wa, gate_w):
    row = lambda width: pl.BlockSpec((TMH, width), lambda m: (m, 0))
    return pl.pallas_call(
        _merge_kernel,
        grid=(NTH,),
        in_specs=[row(D_MODEL), row(POOL_WIDTH), row(D_MODEL),
                  _resident((POOL_WIDTH, D_MODEL)), _resident((D_MODEL, D_MODEL))]
        + [_resident((D_MODEL, TN))] * (2 * _NC),
        out_specs=row(D_MODEL),
        out_shape=jax.ShapeDtypeStruct((ROWS, D_MODEL), BF16),
        compiler_params=_params(1),
        name="branch_merge",
    )(xb, pooled, attn, wp, wa, *gate_w)


def _out_ln_kernel(mrg_ref, wo_ref, x_ref, xs_ref, lng_ref, lnb_ref, x1_ref, x1b_ref):
    m = pl.program_id(0)
    for c in range(_NC):
        cols = slice(c * TN, (c + 1) * TN)
        x1_ref[:, cols] = _dot(mrg_ref[...], wo_ref[:, cols])

    def finish(z):
        x1 = _layer_norm(z, lng_ref[...], lnb_ref[...])
        x1_ref[...] = ALPHA * x1
        x1b_ref[...] = x1.astype(BF16)

    @pl.when(m < NTH - 1)
    def _():
        finish(x1_ref[...] + ALPHA * x_ref[...])

    @pl.when(m == NTH - 1)
    def _():
        finish(jnp.concatenate([x1_ref[0:SPLIT_H, :] + ALPHA * x_ref[0:SPLIT_H, :],
                                x1_ref[SPLIT_H:TMH, :] + ALPHA * xs_ref[...]], axis=0))


def _out_ln(merged, wo, x, xs, ln_g, ln_b):
    row = pl.BlockSpec((TMH, D_MODEL), lambda m: (m, 0))
    return pl.pallas_call(
        _out_ln_kernel,
        grid=(NTH,),
        in_specs=[row, _resident((D_MODEL, D_MODEL)), row, _resident((N_SAMPLE, D_MODEL)),
                  _resident((1, D_MODEL)), _resident((1, D_MODEL))],
        out_specs=[row, row],
        out_shape=[jax.ShapeDtypeStruct((ROWS, D_MODEL), F32), jax.ShapeDtypeStruct((ROWS, D_MODEL), BF16)],
        compiler_params=_params(1),
        name="out_proj_ln1",
    )(merged, wo, x, xs, ln_g, ln_b)


TF = 512
NF = D_FF // TF
CARRY = 8


def _ffn_kernel(x1b_ref, x1_hbm, sc_ref, wg_ref, wu_ref, wd_ref, cw_ref, cb_ref, lng_ref, lnb_ref,
                o_ref, os_ref, glast_ref, scn_ref,
                h0, h1, gext, carry, sem):
    m = pl.program_id(0)
    f = pl.program_id(1)
    last_m = m == NT - 1
    h_slots = (h0, h1)
    x1_copy = pltpu.make_async_copy(x1_hbm.at[pl.ds(m * TM, TM), :], o_ref, sem)

    def up_and_gate(slot, with_down):
        halves = ((0, HALF), (HALF, TM))
        gext[0:CARRY, :] = carry[f]
        for lo, hi in halves:
            gext[CARRY + lo:CARRY + hi, :] = _dot(x1b_ref[lo:hi, :], wg_ref[...])
        ups = [_dot(x1b_ref[lo:hi, :], wu_ref[...]) for lo, hi in halves]
        if with_down:
            o_ref[...] += _dot(h_slots[1 - slot][...], wd_ref[...])
        w0 = cw_ref[0:1, :]
        w1 = cw_ref[1:2, :]
        w2 = cw_ref[2:3, :]
        cb = cb_ref[...]

        def conv(lo, hi, older=None, newer=None):
            older = gext[CARRY - 2 + lo:CARRY - 2 + hi, :] if older is None else older
            newer = gext[CARRY - 1 + lo:CARRY - 1 + hi, :] if newer is None else newer
            return cb + w0 * older + w1 * newer + w2 * gext[CARRY + lo:CARRY + hi, :]

        for (lo, hi), up in zip(halves, ups):
            if hi <= SPLIT:
                gc = conv(lo, hi)
            else:
                older_s = jnp.where(last_m, sc_ref[:, 0, :], gext[CARRY - 2 + SPLIT:CARRY - 2 + TM, :])
                newer_s = jnp.where(last_m, sc_ref[:, 1, :], gext[CARRY - 1 + SPLIT:CARRY - 1 + TM, :])
                gc = jnp.concatenate([conv(lo, SPLIT), conv(SPLIT, TM, older_s, newer_s)], axis=0)
            h_slots[slot][lo:hi, :] = (jax.nn.gelu(gc, approximate=True) * up).astype(BF16)
        carry[f] = gext[TM:CARRY + TM, :]

        @pl.when(last_m)
        def _():
            glast_ref[...] = gext[SPLIT:CARRY + SPLIT, :]
            scn_ref[:, 0, :] = sc_ref[:, 1, :]
            scn_ref[:, 1, :] = gext[CARRY + SPLIT:CARRY + TM, :]

    @pl.when(f == 0)
    def _():
        @pl.when(m == 0)
        def _():
            carry[...] = jnp.zeros_like(carry)

        x1_copy.start()
        up_and_gate(0, False)
        x1_copy.wait()

    for parity in range(2):
        @pl.when((f > 0) & (f < NF) & (f % 2 == parity))
        def _():
            up_and_gate(parity, True)

    @pl.when(f == NF)
    def _():
        h_last = h_slots[(NF - 1) % 2]
        parts = ((0, 352), (352, 704), (704, TM))
        accs = [o_ref[lo:hi, :] + _dot(h_last[lo:hi, :], wd_ref[...]) for lo, hi in parts]
        for (lo, hi), acc in zip(parts, accs):
            o_ref[lo:hi, :] = _layer_norm(acc, lng_ref[...], lnb_ref[...])

        @pl.when(last_m)
        def _():
            os_ref[...] = o_ref[SPLIT:TM, :]


def _ffn(x1b, x1, sc, wg, wu, wd, conv_w, conv_b, ln_g, ln_b):
    full = lambda shape: pl.BlockSpec(shape, lambda m, f: (0, 0))
    g_tile = lambda f: jnp.minimum(f, NF - 1)
    d_tile = lambda f: jnp.maximum(f - 1, 0)
    only_last = lambda m, f: jnp.where(m == NT - 1, g_tile(f), 0)
    row = pl.BlockSpec((TM, D_MODEL), lambda m, f: (m, 0))
    return pl.pallas_call(
        _ffn_kernel,
        grid=(NT, NF + 1),
        in_specs=[
            row,
            pl.BlockSpec(memory_space=pl.ANY),
            pl.BlockSpec((N_SAMPLE, CONV_W - 1, TF), lambda m, f: (0, 0, only_last(m, f))),
            pl.BlockSpec((D_MODEL, TF), lambda m, f: (0, g_tile(f))),
            pl.BlockSpec((D_MODEL, TF), lambda m, f: (0, g_tile(f))),
            pl.BlockSpec((TF, D_MODEL), lambda m, f: (d_tile(f), 0)),
            pl.BlockSpec((CONV_W, TF), lambda m, f: (0, g_tile(f))),
            pl.BlockSpec((1, TF), lambda m, f: (0, g_tile(f))),
            full((1, D_MODEL)), full((1, D_MODEL)),
        ],
        out_specs=[
            row,
            full((N_SAMPLE, D_MODEL)),
            pl.BlockSpec((CARRY, TF), lambda m, f: (0, only_last(m, f))),
            pl.BlockSpec((N_SAMPLE, CONV_W - 1, TF), lambda m, f: (0, 0, only_last(m, f))),
        ],
        out_shape=[
            jax.ShapeDtypeStruct((SEQ, D_MODEL), F32),
            jax.ShapeDtypeStruct((N_SAMPLE, D_MODEL), F32),
            jax.ShapeDtypeStruct((CARRY, D_FF), F32),
            jax.ShapeDtypeStruct((N_SAMPLE, CONV_W - 1, D_FF), F32),
        ],
        scratch_shapes=[
            pltpu.VMEM((TM, TF), BF16), pltpu.VMEM((TM, TF), BF16),
            pltpu.VMEM((CARRY + TM, TF), F32),
            pltpu.VMEM((NF, CARRY, TF), F32),
            pltpu.SemaphoreType.DMA(()),
        ],
        compiler_params=_params(2),
        name="convffn_ln2",
    )(x1b, x1, sc, wg, wu, wd, conv_w, conv_b, ln_g, ln_b)


def _rope_tables(pos):
    half = HEAD_DIM // 2
    inv = ROPE_THETA ** (-jnp.arange(half, dtype=F32) / half)
    lane = jnp.arange(LANES)
    inv_lanes = inv[lane % half]
    sign = jnp.where((lane % HEAD_DIM) < half, -1.0, 1.0).astype(F32)
    ang = pos.astype(F32)[:, None] * inv_lanes[None, :]
    return jnp.cos(ang), jnp.sin(ang) * sign[None, :]


def kernel(x_prompt, x_sample, cache_k, cache_v, state_pool, state_conv, w_in, attn_sinks, w_pool_mix,
           pool_scale, w_attn_branch, w_pool_branch, w_out, ln1_g, ln1_b, w_up, w_gate, conv_w, conv_b,
           w_down, ln2_g, ln2_b):
    x = x_prompt.reshape(SEQ, D_MODEL)
    xs = x_sample.reshape(N_SAMPLE, D_MODEL)
    pos = jnp.concatenate([jnp.arange(SEQ), jnp.full((N_SAMPLE,), PAST_LEN)])
    cos, sin = _rope_tables(pos)

    q, kv, u, xb = _in_proj(x, xs, w_in[0], cos, sin)

    to_dsw = lambda c: jnp.transpose(c[0], (0, 2, 3, 1)).reshape(N_SAMPLE, KV_DIM, WINDOW)
    steps = N_SAMPLE // SEQS_PER_STEP
    kvn_t = jnp.transpose(kv[SEQ:].reshape(steps, SEQS_PER_STEP, 2 * KV_DIM), (0, 2, 1))
    sinks = attn_sinks[0] * LOG2E
    attn_s, new_k, new_v = _attention_sample(
        sinks.reshape(N_HEADS, 1), q[SEQ:].reshape(N_SAMPLE, N_HEADS, HEAD_DIM), kvn_t,
        to_dsw(cache_k), to_dsw(cache_v))
    whole = lambda w: (w, 0, w.shape[1])
    gate_tiles = [(w_in[0], _N_G0 + j, TN) for j in range(2 * _NC)]
    attn, (wp_b, wa_b, wo_b, wg_b, wu_b, wd_b, *gate_w) = _attention(
        sinks, q, kv, attn_s.reshape(N_SAMPLE, D_MODEL),
        [whole(w_pool_branch[0]), whole(w_attn_branch[0]), whole(w_out[0]),
         whole(w_gate[0]), whole(w_up[0]), whole(w_down[0])] + gate_tiles)

    state_pool_t = jnp.transpose(state_pool[0], (1, 0, 2))
    pooled, new_pool_t = _pool(u, state_pool_t, w_pool_mix[0], pool_scale)

    merged = _merge(xb, pooled, attn, wp_b, wa_b, gate_w)
    x1, x1b = _out_ln(merged, wo_b, x, xs, ln1_g, ln1_b)

    y, y_s, g_last, new_conv_s = _ffn(x1b, x1, state_conv[0], wg_b, wu_b, wd_b,
                                      conv_w[0], conv_b, ln2_g, ln2_b)

    from_dsw = lambda c: jnp.transpose(
        c.reshape(N_SAMPLE, N_KV_HEADS, HEAD_DIM, WINDOW), (0, 3, 1, 2))[None]
    return (
        y.reshape(1, SEQ, D_MODEL),
        y_s.reshape(N_SAMPLE, 1, D_MODEL),
        kv[SEQ - WINDOW:SEQ, :KV_DIM].reshape(1, 1, WINDOW, N_KV_HEADS, HEAD_DIM),
        kv[SEQ - WINDOW:SEQ, KV_DIM:].reshape(1, 1, WINDOW, N_KV_HEADS, HEAD_DIM),
        u[SEQ - POOL_HIST:SEQ].reshape(1, 1, POOL_HIST, POOL_WIDTH),
        g_last[CARRY - (CONV_W - 1):].reshape(1, 1, CONV_W - 1, D_FF),
        from_dsw(new_k),
        from_dsw(new_v),
        jnp.transpose(new_pool_t, (1, 0, 2))[None],
        new_conv_s[None],
    )
```

```python
import jax
import jax.numpy as jnp
from jax import lax
from jax.experimental import pallas as pl
from jax.experimental.pallas import tpu as pltpu

F32 = jnp.float32
BF16 = jnp.bfloat16

D_MODEL = 2048
SEQ = 8192
N_SAMPLE = 128
PAST_LEN = 8192
HEAD_DIM = 64
N_HEADS = 32
N_KV_HEADS = 4
GROUP = N_HEADS // N_KV_HEADS
KV_DIM = N_KV_HEADS * HEAD_DIM
WINDOW = 128
ROPE_THETA = 10000.0
POOL_WINDOWS = (2, 4, 8, 16)
POOL_WIDTH = 1024
POOL_GROUP_DIM = 256
POOL_HIST = 15
D_FF = 5632
CONV_W = 3
IN_DIM = 7680
LN_EPS = 1e-5
NEG_INF = -1e30
ALPHA = 2.0 ** 0.25
LOG2E = 1.4426950408889634

LANES = 128
VMEM_LIMIT = 60000 * 1024

ROWS = SEQ + N_SAMPLE
TM = 1040
NT = ROWS // TM
SPLIT = SEQ - (NT - 1) * TM
HALF = 528
TN = 512
assert NT * TM == ROWS and SPLIT + N_SAMPLE == TM and SPLIT % 16 == 0 and HALF % 16 == 0


def _params(ndim, flags=None):
    return pltpu.CompilerParams(dimension_semantics=("arbitrary",) * ndim,
                                vmem_limit_bytes=VMEM_LIMIT, flags=flags)


def _dot(a, b):
    return jnp.dot(a, b, preferred_element_type=F32)


def _layer_norm(z, g, b):
    mu = jnp.mean(z, axis=-1, keepdims=True)
    d = z - mu
    var = jnp.mean(d * d, axis=-1, keepdims=True)
    return d * lax.rsqrt(var + LN_EPS) * g + b


def _rope(x, cos, sin_signed):
    lane = lax.broadcasted_iota(jnp.int32, (1, LANES), 1)
    low_half = (lane % HEAD_DIM) < (HEAD_DIM // 2)
    outs = []
    for c in range(x.shape[1] // LANES):
        xc = x[:, c * LANES:(c + 1) * LANES]
        up = pltpu.roll(xc, LANES - HEAD_DIM // 2, 1)
        down = pltpu.roll(xc, HEAD_DIM // 2, 1)
        outs.append(xc * cos + jnp.where(low_half, up, down) * sin_signed)
    return jnp.concatenate(outs, axis=1)


_NQ = D_MODEL // TN
_N_KV = _NQ
_N_U0 = _N_KV + 1
_N_G0 = _N_U0 + POOL_WIDTH // TN


def _in_proj_kernel(x_ref, xs_ref, w_ref, cos_ref, sin_ref, q_ref, kv_ref, xb):
    m = pl.program_id(0)
    n = pl.program_id(1)

    @pl.when(n == 0)
    def _():
        @pl.when(m < NT - 1)
        def _():
            xb[...] = x_ref[...].astype(BF16)

        @pl.when(m == NT - 1)
        def _():
            xb[0:SPLIT, :] = x_ref[0:SPLIT, :].astype(BF16)
            xb[SPLIT:TM, :] = xs_ref[...].astype(BF16)

    wb = w_ref[...].astype(BF16)
    halves = ((0, HALF), (HALF, TM))
    thirds = ((0, 352), (352, 704), (704, TM))

    def project(epilogue, parts):
        accs = [_dot(xb[lo:hi, :], wb) for lo, hi in parts]
        for (lo, hi), acc in zip(parts, accs):
            epilogue(acc, lo, hi)

    def q_epilogue(acc, lo, hi):
        roped = _rope(acc, cos_ref[lo:hi, :], sin_ref[lo:hi, :])
        q_ref[lo:hi, :] = (roped * (HEAD_DIM ** -0.5 * LOG2E)).astype(BF16)

    def kv_epilogue(acc, lo, hi):
        kv_ref[lo:hi, :KV_DIM] = _rope(acc[:, :KV_DIM], cos_ref[lo:hi, :], sin_ref[lo:hi, :])
        kv_ref[lo:hi, KV_DIM:] = acc[:, KV_DIM:]

    pl.when(n < _NQ)(lambda: project(q_epilogue, thirds))
    pl.when(n == _N_KV)(lambda: project(kv_epilogue, halves))


def _in_proj(x, xs, w_in, cos, sin):
    nn = _N_U0

    return pl.pallas_call(
        _in_proj_kernel,
        grid=(NT, nn),
        in_specs=[
            pl.BlockSpec((TM, D_MODEL), lambda m, n: (m, 0)),
            pl.BlockSpec((N_SAMPLE, D_MODEL), lambda m, n: (0, 0)),
            pl.BlockSpec((D_MODEL, TN), lambda m, n: (0, n)),
            pl.BlockSpec((TM, LANES), lambda m, n: (m, 0)),
            pl.BlockSpec((TM, LANES), lambda m, n: (m, 0)),
        ],
        out_specs=[
            pl.BlockSpec((TM, TN), lambda m, n: (m, jnp.minimum(n, _NQ - 1))),
            pl.BlockSpec((TM, TN), lambda m, n: (m, 0)),
            pl.BlockSpec((TM, D_MODEL), lambda m, n: (m, 0)),
        ],
        out_shape=[
            jax.ShapeDtypeStruct((ROWS, D_MODEL), BF16),
            jax.ShapeDtypeStruct((ROWS, 2 * KV_DIM), F32),
            jax.ShapeDtypeStruct((ROWS, D_MODEL), BF16),
        ],
        compiler_params=_params(2),
        name="in_proj",
    )(x, xs, w_in, cos, sin)


PAIRS = GROUP // 2


def _attn_kernel(sink_ref, q_ref, kp_ref, kc_ref, vp_ref, vc_ref, os_ref, *refs):
    n_w = (len(refs) - 1) // 2
    w_refs, o_ref, wb_refs = refs[:n_w], refs[n_w], refs[n_w + 1:]
    i = pl.program_id(0)
    last = pl.num_programs(0) - 1

    @pl.when(i < last)
    def _():
        for w_ref, wb_ref in zip(w_refs, wb_refs):
            wb_ref[...] = w_ref[...].astype(BF16)
        _attn_block(i, sink_ref, q_ref, kp_ref, kc_ref, vp_ref, vc_ref, o_ref)

    @pl.when(i == last)
    def _():
        o_ref[...] = os_ref[...]


def _attn_block(i, sink_ref, q_ref, kp_ref, kc_ref, vp_ref, vc_ref, o_ref):
    kk =jnp.concatenate([kp_ref[...], kc_ref[...]], axis=0)
    vv = jnp.concatenate([vp_ref[...], vc_ref[...]], axis=0)
    low = lax.broadcasted_iota(jnp.int32, (1, LANES), 1) < HEAD_DIM
    rows = PAIRS * WINDOW
    r = lax.broadcasted_iota(jnp.int32, (rows, 2 * WINDOW), 0) % WINDOW
    c = lax.broadcasted_iota(jnp.int32, (rows, 2 * WINDOW), 1)
    visible = (c > r) & (c <= r + WINDOW) & ((i > 0) | (c >= WINDOW))
    chunk = lax.broadcasted_iota(jnp.int32, (rows, 1), 0) // WINDOW

    def sink_column(first_head):
        col = jnp.full((rows, 1), sink_ref[first_head], F32)
        for t in range(1, PAIRS):
            col = jnp.where(chunk == t, sink_ref[first_head + 2 * t], col)
        return col

    for pair in range(N_KV_HEADS // 2):
        k_pair = kk[:, pair * LANES:(pair + 1) * LANES]
        v_pair = vv[:, pair * LANES:(pair + 1) * LANES]
        k_swap = pltpu.roll(k_pair, HEAD_DIM, 1)
        v_swap = pltpu.roll(v_pair, HEAD_DIM, 1)
        for second in range(2):
            g = 2 * pair + second
            k_in_low, k_in_high = (k_swap, k_pair) if second else (k_pair, k_swap)
            v_in_low, v_in_high = (v_swap, v_pair) if second else (v_pair, v_swap)
            x = jnp.concatenate(
                [q_ref[:, (g * PAIRS + t) * LANES:(g * PAIRS + t + 1) * LANES] for t in range(PAIRS)], axis=0)

            def half(k_pad, v_pad, sink):
                s = lax.dot_general(x, k_pad.astype(BF16), (((1,), (1,)), ((), ())),
                                    preferred_element_type=F32)
                s = jnp.where(visible, s, NEG_INF)
                mx = jnp.maximum(jnp.max(s, axis=-1, keepdims=True), sink)
                e = jnp.exp2(s - mx).astype(BF16)
                return _dot(e, v_pad.astype(BF16)), jnp.exp2(sink - mx)

            a, sink_a = half(jnp.where(low, k_in_low, 0.0), jnp.where(low, v_in_low, 1.0),
                             sink_column(g * GROUP))
            b, sink_b = half(jnp.where(low, 0.0, k_in_high), jnp.where(low, 1.0, v_in_high),
                             sink_column(g * GROUP + 1))
            den = pltpu.roll(jnp.where(low, b, a), HEAD_DIM, 1) + jnp.where(low, sink_a, sink_b)
            out = (jnp.where(low, a, b) / den).astype(BF16)
            for t in range(PAIRS):
                o_ref[:, (g * PAIRS + t) * LANES:(g * PAIRS + t + 1) * LANES] = out[t * WINDOW:(t + 1) * WINDOW]


def _cast_chunk_rows(rows, steps):
    assert rows % 16 == 0
    units = rows // 16
    n = max(d for d in range(1, steps + 1) if units % d == 0)
    return rows // n


def _attention(sinks, q, kv, attn_s, weights):
    nb = SEQ // WINDOW
    cur = lambda i: jnp.minimum(i, nb - 1)
    prev = lambda i: jnp.clip(i - 1, 0, nb - 1)
    w_in_specs, w_out_specs, w_out_shapes = [], [], []
    for w, col, width in weights:
        chunk = _cast_chunk_rows(w.shape[0], nb)
        n = w.shape[0] // chunk
        w_in_specs.append(pl.BlockSpec((chunk, width), lambda i, n=n, col=col: (jnp.minimum(i, n - 1), col)))
        w_out_specs.append(pl.BlockSpec((chunk, width), lambda i, n=n: (jnp.minimum(i, n - 1), 0)))
        w_out_shapes.append(jax.ShapeDtypeStruct((w.shape[0], width), BF16))
    outs = pl.pallas_call(
        _attn_kernel,
        grid=(nb + 1,),
        in_specs=[
            pl.BlockSpec(memory_space=pltpu.SMEM),
            pl.BlockSpec((WINDOW, D_MODEL), lambda i: (cur(i), 0)),
            pl.BlockSpec((WINDOW, KV_DIM), lambda i: (prev(i), 0)),
            pl.BlockSpec((WINDOW, KV_DIM), lambda i: (cur(i), 0)),
            pl.BlockSpec((WINDOW, KV_DIM), lambda i: (prev(i), 1)),
            pl.BlockSpec((WINDOW, KV_DIM), lambda i: (cur(i), 1)),
            pl.BlockSpec((N_SAMPLE, D_MODEL), lambda i: (0, 0)),
        ] + w_in_specs,
        out_specs=[pl.BlockSpec((WINDOW, D_MODEL), lambda i: (i, 0))] + w_out_specs,
        out_shape=[jax.ShapeDtypeStruct((ROWS, D_MODEL), BF16)] + w_out_shapes,
        compiler_params=_params(1),
        name="attn_prompt",
    )(sinks, q, kv, kv, kv, kv, attn_s, *[w for w, _, _ in weights])
    return outs[0], outs[1:]


SEQS_PER_STEP = 8


def _attn_sample_kernel(sink_ref, q_ref, kvn_ref, ck_ref, cv_ref, o_ref, nk_ref, nv_ref):
    head = lax.broadcasted_iota(jnp.int32, (N_HEADS, KV_DIM), 0) // GROUP
    col_group = lax.broadcasted_iota(jnp.int32, (N_HEADS, KV_DIM), 1) // HEAD_DIM
    own = head == col_group
    newest = lax.broadcasted_iota(jnp.int32, (1, WINDOW), 1) == WINDOW - 1
    scores = []
    for b in range(SEQS_PER_STEP):
        nk_ref[b] = jnp.where(newest, kvn_ref[0, 0:KV_DIM, b:b + 1], pltpu.roll(ck_ref[b], WINDOW - 1, 1))
        nv_ref[b] = jnp.where(newest, kvn_ref[0, KV_DIM:2 * KV_DIM, b:b + 1],
                              pltpu.roll(cv_ref[b], WINDOW - 1, 1))
        qb = q_ref[b]
        qe = jnp.where(own, jnp.concatenate([qb] * N_KV_HEADS, axis=1), jnp.zeros((), BF16))
        scores.append(_dot(qe, nk_ref[b].astype(BF16)))
    s = jnp.concatenate(scores, axis=0)
    sink = jnp.concatenate([sink_ref[...]] * SEQS_PER_STEP, axis=0)
    mx = jnp.maximum(jnp.max(s, axis=-1, keepdims=True), sink)
    e = jnp.exp2(s - mx)
    inv = 1.0 / (jnp.sum(e, axis=-1, keepdims=True) + jnp.exp2(sink - mx))
    e = e.astype(BF16)
    for b in range(SEQS_PER_STEP):
        rows = slice(b * N_HEADS, (b + 1) * N_HEADS)
        of = lax.dot_general(e[rows], nv_ref[b].astype(BF16), (((1,), (1,)), ((), ())),
                             preferred_element_type=F32)
        of = jnp.where(own, of, 0.0)
        o = of[:, 0:HEAD_DIM]
        for g in range(1, N_KV_HEADS):
            o = o + of[:, g * HEAD_DIM:(g + 1) * HEAD_DIM]
        o_ref[b] = (o * inv[rows]).astype(BF16)


def _attention_sample(sink_col, q3, kvn_t, ck, cv):
    nb = N_SAMPLE // SEQS_PER_STEP
    cache_spec = pl.BlockSpec((SEQS_PER_STEP, KV_DIM, WINDOW), lambda i: (i, 0, 0))
    return pl.pallas_call(
        _attn_sample_kernel,
        grid=(nb,),
        in_specs=[
            pl.BlockSpec((N_HEADS, 1), lambda i: (0, 0)),
            pl.BlockSpec((SEQS_PER_STEP, N_HEADS, HEAD_DIM), lambda i: (i, 0, 0)),
            pl.BlockSpec((1, 2 * KV_DIM, SEQS_PER_STEP), lambda i: (i, 0, 0)),
            cache_spec, cache_spec,
        ],
        out_specs=[
            pl.BlockSpec((SEQS_PER_STEP, N_HEADS, HEAD_DIM), lambda i: (i, 0, 0)),
            cache_spec, cache_spec,
        ],
        out_shape=[
            jax.ShapeDtypeStruct((N_SAMPLE, N_HEADS, HEAD_DIM), BF16),
            jax.ShapeDtypeStruct((N_SAMPLE, KV_DIM, WINDOW), F32),
            jax.ShapeDtypeStruct((N_SAMPLE, KV_DIM, WINDOW), F32),
        ],
        compiler_params=_params(1),
        name="attn_sample",
    )(sink_col, q3, kvn_t, ck, cv)


POOL_PAD = 16


def _pool_kernel(xb_ref, wu0_ref, wu1_ref, sp_ref, wmix_ref, scale_ref, p_ref, spn_ref, plast_ref, ext):
    m = pl.program_id(0)

    @pl.when(m == 0)
    def _():
        ext[0:POOL_PAD, :] = jnp.zeros((POOL_PAD, POOL_WIDTH), F32)

    pos = m * TM + lax.broadcasted_iota(jnp.int32, (TM, 1), 0)
    for t, wu_ref in enumerate((wu0_ref, wu1_ref)):
        ext[POOL_PAD:POOL_PAD + TM, t * TN:(t + 1) * TN] = _dot(xb_ref[...], wu_ref[...])
    for g, w in enumerate(POOL_WINDOWS):
        cols = slice(g * POOL_GROUP_DIM, (g + 1) * POOL_GROUP_DIM)
        u_g = ext[POOL_PAD:POOL_PAD + TM, cols]
        run = ext[:, cols]
        shift = 1
        while shift < w:
            run = run + pltpu.roll(run, shift, 0)
            shift *= 2
        tot = run[POOL_PAD:POOL_PAD + TM, :]
        inv_cnt = 1.0 / jnp.minimum(w, pos + 1).astype(F32)
        d = tot * inv_cnt - u_g
        y = _dot(d.astype(BF16), wmix_ref[g].astype(BF16))
        p_ref[:, cols] = (y * scale_ref[:, cols]).astype(BF16)

    @pl.when(m == NT - 1)
    def _():
        for g, w in enumerate(POOL_WINDOWS):
            cols = slice(g * POOL_GROUP_DIM, (g + 1) * POOL_GROUP_DIM)
            us = ext[POOL_PAD + SPLIT:POOL_PAD + TM, cols]
            tot = us
            for j in range(1, w):
                tot = tot + sp_ref[POOL_HIST - j, :, cols]
            d = tot * (1.0 / w) - us
            y = _dot(d.astype(BF16), wmix_ref[g].astype(BF16))
            p_ref[SPLIT:TM, cols] = (y * scale_ref[:, cols]).astype(BF16)
        spn_ref[0:POOL_HIST - 1] = sp_ref[1:POOL_HIST]
        spn_ref[POOL_HIST - 1] = ext[POOL_PAD + SPLIT:POOL_PAD + TM, :]
        plast_ref[...] = ext[SPLIT:POOL_PAD + SPLIT, :]

    ext[0:POOL_PAD, :] = ext[TM:TM + POOL_PAD, :]


def _pool(xb, wu0, wu1, sp_t, wmix, scale):
    full = lambda shape: pl.BlockSpec(shape, lambda m: (0,) * len(shape))
    return pl.pallas_call(
        _pool_kernel,
        grid=(NT,),
        in_specs=[
            pl.BlockSpec((TM, D_MODEL), lambda m: (m, 0)),
            _resident((D_MODEL, TN)), _resident((D_MODEL, TN)),
            full((POOL_HIST, N_SAMPLE, POOL_WIDTH)),
            full((len(POOL_WINDOWS), POOL_GROUP_DIM, POOL_GROUP_DIM)),
            full((1, POOL_WIDTH)),
        ],
        out_specs=[pl.BlockSpec((TM, POOL_WIDTH), lambda m: (m, 0)),
                   full((POOL_HIST, N_SAMPLE, POOL_WIDTH)),
                   full((POOL_PAD, POOL_WIDTH))],
        out_shape=[jax.ShapeDtypeStruct((ROWS, POOL_WIDTH), BF16),
                   jax.ShapeDtypeStruct((POOL_HIST, N_SAMPLE, POOL_WIDTH), F32),
                   jax.ShapeDtypeStruct((POOL_PAD, POOL_WIDTH), F32)],
        scratch_shapes=[pltpu.VMEM((POOL_PAD + TM, POOL_WIDTH), F32)],
        compiler_params=_params(1),
        name="pool_mix",
    )(xb, wu0, wu1, sp_t, wmix, scale)


_NC = D_MODEL // TN


TMH = TM // 2
NTH = ROWS // TMH
SPLIT_H = SEQ - (NTH - 1) * TMH
assert NTH * TMH == ROWS and SPLIT_H + N_SAMPLE == TMH and SPLIT_H % 8 == 0


def _resident(shape):
    return pl.BlockSpec(shape, lambda m: (0,) * len(shape), pipeline_mode=pl.Buffered(1))


def _merge_kernel(xb_ref, pool_ref, attn_ref, wp_ref, wa_ref, *refs):
    gate_w, o_ref = refs[:2 * _NC], refs[2 * _NC]
    for c in range(_NC):
        cols = slice(c * TN, (c + 1) * TN)
        gate_pool = jax.nn.sigmoid(_dot(xb_ref[...], gate_w[c][...]))
        gate_attn = jax.nn.sigmoid(_dot(xb_ref[...], gate_w[_NC + c][...]))
        a = _dot(pool_ref[...], wp_ref[:, cols])
        b = _dot(attn_ref[...], wa_ref[:, cols])
        o_ref[:, cols] = (gate_pool * a + gate_attn * b).astype(BF16)


def _merge(xb, pooled, attn, wp, wa, gate_w):
    row = lambda width: pl.BlockSpec((TMH, width), lambda m: (m, 0))
    return pl.pallas_call(
        _merge_kernel,
        grid=(NTH,),
        in_specs=[row(D_MODEL), row(POOL_WIDTH), row(D_MODEL),
                  _resident((POOL_WIDTH, D_MODEL)), _resident((D_MODEL, D_MODEL))]
        + [_resident((D_MODEL, TN))] * (2 * _NC),
        out_specs=row(D_MODEL),
        out_shape=jax.ShapeDtypeStruct((ROWS, D_MODEL), BF16),
        compiler_params=_params(1),
        name="branch_merge",
    )(xb, pooled, attn, wp, wa, *gate_w)


def _out_ln_kernel(mrg_ref, wo_ref, x_ref, xs_ref, lng_ref, lnb_ref, x1_ref, x1b_ref):
    m = pl.program_id(0)
    for c in range(_NC):
        cols = slice(c * TN, (c + 1) * TN)
        x1_ref[:, cols] = _dot(mrg_ref[...], wo_ref[:, cols])

    def finish(z):
        x1 = _layer_norm(z, lng_ref[...], lnb_ref[...])
        x1_ref[...] = ALPHA * x1
        x1b_ref[...] = x1.astype(BF16)

    @pl.when(m < NTH - 1)
    def _():
        finish(x1_ref[...] + ALPHA * x_ref[...])

    @pl.when(m == NTH - 1)
    def _():
        finish(jnp.concatenate([x1_ref[0:SPLIT_H, :] + ALPHA * x_ref[0:SPLIT_H, :],
                                x1_ref[SPLIT_H:TMH, :] + ALPHA * xs_ref[...]], axis=0))


def _out_ln(merged, wo, x, xs, ln_g, ln_b):
    row = pl.BlockSpec((TMH, D_MODEL), lambda m: (m, 0))
    return pl.pallas_call(
        _out_ln_kernel,
        grid=(NTH,),
        in_specs=[row, _resident((D_MODEL, D_MODEL)), row, _resident((N_SAMPLE, D_MODEL)),
                  _resident((1, D_MODEL)), _resident((1, D_MODEL))],
        out_specs=[row, row],
        out_shape=[jax.ShapeDtypeStruct((ROWS, D_MODEL), F32), jax.ShapeDtypeStruct((ROWS, D_MODEL), BF16)],
        compiler_params=_params(1),
        name="out_proj_ln1",
    )(merged, wo, x, xs, ln_g, ln_b)


TF = 512
NF = D_FF // TF
CARRY = 8


def _ffn_kernel(x1b_ref, x1_hbm, sc_ref, wg_ref, wu_ref, wd_ref, cw_ref, cb_ref, lng_ref, lnb_ref,
                o_ref, os_ref, glast_ref, scn_ref,
                h0, h1, gext, carry, sem):
    m = pl.program_id(0)
    f = pl.program_id(1)
    last_m = m == NT - 1
    h_slots = (h0, h1)
    x1_copy = pltpu.make_async_copy(x1_hbm.at[pl.ds(m * TM, TM), :], o_ref, sem)

    def up_and_gate(slot, with_down):
        halves = ((0, HALF), (HALF, TM))
        gext[0:CARRY, :] = carry[f]
        for lo, hi in halves:
            gext[CARRY + lo:CARRY + hi, :] = _dot(x1b_ref[lo:hi, :], wg_ref[...])
        ups = [_dot(x1b_ref[lo:hi, :], wu_ref[...]) for lo, hi in halves]
        if with_down:
            o_ref[...] += _dot(h_slots[1 - slot][...], wd_ref[...])
        w0 = cw_ref[0:1, :]
        w1 = cw_ref[1:2, :]
        w2 = cw_ref[2:3, :]
        cb = cb_ref[...]

        def conv(lo, hi, older=None, newer=None):
            older = gext[CARRY - 2 + lo:CARRY - 2 + hi, :] if older is None else older
            newer = gext[CARRY - 1 + lo:CARRY - 1 + hi, :] if newer is None else newer
            return cb + w0 * older + w1 * newer + w2 * gext[CARRY + lo:CARRY + hi, :]

        for (lo, hi), up in zip(halves, ups):
            if hi <= SPLIT:
                gc = conv(lo, hi)
            else:
                older_s = jnp.where(last_m, sc_ref[:, 0, :], gext[CARRY - 2 + SPLIT:CARRY - 2 + TM, :])
                newer_s = jnp.where(last_m, sc_ref[:, 1, :], gext[CARRY - 1 + SPLIT:CARRY - 1 + TM, :])
                gc = jnp.concatenate([conv(lo, SPLIT), conv(SPLIT, TM, older_s, newer_s)], axis=0)
            h_slots[slot][lo:hi, :] = (jax.nn.gelu(gc, approximate=True) * up).astype(BF16)
        carry[f] = gext[TM:CARRY + TM, :]

        @pl.when(last_m)
        def _():
            glast_ref[...] = gext[SPLIT:CARRY + SPLIT, :]
            scn_ref[:, 0, :] = sc_ref[:, 1, :]
            scn_ref[:, 1, :] = gext[CARRY + SPLIT:CARRY + TM, :]

    @pl.when(f == 0)
    def _():
        @pl.when(m == 0)
        def _():
            carry[...] = jnp.zeros_like(carry)

        x1_copy.start()
        up_and_gate(0, False)
        x1_copy.wait()

    for parity in range(2):
        @pl.when((f > 0) & (f < NF) & (f % 2 == parity))
        def _():
            up_and_gate(parity, True)

    @pl.when(f == NF)
    def _():
        h_last = h_slots[(NF - 1) % 2]
        parts = ((0, 352), (352, 704), (704, TM))
        accs = [o_ref[lo:hi, :] + _dot(h_last[lo:hi, :], wd_ref[...]) for lo, hi in parts]
        for (lo, hi), acc in zip(parts, accs):
            o_ref[lo:hi, :] = _layer_norm(acc, lng_ref[...], lnb_ref[...])

        @pl.when(last_m)
        def _():
            os_ref[...] = o_ref[SPLIT:TM, :]


def _ffn(x1b, x1, sc, wg, wu, wd, conv_w, conv_b, ln_g, ln_b):
    full = lambda shape: pl.BlockSpec(shape, lambda m, f: (0, 0))
    g_tile = lambda f: jnp.minimum(f, NF - 1)
    d_tile = lambda f: jnp.maximum(f - 1, 0)
    only_last = lambda m, f: jnp.where(m == NT - 1, g_tile(f), 0)
    row = pl.BlockSpec((TM, D_MODEL), lambda m, f: (m, 0))
    return pl.pallas_call(
        _ffn_kernel,
        grid=(NT, NF + 1),
        in_specs=[
            row,
            pl.BlockSpec(memory_space=pl.ANY),
            pl.BlockSpec((N_SAMPLE, CONV_W - 1, TF), lambda m, f: (0, 0, only_last(m, f))),
            pl.BlockSpec((D_MODEL, TF), lambda m, f: (0, g_tile(f))),
            pl.BlockSpec((D_MODEL, TF), lambda m, f: (0, g_tile(f))),
            pl.BlockSpec((TF, D_MODEL), lambda m, f: (d_tile(f), 0)),
            pl.BlockSpec((CONV_W, TF), lambda m, f: (0, g_tile(f))),
            pl.BlockSpec((1, TF), lambda m, f: (0, g_tile(f))),
            full((1, D_MODEL)), full((1, D_MODEL)),
        ],
        out_specs=[
            row,
            full((N_SAMPLE, D_MODEL)),
            pl.BlockSpec((CARRY, TF), lambda m, f: (0, only_last(m, f))),
            pl.BlockSpec((N_SAMPLE, CONV_W - 1, TF), lambda m, f: (0, 0, only_last(m, f))),
        ],
        out_shape=[
            jax.ShapeDtypeStruct((SEQ, D_MODEL), F32),
            jax.ShapeDtypeStruct((N_SAMPLE, D_MODEL), F32),
            jax.ShapeDtypeStruct((CARRY, D_FF), F32),
            jax.ShapeDtypeStruct((N_SAMPLE, CONV_W - 1, D_FF), F32),
        ],
        scratch_shapes=[
            pltpu.VMEM((TM, TF), BF16), pltpu.VMEM((TM, TF), BF16),
            pltpu.VMEM((CARRY + TM, TF), F32),
            pltpu.VMEM((NF, CARRY, TF), F32),
            pltpu.SemaphoreType.DMA(()),
        ],
        compiler_params=_params(2),
        name="convffn_ln2",
    )(x1b, x1, sc, wg, wu, wd, conv_w, conv_b, ln_g, ln_b)


def _rope_tables(pos):
    half = HEAD_DIM // 2
    inv = ROPE_THETA ** (-jnp.arange(half, dtype=F32) / half)
    lane = jnp.arange(LANES)
    inv_lanes = inv[lane % half]
    sign = jnp.where((lane % HEAD_DIM) < half, -1.0, 1.0).astype(F32)
    ang = pos.astype(F32)[:, None] * inv_lanes[None, :]
    return jnp.cos(ang), jnp.sin(ang) * sign[None, :]


def kernel(x_prompt, x_sample, cache_k, cache_v, state_pool, state_conv, w_in, attn_sinks, w_pool_mix,
           pool_scale, w_attn_branch, w_pool_branch, w_out, ln1_g, ln1_b, w_up, w_gate, conv_w, conv_b,
           w_down, ln2_g, ln2_b):
    x = x_prompt.reshape(SEQ, D_MODEL)
    xs = x_sample.reshape(N_SAMPLE, D_MODEL)
    pos = jnp.concatenate([jnp.arange(SEQ), jnp.full((N_SAMPLE,), PAST_LEN)])
    cos, sin = _rope_tables(pos)

    q, kv, xb = _in_proj(x, xs, w_in[0], cos, sin)

    to_dsw = lambda c: jnp.transpose(c[0], (0, 2, 3, 1)).reshape(N_SAMPLE, KV_DIM, WINDOW)
    steps = N_SAMPLE // SEQS_PER_STEP
    kvn_t = jnp.transpose(kv[SEQ:].reshape(steps, SEQS_PER_STEP, 2 * KV_DIM), (0, 2, 1))
    sinks = attn_sinks[0] * LOG2E
    attn_s, new_k, new_v = _attention_sample(
        sinks.reshape(N_HEADS, 1), q[SEQ:].reshape(N_SAMPLE, N_HEADS, HEAD_DIM), kvn_t,
        to_dsw(cache_k), to_dsw(cache_v))
    whole = lambda w: (w, 0, w.shape[1])
    late_tiles = [(w_in[0], j, TN) for j in range(_N_U0, _N_G0 + 2 * _NC)]
    attn, (wp_b, wa_b, wo_b, wg_b, wu_b, wd_b, wpool0, wpool1, *gate_w) = _attention(
        sinks, q, kv, attn_s.reshape(N_SAMPLE, D_MODEL),
        [whole(w_pool_branch[0]), whole(w_attn_branch[0]), whole(w_out[0]),
         whole(w_gate[0]), whole(w_up[0]), whole(w_down[0])] + late_tiles)

    state_pool_t = jnp.transpose(state_pool[0], (1, 0, 2))
    pooled, new_pool_t, pool_last = _pool(xb, wpool0, wpool1, state_pool_t, w_pool_mix[0], pool_scale)

    merged = _merge(xb, pooled, attn, wp_b, wa_b, gate_w)
    x1, x1b = _out_ln(merged, wo_b, x, xs, ln1_g, ln1_b)

    y, y_s, g_last, new_conv_s = _ffn(x1b, x1, state_conv[0], wg_b, wu_b, wd_b,
                                      conv_w[0], conv_b, ln2_g, ln2_b)

    from_dsw = lambda c: jnp.transpose(
        c.reshape(N_SAMPLE, N_KV_HEADS, HEAD_DIM, WINDOW), (0, 3, 1, 2))[None]
    return (
        y.reshape(1, SEQ, D_MODEL),
        y_s.reshape(N_SAMPLE, 1, D_MODEL),
        kv[SEQ - WINDOW:SEQ, :KV_DIM].reshape(1, 1, WINDOW, N_KV_HEADS, HEAD_DIM),
        kv[SEQ - WINDOW:SEQ, KV_DIM:].reshape(1, 1, WINDOW, N_KV_HEADS, HEAD_DIM),
        pool_last[POOL_PAD - POOL_HIST:].reshape(1, 1, POOL_HIST, POOL_WIDTH),
        g_last[CARRY - (CONV_W - 1):].reshape(1, 1, CONV_W - 1, D_FF),
        from_dsw(new_k),
        from_dsw(new_v),
        jnp.transpose(new_pool_t, (1, 0, 2))[None],
        new_conv_s[None],
    )
```

```python
import jax
import jax.numpy as jnp
from jax import lax
from jax.experimental import pallas as pl
from jax.experimental.pallas import tpu as pltpu

F32 = jnp.float32
BF16 = jnp.bfloat16

D_MODEL = 2048
SEQ = 8192
N_SAMPLE = 128
PAST_LEN = 8192
HEAD_DIM = 64
N_HEADS = 32
N_KV_HEADS = 4
GROUP = N_HEADS // N_KV_HEADS
KV_DIM = N_KV_HEADS * HEAD_DIM
WINDOW = 128
ROPE_THETA = 10000.0
POOL_WINDOWS = (2, 4, 8, 16)
POOL_WIDTH = 1024
POOL_GROUP_DIM = 256
POOL_HIST = 15
D_FF = 5632
CONV_W = 3
IN_DIM = 7680
LN_EPS = 1e-5
NEG_INF = -1e30
ALPHA = 2.0 ** 0.25
LOG2E = 1.4426950408889634

LANES = 128
VMEM_LIMIT = 60000 * 1024

ROWS = SEQ + N_SAMPLE
TM = 1040
NT = ROWS // TM
SPLIT = SEQ - (NT - 1) * TM
HALF = 528
TN = 512
assert NT * TM == ROWS and SPLIT + N_SAMPLE == TM and SPLIT % 16 == 0 and HALF % 16 == 0


def _params(ndim, flags=None):
    return pltpu.CompilerParams(dimension_semantics=("arbitrary",) * ndim,
                                vmem_limit_bytes=VMEM_LIMIT, flags=flags)


def _dot(a, b):
    return jnp.dot(a, b, preferred_element_type=F32)


def _layer_norm(z, g, b):
    mu = jnp.mean(z, axis=-1, keepdims=True)
    d = z - mu
    var = jnp.mean(d * d, axis=-1, keepdims=True)
    return d * lax.rsqrt(var + LN_EPS) * g + b


def _rope(x, cos, sin_signed):
    lane = lax.broadcasted_iota(jnp.int32, (1, LANES), 1)
    low_half = (lane % HEAD_DIM) < (HEAD_DIM // 2)
    outs = []
    for c in range(x.shape[1] // LANES):
        xc = x[:, c * LANES:(c + 1) * LANES]
        up = pltpu.roll(xc, LANES - HEAD_DIM // 2, 1)
        down = pltpu.roll(xc, HEAD_DIM // 2, 1)
        outs.append(xc * cos + jnp.where(low_half, up, down) * sin_signed)
    return jnp.concatenate(outs, axis=1)


_NQ = D_MODEL // TN
_N_KV = _NQ
_N_U0 = _N_KV + 1
_N_G0 = _N_U0 + POOL_WIDTH // TN


def _in_proj_kernel(x_ref, xs_ref, w_ref, cos_ref, sin_ref, q_ref, kv_ref, xb):
    m = pl.program_id(0)
    n = pl.program_id(1)

    @pl.when(n == 0)
    def _():
        @pl.when(m < NT - 1)
        def _():
            xb[...] = x_ref[...].astype(BF16)

        @pl.when(m == NT - 1)
        def _():
            xb[0:SPLIT, :] = x_ref[0:SPLIT, :].astype(BF16)
            xb[SPLIT:TM, :] = xs_ref[...].astype(BF16)

    wb = w_ref[...].astype(BF16)
    halves = ((0, HALF), (HALF, TM))
    thirds = ((0, 352), (352, 704), (704, TM))

    def project(epilogue, parts):
        accs = [_dot(xb[lo:hi, :], wb) for lo, hi in parts]
        for (lo, hi), acc in zip(parts, accs):
            epilogue(acc, lo, hi)

    def q_epilogue(acc, lo, hi):
        roped = _rope(acc, cos_ref[lo:hi, :], sin_ref[lo:hi, :])
        q_ref[lo:hi, :] = (roped * (HEAD_DIM ** -0.5 * LOG2E)).astype(BF16)

    def kv_epilogue(acc, lo, hi):
        kv_ref[lo:hi, :KV_DIM] = _rope(acc[:, :KV_DIM], cos_ref[lo:hi, :], sin_ref[lo:hi, :])
        kv_ref[lo:hi, KV_DIM:] = acc[:, KV_DIM:]

    pl.when(n < _NQ)(lambda: project(q_epilogue, thirds))
    pl.when(n == _N_KV)(lambda: project(kv_epilogue, halves))


def _in_proj(x, xs, w_in, cos, sin):
    nn = _N_U0

    return pl.pallas_call(
        _in_proj_kernel,
        grid=(NT, nn),
        in_specs=[
            pl.BlockSpec((TM, D_MODEL), lambda m, n: (m, 0)),
            pl.BlockSpec((N_SAMPLE, D_MODEL), lambda m, n: (0, 0)),
            pl.BlockSpec((D_MODEL, TN), lambda m, n: (0, n)),
            pl.BlockSpec((TM, LANES), lambda m, n: (m, 0)),
            pl.BlockSpec((TM, LANES), lambda m, n: (m, 0)),
        ],
        out_specs=[
            pl.BlockSpec((TM, TN), lambda m, n: (m, jnp.minimum(n, _NQ - 1))),
            pl.BlockSpec((TM, TN), lambda m, n: (m, 0)),
            pl.BlockSpec((TM, D_MODEL), lambda m, n: (m, 0)),
        ],
        out_shape=[
            jax.ShapeDtypeStruct((ROWS, D_MODEL), BF16),
            jax.ShapeDtypeStruct((ROWS, 2 * KV_DIM), F32),
            jax.ShapeDtypeStruct((ROWS, D_MODEL), BF16),
        ],
        compiler_params=_params(2),
        name="in_proj",
    )(x, xs, w_in, cos, sin)


PAIRS = GROUP // 2


def _attn_kernel(sink_ref, q_ref, kp_ref, kc_ref, vp_ref, vc_ref, os_ref, *refs):
    n_w = (len(refs) - 1) // 2
    w_refs, o_ref, wb_refs = refs[:n_w], refs[n_w], refs[n_w + 1:]
    i = pl.program_id(0)
    last = pl.num_programs(0) - 1

    @pl.when(i < last)
    def _():
        for w_ref, wb_ref in zip(w_refs, wb_refs):
            if len(wb_ref.shape) == 3:
                for j in range(wb_ref.shape[0]):
                    wb_ref[j] = w_ref[:, j * TN:(j + 1) * TN].astype(BF16)
            else:
                wb_ref[...] = w_ref[...].astype(BF16)
        _attn_block(i, sink_ref, q_ref, kp_ref, kc_ref, vp_ref, vc_ref, o_ref)

    @pl.when(i == last)
    def _():
        o_ref[...] = os_ref[...]


def _attn_block(i, sink_ref, q_ref, kp_ref, kc_ref, vp_ref, vc_ref, o_ref):
    kk =jnp.concatenate([kp_ref[...], kc_ref[...]], axis=0)
    vv = jnp.concatenate([vp_ref[...], vc_ref[...]], axis=0)
    low = lax.broadcasted_iota(jnp.int32, (1, LANES), 1) < HEAD_DIM
    rows = PAIRS * WINDOW
    r = lax.broadcasted_iota(jnp.int32, (rows, 2 * WINDOW), 0) % WINDOW
    c = lax.broadcasted_iota(jnp.int32, (rows, 2 * WINDOW), 1)
    visible = (c > r) & (c <= r + WINDOW) & ((i > 0) | (c >= WINDOW))
    chunk = lax.broadcasted_iota(jnp.int32, (rows, 1), 0) // WINDOW

    def sink_column(first_head):
        col = jnp.full((rows, 1), sink_ref[first_head], F32)
        for t in range(1, PAIRS):
            col = jnp.where(chunk == t, sink_ref[first_head + 2 * t], col)
        return col

    for pair in range(N_KV_HEADS // 2):
        k_pair = kk[:, pair * LANES:(pair + 1) * LANES]
        v_pair = vv[:, pair * LANES:(pair + 1) * LANES]
        k_swap = pltpu.roll(k_pair, HEAD_DIM, 1)
        v_swap = pltpu.roll(v_pair, HEAD_DIM, 1)
        for second in range(2):
            g = 2 * pair + second
            k_in_low, k_in_high = (k_swap, k_pair) if second else (k_pair, k_swap)
            v_in_low, v_in_high = (v_swap, v_pair) if second else (v_pair, v_swap)
            x = jnp.concatenate(
                [q_ref[:, (g * PAIRS + t) * LANES:(g * PAIRS + t + 1) * LANES] for t in range(PAIRS)], axis=0)

            def half(k_pad, v_pad, sink):
                s = lax.dot_general(x, k_pad.astype(BF16), (((1,), (1,)), ((), ())),
                                    preferred_element_type=F32)
                s = jnp.where(visible, s, NEG_INF)
                mx = jnp.maximum(jnp.max(s, axis=-1, keepdims=True), sink)
                e = jnp.exp2(s - mx).astype(BF16)
                return _dot(e, v_pad.astype(BF16)), jnp.exp2(sink - mx)

            a, sink_a = half(jnp.where(low, k_in_low, 0.0), jnp.where(low, v_in_low, 1.0),
                             sink_column(g * GROUP))
            b, sink_b = half(jnp.where(low, 0.0, k_in_high), jnp.where(low, 1.0, v_in_high),
                             sink_column(g * GROUP + 1))
            den = pltpu.roll(jnp.where(low, b, a), HEAD_DIM, 1) + jnp.where(low, sink_a, sink_b)
            out = (jnp.where(low, a, b) / den).astype(BF16)
            for t in range(PAIRS):
                o_ref[:, (g * PAIRS + t) * LANES:(g * PAIRS + t + 1) * LANES] = out[t * WINDOW:(t + 1) * WINDOW]


def _cast_chunk_rows(rows, steps):
    assert rows % 16 == 0
    units = rows // 16
    n = max(d for d in range(1, steps + 1) if units % d == 0)
    return rows // n


def _attention(sinks, q, kv, attn_s, weights):
    nb = SEQ // WINDOW
    cur = lambda i: jnp.minimum(i, nb - 1)
    prev = lambda i: jnp.clip(i - 1, 0, nb - 1)
    w_in_specs, w_out_specs, w_out_shapes = [], [], []
    for w, col, width, tiled in weights:
        chunk = _cast_chunk_rows(w.shape[0], nb)
        n = w.shape[0] // chunk
        w_in_specs.append(pl.BlockSpec((chunk, width), lambda i, n=n, col=col: (jnp.minimum(i, n - 1), col)))
        if tiled:
            w_out_specs.append(pl.BlockSpec((width // TN, chunk, TN), lambda i, n=n: (0, jnp.minimum(i, n - 1), 0)))
            w_out_shapes.append(jax.ShapeDtypeStruct((width // TN, w.shape[0], TN), BF16))
        else:
            w_out_specs.append(pl.BlockSpec((chunk, width), lambda i, n=n: (jnp.minimum(i, n - 1), 0)))
            w_out_shapes.append(jax.ShapeDtypeStruct((w.shape[0], width), BF16))
    outs = pl.pallas_call(
        _attn_kernel,
        grid=(nb + 1,),
        in_specs=[
            pl.BlockSpec(memory_space=pltpu.SMEM),
            pl.BlockSpec((WINDOW, D_MODEL), lambda i: (cur(i), 0)),
            pl.BlockSpec((WINDOW, KV_DIM), lambda i: (prev(i), 0)),
            pl.BlockSpec((WINDOW, KV_DIM), lambda i: (cur(i), 0)),
            pl.BlockSpec((WINDOW, KV_DIM), lambda i: (prev(i), 1)),
            pl.BlockSpec((WINDOW, KV_DIM), lambda i: (cur(i), 1)),
            pl.BlockSpec((N_SAMPLE, D_MODEL), lambda i: (0, 0)),
        ] + w_in_specs,
        out_specs=[pl.BlockSpec((WINDOW, D_MODEL), lambda i: (i, 0))] + w_out_specs,
        out_shape=[jax.ShapeDtypeStruct((ROWS, D_MODEL), BF16)] + w_out_shapes,
        compiler_params=_params(1),
        name="attn_prompt",
    )(sinks, q, kv, kv, kv, kv, attn_s, *[entry[0] for entry in weights])
    return outs[0], outs[1:]


SEQS_PER_STEP = 8


def _attn_sample_kernel(sink_ref, q_ref, kvn_ref, ck_ref, cv_ref, o_ref, nk_ref, nv_ref):
    head = lax.broadcasted_iota(jnp.int32, (N_HEADS, KV_DIM), 0) // GROUP
    col_group = lax.broadcasted_iota(jnp.int32, (N_HEADS, KV_DIM), 1) // HEAD_DIM
    own = head == col_group
    newest = lax.broadcasted_iota(jnp.int32, (1, WINDOW), 1) == WINDOW - 1
    scores = []
    for b in range(SEQS_PER_STEP):
        nk_ref[b] = jnp.where(newest, kvn_ref[0, 0:KV_DIM, b:b + 1], pltpu.roll(ck_ref[b], WINDOW - 1, 1))
        nv_ref[b] = jnp.where(newest, kvn_ref[0, KV_DIM:2 * KV_DIM, b:b + 1],
                              pltpu.roll(cv_ref[b], WINDOW - 1, 1))
        qb = q_ref[b]
        qe = jnp.where(own, jnp.concatenate([qb] * N_KV_HEADS, axis=1), jnp.zeros((), BF16))
        scores.append(_dot(qe, nk_ref[b].astype(BF16)))
    s = jnp.concatenate(scores, axis=0)
    sink = jnp.concatenate([sink_ref[...]] * SEQS_PER_STEP, axis=0)
    mx = jnp.maximum(jnp.max(s, axis=-1, keepdims=True), sink)
    e = jnp.exp2(s - mx)
    inv = 1.0 / (jnp.sum(e, axis=-1, keepdims=True) + jnp.exp2(sink - mx))
    e = e.astype(BF16)
    for b in range(SEQS_PER_STEP):
        rows = slice(b * N_HEADS, (b + 1) * N_HEADS)
        of = lax.dot_general(e[rows], nv_ref[b].astype(BF16), (((1,), (1,)), ((), ())),
                             preferred_element_type=F32)
        of = jnp.where(own, of, 0.0)
        o = of[:, 0:HEAD_DIM]
        for g in range(1, N_KV_HEADS):
            o = o + of[:, g * HEAD_DIM:(g + 1) * HEAD_DIM]
        o_ref[b] = (o * inv[rows]).astype(BF16)


def _attention_sample(sink_col, q3, kvn_t, ck, cv):
    nb = N_SAMPLE // SEQS_PER_STEP
    cache_spec = pl.BlockSpec((SEQS_PER_STEP, KV_DIM, WINDOW), lambda i: (i, 0, 0))
    return pl.pallas_call(
        _attn_sample_kernel,
        grid=(nb,),
        in_specs=[
            pl.BlockSpec((N_HEADS, 1), lambda i: (0, 0)),
            pl.BlockSpec((SEQS_PER_STEP, N_HEADS, HEAD_DIM), lambda i: (i, 0, 0)),
            pl.BlockSpec((1, 2 * KV_DIM, SEQS_PER_STEP), lambda i: (i, 0, 0)),
            cache_spec, cache_spec,
        ],
        out_specs=[
            pl.BlockSpec((SEQS_PER_STEP, N_HEADS, HEAD_DIM), lambda i: (i, 0, 0)),
            cache_spec, cache_spec,
        ],
        out_shape=[
            jax.ShapeDtypeStruct((N_SAMPLE, N_HEADS, HEAD_DIM), BF16),
            jax.ShapeDtypeStruct((N_SAMPLE, KV_DIM, WINDOW), F32),
            jax.ShapeDtypeStruct((N_SAMPLE, KV_DIM, WINDOW), F32),
        ],
        compiler_params=_params(1),
        name="attn_sample",
    )(sink_col, q3, kvn_t, ck, cv)


POOL_PAD = 16


def _pool_kernel(xb_ref, wu0_ref, wu1_ref, sp_ref, wmix_ref, scale_ref, p_ref, spn_ref, plast_ref, ext):
    m = pl.program_id(0)

    @pl.when(m == 0)
    def _():
        ext[0:POOL_PAD, :] = jnp.zeros((POOL_PAD, POOL_WIDTH), F32)

    pos = m * TM + lax.broadcasted_iota(jnp.int32, (TM, 1), 0)
    for t, wu_ref in enumerate((wu0_ref, wu1_ref)):
        ext[POOL_PAD:POOL_PAD + TM, t * TN:(t + 1) * TN] = _dot(xb_ref[...], wu_ref[...])
    for g, w in enumerate(POOL_WINDOWS):
        cols = slice(g * POOL_GROUP_DIM, (g + 1) * POOL_GROUP_DIM)
        u_g = ext[POOL_PAD:POOL_PAD + TM, cols]
        run = ext[:, cols]
        shift = 1
        while shift < w:
            run = run + pltpu.roll(run, shift, 0)
            shift *= 2
        tot = run[POOL_PAD:POOL_PAD + TM, :]
        inv_cnt = 1.0 / jnp.minimum(w, pos + 1).astype(F32)
        d = tot * inv_cnt - u_g
        y = _dot(d.astype(BF16), wmix_ref[g].astype(BF16))
        p_ref[:, cols] = (y * scale_ref[:, cols]).astype(BF16)

    @pl.when(m == NT - 1)
    def _():
        for g, w in enumerate(POOL_WINDOWS):
            cols = slice(g * POOL_GROUP_DIM, (g + 1) * POOL_GROUP_DIM)
            us = ext[POOL_PAD + SPLIT:POOL_PAD + TM, cols]
            tot = us
            for j in range(1, w):
                tot = tot + sp_ref[POOL_HIST - j, :, cols]
            d = tot * (1.0 / w) - us
            y = _dot(d.astype(BF16), wmix_ref[g].astype(BF16))
            p_ref[SPLIT:TM, cols] = (y * scale_ref[:, cols]).astype(BF16)
        spn_ref[0:POOL_HIST - 1] = sp_ref[1:POOL_HIST]
        spn_ref[POOL_HIST - 1] = ext[POOL_PAD + SPLIT:POOL_PAD + TM, :]
        plast_ref[...] = ext[SPLIT:POOL_PAD + SPLIT, :]

    ext[0:POOL_PAD, :] = ext[TM:TM + POOL_PAD, :]


def _pool(xb, wu0, wu1, sp_t, wmix, scale):
    full = lambda shape: pl.BlockSpec(shape, lambda m: (0,) * len(shape))
    return pl.pallas_call(
        _pool_kernel,
        grid=(NT,),
        in_specs=[
            pl.BlockSpec((TM, D_MODEL), lambda m: (m, 0)),
            _resident((D_MODEL, TN)), _resident((D_MODEL, TN)),
            full((POOL_HIST, N_SAMPLE, POOL_WIDTH)),
            full((len(POOL_WINDOWS), POOL_GROUP_DIM, POOL_GROUP_DIM)),
            full((1, POOL_WIDTH)),
        ],
        out_specs=[pl.BlockSpec((TM, POOL_WIDTH), lambda m: (m, 0)),
                   full((POOL_HIST, N_SAMPLE, POOL_WIDTH)),
                   full((POOL_PAD, POOL_WIDTH))],
        out_shape=[jax.ShapeDtypeStruct((ROWS, POOL_WIDTH), BF16),
                   jax.ShapeDtypeStruct((POOL_HIST, N_SAMPLE, POOL_WIDTH), F32),
                   jax.ShapeDtypeStruct((POOL_PAD, POOL_WIDTH), F32)],
        scratch_shapes=[pltpu.VMEM((POOL_PAD + TM, POOL_WIDTH), F32)],
        compiler_params=_params(1),
        name="pool_mix",
    )(xb, wu0, wu1, sp_t, wmix, scale)


_NC = D_MODEL // TN


TMH = TM // 2
NTH = ROWS // TMH
SPLIT_H = SEQ - (NTH - 1) * TMH
assert NTH * TMH == ROWS and SPLIT_H + N_SAMPLE == TMH and SPLIT_H % 8 == 0


def _resident(shape):
    return pl.BlockSpec(shape, lambda m: (0,) * len(shape), pipeline_mode=pl.Buffered(1))


def _merge_kernel(xb_ref, pool_ref, attn_ref, wp_ref, wa_ref, *refs):
    gate_w, o_ref = refs[:2 * _NC], refs[2 * _NC]
    for c in range(_NC):
        cols = slice(c * TN, (c + 1) * TN)
        gate_pool = jax.nn.sigmoid(_dot(xb_ref[...], gate_w[c][...]))
        gate_attn = jax.nn.sigmoid(_dot(xb_ref[...], gate_w[_NC + c][...]))
        a = _dot(pool_ref[...], wp_ref[:, cols])
        b = _dot(attn_ref[...], wa_ref[:, cols])
        o_ref[:, cols] = (gate_pool * a + gate_attn * b).astype(BF16)


def _merge(xb, pooled, attn, wp, wa, gate_w):
    row = lambda width: pl.BlockSpec((TMH, width), lambda m: (m, 0))
    return pl.pallas_call(
        _merge_kernel,
        grid=(NTH,),
        in_specs=[row(D_MODEL), row(POOL_WIDTH), row(D_MODEL),
                  _resident((POOL_WIDTH, D_MODEL)), _resident((D_MODEL, D_MODEL))]
        + [_resident((D_MODEL, TN))] * (2 * _NC),
        out_specs=row(D_MODEL),
        out_shape=jax.ShapeDtypeStruct((ROWS, D_MODEL), BF16),
        compiler_params=_params(1),
        name="branch_merge",
    )(xb, pooled, attn, wp, wa, *gate_w)


def _out_ln_kernel(mrg_ref, wo_ref, x_ref, xs_ref, lng_ref, lnb_ref, x1_ref, x1b_ref):
    m = pl.program_id(0)
    for c in range(_NC):
        cols = slice(c * TN, (c + 1) * TN)
        x1_ref[:, cols] = _dot(mrg_ref[...], wo_ref[:, cols])

    def finish(z):
        x1 = _layer_norm(z, lng_ref[...], lnb_ref[...])
        x1_ref[...] = ALPHA * x1
        x1b_ref[...] = x1.astype(BF16)

    @pl.when(m < NTH - 1)
    def _():
        finish(x1_ref[...] + ALPHA * x_ref[...])

    @pl.when(m == NTH - 1)
    def _():
        finish(jnp.concatenate([x1_ref[0:SPLIT_H, :] + ALPHA * x_ref[0:SPLIT_H, :],
                                x1_ref[SPLIT_H:TMH, :] + ALPHA * xs_ref[...]], axis=0))


def _out_ln(merged, wo, x, xs, ln_g, ln_b):
    row = pl.BlockSpec((TMH, D_MODEL), lambda m: (m, 0))
    return pl.pallas_call(
        _out_ln_kernel,
        grid=(NTH,),
        in_specs=[row, _resident((D_MODEL, D_MODEL)), row, _resident((N_SAMPLE, D_MODEL)),
                  _resident((1, D_MODEL)), _resident((1, D_MODEL))],
        out_specs=[row, row],
        out_shape=[jax.ShapeDtypeStruct((ROWS, D_MODEL), F32), jax.ShapeDtypeStruct((ROWS, D_MODEL), BF16)],
        compiler_params=_params(1),
        name="out_proj_ln1",
    )(merged, wo, x, xs, ln_g, ln_b)


TF = 512
NF = D_FF // TF
assert TF == TN
CARRY = 8


def _ffn_kernel(x1b_ref, x1_hbm, sc_ref, wg_ref, wu_ref, wd_ref, cw_ref, cb_ref, lng_ref, lnb_ref,
                o_ref, os_ref, glast_ref, scn_ref,
                h0, h1, gext, carry, sem):
    m = pl.program_id(0)
    f = pl.program_id(1)
    last_m = m == NT - 1
    h_slots = (h0, h1)
    x1_copy = pltpu.make_async_copy(x1_hbm.at[pl.ds(m * TM, TM), :], o_ref, sem)

    def up_and_gate(slot, with_down):
        halves = ((0, HALF), (HALF, TM))
        gext[0:CARRY, :] = carry[f]
        for lo, hi in halves:
            gext[CARRY + lo:CARRY + hi, :] = _dot(x1b_ref[lo:hi, :], wg_ref[0])
        ups = [_dot(x1b_ref[lo:hi, :], wu_ref[0]) for lo, hi in halves]
        if with_down:
            o_ref[...] += _dot(h_slots[1 - slot][...], wd_ref[...])
        w0 = cw_ref[0:1, :]
        w1 = cw_ref[1:2, :]
        w2 = cw_ref[2:3, :]
        cb = cb_ref[...]

        def conv(lo, hi, older=None, newer=None):
            older = gext[CARRY - 2 + lo:CARRY - 2 + hi, :] if older is None else older
            newer = gext[CARRY - 1 + lo:CARRY - 1 + hi, :] if newer is None else newer
            return cb + w0 * older + w1 * newer + w2 * gext[CARRY + lo:CARRY + hi, :]

        for (lo, hi), up in zip(halves, ups):
            if hi <= SPLIT:
                gc = conv(lo, hi)
            else:
                older_s = jnp.where(last_m, sc_ref[:, 0, :], gext[CARRY - 2 + SPLIT:CARRY - 2 + TM, :])
                newer_s = jnp.where(last_m, sc_ref[:, 1, :], gext[CARRY - 1 + SPLIT:CARRY - 1 + TM, :])
                gc = jnp.concatenate([conv(lo, SPLIT), conv(SPLIT, TM, older_s, newer_s)], axis=0)
            h_slots[slot][lo:hi, :] = (jax.nn.gelu(gc, approximate=True) * up).astype(BF16)
        carry[f] = gext[TM:CARRY + TM, :]

        @pl.when(last_m)
        def _():
            glast_ref[...] = gext[SPLIT:CARRY + SPLIT, :]
            scn_ref[:, 0, :] = sc_ref[:, 1, :]
            scn_ref[:, 1, :] = gext[CARRY + SPLIT:CARRY + TM, :]

    @pl.when(f == 0)
    def _():
        @pl.when(m == 0)
        def _():
            carry[...] = jnp.zeros_like(carry)

        x1_copy.start()
        up_and_gate(0, False)
        x1_copy.wait()

    for parity in range(2):
        @pl.when((f > 0) & (f < NF) & (f % 2 == parity))
        def _():
            up_and_gate(parity, True)

    @pl.when(f == NF)
    def _():
        h_last = h_slots[(NF - 1) % 2]
        parts = ((0, 352), (352, 704), (704, TM))
        accs = [o_ref[lo:hi, :] + _dot(h_last[lo:hi, :], wd_ref[...]) for lo, hi in parts]
        for (lo, hi), acc in zip(parts, accs):
            o_ref[lo:hi, :] = _layer_norm(acc, lng_ref[...], lnb_ref[...])

        @pl.when(last_m)
        def _():
            os_ref[...] = o_ref[SPLIT:TM, :]


def _ffn(x1b, x1, sc, wg, wu, wd, conv_w, conv_b, ln_g, ln_b):
    full = lambda shape: pl.BlockSpec(shape, lambda m, f: (0, 0))
    g_tile = lambda f: jnp.minimum(f, NF - 1)
    d_tile = lambda f: jnp.maximum(f - 1, 0)
    only_last = lambda m, f: jnp.where(m == NT - 1, g_tile(f), 0)
    row = pl.BlockSpec((TM, D_MODEL), lambda m, f: (m, 0))
    return pl.pallas_call(
        _ffn_kernel,
        grid=(NT, NF + 1),
        in_specs=[
            row,
            pl.BlockSpec(memory_space=pl.ANY),
            pl.BlockSpec((N_SAMPLE, CONV_W - 1, TF), lambda m, f: (0, 0, only_last(m, f))),
            pl.BlockSpec((1, D_MODEL, TF), lambda m, f: (g_tile(f), 0, 0)),
            pl.BlockSpec((1, D_MODEL, TF), lambda m, f: (g_tile(f), 0, 0)),
            pl.BlockSpec((TF, D_MODEL), lambda m, f: (d_tile(f), 0)),
            pl.BlockSpec((CONV_W, TF), lambda m, f: (0, g_tile(f))),
            pl.BlockSpec((1, TF), lambda m, f: (0, g_tile(f))),
            full((1, D_MODEL)), full((1, D_MODEL)),
        ],
        out_specs=[
            row,
            full((N_SAMPLE, D_MODEL)),
            pl.BlockSpec((CARRY, TF), lambda m, f: (0, only_last(m, f))),
            pl.BlockSpec((N_SAMPLE, CONV_W - 1, TF), lambda m, f: (0, 0, only_last(m, f))),
        ],
        out_shape=[
            jax.ShapeDtypeStruct((SEQ, D_MODEL), F32),
            jax.ShapeDtypeStruct((N_SAMPLE, D_MODEL), F32),
            jax.ShapeDtypeStruct((CARRY, D_FF), F32),
            jax.ShapeDtypeStruct((N_SAMPLE, CONV_W - 1, D_FF), F32),
        ],
        scratch_shapes=[
            pltpu.VMEM((TM, TF), BF16), pltpu.VMEM((TM, TF), BF16),
            pltpu.VMEM((CARRY + TM, TF), F32),
            pltpu.VMEM((NF, CARRY, TF), F32),
            pltpu.SemaphoreType.DMA(()),
        ],
        compiler_params=_params(2),
        name="convffn_ln2",
    )(x1b, x1, sc, wg, wu, wd, conv_w, conv_b, ln_g, ln_b)


def _rope_tables(pos):
    half = HEAD_DIM // 2
    inv = ROPE_THETA ** (-jnp.arange(half, dtype=F32) / half)
    lane = jnp.arange(LANES)
    inv_lanes = inv[lane % half]
    sign = jnp.where((lane % HEAD_DIM) < half, -1.0, 1.0).astype(F32)
    ang = pos.astype(F32)[:, None] * inv_lanes[None, :]
    return jnp.cos(ang), jnp.sin(ang) * sign[None, :]


def kernel(x_prompt, x_sample, cache_k, cache_v, state_pool, state_conv, w_in, attn_sinks, w_pool_mix,
           pool_scale, w_attn_branch, w_pool_branch, w_out, ln1_g, ln1_b, w_up, w_gate, conv_w, conv_b,
           w_down, ln2_g, ln2_b):
    x = x_prompt.reshape(SEQ, D_MODEL)
    xs = x_sample.reshape(N_SAMPLE, D_MODEL)
    pos = jnp.concatenate([jnp.arange(SEQ), jnp.full((N_SAMPLE,), PAST_LEN)])
    cos, sin = _rope_tables(pos)

    q, kv, xb = _in_proj(x, xs, w_in[0], cos, sin)

    to_dsw = lambda c: jnp.transpose(c[0], (0, 2, 3, 1)).reshape(N_SAMPLE, KV_DIM, WINDOW)
    steps = N_SAMPLE // SEQS_PER_STEP
    kvn_t = jnp.transpose(kv[SEQ:].reshape(steps, SEQS_PER_STEP, 2 * KV_DIM), (0, 2, 1))
    sinks = attn_sinks[0] * LOG2E
    attn_s, new_k, new_v = _attention_sample(
        sinks.reshape(N_HEADS, 1), q[SEQ:].reshape(N_SAMPLE, N_HEADS, HEAD_DIM), kvn_t,
        to_dsw(cache_k), to_dsw(cache_v))
    whole = lambda w, tiled=False: (w, 0, w.shape[1], tiled)
    late_tiles = [(w_in[0], j, TN, False) for j in range(_N_U0, _N_G0 + 2 * _NC)]
    attn, (wp_b, wa_b, wo_b, wg_b, wu_b, wd_b, wpool0, wpool1, *gate_w) = _attention(
        sinks, q, kv, attn_s.reshape(N_SAMPLE, D_MODEL),
        [whole(w_pool_branch[0]), whole(w_attn_branch[0]), whole(w_out[0]),
         whole(w_gate[0], tiled=True), whole(w_up[0], tiled=True), whole(w_down[0])] + late_tiles)

    state_pool_t = jnp.transpose(state_pool[0], (1, 0, 2))
    pooled, new_pool_t, pool_last = _pool(xb, wpool0, wpool1, state_pool_t, w_pool_mix[0], pool_scale)

    merged = _merge(xb, pooled, attn, wp_b, wa_b, gate_w)
    x1, x1b = _out_ln(merged, wo_b, x, xs, ln1_g, ln1_b)

    y, y_s, g_last, new_conv_s = _ffn(x1b, x1, state_conv[0], wg_b, wu_b, wd_b,
                                      conv_w[0], conv_b, ln2_g, ln2_b)

    from_dsw = lambda c: jnp.transpose(
        c.reshape(N_SAMPLE, N_KV_HEADS, HEAD_DIM, WINDOW), (0, 3, 1, 2))[None]
    return (
        y.reshape(1, SEQ, D_MODEL),
        y_s.reshape(N_SAMPLE, 1, D_MODEL),
        kv[SEQ - WINDOW:SEQ, :KV_DIM].reshape(1, 1, WINDOW, N_KV_HEADS, HEAD_DIM),
        kv[SEQ - WINDOW:SEQ, KV_DIM:].reshape(1, 1, WINDOW, N_KV_HEADS, HEAD_DIM),
        pool_last[POOL_PAD - POOL_HIST:].reshape(1, 1, POOL_HIST, POOL_WIDTH),
        g_last[CARRY - (CONV_W - 1):].reshape(1, 1, CONV_W - 1, D_FF),
        from_dsw(new_k),
        from_dsw(new_v),
        jnp.transpose(new_pool_t, (1, 0, 2))[None],
        new_conv_s[None],
    )
```

```python
import jax
import jax.numpy as jnp
from jax import lax
from jax.experimental import pallas as pl
from jax.experimental.pallas import tpu as pltpu

F32 = jnp.float32
BF16 = jnp.bfloat16

D_MODEL = 2048
SEQ = 8192
N_SAMPLE = 128
PAST_LEN = 8192
HEAD_DIM = 64
N_HEADS = 32
N_KV_HEADS = 4
GROUP = N_HEADS // N_KV_HEADS
KV_DIM = N_KV_HEADS * HEAD_DIM
WINDOW = 128
ROPE_THETA = 10000.0
POOL_WINDOWS = (2, 4, 8, 16)
POOL_WIDTH = 1024
POOL_GROUP_DIM = 256
POOL_HIST = 15
D_FF = 5632
CONV_W = 3
IN_DIM = 7680
LN_EPS = 1e-5
NEG_INF = -1e30
ALPHA = 2.0 ** 0.25
LOG2E = 1.4426950408889634

LANES = 128
VMEM_LIMIT = 60000 * 1024

ROWS = SEQ + N_SAMPLE
TM = 1040
NT = ROWS // TM
SPLIT = SEQ - (NT - 1) * TM
HALF = 528
TN = 512
assert NT * TM == ROWS and SPLIT + N_SAMPLE == TM and SPLIT % 16 == 0 and HALF % 16 == 0


def _params(ndim, flags=None):
    return pltpu.CompilerParams(dimension_semantics=("arbitrary",) * ndim,
                                vmem_limit_bytes=VMEM_LIMIT, flags=flags)


def _dot(a, b):
    return jnp.dot(a, b, preferred_element_type=F32)


def _layer_norm(z, g, b):
    mu = jnp.mean(z, axis=-1, keepdims=True)
    d = z - mu
    var = jnp.mean(d * d, axis=-1, keepdims=True)
    return d * lax.rsqrt(var + LN_EPS) * g + b


def _rope(x, cos, sin_signed):
    lane = lax.broadcasted_iota(jnp.int32, (1, LANES), 1)
    low_half = (lane % HEAD_DIM) < (HEAD_DIM // 2)
    outs = []
    for c in range(x.shape[1] // LANES):
        xc = x[:, c * LANES:(c + 1) * LANES]
        up = pltpu.roll(xc, LANES - HEAD_DIM // 2, 1)
        down = pltpu.roll(xc, HEAD_DIM // 2, 1)
        outs.append(xc * cos + jnp.where(low_half, up, down) * sin_signed)
    return jnp.concatenate(outs, axis=1)


_NQ = D_MODEL // TN
_N_KV = _NQ
_N_U0 = _N_KV + 1
_N_G0 = _N_U0 + POOL_WIDTH // TN


def _in_proj_kernel(x_ref, xs_ref, w_ref, cos_ref, sin_ref, q_ref, kv_ref, xb, wres):
    m = pl.program_id(0)
    n = pl.program_id(1)

    @pl.when(n == 0)
    def _():
        @pl.when(m < NT - 1)
        def _():
            xb[...] = x_ref[...].astype(BF16)

        @pl.when(m == NT - 1)
        def _():
            xb[0:SPLIT, :] = x_ref[0:SPLIT, :].astype(BF16)
            xb[SPLIT:TM, :] = xs_ref[...].astype(BF16)

    @pl.when(m == 0)
    def _():
        wres[n] = w_ref[...].astype(BF16)

    halves = ((0, HALF), (HALF, TM))
    thirds = ((0, 352), (352, 704), (704, TM))

    def project(epilogue, parts):
        accs = [_dot(xb[lo:hi, :], wres[n]) for lo, hi in parts]
        for (lo, hi), acc in zip(parts, accs):
            epilogue(acc, lo, hi)

    def q_epilogue(acc, lo, hi):
        roped = _rope(acc, cos_ref[lo:hi, :], sin_ref[lo:hi, :])
        q_ref[lo:hi, :] = (roped * (HEAD_DIM ** -0.5 * LOG2E)).astype(BF16)

    def kv_epilogue(acc, lo, hi):
        kv_ref[lo:hi, :KV_DIM] = _rope(acc[:, :KV_DIM], cos_ref[lo:hi, :], sin_ref[lo:hi, :])
        kv_ref[lo:hi, KV_DIM:] = acc[:, KV_DIM:]

    pl.when(n < _NQ)(lambda: project(q_epilogue, thirds))
    pl.when(n == _N_KV)(lambda: project(kv_epilogue, halves))


def _in_proj(x, xs, w_in, cos, sin):
    nn = _N_U0

    return pl.pallas_call(
        _in_proj_kernel,
        grid=(NT, nn),
        in_specs=[
            pl.BlockSpec((TM, D_MODEL), lambda m, n: (m, 0)),
            pl.BlockSpec((N_SAMPLE, D_MODEL), lambda m, n: (0, 0)),
            pl.BlockSpec((D_MODEL, TN), lambda m, n: (0, jnp.where(m == 0, n, nn - 1))),
            pl.BlockSpec((TM, LANES), lambda m, n: (m, 0)),
            pl.BlockSpec((TM, LANES), lambda m, n: (m, 0)),
        ],
        out_specs=[
            pl.BlockSpec((TM, TN), lambda m, n: (m, jnp.minimum(n, _NQ - 1))),
            pl.BlockSpec((TM, TN), lambda m, n: (m, 0)),
            pl.BlockSpec((TM, D_MODEL), lambda m, n: (m, 0)),
        ],
        out_shape=[
            jax.ShapeDtypeStruct((ROWS, D_MODEL), BF16),
            jax.ShapeDtypeStruct((ROWS, 2 * KV_DIM), F32),
            jax.ShapeDtypeStruct((ROWS, D_MODEL), BF16),
        ],
        scratch_shapes=[pltpu.VMEM((nn, D_MODEL, TN), BF16)],
        compiler_params=_params(2),
        name="in_proj",
    )(x, xs, w_in, cos, sin)


PAIRS = GROUP // 2


def _attn_kernel(sink_ref, q_ref, kp_ref, kc_ref, vp_ref, vc_ref, os_ref, *refs):
    n_w = (len(refs) - 1) // 2
    w_refs, o_ref, wb_refs = refs[:n_w], refs[n_w], refs[n_w + 1:]
    i = pl.program_id(0)
    last = pl.num_programs(0) - 1

    @pl.when(i < last)
    def _():
        for w_ref, wb_ref in zip(w_refs, wb_refs):
            wb_ref[...] = w_ref[...].astype(BF16)
        _attn_block(i, sink_ref, q_ref, kp_ref, kc_ref, vp_ref, vc_ref, o_ref)

    @pl.when(i == last)
    def _():
        o_ref[...] = os_ref[...]


def _attn_block(i, sink_ref, q_ref, kp_ref, kc_ref, vp_ref, vc_ref, o_ref):
    kk =jnp.concatenate([kp_ref[...], kc_ref[...]], axis=0)
    vv = jnp.concatenate([vp_ref[...], vc_ref[...]], axis=0)
    low = lax.broadcasted_iota(jnp.int32, (1, LANES), 1) < HEAD_DIM
    rows = PAIRS * WINDOW
    r = lax.broadcasted_iota(jnp.int32, (rows, 2 * WINDOW), 0) % WINDOW
    c = lax.broadcasted_iota(jnp.int32, (rows, 2 * WINDOW), 1)
    visible = (c > r) & (c <= r + WINDOW) & ((i > 0) | (c >= WINDOW))
    chunk = lax.broadcasted_iota(jnp.int32, (rows, 1), 0) // WINDOW

    def sink_column(first_head):
        col = jnp.full((rows, 1), sink_ref[first_head], F32)
        for t in range(1, PAIRS):
            col = jnp.where(chunk == t, sink_ref[first_head + 2 * t], col)
        return col

    for pair in range(N_KV_HEADS // 2):
        k_pair = kk[:, pair * LANES:(pair + 1) * LANES]
        v_pair = vv[:, pair * LANES:(pair + 1) * LANES]
        k_swap = pltpu.roll(k_pair, HEAD_DIM, 1)
        v_swap = pltpu.roll(v_pair, HEAD_DIM, 1)
        for second in range(2):
            g = 2 * pair + second
            k_in_low, k_in_high = (k_swap, k_pair) if second else (k_pair, k_swap)
            v_in_low, v_in_high = (v_swap, v_pair) if second else (v_pair, v_swap)
            x = jnp.concatenate(
                [q_ref[:, (g * PAIRS + t) * LANES:(g * PAIRS + t + 1) * LANES] for t in range(PAIRS)], axis=0)

            def half(k_pad, v_pad, sink):
                s = lax.dot_general(x, k_pad.astype(BF16), (((1,), (1,)), ((), ())),
                                    preferred_element_type=F32)
                s = jnp.where(visible, s, NEG_INF)
                mx = jnp.maximum(jnp.max(s, axis=-1, keepdims=True), sink)
                e = jnp.exp2(s - mx).astype(BF16)
                return _dot(e, v_pad.astype(BF16)), jnp.exp2(sink - mx)

            a, sink_a = half(jnp.where(low, k_in_low, 0.0), jnp.where(low, v_in_low, 1.0),
                             sink_column(g * GROUP))
            b, sink_b = half(jnp.where(low, 0.0, k_in_high), jnp.where(low, 1.0, v_in_high),
                             sink_column(g * GROUP + 1))
            den = pltpu.roll(jnp.where(low, b, a), HEAD_DIM, 1) + jnp.where(low, sink_a, sink_b)
            out = (jnp.where(low, a, b) / den).astype(BF16)
            for t in range(PAIRS):
                o_ref[:, (g * PAIRS + t) * LANES:(g * PAIRS + t + 1) * LANES] = out[t * WINDOW:(t + 1) * WINDOW]


def _cast_chunk_rows(rows, steps):
    assert rows % 16 == 0
    units = rows // 16
    n = max(d for d in range(1, steps + 1) if units % d == 0)
    return rows // n


def _attention(sinks, q, kv, attn_s, weights):
    nb = SEQ // WINDOW
    cur = lambda i: jnp.minimum(i, nb - 1)
    prev = lambda i: jnp.clip(i - 1, 0, nb - 1)
    w_in_specs, w_out_specs, w_out_shapes = [], [], []
    for w, col, width in weights:
        chunk = _cast_chunk_rows(w.shape[0], nb)
        n = w.shape[0] // chunk
        w_in_specs.append(pl.BlockSpec((chunk, width), lambda i, n=n, col=col: (jnp.minimum(i, n - 1), col)))
        w_out_specs.append(pl.BlockSpec((chunk, width), lambda i, n=n: (jnp.minimum(i, n - 1), 0)))
        w_out_shapes.append(jax.ShapeDtypeStruct((w.shape[0], width), BF16))
    outs = pl.pallas_call(
        _attn_kernel,
        grid=(nb + 1,),
        in_specs=[
            pl.BlockSpec(memory_space=pltpu.SMEM),
            pl.BlockSpec((WINDOW, D_MODEL), lambda i: (cur(i), 0)),
            pl.BlockSpec((WINDOW, KV_DIM), lambda i: (prev(i), 0)),
            pl.BlockSpec((WINDOW, KV_DIM), lambda i: (cur(i), 0)),
            pl.BlockSpec((WINDOW, KV_DIM), lambda i: (prev(i), 1)),
            pl.BlockSpec((WINDOW, KV_DIM), lambda i: (cur(i), 1)),
            pl.BlockSpec((N_SAMPLE, D_MODEL), lambda i: (0, 0)),
        ] + w_in_specs,
        out_specs=[pl.BlockSpec((WINDOW, D_MODEL), lambda i: (i, 0))] + w_out_specs,
        out_shape=[jax.ShapeDtypeStruct((ROWS, D_MODEL), BF16)] + w_out_shapes,
        compiler_params=_params(1),
        name="attn_prompt",
    )(sinks, q, kv, kv, kv, kv, attn_s, *[w for w, _, _ in weights])
    return outs[0], outs[1:]


SEQS_PER_STEP = 8


def _attn_sample_kernel(sink_ref, q_ref, kvn_ref, ck_ref, cv_ref, o_ref, nk_ref, nv_ref):
    head = lax.broadcasted_iota(jnp.int32, (N_HEADS, KV_DIM), 0) // GROUP
    col_group = lax.broadcasted_iota(jnp.int32, (N_HEADS, KV_DIM), 1) // HEAD_DIM
    own = head == col_group
    newest = lax.broadcasted_iota(jnp.int32, (1, WINDOW), 1) == WINDOW - 1
    scores = []
    for b in range(SEQS_PER_STEP):
        nk_ref[b] = jnp.where(newest, kvn_ref[0, 0:KV_DIM, b:b + 1], pltpu.roll(ck_ref[b], WINDOW - 1, 1))
        nv_ref[b] = jnp.where(newest, kvn_ref[0, KV_DIM:2 * KV_DIM, b:b + 1],
                              pltpu.roll(cv_ref[b], WINDOW - 1, 1))
        qb = q_ref[b]
        qe = jnp.where(own, jnp.concatenate([qb] * N_KV_HEADS, axis=1), jnp.zeros((), BF16))
        scores.append(_dot(qe, nk_ref[b].astype(BF16)))
    s = jnp.concatenate(scores, axis=0)
    sink = jnp.concatenate([sink_ref[...]] * SEQS_PER_STEP, axis=0)
    mx = jnp.maximum(jnp.max(s, axis=-1, keepdims=True), sink)
    e = jnp.exp2(s - mx)
    inv = 1.0 / (jnp.sum(e, axis=-1, keepdims=True) + jnp.exp2(sink - mx))
    e = e.astype(BF16)
    for b in range(SEQS_PER_STEP):
        rows = slice(b * N_HEADS, (b + 1) * N_HEADS)
        of = lax.dot_general(e[rows], nv_ref[b].astype(BF16), (((1,), (1,)), ((), ())),
                             preferred_element_type=F32)
        of = jnp.where(own, of, 0.0)
        o = of[:, 0:HEAD_DIM]
        for g in range(1, N_KV_HEADS):
            o = o + of[:, g * HEAD_DIM:(g + 1) * HEAD_DIM]
        o_ref[b] = (o * inv[rows]).astype(BF16)


def _attention_sample(sink_col, q3, kvn_t, ck, cv):
    nb = N_SAMPLE // SEQS_PER_STEP
    cache_spec = pl.BlockSpec((SEQS_PER_STEP, KV_DIM, WINDOW), lambda i: (i, 0, 0))
    return pl.pallas_call(
        _attn_sample_kernel,
        grid=(nb,),
        in_specs=[
            pl.BlockSpec((N_HEADS, 1), lambda i: (0, 0)),
            pl.BlockSpec((SEQS_PER_STEP, N_HEADS, HEAD_DIM), lambda i: (i, 0, 0)),
            pl.BlockSpec((1, 2 * KV_DIM, SEQS_PER_STEP), lambda i: (i, 0, 0)),
            cache_spec, cache_spec,
        ],
        out_specs=[
            pl.BlockSpec((SEQS_PER_STEP, N_HEADS, HEAD_DIM), lambda i: (i, 0, 0)),
            cache_spec, cache_spec,
        ],
        out_shape=[
            jax.ShapeDtypeStruct((N_SAMPLE, N_HEADS, HEAD_DIM), BF16),
            jax.ShapeDtypeStruct((N_SAMPLE, KV_DIM, WINDOW), F32),
            jax.ShapeDtypeStruct((N_SAMPLE, KV_DIM, WINDOW), F32),
        ],
        compiler_params=_params(1),
        name="attn_sample",
    )(sink_col, q3, kvn_t, ck, cv)


POOL_PAD = 16


def _pool_kernel(xb_ref, wu0_ref, wu1_ref, sp_ref, wmix_ref, scale_ref, p_ref, spn_ref, plast_ref, ext):
    m = pl.program_id(0)

    @pl.when(m == 0)
    def _():
        ext[0:POOL_PAD, :] = jnp.zeros((POOL_PAD, POOL_WIDTH), F32)

    pos = m * TM + lax.broadcasted_iota(jnp.int32, (TM, 1), 0)
    for t, wu_ref in enumerate((wu0_ref, wu1_ref)):
        ext[POOL_PAD:POOL_PAD + TM, t * TN:(t + 1) * TN] = _dot(xb_ref[...], wu_ref[...])
    for g, w in enumerate(POOL_WINDOWS):
        cols = slice(g * POOL_GROUP_DIM, (g + 1) * POOL_GROUP_DIM)
        u_g = ext[POOL_PAD:POOL_PAD + TM, cols]
        run = ext[:, cols]
        shift = 1
        while shift < w:
            run = run + pltpu.roll(run, shift, 0)
            shift *= 2
        tot = run[POOL_PAD:POOL_PAD + TM, :]
        inv_cnt = 1.0 / jnp.minimum(w, pos + 1).astype(F32)
        d = tot * inv_cnt - u_g
        y = _dot(d.astype(BF16), wmix_ref[g].astype(BF16))
        p_ref[:, cols] = (y * scale_ref[:, cols]).astype(BF16)

    @pl.when(m == NT - 1)
    def _():
        for g, w in enumerate(POOL_WINDOWS):
            cols = slice(g * POOL_GROUP_DIM, (g + 1) * POOL_GROUP_DIM)
            us = ext[POOL_PAD + SPLIT:POOL_PAD + TM, cols]
            tot = us
            for j in range(1, w):
                tot = tot + sp_ref[POOL_HIST - j, :, cols]
            d = tot * (1.0 / w) - us
            y = _dot(d.astype(BF16), wmix_ref[g].astype(BF16))
            p_ref[SPLIT:TM, cols] = (y * scale_ref[:, cols]).astype(BF16)
        spn_ref[0:POOL_HIST - 1] = sp_ref[1:POOL_HIST]
        spn_ref[POOL_HIST - 1] = ext[POOL_PAD + SPLIT:POOL_PAD + TM, :]
        plast_ref[...] = ext[SPLIT:POOL_PAD + SPLIT, :]

    ext[0:POOL_PAD, :] = ext[TM:TM + POOL_PAD, :]


def _pool(xb, wu0, wu1, sp_t, wmix, scale):
    full = lambda shape: pl.BlockSpec(shape, lambda m: (0,) * len(shape))
    return pl.pallas_call(
        _pool_kernel,
        grid=(NT,),
        in_specs=[
            pl.BlockSpec((TM, D_MODEL), lambda m: (m, 0)),
            _resident((D_MODEL, TN)), _resident((D_MODEL, TN)),
            full((POOL_HIST, N_SAMPLE, POOL_WIDTH)),
            full((len(POOL_WINDOWS), POOL_GROUP_DIM, POOL_GROUP_DIM)),
            full((1, POOL_WIDTH)),
        ],
        out_specs=[pl.BlockSpec((TM, POOL_WIDTH), lambda m: (m, 0)),
                   full((POOL_HIST, N_SAMPLE, POOL_WIDTH)),
                   full((POOL_PAD, POOL_WIDTH))],
        out_shape=[jax.ShapeDtypeStruct((ROWS, POOL_WIDTH), BF16),
                   jax.ShapeDtypeStruct((POOL_HIST, N_SAMPLE, POOL_WIDTH), F32),
                   jax.ShapeDtypeStruct((POOL_PAD, POOL_WIDTH), F32)],
        scratch_shapes=[pltpu.VMEM((POOL_PAD + TM, POOL_WIDTH), F32)],
        compiler_params=_params(1),
        name="pool_mix",
    )(xb, wu0, wu1, sp_t, wmix, scale)


_NC = D_MODEL // TN


TMH = TM // 2
NTH = ROWS // TMH
SPLIT_H = SEQ - (NTH - 1) * TMH
assert NTH * TMH == ROWS and SPLIT_H + N_SAMPLE == TMH and SPLIT_H % 8 == 0


def _resident(shape):
    return pl.BlockSpec(shape, lambda m: (0,) * len(shape), pipeline_mode=pl.Buffered(1))


def _merge_kernel(xb_ref, pool_ref, attn_ref, wp_ref, wa_ref, *refs):
    gate_w, o_ref = refs[:2 * _NC], refs[2 * _NC]
    for c in range(_NC):
        cols = slice(c * TN, (c + 1) * TN)
        gate_pool = jax.nn.sigmoid(_dot(xb_ref[...], gate_w[c][...]))
        gate_attn = jax.nn.sigmoid(_dot(xb_ref[...], gate_w[_NC + c][...]))
        a = _dot(pool_ref[...], wp_ref[:, cols])
        b = _dot(attn_ref[...], wa_ref[:, cols])
        o_ref[:, cols] = (gate_pool * a + gate_attn * b).astype(BF16)


def _merge(xb, pooled, attn, wp, wa, gate_w):
    row = lambda width: pl.BlockSpec((TMH, width), lambda m: (m, 0))
    return pl.pallas_call(
        _merge_kernel,
        grid=(NTH,),
        in_specs=[row(D_MODEL), row(POOL_WIDTH), row(D_MODEL),
                  _resident((POOL_WIDTH, D_MODEL)), _resident((D_MODEL, D_MODEL))]
        + [_resident((D_MODEL, TN))] * (2 * _NC),
        out_specs=row(D_MODEL),
        out_shape=jax.ShapeDtypeStruct((ROWS, D_MODEL), BF16),
        compiler_params=_params(1),
        name="branch_merge",
    )(xb, pooled, attn, wp, wa, *gate_w)


def _out_ln_kernel(mrg_ref, wo_ref, x_ref, xs_ref, lng_ref, lnb_ref, x1_ref, x1b_ref):
    m = pl.program_id(0)
    for c in range(_NC):
        cols = slice(c * TN, (c + 1) * TN)
        x1_ref[:, cols] = _dot(mrg_ref[...], wo_ref[:, cols])

    def finish(z):
        x1 = _layer_norm(z, lng_ref[...], lnb_ref[...])
        x1_ref[...] = ALPHA * x1
        x1b_ref[...] = x1.astype(BF16)

    @pl.when(m < NTH - 1)
    def _():
        finish(x1_ref[...] + ALPHA * x_ref[...])

    @pl.when(m == NTH - 1)
    def _():
        finish(jnp.concatenate([x1_ref[0:SPLIT_H, :] + ALPHA * x_ref[0:SPLIT_H, :],
                                x1_ref[SPLIT_H:TMH, :] + ALPHA * xs_ref[...]], axis=0))


def _out_ln(merged, wo, x, xs, ln_g, ln_b):
    row = pl.BlockSpec((TMH, D_MODEL), lambda m: (m, 0))
    return pl.pallas_call(
        _out_ln_kernel,
        grid=(NTH,),
        in_specs=[row, _resident((D_MODEL, D_MODEL)), row, _resident((N_SAMPLE, D_MODEL)),
                  _resident((1, D_MODEL)), _resident((1, D_MODEL))],
        out_specs=[row, row],
        out_shape=[jax.ShapeDtypeStruct((ROWS, D_MODEL), F32), jax.ShapeDtypeStruct((ROWS, D_MODEL), BF16)],
        compiler_params=_params(1),
        name="out_proj_ln1",
    )(merged, wo, x, xs, ln_g, ln_b)


TF = 512
NF = D_FF // TF
CARRY = 8


def _ffn_kernel(x1b_ref, x1_hbm, sc_ref, wg_ref, wu_ref, wd_ref, cw_ref, cb_ref, lng_ref, lnb_ref,
                o_ref, os_ref, glast_ref, scn_ref,
                h0, h1, gext, carry, sem):
    m = pl.program_id(0)
    f = pl.program_id(1)
    last_m = m == NT - 1
    h_slots = (h0, h1)
    x1_copy = pltpu.make_async_copy(x1_hbm.at[pl.ds(m * TM, TM), :], o_ref, sem)

    def up_and_gate(slot, with_down):
        halves = ((0, HALF), (HALF, TM))
        gext[0:CARRY, :] = carry[f]
        for lo, hi in halves:
            gext[CARRY + lo:CARRY + hi, :] = _dot(x1b_ref[lo:hi, :], wg_ref[...])
        ups = [_dot(x1b_ref[lo:hi, :], wu_ref[...]) for lo, hi in halves]
        if with_down:
            o_ref[...] += _dot(h_slots[1 - slot][...], wd_ref[...])
        w0 = cw_ref[0:1, :]
        w1 = cw_ref[1:2, :]
        w2 = cw_ref[2:3, :]
        cb = cb_ref[...]

        def conv(lo, hi, older=None, newer=None):
            older = gext[CARRY - 2 + lo:CARRY - 2 + hi, :] if older is None else older
            newer = gext[CARRY - 1 + lo:CARRY - 1 + hi, :] if newer is None else newer
            return cb + w0 * older + w1 * newer + w2 * gext[CARRY + lo:CARRY + hi, :]

        for (lo, hi), up in zip(halves, ups):
            if hi <= SPLIT:
                gc = conv(lo, hi)
            else:
                older_s = jnp.where(last_m, sc_ref[:, 0, :], gext[CARRY - 2 + SPLIT:CARRY - 2 + TM, :])
                newer_s = jnp.where(last_m, sc_ref[:, 1, :], gext[CARRY - 1 + SPLIT:CARRY - 1 + TM, :])
                gc = jnp.concatenate([conv(lo, SPLIT), conv(SPLIT, TM, older_s, newer_s)], axis=0)
            h_slots[slot][lo:hi, :] = (jax.nn.gelu(gc, approximate=True) * up).astype(BF16)
        carry[f] = gext[TM:CARRY + TM, :]

        @pl.when(last_m)
        def _():
            glast_ref[...] = gext[SPLIT:CARRY + SPLIT, :]
            scn_ref[:, 0, :] = sc_ref[:, 1, :]
            scn_ref[:, 1, :] = gext[CARRY + SPLIT:CARRY + TM, :]

    @pl.when(f == 0)
    def _():
        @pl.when(m == 0)
        def _():
            carry[...] = jnp.zeros_like(carry)

        x1_copy.start()
        up_and_gate(0, False)
        x1_copy.wait()

    for parity in range(2):
        @pl.when((f > 0) & (f < NF) & (f % 2 == parity))
        def _():
            up_and_gate(parity, True)

    @pl.when(f == NF)
    def _():
        h_last = h_slots[(NF - 1) % 2]
        parts = ((0, 352), (352, 704), (704, TM))
        accs = [o_ref[lo:hi, :] + _dot(h_last[lo:hi, :], wd_ref[...]) for lo, hi in parts]
        for (lo, hi), acc in zip(parts, accs):
            o_ref[lo:hi, :] = _layer_norm(acc, lng_ref[...], lnb_ref[...])

        @pl.when(last_m)
        def _():
            os_ref[...] = o_ref[SPLIT:TM, :]


def _ffn(x1b, x1, sc, wg, wu, wd, conv_w, conv_b, ln_g, ln_b):
    full = lambda shape: pl.BlockSpec(shape, lambda m, f: (0, 0))
    g_tile = lambda f: jnp.minimum(f, NF - 1)
    d_tile = lambda f: jnp.maximum(f - 1, 0)
    only_last = lambda m, f: jnp.where(m == NT - 1, g_tile(f), 0)
    row = pl.BlockSpec((TM, D_MODEL), lambda m, f: (m, 0))
    return pl.pallas_call(
        _ffn_kernel,
        grid=(NT, NF + 1),
        in_specs=[
            row,
            pl.BlockSpec(memory_space=pl.ANY),
            pl.BlockSpec((N_SAMPLE, CONV_W - 1, TF), lambda m, f: (0, 0, only_last(m, f))),
            pl.BlockSpec((D_MODEL, TF), lambda m, f: (0, g_tile(f))),
            pl.BlockSpec((D_MODEL, TF), lambda m, f: (0, g_tile(f))),
            pl.BlockSpec((TF, D_MODEL), lambda m, f: (d_tile(f), 0)),
            pl.BlockSpec((CONV_W, TF), lambda m, f: (0, g_tile(f))),
            pl.BlockSpec((1, TF), lambda m, f: (0, g_tile(f))),
            full((1, D_MODEL)), full((1, D_MODEL)),
        ],
        out_specs=[
            row,
            full((N_SAMPLE, D_MODEL)),
            pl.BlockSpec((CARRY, TF), lambda m, f: (0, only_last(m, f))),
            pl.BlockSpec((N_SAMPLE, CONV_W - 1, TF), lambda m, f: (0, 0, only_last(m, f))),
        ],
        out_shape=[
            jax.ShapeDtypeStruct((SEQ, D_MODEL), F32),
            jax.ShapeDtypeStruct((N_SAMPLE, D_MODEL), F32),
            jax.ShapeDtypeStruct((CARRY, D_FF), F32),
            jax.ShapeDtypeStruct((N_SAMPLE, CONV_W - 1, D_FF), F32),
        ],
        scratch_shapes=[
            pltpu.VMEM((TM, TF), BF16), pltpu.VMEM((TM, TF), BF16),
            pltpu.VMEM((CARRY + TM, TF), F32),
            pltpu.VMEM((NF, CARRY, TF), F32),
            pltpu.SemaphoreType.DMA(()),
        ],
        compiler_params=_params(2),
        name="convffn_ln2",
    )(x1b, x1, sc, wg, wu, wd, conv_w, conv_b, ln_g, ln_b)


def _rope_tables(pos):
    half = HEAD_DIM // 2
    inv = ROPE_THETA ** (-jnp.arange(half, dtype=F32) / half)
    lane = jnp.arange(LANES)
    inv_lanes = inv[lane % half]
    sign = jnp.where((lane % HEAD_DIM) < half, -1.0, 1.0).astype(F32)
    ang = pos.astype(F32)[:, None] * inv_lanes[None, :]
    return jnp.cos(ang), jnp.sin(ang) * sign[None, :]


def kernel(x_prompt, x_sample, cache_k, cache_v, state_pool, state_conv, w_in, attn_sinks, w_pool_mix,
           pool_scale, w_attn_branch, w_pool_branch, w_out, ln1_g, ln1_b, w_up, w_gate, conv_w, conv_b,
           w_down, ln2_g, ln2_b):
    x = x_prompt.reshape(SEQ, D_MODEL)
    xs = x_sample.reshape(N_SAMPLE, D_MODEL)
    pos = jnp.concatenate([jnp.arange(SEQ), jnp.full((N_SAMPLE,), PAST_LEN)])
    cos, sin = _rope_tables(pos)

    q, kv, xb = _in_proj(x, xs, w_in[0], cos, sin)

    to_dsw = lambda c: jnp.transpose(c[0], (0, 2, 3, 1)).reshape(N_SAMPLE, KV_DIM, WINDOW)
    steps = N_SAMPLE // SEQS_PER_STEP
    kvn_t = jnp.transpose(kv[SEQ:].reshape(steps, SEQS_PER_STEP, 2 * KV_DIM), (0, 2, 1))
    sinks = attn_sinks[0] * LOG2E
    attn_s, new_k, new_v = _attention_sample(
        sinks.reshape(N_HEADS, 1), q[SEQ:].reshape(N_SAMPLE, N_HEADS, HEAD_DIM), kvn_t,
        to_dsw(cache_k), to_dsw(cache_v))
    whole = lambda w: (w, 0, w.shape[1])
    late_tiles = [(w_in[0], j, TN) for j in range(_N_U0, _N_G0 + 2 * _NC)]
    attn, (wp_b, wa_b, wo_b, wg_b, wu_b, wd_b, wpool0, wpool1, *gate_w) = _attention(
        sinks, q, kv, attn_s.reshape(N_SAMPLE, D_MODEL),
        [whole(w_pool_branch[0]), whole(w_attn_branch[0]), whole(w_out[0]),
         whole(w_gate[0]), whole(w_up[0]), whole(w_down[0])] + late_tiles)

    state_pool_t = jnp.transpose(state_pool[0], (1, 0, 2))
    pooled, new_pool_t, pool_last = _pool(xb, wpool0, wpool1, state_pool_t, w_pool_mix[0], pool_scale)

    merged = _merge(xb, pooled, attn, wp_b, wa_b, gate_w)
    x1, x1b = _out_ln(merged, wo_b, x, xs, ln1_g, ln1_b)

    y, y_s, g_last, new_conv_s = _ffn(x1b, x1, state_conv[0], wg_b, wu_b, wd_b,
                                      conv_w[0], conv_b, ln2_g, ln2_b)

    from_dsw = lambda c: jnp.transpose(
        c.reshape(N_SAMPLE, N_KV_HEADS, HEAD_DIM, WINDOW), (0, 3, 1, 2))[None]
    return (
        y.reshape(1, SEQ, D_MODEL),
        y_s.reshape(N_SAMPLE, 1, D_MODEL),
        kv[SEQ - WINDOW:SEQ, :KV_DIM].reshape(1, 1, WINDOW, N_KV_HEADS, HEAD_DIM),
        kv[SEQ - WINDOW:SEQ, KV_DIM:].reshape(1, 1, WINDOW, N_KV_HEADS, HEAD_DIM),
        pool_last[POOL_PAD - POOL_HIST:].reshape(1, 1, POOL_HIST, POOL_WIDTH),
        g_last[CARRY - (CONV_W - 1):].reshape(1, 1, CONV_W - 1, D_FF),
        from_dsw(new_k),
        from_dsw(new_v),
        jnp.transpose(new_pool_t, (1, 0, 2))[None],
        new_conv_s[None],
    )
```

```python
import jax
import jax.numpy as jnp
from jax import lax
from jax.experimental import pallas as pl
from jax.experimental.pallas import tpu as pltpu

F32 = jnp.float32
BF16 = jnp.bfloat16

D_MODEL = 2048
SEQ = 8192
N_SAMPLE = 128
PAST_LEN = 8192
HEAD_DIM = 64
N_HEADS = 32
N_KV_HEADS = 4
GROUP = N_HEADS // N_KV_HEADS
KV_DIM = N_KV_HEADS * HEAD_DIM
WINDOW = 128
ROPE_THETA = 10000.0
POOL_WINDOWS = (2, 4, 8, 16)
POOL_WIDTH = 1024
POOL_GROUP_DIM = 256
POOL_HIST = 15
D_FF = 5632
CONV_W = 3
IN_DIM = 7680
LN_EPS = 1e-5
NEG_INF = -1e30
ALPHA = 2.0 ** 0.25
LOG2E = 1.4426950408889634

LANES = 128
VMEM_LIMIT = 60000 * 1024

ROWS = SEQ + N_SAMPLE
TM = 1040
NT = ROWS // TM
SPLIT = SEQ - (NT - 1) * TM
HALF = 528
TN = 512
assert NT * TM == ROWS and SPLIT + N_SAMPLE == TM and SPLIT % 16 == 0 and HALF % 16 == 0


def _params(ndim, flags=None):
    return pltpu.CompilerParams(dimension_semantics=("arbitrary",) * ndim,
                                vmem_limit_bytes=VMEM_LIMIT, flags=flags)


def _dot(a, b):
    return jnp.dot(a, b, preferred_element_type=F32)


def _layer_norm(z, g, b):
    mu = jnp.mean(z, axis=-1, keepdims=True)
    d = z - mu
    var = jnp.mean(d * d, axis=-1, keepdims=True)
    return d * lax.rsqrt(var + LN_EPS) * g + b


def _rope(x, cos, sin_signed):
    lane = lax.broadcasted_iota(jnp.int32, (1, LANES), 1)
    low_half = (lane % HEAD_DIM) < (HEAD_DIM // 2)
    outs = []
    for c in range(x.shape[1] // LANES):
        xc = x[:, c * LANES:(c + 1) * LANES]
        up = pltpu.roll(xc, LANES - HEAD_DIM // 2, 1)
        down = pltpu.roll(xc, HEAD_DIM // 2, 1)
        outs.append(xc * cos + jnp.where(low_half, up, down) * sin_signed)
    return jnp.concatenate(outs, axis=1)


_NQ = D_MODEL // TN
_N_KV = _NQ
_N_U0 = _N_KV + 1
_N_G0 = _N_U0 + POOL_WIDTH // TN


def _in_proj_kernel(x_ref, xs_ref, w_ref, cos_ref, sin_ref, q_ref, kv_ref, xb, wres):
    m = pl.program_id(0)
    n = pl.program_id(1)

    @pl.when(n == 0)
    def _():
        @pl.when(m < NT - 1)
        def _():
            xb[...] = x_ref[...].astype(BF16)

        @pl.when(m == NT - 1)
        def _():
            xb[0:SPLIT, :] = x_ref[0:SPLIT, :].astype(BF16)
            xb[SPLIT:TM, :] = xs_ref[...].astype(BF16)

    @pl.when(m == 0)
    def _():
        wres[n] = w_ref[...].astype(BF16)

    halves = ((0, HALF), (HALF, TM))
    thirds = ((0, 352), (352, 704), (704, TM))

    def project(epilogue, parts):
        accs = [_dot(xb[lo:hi, :], wres[n]) for lo, hi in parts]
        for (lo, hi), acc in zip(parts, accs):
            epilogue(acc, lo, hi)

    def q_epilogue(acc, lo, hi):
        roped = _rope(acc, cos_ref[lo:hi, :], sin_ref[lo:hi, :])
        q_ref[lo:hi, :] = (roped * (HEAD_DIM ** -0.5 * LOG2E)).astype(BF16)

    def kv_epilogue(acc, lo, hi):
        kv_ref[lo:hi, :KV_DIM] = _rope(acc[:, :KV_DIM], cos_ref[lo:hi, :], sin_ref[lo:hi, :])
        kv_ref[lo:hi, KV_DIM:] = acc[:, KV_DIM:]

    pl.when(n < _NQ)(lambda: project(q_epilogue, thirds))
    pl.when(n == _N_KV)(lambda: project(kv_epilogue, halves))


def _in_proj(x, xs, w_in, cos, sin):
    nn = _N_U0

    return pl.pallas_call(
        _in_proj_kernel,
        grid=(NT, nn),
        in_specs=[
            pl.BlockSpec((TM, D_MODEL), lambda m, n: (m, 0)),
            pl.BlockSpec((N_SAMPLE, D_MODEL), lambda m, n: (0, 0)),
            pl.BlockSpec((D_MODEL, TN), lambda m, n: (0, jnp.where(m == 0, n, nn - 1))),
            pl.BlockSpec((TM, LANES), lambda m, n: (m, 0)),
            pl.BlockSpec((TM, LANES), lambda m, n: (m, 0)),
        ],
        out_specs=[
            pl.BlockSpec((TM, TN), lambda m, n: (m, jnp.minimum(n, _NQ - 1))),
            pl.BlockSpec((TM, TN), lambda m, n: (m, 0)),
            pl.BlockSpec((TM, D_MODEL), lambda m, n: (m, 0)),
        ],
        out_shape=[
            jax.ShapeDtypeStruct((ROWS, D_MODEL), BF16),
            jax.ShapeDtypeStruct((ROWS, 2 * KV_DIM), F32),
            jax.ShapeDtypeStruct((ROWS, D_MODEL), BF16),
        ],
        scratch_shapes=[pltpu.VMEM((nn, D_MODEL, TN), BF16)],
        compiler_params=_params(2),
        name="in_proj",
    )(x, xs, w_in, cos, sin)


PAIRS = GROUP // 2
Q_BLOCKS = 2


def _attn_kernel(fill_ref, q_ref, kp_ref, kc_ref, vp_ref, vc_ref, os_ref, *refs):
    n_w = (len(refs) - 1) // 2
    w_refs, o_ref, wb_refs = refs[:n_w], refs[n_w], refs[n_w + 1:]
    i = pl.program_id(0)
    last = pl.num_programs(0) - 1

    @pl.when(i < last)
    def _():
        for w_ref, wb_ref in zip(w_refs, wb_refs):
            wb_ref[...] = w_ref[...].astype(BF16)
        keys = (kp_ref[...],) + tuple(kc_ref[b * WINDOW:(b + 1) * WINDOW, :] for b in range(Q_BLOCKS))
        vals = (vp_ref[...],) + tuple(vc_ref[b * WINDOW:(b + 1) * WINDOW, :] for b in range(Q_BLOCKS))
        for b in range(Q_BLOCKS):
            _attn_block(Q_BLOCKS * i + b, fill_ref, q_ref, o_ref, b * WINDOW,
                        keys[b], keys[b + 1], vals[b], vals[b + 1])

    @pl.when(i == last)
    def _():
        o_ref[0:N_SAMPLE, :] = os_ref[...]


def _attn_block(i, fill_ref, q_ref, o_ref, row0, k_prev, k_cur, v_prev, v_cur):
    q_rows = slice(row0, row0 + WINDOW)
    kk = jnp.concatenate([k_prev, k_cur], axis=0)
    vv = jnp.concatenate([v_prev, v_cur], axis=0)
    key = lax.broadcasted_iota(jnp.int32, (2 * WINDOW, 1), 0)
    vv = jnp.where(key == 0, 0.0, vv)
    low = lax.broadcasted_iota(jnp.int32, (1, LANES), 1) < HEAD_DIM
    rows = PAIRS * WINDOW
    r = lax.broadcasted_iota(jnp.int32, (rows, 2 * WINDOW), 0) % WINDOW
    c = lax.broadcasted_iota(jnp.int32, (rows, 2 * WINDOW), 1)
    visible = (c > r) & (c <= r + WINDOW) & ((i > 0) | (c >= WINDOW))

    for pair in range(N_KV_HEADS // 2):
        k_pair = kk[:, pair * LANES:(pair + 1) * LANES]
        v_pair = vv[:, pair * LANES:(pair + 1) * LANES]
        k_swap = pltpu.roll(k_pair, HEAD_DIM, 1)
        v_swap = pltpu.roll(v_pair, HEAD_DIM, 1)
        for second in range(2):
            g = 2 * pair + second
            k_in_low, k_in_high = (k_swap, k_pair) if second else (k_pair, k_swap)
            v_in_low, v_in_high = (v_swap, v_pair) if second else (v_pair, v_swap)
            x = jnp.concatenate(
                [q_ref[q_rows, (g * PAIRS + t) * LANES:(g * PAIRS + t + 1) * LANES] for t in range(PAIRS)],
                axis=0)

            def half(k_pad, v_pad, fill):
                s = lax.dot_general(x, k_pad.astype(BF16), (((1,), (1,)), ((), ())),
                                    preferred_element_type=F32)
                s = jnp.where(visible, s, fill)
                e = jnp.exp2(s - jnp.max(s, axis=-1, keepdims=True)).astype(BF16)
                return _dot(e, v_pad.astype(BF16))

            a = half(jnp.where(low, k_in_low, 0.0), jnp.where(low, v_in_low, 1.0), fill_ref[2 * g])
            b = half(jnp.where(low, 0.0, k_in_high), jnp.where(low, 1.0, v_in_high), fill_ref[2 * g + 1])
            den = pltpu.roll(jnp.where(low, b, a), HEAD_DIM, 1)
            out = (jnp.where(low, a, b) / den).astype(BF16)
            for t in range(PAIRS):
                o_ref[q_rows, (g * PAIRS + t) * LANES:(g * PAIRS + t + 1) * LANES] = \
                    out[t * WINDOW:(t + 1) * WINDOW]


def _cast_chunk_rows(rows, steps):
    assert rows % 16 == 0
    units = rows // 16
    n = max(d for d in range(1, steps + 1) if units % d == 0)
    return rows // n


def _attention(sinks, q, kv, attn_s, weights):
    nb = SEQ // (Q_BLOCKS * WINDOW)
    cur = lambda i: jnp.minimum(i, nb - 1)
    prev = lambda i: jnp.clip(Q_BLOCKS * i - 1, 0, Q_BLOCKS * nb - 1)
    head_set = jnp.arange(2 * N_KV_HEADS)[:, None]
    row_chunk = jnp.arange(PAIRS * WINDOW)[None, :] // WINDOW
    head = (head_set // 2) * GROUP + 2 * row_chunk + head_set % 2
    first_col = jnp.arange(2 * WINDOW)[None, None, :] == 0
    fill = jnp.where(first_col, sinks[head][:, :, None], NEG_INF).astype(F32)
    w_in_specs, w_out_specs, w_out_shapes = [], [], []
    for w, col, width in weights:
        chunk = _cast_chunk_rows(w.shape[0], nb)
        n = w.shape[0] // chunk
        w_in_specs.append(pl.BlockSpec((chunk, width), lambda i, n=n, col=col: (jnp.minimum(i, n - 1), col)))
        w_out_specs.append(pl.BlockSpec((chunk, width), lambda i, n=n: (jnp.minimum(i, n - 1), 0)))
        w_out_shapes.append(jax.ShapeDtypeStruct((w.shape[0], width), BF16))
    outs = pl.pallas_call(
        _attn_kernel,
        grid=(nb + 1,),
        in_specs=[
            _resident(fill.shape),
            pl.BlockSpec((Q_BLOCKS * WINDOW, D_MODEL), lambda i: (cur(i), 0)),
            pl.BlockSpec((WINDOW, KV_DIM), lambda i: (prev(i), 0)),
            pl.BlockSpec((Q_BLOCKS * WINDOW, KV_DIM), lambda i: (cur(i), 0)),
            pl.BlockSpec((WINDOW, KV_DIM), lambda i: (prev(i), 1)),
            pl.BlockSpec((Q_BLOCKS * WINDOW, KV_DIM), lambda i: (cur(i), 1)),
            pl.BlockSpec((N_SAMPLE, D_MODEL), lambda i: (0, 0)),
        ] + w_in_specs,
        out_specs=[pl.BlockSpec((Q_BLOCKS * WINDOW, D_MODEL), lambda i: (i, 0))] + w_out_specs,
        out_shape=[jax.ShapeDtypeStruct((ROWS, D_MODEL), BF16)] + w_out_shapes,
        compiler_params=_params(1),
        name="attn_prompt",
    )(fill, q, kv, kv, kv, kv, attn_s, *[w for w, _, _ in weights])
    return outs[0], outs[1:]


SEQS_PER_STEP = 8


def _attn_sample_kernel(sink_ref, q_ref, kvn_ref, ck_ref, cv_ref, o_ref, nk_ref, nv_ref):
    head = lax.broadcasted_iota(jnp.int32, (N_HEADS, KV_DIM), 0) // GROUP
    col_group = lax.broadcasted_iota(jnp.int32, (N_HEADS, KV_DIM), 1) // HEAD_DIM
    own = head == col_group
    newest = lax.broadcasted_iota(jnp.int32, (1, WINDOW), 1) == WINDOW - 1
    scores = []
    for b in range(SEQS_PER_STEP):
        nk_ref[b] = jnp.where(newest, kvn_ref[0, 0:KV_DIM, b:b + 1], pltpu.roll(ck_ref[b], WINDOW - 1, 1))
        nv_ref[b] = jnp.where(newest, kvn_ref[0, KV_DIM:2 * KV_DIM, b:b + 1],
                              pltpu.roll(cv_ref[b], WINDOW - 1, 1))
        qb = q_ref[b]
        qe = jnp.where(own, jnp.concatenate([qb] * N_KV_HEADS, axis=1), jnp.zeros((), BF16))
        scores.append(_dot(qe, nk_ref[b].astype(BF16)))
    s = jnp.concatenate(scores, axis=0)
    sink = jnp.concatenate([sink_ref[...]] * SEQS_PER_STEP, axis=0)
    mx = jnp.maximum(jnp.max(s, axis=-1, keepdims=True), sink)
    e = jnp.exp2(s - mx)
    inv = 1.0 / (jnp.sum(e, axis=-1, keepdims=True) + jnp.exp2(sink - mx))
    e = e.astype(BF16)
    for b in range(SEQS_PER_STEP):
        rows = slice(b * N_HEADS, (b + 1) * N_HEADS)
        of = lax.dot_general(e[rows], nv_ref[b].astype(BF16), (((1,), (1,)), ((), ())),
                             preferred_element_type=F32)
        of = jnp.where(own, of, 0.0)
        o = of[:, 0:HEAD_DIM]
        for g in range(1, N_KV_HEADS):
            o = o + of[:, g * HEAD_DIM:(g + 1) * HEAD_DIM]
        o_ref[b] = (o * inv[rows]).astype(BF16)


def _attention_sample(sink_col, q3, kvn_t, ck, cv):
    nb = N_SAMPLE // SEQS_PER_STEP
    cache_spec = pl.BlockSpec((SEQS_PER_STEP, KV_DIM, WINDOW), lambda i: (i, 0, 0))
    return pl.pallas_call(
        _attn_sample_kernel,
        grid=(nb,),
        in_specs=[
            pl.BlockSpec((N_HEADS, 1), lambda i: (0, 0)),
            pl.BlockSpec((SEQS_PER_STEP, N_HEADS, HEAD_DIM), lambda i: (i, 0, 0)),
            pl.BlockSpec((1, 2 * KV_DIM, SEQS_PER_STEP), lambda i: (i, 0, 0)),
            cache_spec, cache_spec,
        ],
        out_specs=[
            pl.BlockSpec((SEQS_PER_STEP, N_HEADS, HEAD_DIM), lambda i: (i, 0, 0)),
            cache_spec, cache_spec,
        ],
        out_shape=[
            jax.ShapeDtypeStruct((N_SAMPLE, N_HEADS, HEAD_DIM), BF16),
            jax.ShapeDtypeStruct((N_SAMPLE, KV_DIM, WINDOW), F32),
            jax.ShapeDtypeStruct((N_SAMPLE, KV_DIM, WINDOW), F32),
        ],
        compiler_params=_params(1),
        name="attn_sample",
    )(sink_col, q3, kvn_t, ck, cv)


POOL_PAD = 16


def _pool_kernel(xb_ref, wu0_ref, wu1_ref, sp_ref, wmix_ref, scale_ref, p_ref, spn_ref, plast_ref, ext):
    m = pl.program_id(0)

    @pl.when(m == 0)
    def _():
        ext[0:POOL_PAD, :] = jnp.zeros((POOL_PAD, POOL_WIDTH), F32)

    pos = m * TM + lax.broadcasted_iota(jnp.int32, (TM, 1), 0)
    for t, wu_ref in enumerate((wu0_ref, wu1_ref)):
        ext[POOL_PAD:POOL_PAD + TM, t * TN:(t + 1) * TN] = _dot(xb_ref[...], wu_ref[...])
    for g, w in enumerate(POOL_WINDOWS):
        cols = slice(g * POOL_GROUP_DIM, (g + 1) * POOL_GROUP_DIM)
        u_g = ext[POOL_PAD:POOL_PAD + TM, cols]
        run = ext[:, cols]
        shift = 1
        while shift < w:
            run = run + pltpu.roll(run, shift, 0)
            shift *= 2
        tot = run[POOL_PAD:POOL_PAD + TM, :]
        inv_cnt = 1.0 / jnp.minimum(w, pos + 1).astype(F32)
        d = tot * inv_cnt - u_g
        y = _dot(d.astype(BF16), wmix_ref[g].astype(BF16))
        p_ref[:, cols] = (y * scale_ref[:, cols]).astype(BF16)

    @pl.when(m == NT - 1)
    def _():
        for g, w in enumerate(POOL_WINDOWS):
            cols = slice(g * POOL_GROUP_DIM, (g + 1) * POOL_GROUP_DIM)
            us = ext[POOL_PAD + SPLIT:POOL_PAD + TM, cols]
            tot = us
            for j in range(1, w):
                tot = tot + sp_ref[POOL_HIST - j, :, cols]
            d = tot * (1.0 / w) - us
            y = _dot(d.astype(BF16), wmix_ref[g].astype(BF16))
            p_ref[SPLIT:TM, cols] = (y * scale_ref[:, cols]).astype(BF16)
        spn_ref[0:POOL_HIST - 1] = sp_ref[1:POOL_HIST]
        spn_ref[POOL_HIST - 1] = ext[POOL_PAD + SPLIT:POOL_PAD + TM, :]
        plast_ref[...] = ext[SPLIT:POOL_PAD + SPLIT, :]

    ext[0:POOL_PAD, :] = ext[TM:TM + POOL_PAD, :]


def _pool(xb, wu0, wu1, sp_t, wmix, scale):
    full = lambda shape: pl.BlockSpec(shape, lambda m: (0,) * len(shape))
    return pl.pallas_call(
        _pool_kernel,
        grid=(NT,),
        in_specs=[
            pl.BlockSpec((TM, D_MODEL), lambda m: (m, 0)),
            _resident((D_MODEL, TN)), _resident((D_MODEL, TN)),
            full((POOL_HIST, N_SAMPLE, POOL_WIDTH)),
            full((len(POOL_WINDOWS), POOL_GROUP_DIM, POOL_GROUP_DIM)),
            full((1, POOL_WIDTH)),
        ],
        out_specs=[pl.BlockSpec((TM, POOL_WIDTH), lambda m: (m, 0)),
                   full((POOL_HIST, N_SAMPLE, POOL_WIDTH)),
                   full((POOL_PAD, POOL_WIDTH))],
        out_shape=[jax.ShapeDtypeStruct((ROWS, POOL_WIDTH), BF16),
                   jax.ShapeDtypeStruct((POOL_HIST, N_SAMPLE, POOL_WIDTH), F32),
                   jax.ShapeDtypeStruct((POOL_PAD, POOL_WIDTH), F32)],
        scratch_shapes=[pltpu.VMEM((POOL_PAD + TM, POOL_WIDTH), F32)],
        compiler_params=_params(1),
        name="pool_mix",
    )(xb, wu0, wu1, sp_t, wmix, scale)


_NC = D_MODEL // TN


TMH = TM // 2
NTH = ROWS // TMH
SPLIT_H = SEQ - (NTH - 1) * TMH
assert NTH * TMH == ROWS and SPLIT_H + N_SAMPLE == TMH and SPLIT_H % 8 == 0


def _resident(shape):
    return pl.BlockSpec(shape, lambda m: (0,) * len(shape), pipeline_mode=pl.Buffered(1))


def _merge_kernel(xb_ref, pool_ref, attn_ref, wp_ref, wa_ref, *refs):
    gate_w, o_ref = refs[:2 * _NC], refs[2 * _NC]
    for c in range(_NC):
        cols = slice(c * TN, (c + 1) * TN)
        gate_pool = jax.nn.sigmoid(_dot(xb_ref[...], gate_w[c][...]))
        gate_attn = jax.nn.sigmoid(_dot(xb_ref[...], gate_w[_NC + c][...]))
        a = _dot(pool_ref[...], wp_ref[:, cols])
        b = _dot(attn_ref[...], wa_ref[:, cols])
        o_ref[:, cols] = (gate_pool * a + gate_attn * b).astype(BF16)


def _merge(xb, pooled, attn, wp, wa, gate_w):
    row = lambda width: pl.BlockSpec((TMH, width), lambda m: (m, 0))
    return pl.pallas_call(
        _merge_kernel,
        grid=(NTH,),
        in_specs=[row(D_MODEL), row(POOL_WIDTH), row(D_MODEL),
                  _resident((POOL_WIDTH, D_MODEL)), _resident((D_MODEL, D_MODEL))]
        + [_resident((D_MODEL, TN))] * (2 * _NC),
        out_specs=row(D_MODEL),
        out_shape=jax.ShapeDtypeStruct((ROWS, D_MODEL), BF16),
        compiler_params=_params(1),
        name="branch_merge",
    )(xb, pooled, attn, wp, wa, *gate_w)


def _out_ln_kernel(mrg_ref, wo_ref, x_ref, xs_ref, lng_ref, lnb_ref, x1_ref, x1b_ref):
    m = pl.program_id(0)
    for c in range(_NC):
        cols = slice(c * TN, (c + 1) * TN)
        x1_ref[:, cols] = _dot(mrg_ref[...], wo_ref[:, cols])

    def finish(z):
        x1 = _layer_norm(z, lng_ref[...], lnb_ref[...])
        x1_ref[...] = ALPHA * x1
        x1b_ref[...] = x1.astype(BF16)

    @pl.when(m < NTH - 1)
    def _():
        finish(x1_ref[...] + ALPHA * x_ref[...])

    @pl.when(m == NTH - 1)
    def _():
        finish(jnp.concatenate([x1_ref[0:SPLIT_H, :] + ALPHA * x_ref[0:SPLIT_H, :],
                                x1_ref[SPLIT_H:TMH, :] + ALPHA * xs_ref[...]], axis=0))


def _out_ln(merged, wo, x, xs, ln_g, ln_b):
    row = pl.BlockSpec((TMH, D_MODEL), lambda m: (m, 0))
    return pl.pallas_call(
        _out_ln_kernel,
        grid=(NTH,),
        in_specs=[row, _resident((D_MODEL, D_MODEL)), row, _resident((N_SAMPLE, D_MODEL)),
                  _resident((1, D_MODEL)), _resident((1, D_MODEL))],
        out_specs=[row, row],
        out_shape=[jax.ShapeDtypeStruct((ROWS, D_MODEL), F32), jax.ShapeDtypeStruct((ROWS, D_MODEL), BF16)],
        compiler_params=_params(1),
        name="out_proj_ln1",
    )(merged, wo, x, xs, ln_g, ln_b)


TF = 512
NF = D_FF // TF
CARRY = 8


def _ffn_kernel(x1b_ref, x1_hbm, sc_ref, wg_ref, wu_ref, wd_ref, cw_ref, cb_ref, lng_ref, lnb_ref,
                o_ref, os_ref, glast_ref, scn_ref,
                h0, h1, gext, carry, sem):
    m = pl.program_id(0)
    f = pl.program_id(1)
    last_m = m == NT - 1
    h_slots = (h0, h1)
    x1_copy = pltpu.make_async_copy(x1_hbm.at[pl.ds(m * TM, TM), :], o_ref, sem)

    def up_and_gate(slot, with_down):
        halves = ((0, HALF), (HALF, TM))
        gext[0:CARRY, :] = carry[f]
        for lo, hi in halves:
            gext[CARRY + lo:CARRY + hi, :] = _dot(x1b_ref[lo:hi, :], wg_ref[...])
        ups = [_dot(x1b_ref[lo:hi, :], wu_ref[...]) for lo, hi in halves]
        if with_down:
            o_ref[...] += _dot(h_slots[1 - slot][...], wd_ref[...])
        w0 = cw_ref[0:1, :]
        w1 = cw_ref[1:2, :]
        w2 = cw_ref[2:3, :]
        cb = cb_ref[...]

        def conv(lo, hi, older=None, newer=None):
            older = gext[CARRY - 2 + lo:CARRY - 2 + hi, :] if older is None else older
            newer = gext[CARRY - 1 + lo:CARRY - 1 + hi, :] if newer is None else newer
            return cb + w0 * older + w1 * newer + w2 * gext[CARRY + lo:CARRY + hi, :]

        for (lo, hi), up in zip(halves, ups):
            if hi <= SPLIT:
                gc = conv(lo, hi)
            else:
                older_s = jnp.where(last_m, sc_ref[:, 0, :], gext[CARRY - 2 + SPLIT:CARRY - 2 + TM, :])
                newer_s = jnp.where(last_m, sc_ref[:, 1, :], gext[CARRY - 1 + SPLIT:CARRY - 1 + TM, :])
                gc = jnp.concatenate([conv(lo, SPLIT), conv(SPLIT, TM, older_s, newer_s)], axis=0)
            h_slots[slot][lo:hi, :] = (jax.nn.gelu(gc, approximate=True) * up).astype(BF16)
        carry[f] = gext[TM:CARRY + TM, :]

        @pl.when(last_m)
        def _():
            glast_ref[...] = gext[SPLIT:CARRY + SPLIT, :]
            scn_ref[:, 0, :] = sc_ref[:, 1, :]
            scn_ref[:, 1, :] = gext[CARRY + SPLIT:CARRY + TM, :]

    @pl.when(f == 0)
    def _():
        @pl.when(m == 0)
        def _():
            carry[...] = jnp.zeros_like(carry)

        x1_copy.start()
        up_and_gate(0, False)
        x1_copy.wait()

    for parity in range(2):
        @pl.when((f > 0) & (f < NF) & (f % 2 == parity))
        def _():
            up_and_gate(parity, True)

    @pl.when(f == NF)
    def _():
        h_last = h_slots[(NF - 1) % 2]
        parts = ((0, 352), (352, 704), (704, TM))
        accs = [o_ref[lo:hi, :] + _dot(h_last[lo:hi, :], wd_ref[...]) for lo, hi in parts]
        for (lo, hi), acc in zip(parts, accs):
            o_ref[lo:hi, :] = _layer_norm(acc, lng_ref[...], lnb_ref[...])

        @pl.when(last_m)
        def _():
            os_ref[...] = o_ref[SPLIT:TM, :]


def _ffn(x1b, x1, sc, wg, wu, wd, conv_w, conv_b, ln_g, ln_b):
    full = lambda shape: pl.BlockSpec(shape, lambda m, f: (0, 0))
    g_tile = lambda f: jnp.minimum(f, NF - 1)
    d_tile = lambda f: jnp.maximum(f - 1, 0)
    only_last = lambda m, f: jnp.where(m == NT - 1, g_tile(f), 0)
    row = pl.BlockSpec((TM, D_MODEL), lambda m, f: (m, 0))
    return pl.pallas_call(
        _ffn_kernel,
        grid=(NT, NF + 1),
        in_specs=[
            row,
            pl.BlockSpec(memory_space=pl.ANY),
            pl.BlockSpec((N_SAMPLE, CONV_W - 1, TF), lambda m, f: (0, 0, only_last(m, f))),
            pl.BlockSpec((D_MODEL, TF), lambda m, f: (0, g_tile(f))),
            pl.BlockSpec((D_MODEL, TF), lambda m, f: (0, g_tile(f))),
            pl.BlockSpec((TF, D_MODEL), lambda m, f: (d_tile(f), 0)),
            pl.BlockSpec((CONV_W, TF), lambda m, f: (0, g_tile(f))),
            pl.BlockSpec((1, TF), lambda m, f: (0, g_tile(f))),
            full((1, D_MODEL)), full((1, D_MODEL)),
        ],
        out_specs=[
            row,
            full((N_SAMPLE, D_MODEL)),
            pl.BlockSpec((CARRY, TF), lambda m, f: (0, only_last(m, f))),
            pl.BlockSpec((N_SAMPLE, CONV_W - 1, TF), lambda m, f: (0, 0, only_last(m, f))),
        ],
        out_shape=[
            jax.ShapeDtypeStruct((SEQ, D_MODEL), F32),
            jax.ShapeDtypeStruct((N_SAMPLE, D_MODEL), F32),
            jax.ShapeDtypeStruct((CARRY, D_FF), F32),
            jax.ShapeDtypeStruct((N_SAMPLE, CONV_W - 1, D_FF), F32),
        ],
        scratch_shapes=[
            pltpu.VMEM((TM, TF), BF16), pltpu.VMEM((TM, TF), BF16),
            pltpu.VMEM((CARRY + TM, TF), F32),
            pltpu.VMEM((NF, CARRY, TF), F32),
            pltpu.SemaphoreType.DMA(()),
        ],
        compiler_params=_params(2),
        name="convffn_ln2",
    )(x1b, x1, sc, wg, wu, wd, conv_w, conv_b, ln_g, ln_b)


def _rope_tables(pos):
    half = HEAD_DIM // 2
    inv = ROPE_THETA ** (-jnp.arange(half, dtype=F32) / half)
    lane = jnp.arange(LANES)
    inv_lanes = inv[lane % half]
    sign = jnp.where((lane % HEAD_DIM) < half, -1.0, 1.0).astype(F32)
    ang = pos.astype(F32)[:, None] * inv_lanes[None, :]
    return jnp.cos(ang), jnp.sin(ang) * sign[None, :]


def kernel(x_prompt, x_sample, cache_k, cache_v, state_pool, state_conv, w_in, attn_sinks, w_pool_mix,
           pool_scale, w_attn_branch, w_pool_branch, w_out, ln1_g, ln1_b, w_up, w_gate, conv_w, conv_b,
           w_down, ln2_g, ln2_b):
    x = x_prompt.reshape(SEQ, D_MODEL)
    xs = x_sample.reshape(N_SAMPLE, D_MODEL)
    pos = jnp.concatenate([jnp.arange(SEQ), jnp.full((N_SAMPLE,), PAST_LEN)])
    cos, sin = _rope_tables(pos)

    q, kv, xb = _in_proj(x, xs, w_in[0], cos, sin)

    to_dsw = lambda c: jnp.transpose(c[0], (0, 2, 3, 1)).reshape(N_SAMPLE, KV_DIM, WINDOW)
    steps = N_SAMPLE // SEQS_PER_STEP
    kvn_t = jnp.transpose(kv[SEQ:].reshape(steps, SEQS_PER_STEP, 2 * KV_DIM), (0, 2, 1))
    sinks = attn_sinks[0] * LOG2E
    attn_s, new_k, new_v = _attention_sample(
        sinks.reshape(N_HEADS, 1), q[SEQ:].reshape(N_SAMPLE, N_HEADS, HEAD_DIM), kvn_t,
        to_dsw(cache_k), to_dsw(cache_v))
    whole = lambda w: (w, 0, w.shape[1])
    late_tiles = [(w_in[0], j, TN) for j in range(_N_U0, _N_G0 + 2 * _NC)]
    attn, (wp_b, wa_b, wo_b, wg_b, wu_b, wd_b, wpool0, wpool1, *gate_w) = _attention(
        sinks, q, kv, attn_s.reshape(N_SAMPLE, D_MODEL),
        [whole(w_pool_branch[0]), whole(w_attn_branch[0]), whole(w_out[0]),
         whole(w_gate[0]), whole(w_up[0]), whole(w_down[0])] + late_tiles)

    state_pool_t = jnp.transpose(state_pool[0], (1, 0, 2))
    pooled, new_pool_t, pool_last = _pool(xb, wpool0, wpool1, state_pool_t, w_pool_mix[0], pool_scale)

    merged = _merge(xb, pooled, attn, wp_b, wa_b, gate_w)
    x1, x1b = _out_ln(merged, wo_b, x, xs, ln1_g, ln1_b)

    y, y_s, g_last, new_conv_s = _ffn(x1b, x1, state_conv[0], wg_b, wu_b, wd_b,
                                      conv_w[0], conv_b, ln2_g, ln2_b)

    from_dsw = lambda c: jnp.transpose(
        c.reshape(N_SAMPLE, N_KV_HEADS, HEAD_DIM, WINDOW), (0, 3, 1, 2))[None]
    return (
        y.reshape(1, SEQ, D_MODEL),
        y_s.reshape(N_SAMPLE, 1, D_MODEL),
        kv[SEQ - WINDOW:SEQ, :KV_DIM].reshape(1, 1, WINDOW, N_KV_HEADS, HEAD_DIM),
        kv[SEQ - WINDOW:SEQ, KV_DIM:].reshape(1, 1, WINDOW, N_KV_HEADS, HEAD_DIM),
        pool_last[POOL_PAD - POOL_HIST:].reshape(1, 1, POOL_HIST, POOL_WIDTH),
        g_last[CARRY - (CONV_W - 1):].reshape(1, 1, CONV_W - 1, D_FF),
        from_dsw(new_k),
        from_dsw(new_v),
        jnp.transpose(new_pool_t, (1, 0, 2))[None],
        new_conv_s[None],
    )
```

```python
import jax
import jax.numpy as jnp
from jax import lax
from jax.experimental import pallas as pl
from jax.experimental.pallas import tpu as pltpu

F32 = jnp.float32
BF16 = jnp.bfloat16

D_MODEL = 2048
SEQ = 8192
N_SAMPLE = 128
PAST_LEN = 8192
HEAD_DIM = 64
N_HEADS = 32
N_KV_HEADS = 4
GROUP = N_HEADS // N_KV_HEADS
KV_DIM = N_KV_HEADS * HEAD_DIM
WINDOW = 128
ROPE_THETA = 10000.0
POOL_WINDOWS = (2, 4, 8, 16)
POOL_WIDTH = 1024
POOL_GROUP_DIM = 256
POOL_HIST = 15
D_FF = 5632
CONV_W = 3
IN_DIM = 7680
LN_EPS = 1e-5
NEG_INF = -1e30
ALPHA = 2.0 ** 0.25
LOG2E = 1.4426950408889634

LANES = 128
VMEM_LIMIT = 60000 * 1024

ROWS = SEQ + N_SAMPLE
TM = 1040
NT = ROWS // TM
SPLIT = SEQ - (NT - 1) * TM
HALF = 528
TN = 512
assert NT * TM == ROWS and SPLIT + N_SAMPLE == TM and SPLIT % 16 == 0 and HALF % 16 == 0


def _params(ndim, flags=None):
    return pltpu.CompilerParams(dimension_semantics=("arbitrary",) * ndim,
                                vmem_limit_bytes=VMEM_LIMIT, flags=flags)


def _dot(a, b):
    return jnp.dot(a, b, preferred_element_type=F32)


def _layer_norm(z, g, b):
    mu = jnp.mean(z, axis=-1, keepdims=True)
    d = z - mu
    var = jnp.mean(d * d, axis=-1, keepdims=True)
    return d * lax.rsqrt(var + LN_EPS) * g + b


def _rope(x, cos, sin_signed):
    lane = lax.broadcasted_iota(jnp.int32, (1, LANES), 1)
    low_half = (lane % HEAD_DIM) < (HEAD_DIM // 2)
    outs = []
    for c in range(x.shape[1] // LANES):
        xc = x[:, c * LANES:(c + 1) * LANES]
        up = pltpu.roll(xc, LANES - HEAD_DIM // 2, 1)
        down = pltpu.roll(xc, HEAD_DIM // 2, 1)
        outs.append(xc * cos + jnp.where(low_half, up, down) * sin_signed)
    return jnp.concatenate(outs, axis=1)


_NQ = D_MODEL // TN
_N_KV = _NQ
_N_U0 = _N_KV + 1
_N_G0 = _N_U0 + POOL_WIDTH // TN


def _in_proj_kernel(x_ref, xs_ref, w_ref, cos_ref, sin_ref, q_ref, kv_ref, xb, wres):
    m = pl.program_id(0)
    n = pl.program_id(1)

    @pl.when(n == 0)
    def _():
        @pl.when(m < NT - 1)
        def _():
            xb[...] = x_ref[...].astype(BF16)

        @pl.when(m == NT - 1)
        def _():
            xb[0:SPLIT, :] = x_ref[0:SPLIT, :].astype(BF16)
            xb[SPLIT:TM, :] = xs_ref[...].astype(BF16)

    @pl.when(m == 0)
    def _():
        wres[n] = w_ref[...].astype(BF16)

    halves = ((0, HALF), (HALF, TM))
    thirds = ((0, 352), (352, 704), (704, TM))

    def project(epilogue, parts):
        accs = [_dot(xb[lo:hi, :], wres[n]) for lo, hi in parts]
        for (lo, hi), acc in zip(parts, accs):
            epilogue(acc, lo, hi)

    def q_epilogue(acc, lo, hi):
        roped = _rope(acc, cos_ref[lo:hi, :], sin_ref[lo:hi, :])
        q_ref[lo:hi, :] = (roped * (HEAD_DIM ** -0.5 * LOG2E)).astype(BF16)

    def kv_epilogue(acc, lo, hi):
        kv_ref[lo:hi, :KV_DIM] = _rope(acc[:, :KV_DIM], cos_ref[lo:hi, :], sin_ref[lo:hi, :])
        kv_ref[lo:hi, KV_DIM:] = acc[:, KV_DIM:]

    pl.when(n < _NQ)(lambda: project(q_epilogue, thirds))
    pl.when(n == _N_KV)(lambda: project(kv_epilogue, halves))


def _in_proj(x, xs, w_in, cos, sin):
    nn = _N_U0

    return pl.pallas_call(
        _in_proj_kernel,
        grid=(NT, nn),
        in_specs=[
            pl.BlockSpec((TM, D_MODEL), lambda m, n: (m, 0)),
            pl.BlockSpec((N_SAMPLE, D_MODEL), lambda m, n: (0, 0)),
            pl.BlockSpec((D_MODEL, TN), lambda m, n: (0, jnp.where(m == 0, n, nn - 1))),
            pl.BlockSpec((TM, LANES), lambda m, n: (m, 0)),
            pl.BlockSpec((TM, LANES), lambda m, n: (m, 0)),
        ],
        out_specs=[
            pl.BlockSpec((TM, TN), lambda m, n: (m, jnp.minimum(n, _NQ - 1))),
            pl.BlockSpec((TM, TN), lambda m, n: (m, 0)),
            pl.BlockSpec((TM, D_MODEL), lambda m, n: (m, 0)),
        ],
        out_shape=[
            jax.ShapeDtypeStruct((ROWS, D_MODEL), BF16),
            jax.ShapeDtypeStruct((ROWS, 2 * KV_DIM), F32),
            jax.ShapeDtypeStruct((ROWS, D_MODEL), BF16),
        ],
        scratch_shapes=[pltpu.VMEM((nn, D_MODEL, TN), BF16)],
        compiler_params=_params(2),
        name="in_proj",
    )(x, xs, w_in, cos, sin)


PAIRS = GROUP // 2
Q_BLOCKS = 2


def _attn_kernel(fill_ref, q_ref, kp_ref, kc_ref, vp_ref, vc_ref, os_ref, *refs):
    n_w = (len(refs) - 1) // 2
    w_refs, o_ref, wb_refs = refs[:n_w], refs[n_w], refs[n_w + 1:]
    i = pl.program_id(0)
    last = pl.num_programs(0) - 1

    @pl.when(i < last)
    def _():
        for w_ref, wb_ref in zip(w_refs, wb_refs):
            wb_ref[...] = w_ref[...].astype(BF16)
        keys = (kp_ref[...],) + tuple(kc_ref[b * WINDOW:(b + 1) * WINDOW, :] for b in range(Q_BLOCKS))
        vals = (vp_ref[...],) + tuple(vc_ref[b * WINDOW:(b + 1) * WINDOW, :] for b in range(Q_BLOCKS))
        for b in range(Q_BLOCKS):
            _attn_block(Q_BLOCKS * i + b, fill_ref, q_ref, o_ref, b * WINDOW,
                        keys[b], keys[b + 1], vals[b], vals[b + 1])

    @pl.when(i == last)
    def _():
        o_ref[0:N_SAMPLE, :] = os_ref[...]


def _attn_block(i, fill_ref, q_ref, o_ref, row0, k_prev, k_cur, v_prev, v_cur):
    q_rows = slice(row0, row0 + WINDOW)
    kk = jnp.concatenate([k_prev, k_cur], axis=0)
    vv = jnp.concatenate([v_prev, v_cur], axis=0)
    key = lax.broadcasted_iota(jnp.int32, (2 * WINDOW, 1), 0)
    vv = jnp.where(key == 0, 0.0, vv)
    low = lax.broadcasted_iota(jnp.int32, (1, LANES), 1) < HEAD_DIM
    rows = PAIRS * WINDOW
    r = lax.broadcasted_iota(jnp.int32, (rows, 2 * WINDOW), 0) % WINDOW
    c = lax.broadcasted_iota(jnp.int32, (rows, 2 * WINDOW), 1)
    visible = (c > r) & (c <= r + WINDOW) & ((i > 0) | (c >= WINDOW))

    for pair in range(N_KV_HEADS // 2):
        k_pair = kk[:, pair * LANES:(pair + 1) * LANES]
        v_pair = vv[:, pair * LANES:(pair + 1) * LANES]
        k_swap = pltpu.roll(k_pair, HEAD_DIM, 1)
        v_swap = pltpu.roll(v_pair, HEAD_DIM, 1)
        for second in range(2):
            g = 2 * pair + second
            k_in_low, k_in_high = (k_swap, k_pair) if second else (k_pair, k_swap)
            v_in_low, v_in_high = (v_swap, v_pair) if second else (v_pair, v_swap)
            x = jnp.concatenate(
                [q_ref[q_rows, (g * PAIRS + t) * LANES:(g * PAIRS + t + 1) * LANES] for t in range(PAIRS)],
                axis=0)

            def half(k_pad, v_pad, fill):
                s = lax.dot_general(x, k_pad.astype(BF16), (((1,), (1,)), ((), ())),
                                    preferred_element_type=F32)
                s = jnp.where(visible, s, fill)
                e = jnp.exp2(s - jnp.max(s, axis=-1, keepdims=True)).astype(BF16)
                return _dot(e, v_pad.astype(BF16))

            a = half(jnp.where(low, k_in_low, 0.0), jnp.where(low, v_in_low, 1.0), fill_ref[2 * g])
            b = half(jnp.where(low, 0.0, k_in_high), jnp.where(low, 1.0, v_in_high), fill_ref[2 * g + 1])
            den = pltpu.roll(jnp.where(low, b, a), HEAD_DIM, 1)
            out = (jnp.where(low, a, b) / den).astype(BF16)
            for t in range(PAIRS):
                o_ref[q_rows, (g * PAIRS + t) * LANES:(g * PAIRS + t + 1) * LANES] = \
                    out[t * WINDOW:(t + 1) * WINDOW]


def _cast_chunk_rows(rows, steps):
    assert rows % 16 == 0
    units = rows // 16
    n = max(d for d in range(1, steps + 1) if units % d == 0)
    return rows // n


def _attention(sinks, q, kv, attn_s, weights):
    nb = SEQ // (Q_BLOCKS * WINDOW)
    cur = lambda i: jnp.minimum(i, nb - 1)
    prev = lambda i: jnp.clip(Q_BLOCKS * i - 1, 0, Q_BLOCKS * nb - 1)
    per_chunk = jnp.transpose(sinks.reshape(N_KV_HEADS, PAIRS, 2), (0, 2, 1))
    per_row = jnp.broadcast_to(per_chunk.reshape(2 * N_KV_HEADS, PAIRS, 1),
                               (2 * N_KV_HEADS, PAIRS, WINDOW)).reshape(2 * N_KV_HEADS, PAIRS * WINDOW)
    first_col = jnp.arange(2 * WINDOW)[None, None, :] == 0
    fill = jnp.where(first_col, per_row[:, :, None], NEG_INF).astype(F32)
    w_in_specs, w_out_specs, w_out_shapes = [], [], []
    for w, col, width in weights:
        chunk = _cast_chunk_rows(w.shape[0], nb)
        n = w.shape[0] // chunk
        w_in_specs.append(pl.BlockSpec((chunk, width), lambda i, n=n, col=col: (jnp.minimum(i, n - 1), col)))
        w_out_specs.append(pl.BlockSpec((chunk, width), lambda i, n=n: (jnp.minimum(i, n - 1), 0)))
        w_out_shapes.append(jax.ShapeDtypeStruct((w.shape[0], width), BF16))
    outs = pl.pallas_call(
        _attn_kernel,
        grid=(nb + 1,),
        in_specs=[
            _resident(fill.shape),
            pl.BlockSpec((Q_BLOCKS * WINDOW, D_MODEL), lambda i: (cur(i), 0)),
            pl.BlockSpec((WINDOW, KV_DIM), lambda i: (prev(i), 0)),
            pl.BlockSpec((Q_BLOCKS * WINDOW, KV_DIM), lambda i: (cur(i), 0)),
            pl.BlockSpec((WINDOW, KV_DIM), lambda i: (prev(i), 1)),
            pl.BlockSpec((Q_BLOCKS * WINDOW, KV_DIM), lambda i: (cur(i), 1)),
            pl.BlockSpec((N_SAMPLE, D_MODEL), lambda i: (0, 0)),
        ] + w_in_specs,
        out_specs=[pl.BlockSpec((Q_BLOCKS * WINDOW, D_MODEL), lambda i: (i, 0))] + w_out_specs,
        out_shape=[jax.ShapeDtypeStruct((ROWS, D_MODEL), BF16)] + w_out_shapes,
        compiler_params=_params(1),
        name="attn_prompt",
    )(fill, q, kv, kv, kv, kv, attn_s, *[w for w, _, _ in weights])
    return outs[0], outs[1:]


SEQS_PER_STEP = 8


def _attn_sample_kernel(sink_ref, q_ref, kvn_ref, ck_ref, cv_ref, o_ref, nk_ref, nv_ref):
    head = lax.broadcasted_iota(jnp.int32, (N_HEADS, KV_DIM), 0) // GROUP
    col_group = lax.broadcasted_iota(jnp.int32, (N_HEADS, KV_DIM), 1) // HEAD_DIM
    own = head == col_group
    newest = lax.broadcasted_iota(jnp.int32, (1, WINDOW), 1) == WINDOW - 1
    scores = []
    for b in range(SEQS_PER_STEP):
        nk_ref[b] = jnp.where(newest, kvn_ref[0, 0:KV_DIM, b:b + 1], pltpu.roll(ck_ref[b], WINDOW - 1, 1))
        nv_ref[b] = jnp.where(newest, kvn_ref[0, KV_DIM:2 * KV_DIM, b:b + 1],
                              pltpu.roll(cv_ref[b], WINDOW - 1, 1))
        qb = q_ref[b]
        qe = jnp.where(own, jnp.concatenate([qb] * N_KV_HEADS, axis=1), jnp.zeros((), BF16))
        scores.append(_dot(qe, nk_ref[b].astype(BF16)))
    s = jnp.concatenate(scores, axis=0)
    sink = jnp.concatenate([sink_ref[...]] * SEQS_PER_STEP, axis=0)
    mx = jnp.maximum(jnp.max(s, axis=-1, keepdims=True), sink)
    e = jnp.exp2(s - mx)
    inv = 1.0 / (jnp.sum(e, axis=-1, keepdims=True) + jnp.exp2(sink - mx))
    e = e.astype(BF16)
    for b in range(SEQS_PER_STEP):
        rows = slice(b * N_HEADS, (b + 1) * N_HEADS)
        of = lax.dot_general(e[rows], nv_ref[b].astype(BF16), (((1,), (1,)), ((), ())),
                             preferred_element_type=F32)
        of = jnp.where(own, of, 0.0)
        o = of[:, 0:HEAD_DIM]
        for g in range(1, N_KV_HEADS):
            o = o + of[:, g * HEAD_DIM:(g + 1) * HEAD_DIM]
        o_ref[b] = (o * inv[rows]).astype(BF16)


def _attention_sample(sink_col, q3, kvn_t, ck, cv):
    nb = N_SAMPLE // SEQS_PER_STEP
    cache_spec = pl.BlockSpec((SEQS_PER_STEP, KV_DIM, WINDOW), lambda i: (i, 0, 0))
    return pl.pallas_call(
        _attn_sample_kernel,
        grid=(nb,),
        in_specs=[
            pl.BlockSpec((N_HEADS, 1), lambda i: (0, 0)),
            pl.BlockSpec((SEQS_PER_STEP, N_HEADS, HEAD_DIM), lambda i: (i, 0, 0)),
            pl.BlockSpec((1, 2 * KV_DIM, SEQS_PER_STEP), lambda i: (i, 0, 0)),
            cache_spec, cache_spec,
        ],
        out_specs=[
            pl.BlockSpec((SEQS_PER_STEP, N_HEADS, HEAD_DIM), lambda i: (i, 0, 0)),
            cache_spec, cache_spec,
        ],
        out_shape=[
            jax.ShapeDtypeStruct((N_SAMPLE, N_HEADS, HEAD_DIM), BF16),
            jax.ShapeDtypeStruct((N_SAMPLE, KV_DIM, WINDOW), F32),
            jax.ShapeDtypeStruct((N_SAMPLE, KV_DIM, WINDOW), F32),
        ],
        compiler_params=_params(1),
        name="attn_sample",
    )(sink_col, q3, kvn_t, ck, cv)


POOL_PAD = 16


def _pool_kernel(xb_ref, wu0_ref, wu1_ref, sp_ref, wmix_ref, scale_ref, p_ref, spn_ref, plast_ref, ext):
    m = pl.program_id(0)

    @pl.when(m == 0)
    def _():
        ext[0:POOL_PAD, :] = jnp.zeros((POOL_PAD, POOL_WIDTH), F32)

    pos = m * TM + lax.broadcasted_iota(jnp.int32, (TM, 1), 0)
    for t, wu_ref in enumerate((wu0_ref, wu1_ref)):
        ext[POOL_PAD:POOL_PAD + TM, t * TN:(t + 1) * TN] = _dot(xb_ref[...], wu_ref[...])
    for g, w in enumerate(POOL_WINDOWS):
        cols = slice(g * POOL_GROUP_DIM, (g + 1) * POOL_GROUP_DIM)
        u_g = ext[POOL_PAD:POOL_PAD + TM, cols]
        run = ext[:, cols]
        shift = 1
        while shift < w:
            run = run + pltpu.roll(run, shift, 0)
            shift *= 2
        tot = run[POOL_PAD:POOL_PAD + TM, :]
        inv_cnt = 1.0 / jnp.minimum(w, pos + 1).astype(F32)
        d = tot * inv_cnt - u_g
        y = _dot(d.astype(BF16), wmix_ref[g].astype(BF16))
        p_ref[:, cols] = (y * scale_ref[:, cols]).astype(BF16)

    @pl.when(m == NT - 1)
    def _():
        for g, w in enumerate(POOL_WINDOWS):
            cols = slice(g * POOL_GROUP_DIM, (g + 1) * POOL_GROUP_DIM)
            us = ext[POOL_PAD + SPLIT:POOL_PAD + TM, cols]
            tot = us
            for j in range(1, w):
                tot = tot + sp_ref[POOL_HIST - j, :, cols]
            d = tot * (1.0 / w) - us
            y = _dot(d.astype(BF16), wmix_ref[g].astype(BF16))
            p_ref[SPLIT:TM, cols] = (y * scale_ref[:, cols]).astype(BF16)
        spn_ref[0:POOL_HIST - 1] = sp_ref[1:POOL_HIST]
        spn_ref[POOL_HIST - 1] = ext[POOL_PAD + SPLIT:POOL_PAD + TM, :]
        plast_ref[...] = ext[SPLIT:POOL_PAD + SPLIT, :]

    ext[0:POOL_PAD, :] = ext[TM:TM + POOL_PAD, :]


def _pool(xb, wu0, wu1, sp_t, wmix, scale):
    full = lambda shape: pl.BlockSpec(shape, lambda m: (0,) * len(shape))
    return pl.pallas_call(
        _pool_kernel,
        grid=(NT,),
        in_specs=[
            pl.BlockSpec((TM, D_MODEL), lambda m: (m, 0)),
            _resident((D_MODEL, TN)), _resident((D_MODEL, TN)),
            full((POOL_HIST, N_SAMPLE, POOL_WIDTH)),
            full((len(POOL_WINDOWS), POOL_GROUP_DIM, POOL_GROUP_DIM)),
            full((1, POOL_WIDTH)),
        ],
        out_specs=[pl.BlockSpec((TM, POOL_WIDTH), lambda m: (m, 0)),
                   full((POOL_HIST, N_SAMPLE, POOL_WIDTH)),
                   full((POOL_PAD, POOL_WIDTH))],
        out_shape=[jax.ShapeDtypeStruct((ROWS, POOL_WIDTH), BF16),
                   jax.ShapeDtypeStruct((POOL_HIST, N_SAMPLE, POOL_WIDTH), F32),
                   jax.ShapeDtypeStruct((POOL_PAD, POOL_WIDTH), F32)],
        scratch_shapes=[pltpu.VMEM((POOL_PAD + TM, POOL_WIDTH), F32)],
        compiler_params=_params(1),
        name="pool_mix",
    )(xb, wu0, wu1, sp_t, wmix, scale)


_NC = D_MODEL // TN


TMH = TM // 2
NTH = ROWS // TMH
SPLIT_H = SEQ - (NTH - 1) * TMH
assert NTH * TMH == ROWS and SPLIT_H + N_SAMPLE == TMH and SPLIT_H % 8 == 0


def _resident(shape):
    return pl.BlockSpec(shape, lambda m: (0,) * len(shape), pipeline_mode=pl.Buffered(1))


def _merge_kernel(xb_ref, pool_ref, attn_ref, wp_ref, wa_ref, *refs):
    gate_w, o_ref = refs[:2 * _NC], refs[2 * _NC]
    for c in range(_NC):
        cols = slice(c * TN, (c + 1) * TN)
        gate_pool = jax.nn.sigmoid(_dot(xb_ref[...], gate_w[c][...]))
        gate_attn = jax.nn.sigmoid(_dot(xb_ref[...], gate_w[_NC + c][...]))
        a = _dot(pool_ref[...], wp_ref[:, cols])
        b = _dot(attn_ref[...], wa_ref[:, cols])
        o_ref[:, cols] = (gate_pool * a + gate_attn * b).astype(BF16)


def _merge(xb, pooled, attn, wp, wa, gate_w):
    row = lambda width: pl.BlockSpec((TMH, width), lambda m: (m, 0))
    return pl.pallas_call(
        _merge_kernel,
        grid=(NTH,),
        in_specs=[row(D_MODEL), row(POOL_WIDTH), row(D_MODEL),
                  _resident((POOL_WIDTH, D_MODEL)), _resident((D_MODEL, D_MODEL))]
        + [_resident((D_MODEL, TN))] * (2 * _NC),
        out_specs=row(D_MODEL),
        out_shape=jax.ShapeDtypeStruct((ROWS, D_MODEL), BF16),
        compiler_params=_params(1),
        name="branch_merge",
    )(xb, pooled, attn, wp, wa, *gate_w)


def _out_ln_kernel(mrg_ref, wo_ref, x_ref, xs_ref, lng_ref, lnb_ref, x1_ref, x1b_ref):
    m = pl.program_id(0)
    for c in range(_NC):
        cols = slice(c * TN, (c + 1) * TN)
        x1_ref[:, cols] = _dot(mrg_ref[...], wo_ref[:, cols])

    def finish(z):
        x1 = _layer_norm(z, lng_ref[...], lnb_ref[...])
        x1_ref[...] = ALPHA * x1
        x1b_ref[...] = x1.astype(BF16)

    @pl.when(m < NTH - 1)
    def _():
        finish(x1_ref[...] + ALPHA * x_ref[...])

    @pl.when(m == NTH - 1)
    def _():
        finish(jnp.concatenate([x1_ref[0:SPLIT_H, :] + ALPHA * x_ref[0:SPLIT_H, :],
                                x1_ref[SPLIT_H:TMH, :] + ALPHA * xs_ref[...]], axis=0))


def _out_ln(merged, wo, x, xs, ln_g, ln_b):
    row = pl.BlockSpec((TMH, D_MODEL), lambda m: (m, 0))
    return pl.pallas_call(
        _out_ln_kernel,
        grid=(NTH,),
        in_specs=[row, _resident((D_MODEL, D_MODEL)), row, _resident((N_SAMPLE, D_MODEL)),
                  _resident((1, D_MODEL)), _resident((1, D_MODEL))],
        out_specs=[row, row],
        out_shape=[jax.ShapeDtypeStruct((ROWS, D_MODEL), F32), jax.ShapeDtypeStruct((ROWS, D_MODEL), BF16)],
        compiler_params=_params(1),
        name="out_proj_ln1",
    )(merged, wo, x, xs, ln_g, ln_b)


TF = 512
NF = D_FF // TF
CARRY = 8


def _ffn_kernel(x1b_ref, x1_hbm, sc_ref, wg_ref, wu_ref, wd_ref, cw_ref, cb_ref, lng_ref, lnb_ref,
                o_ref, os_ref, glast_ref, scn_ref,
                h0, h1, gext, carry, sem):
    m = pl.program_id(0)
    f = pl.program_id(1)
    last_m = m == NT - 1
    h_slots = (h0, h1)
    x1_copy = pltpu.make_async_copy(x1_hbm.at[pl.ds(m * TM, TM), :], o_ref, sem)

    def up_and_gate(slot, with_down):
        halves = ((0, HALF), (HALF, TM))
        gext[0:CARRY, :] = carry[f]
        for lo, hi in halves:
            gext[CARRY + lo:CARRY + hi, :] = _dot(x1b_ref[lo:hi, :], wg_ref[...])
        ups = [_dot(x1b_ref[lo:hi, :], wu_ref[...]) for lo, hi in halves]
        if with_down:
            o_ref[...] += _dot(h_slots[1 - slot][...], wd_ref[...])
        w0 = cw_ref[0:1, :]
        w1 = cw_ref[1:2, :]
        w2 = cw_ref[2:3, :]
        cb = cb_ref[...]

        def conv(lo, hi, older=None, newer=None):
            older = gext[CARRY - 2 + lo:CARRY - 2 + hi, :] if older is None else older
            newer = gext[CARRY - 1 + lo:CARRY - 1 + hi, :] if newer is None else newer
            return cb + w0 * older + w1 * newer + w2 * gext[CARRY + lo:CARRY + hi, :]

        for (lo, hi), up in zip(halves, ups):
            if hi <= SPLIT:
                gc = conv(lo, hi)
            else:
                older_s = jnp.where(last_m, sc_ref[:, 0, :], gext[CARRY - 2 + SPLIT:CARRY - 2 + TM, :])
                newer_s = jnp.where(last_m, sc_ref[:, 1, :], gext[CARRY - 1 + SPLIT:CARRY - 1 + TM, :])
                gc = jnp.concatenate([conv(lo, SPLIT), conv(SPLIT, TM, older_s, newer_s)], axis=0)
            h_slots[slot][lo:hi, :] = (jax.nn.gelu(gc, approximate=True) * up).astype(BF16)
        carry[f] = gext[TM:CARRY + TM, :]

        @pl.when(last_m)
        def _():
            glast_ref[...] = gext[SPLIT:CARRY + SPLIT, :]
            scn_ref[:, 0, :] = sc_ref[:, 1, :]
            scn_ref[:, 1, :] = gext[CARRY + SPLIT:CARRY + TM, :]

    @pl.when(f == 0)
    def _():
        @pl.when(m == 0)
        def _():
            carry[...] = jnp.zeros_like(carry)

        x1_copy.start()
        up_and_gate(0, False)
        x1_copy.wait()

    for parity in range(2):
        @pl.when((f > 0) & (f < NF) & (f % 2 == parity))
        def _():
            up_and_gate(parity, True)

    @pl.when(f == NF)
    def _():
        h_last = h_slots[(NF - 1) % 2]
        parts = ((0, 352), (352, 704), (704, TM))
        accs = [o_ref[lo:hi, :] + _dot(h_last[lo:hi, :], wd_ref[...]) for lo, hi in parts]
        for (lo, hi), acc in zip(parts, accs):
            o_ref[lo:hi, :] = _layer_norm(acc, lng_ref[...], lnb_ref[...])

        @pl.when(last_m)
        def _():
            os_ref[...] = o_ref[SPLIT:TM, :]


def _ffn(x1b, x1, sc, wg, wu, wd, conv_w, conv_b, ln_g, ln_b):
    full = lambda shape: pl.BlockSpec(shape, lambda m, f: (0, 0))
    g_tile = lambda f: jnp.minimum(f, NF - 1)
    d_tile = lambda f: jnp.maximum(f - 1, 0)
    only_last = lambda m, f: jnp.where(m == NT - 1, g_tile(f), 0)
    row = pl.BlockSpec((TM, D_MODEL), lambda m, f: (m, 0))
    return pl.pallas_call(
        _ffn_kernel,
        grid=(NT, NF + 1),
        in_specs=[
            row,
            pl.BlockSpec(memory_space=pl.ANY),
            pl.BlockSpec((N_SAMPLE, CONV_W - 1, TF), lambda m, f: (0, 0, only_last(m, f))),
            pl.BlockSpec((D_MODEL, TF), lambda m, f: (0, g_tile(f))),
            pl.BlockSpec((D_MODEL, TF), lambda m, f: (0, g_tile(f))),
            pl.BlockSpec((TF, D_MODEL), lambda m, f: (d_tile(f), 0)),
            pl.BlockSpec((CONV_W, TF), lambda m, f: (0, g_tile(f))),
            pl.BlockSpec((1, TF), lambda m, f: (0, g_tile(f))),
            full((1, D_MODEL)), full((1, D_MODEL)),
        ],
        out_specs=[
            row,
            full((N_SAMPLE, D_MODEL)),
            pl.BlockSpec((CARRY, TF), lambda m, f: (0, only_last(m, f))),
            pl.BlockSpec((N_SAMPLE, CONV_W - 1, TF), lambda m, f: (0, 0, only_last(m, f))),
        ],
        out_shape=[
            jax.ShapeDtypeStruct((SEQ, D_MODEL), F32),
            jax.ShapeDtypeStruct((N_SAMPLE, D_MODEL), F32),
            jax.ShapeDtypeStruct((CARRY, D_FF), F32),
            jax.ShapeDtypeStruct((N_SAMPLE, CONV_W - 1, D_FF), F32),
        ],
        scratch_shapes=[
            pltpu.VMEM((TM, TF), BF16), pltpu.VMEM((TM, TF), BF16),
            pltpu.VMEM((CARRY + TM, TF), F32),
            pltpu.VMEM((NF, CARRY, TF), F32),
            pltpu.SemaphoreType.DMA(()),
        ],
        compiler_params=_params(2),
        name="convffn_ln2",
    )(x1b, x1, sc, wg, wu, wd, conv_w, conv_b, ln_g, ln_b)


def _rope_tables(pos):
    half = HEAD_DIM // 2
    inv = ROPE_THETA ** (-jnp.arange(half, dtype=F32) / half)
    lane = jnp.arange(LANES)
    inv_lanes = inv[lane % half]
    sign = jnp.where((lane % HEAD_DIM) < half, -1.0, 1.0).astype(F32)
    ang = pos.astype(F32)[:, None] * inv_lanes[None, :]
    return jnp.cos(ang), jnp.sin(ang) * sign[None, :]


def kernel(x_prompt, x_sample, cache_k, cache_v, state_pool, state_conv, w_in, attn_sinks, w_pool_mix,
           pool_scale, w_attn_branch, w_pool_branch, w_out, ln1_g, ln1_b, w_up, w_gate, conv_w, conv_b,
           w_down, ln2_g, ln2_b):
    x = x_prompt.reshape(SEQ, D_MODEL)
    xs = x_sample.reshape(N_SAMPLE, D_MODEL)
    pos = jnp.concatenate([jnp.arange(SEQ), jnp.full((N_SAMPLE,), PAST_LEN)])
    cos, sin = _rope_tables(pos)

    q, kv, xb = _in_proj(x, xs, w_in[0], cos, sin)

    to_dsw = lambda c: jnp.transpose(c[0], (0, 2, 3, 1)).reshape(N_SAMPLE, KV_DIM, WINDOW)
    steps = N_SAMPLE // SEQS_PER_STEP
    kvn_t = jnp.transpose(kv[SEQ:].reshape(steps, SEQS_PER_STEP, 2 * KV_DIM), (0, 2, 1))
    sinks = attn_sinks[0] * LOG2E
    attn_s, new_k, new_v = _attention_sample(
        sinks.reshape(N_HEADS, 1), q[SEQ:].reshape(N_SAMPLE, N_HEADS, HEAD_DIM), kvn_t,
        to_dsw(cache_k), to_dsw(cache_v))
    whole = lambda w: (w, 0, w.shape[1])
    late_tiles = [(w_in[0], j, TN) for j in range(_N_U0, _N_G0 + 2 * _NC)]
    attn, (wp_b, wa_b, wo_b, wg_b, wu_b, wd_b, wpool0, wpool1, *gate_w) = _attention(
        sinks, q, kv, attn_s.reshape(N_SAMPLE, D_MODEL),
        [whole(w_pool_branch[0]), whole(w_attn_branch[0]), whole(w_out[0]),
         whole(w_gate[0]), whole(w_up[0]), whole(w_down[0])] + late_tiles)

    state_pool_t = jnp.transpose(state_pool[0], (1, 0, 2))
    pooled, new_pool_t, pool_last = _pool(xb, wpool0, wpool1, state_pool_t, w_pool_mix[0], pool_scale)

    merged = _merge(xb, pooled, attn, wp_b, wa_b, gate_w)
    x1, x1b = _out_ln(merged, wo_b, x, xs, ln1_g, ln1_b)

    y, y_s, g_last, new_conv_s = _ffn(x1b, x1, state_conv[0], wg_b, wu_b, wd_b,
                                      conv_w[0], conv_b, ln2_g, ln2_b)

    from_dsw = lambda c: jnp.transpose(
        c.reshape(N_SAMPLE, N_KV_HEADS, HEAD_DIM, WINDOW), (0, 3, 1, 2))[None]
    return (
        y.reshape(1, SEQ, D_MODEL),
        y_s.reshape(N_SAMPLE, 1, D_MODEL),
        kv[SEQ - WINDOW:SEQ, :KV_DIM].reshape(1, 1, WINDOW, N_KV_HEADS, HEAD_DIM),
        kv[SEQ - WINDOW:SEQ, KV_DIM:].reshape(1, 1, WINDOW, N_KV_HEADS, HEAD_DIM),
        pool_last[POOL_PAD - POOL_HIST:].reshape(1, 1, POOL_HIST, POOL_WIDTH),
        g_last[CARRY - (CONV_W - 1):].reshape(1, 1, CONV_W - 1, D_FF),
        from_dsw(new_k),
        from_dsw(new_v),
        jnp.transpose(new_pool_t, (1, 0, 2))[None],
        new_conv_s[None],
    )
```

```python
import jax
import jax.numpy as jnp
from jax import lax
from jax.experimental import pallas as pl
from jax.experimental.pallas import tpu as pltpu

F32 = jnp.float32
BF16 = jnp.bfloat16

D_MODEL = 2048
SEQ = 8192
N_SAMPLE = 128
PAST_LEN = 8192
HEAD_DIM = 64
N_HEADS = 32
N_KV_HEADS = 4
GROUP = N_HEADS // N_KV_HEADS
KV_DIM = N_KV_HEADS * HEAD_DIM
WINDOW = 128
ROPE_THETA = 10000.0
POOL_WINDOWS = (2, 4, 8, 16)
POOL_WIDTH = 1024
POOL_GROUP_DIM = 256
POOL_HIST = 15
D_FF = 5632
CONV_W = 3
IN_DIM = 7680
LN_EPS = 1e-5
NEG_INF = -1e30
ALPHA = 2.0 ** 0.25
LOG2E = 1.4426950408889634

LANES = 128
VMEM_LIMIT = 60000 * 1024

ROWS = SEQ + N_SAMPLE
TM = 1040
NT = ROWS // TM
SPLIT = SEQ - (NT - 1) * TM
HALF = 528
TN = 512
assert NT * TM == ROWS and SPLIT + N_SAMPLE == TM and SPLIT % 16 == 0 and HALF % 16 == 0


def _params(ndim, flags=None):
    return pltpu.CompilerParams(dimension_semantics=("arbitrary",) * ndim,
                                vmem_limit_bytes=VMEM_LIMIT, flags=flags)


def _dot(a, b):
    return jnp.dot(a, b, preferred_element_type=F32)


def _layer_norm(z, g, b):
    mu = jnp.mean(z, axis=-1, keepdims=True)
    d = z - mu
    var = jnp.mean(d * d, axis=-1, keepdims=True)
    return d * lax.rsqrt(var + LN_EPS) * g + b


def _rope(x, cos, sin_signed):
    lane = lax.broadcasted_iota(jnp.int32, (1, LANES), 1)
    low_half = (lane % HEAD_DIM) < (HEAD_DIM // 2)
    outs = []
    for c in range(x.shape[1] // LANES):
        xc = x[:, c * LANES:(c + 1) * LANES]
        up = pltpu.roll(xc, LANES - HEAD_DIM // 2, 1)
        down = pltpu.roll(xc, HEAD_DIM // 2, 1)
        outs.append(xc * cos + jnp.where(low_half, up, down) * sin_signed)
    return jnp.concatenate(outs, axis=1)


_NQ = D_MODEL // TN
_N_KV = _NQ
_N_U0 = _N_KV + 1
_N_G0 = _N_U0 + POOL_WIDTH // TN


def _in_proj_kernel(x_ref, xs_ref, w_ref, cos_ref, sin_ref, q_ref, kv_ref, xb, wres):
    m = pl.program_id(0)
    n = pl.program_id(1)

    @pl.when(n == 0)
    def _():
        @pl.when(m < NT - 1)
        def _():
            xb[...] = x_ref[...].astype(BF16)

        @pl.when(m == NT - 1)
        def _():
            xb[0:SPLIT, :] = x_ref[0:SPLIT, :].astype(BF16)
            xb[SPLIT:TM, :] = xs_ref[...].astype(BF16)

    @pl.when(m == 0)
    def _():
        wres[n] = w_ref[...].astype(BF16)

    halves = ((0, HALF), (HALF, TM))
    thirds = ((0, 352), (352, 704), (704, TM))

    def project(epilogue, parts):
        accs = [_dot(xb[lo:hi, :], wres[n]) for lo, hi in parts]
        for (lo, hi), acc in zip(parts, accs):
            epilogue(acc, lo, hi)

    def q_epilogue(acc, lo, hi):
        roped = _rope(acc, cos_ref[lo:hi, :], sin_ref[lo:hi, :])
        q_ref[lo:hi, :] = (roped * (HEAD_DIM ** -0.5 * LOG2E)).astype(BF16)

    def kv_epilogue(acc, lo, hi):
        kv_ref[lo:hi, :KV_DIM] = _rope(acc[:, :KV_DIM], cos_ref[lo:hi, :], sin_ref[lo:hi, :])
        kv_ref[lo:hi, KV_DIM:] = acc[:, KV_DIM:]

    pl.when(n < _NQ)(lambda: project(q_epilogue, thirds))
    pl.when(n == _N_KV)(lambda: project(kv_epilogue, halves))


def _in_proj(x, xs, w_in, cos, sin):
    nn = _N_U0

    return pl.pallas_call(
        _in_proj_kernel,
        grid=(NT, nn),
        in_specs=[
            pl.BlockSpec((TM, D_MODEL), lambda m, n: (m, 0)),
            pl.BlockSpec((N_SAMPLE, D_MODEL), lambda m, n: (0, 0)),
            pl.BlockSpec((D_MODEL, TN), lambda m, n: (0, jnp.where(m == 0, n, nn - 1))),
            pl.BlockSpec((TM, LANES), lambda m, n: (m, 0)),
            pl.BlockSpec((TM, LANES), lambda m, n: (m, 0)),
        ],
        out_specs=[
            pl.BlockSpec((TM, TN), lambda m, n: (m, jnp.minimum(n, _NQ - 1))),
            pl.BlockSpec((TM, TN), lambda m, n: (m, 0)),
            pl.BlockSpec((TM, D_MODEL), lambda m, n: (m, 0)),
        ],
        out_shape=[
            jax.ShapeDtypeStruct((ROWS, D_MODEL), BF16),
            jax.ShapeDtypeStruct((ROWS, 2 * KV_DIM), F32),
            jax.ShapeDtypeStruct((ROWS, D_MODEL), BF16),
        ],
        scratch_shapes=[pltpu.VMEM((nn, D_MODEL, TN), BF16)],
        compiler_params=_params(2),
        name="in_proj",
    )(x, xs, w_in, cos, sin)


PAIRS = GROUP // 2
Q_BLOCKS = 2


def _attn_kernel(fill_ref, q_ref, kp_ref, kc_ref, vp_ref, vc_ref, os_ref, *refs):
    n_w = (len(refs) - 1) // 2
    w_refs, o_ref, wb_refs = refs[:n_w], refs[n_w], refs[n_w + 1:]
    i = pl.program_id(0)
    last = pl.num_programs(0) - 1

    @pl.when(i < last)
    def _():
        for w_ref, wb_ref in zip(w_refs, wb_refs):
            wb_ref[...] = w_ref[...].astype(BF16)
        keys = (kp_ref[...],) + tuple(kc_ref[b * WINDOW:(b + 1) * WINDOW, :] for b in range(Q_BLOCKS))
        vals = (vp_ref[...],) + tuple(vc_ref[b * WINDOW:(b + 1) * WINDOW, :] for b in range(Q_BLOCKS))
        for b in range(Q_BLOCKS):
            _attn_block(Q_BLOCKS * i + b, fill_ref, q_ref, o_ref, b * WINDOW,
                        keys[b], keys[b + 1], vals[b], vals[b + 1])

    @pl.when(i == last)
    def _():
        o_ref[0:N_SAMPLE, :] = os_ref[...]


def _attn_block(i, fill_ref, q_ref, o_ref, row0, k_prev, k_cur, v_prev, v_cur):
    q_rows = slice(row0, row0 + WINDOW)
    kk = jnp.concatenate([k_prev, k_cur], axis=0)
    vv = jnp.concatenate([v_prev, v_cur], axis=0)
    key = lax.broadcasted_iota(jnp.int32, (2 * WINDOW, 1), 0)
    vv = jnp.where(key == 0, 0.0, vv)
    low = lax.broadcasted_iota(jnp.int32, (1, LANES), 1) < HEAD_DIM
    rows = PAIRS * WINDOW
    r = lax.broadcasted_iota(jnp.int32, (rows, 2 * WINDOW), 0) % WINDOW
    c = lax.broadcasted_iota(jnp.int32, (rows, 2 * WINDOW), 1)
    visible = (c > r) & (c <= r + WINDOW) & ((i > 0) | (c >= WINDOW))

    for pair in range(N_KV_HEADS // 2):
        k_pair = kk[:, pair * LANES:(pair + 1) * LANES]
        v_pair = vv[:, pair * LANES:(pair + 1) * LANES]
        k_swap = pltpu.roll(k_pair, HEAD_DIM, 1)
        v_swap = pltpu.roll(v_pair, HEAD_DIM, 1)
        for second in range(2):
            g = 2 * pair + second
            k_in_low, k_in_high = (k_swap, k_pair) if second else (k_pair, k_swap)
            v_in_low, v_in_high = (v_swap, v_pair) if second else (v_pair, v_swap)
            x = jnp.concatenate(
                [q_ref[q_rows, (g * PAIRS + t) * LANES:(g * PAIRS + t + 1) * LANES] for t in range(PAIRS)],
                axis=0)

            def half(k_pad, v_pad, fill):
                s = lax.dot_general(x, k_pad.astype(BF16), (((1,), (1,)), ((), ())),
                                    preferred_element_type=F32)
                s = jnp.where(visible, s, fill)
                e = jnp.exp2(s - jnp.max(s, axis=-1, keepdims=True)).astype(BF16)
                return _dot(e, v_pad.astype(BF16))

            a = half(jnp.where(low, k_in_low, 0.0), jnp.where(low, v_in_low, 1.0), fill_ref[2 * g])
            b = half(jnp.where(low, 0.0, k_in_high), jnp.where(low, 1.0, v_in_high), fill_ref[2 * g + 1])
            den = pltpu.roll(jnp.where(low, b, a), HEAD_DIM, 1)
            out = (jnp.where(low, a, b) / den).astype(BF16)
            for t in range(PAIRS):
                o_ref[q_rows, (g * PAIRS + t) * LANES:(g * PAIRS + t + 1) * LANES] = \
                    out[t * WINDOW:(t + 1) * WINDOW]


def _cast_chunk_rows(rows, steps):
    assert rows % 16 == 0
    units = rows // 16
    n = max(d for d in range(1, steps + 1) if units % d == 0)
    return rows // n


def _attention(sinks, q, kv, attn_s, weights):
    nb = SEQ // (Q_BLOCKS * WINDOW)
    cur = lambda i: jnp.minimum(i, nb - 1)
    prev = lambda i: jnp.clip(Q_BLOCKS * i - 1, 0, Q_BLOCKS * nb - 1)
    per_chunk = jnp.transpose(sinks.reshape(N_KV_HEADS, PAIRS, 2), (0, 2, 1))
    per_row = jnp.broadcast_to(per_chunk.reshape(2 * N_KV_HEADS, PAIRS, 1),
                               (2 * N_KV_HEADS, PAIRS, WINDOW)).reshape(2 * N_KV_HEADS, PAIRS * WINDOW)
    first_col = jnp.arange(2 * WINDOW)[None, None, :] == 0
    fill = jnp.where(first_col, per_row[:, :, None], NEG_INF).astype(F32)
    w_in_specs, w_out_specs, w_out_shapes = [], [], []
    for w, col, width in weights:
        chunk = _cast_chunk_rows(w.shape[0], nb)
        n = w.shape[0] // chunk
        w_in_specs.append(pl.BlockSpec((chunk, width), lambda i, n=n, col=col: (jnp.minimum(i, n - 1), col)))
        w_out_specs.append(pl.BlockSpec((chunk, width), lambda i, n=n: (jnp.minimum(i, n - 1), 0)))
        w_out_shapes.append(jax.ShapeDtypeStruct((w.shape[0], width), BF16))
    outs = pl.pallas_call(
        _attn_kernel,
        grid=(nb + 1,),
        in_specs=[
            _resident(fill.shape),
            pl.BlockSpec((Q_BLOCKS * WINDOW, D_MODEL), lambda i: (cur(i), 0)),
            pl.BlockSpec((WINDOW, KV_DIM), lambda i: (prev(i), 0)),
            pl.BlockSpec((Q_BLOCKS * WINDOW, KV_DIM), lambda i: (cur(i), 0)),
            pl.BlockSpec((WINDOW, KV_DIM), lambda i: (prev(i), 1)),
            pl.BlockSpec((Q_BLOCKS * WINDOW, KV_DIM), lambda i: (cur(i), 1)),
            pl.BlockSpec((N_SAMPLE, D_MODEL), lambda i: (0, 0)),
        ] + w_in_specs,
        out_specs=[pl.BlockSpec((Q_BLOCKS * WINDOW, D_MODEL), lambda i: (i, 0))] + w_out_specs,
        out_shape=[jax.ShapeDtypeStruct((ROWS, D_MODEL), BF16)] + w_out_shapes,
        compiler_params=_params(1),
        name="attn_prompt",
    )(fill, q, kv, kv, kv, kv, attn_s, *[w for w, _, _ in weights])
    return outs[0], outs[1:]


SEQS_PER_STEP = 8


def _attn_sample_kernel(sink_ref, q_ref, kvn_ref, ck_ref, cv_ref, o_ref, nk_ref, nv_ref):
    head = lax.broadcasted_iota(jnp.int32, (N_HEADS, KV_DIM), 0) // GROUP
    col_group = lax.broadcasted_iota(jnp.int32, (N_HEADS, KV_DIM), 1) // HEAD_DIM
    own = head == col_group
    newest = lax.broadcasted_iota(jnp.int32, (1, WINDOW), 1) == WINDOW - 1
    scores = []
    for b in range(SEQS_PER_STEP):
        nk_ref[b] = jnp.where(newest, kvn_ref[0, 0:KV_DIM, b:b + 1], pltpu.roll(ck_ref[b], WINDOW - 1, 1))
        nv_ref[b] = jnp.where(newest, kvn_ref[0, KV_DIM:2 * KV_DIM, b:b + 1],
                              pltpu.roll(cv_ref[b], WINDOW - 1, 1))
        qb = q_ref[b]
        qe = jnp.where(own, jnp.concatenate([qb] * N_KV_HEADS, axis=1), jnp.zeros((), BF16))
        scores.append(_dot(qe, nk_ref[b].astype(BF16)))
    s = jnp.concatenate(scores, axis=0)
    sink = jnp.concatenate([sink_ref[...]] * SEQS_PER_STEP, axis=0)
    mx = jnp.maximum(jnp.max(s, axis=-1, keepdims=True), sink)
    e = jnp.exp2(s - mx)
    inv = 1.0 / (jnp.sum(e, axis=-1, keepdims=True) + jnp.exp2(sink - mx))
    e = e.astype(BF16)
    for b in range(SEQS_PER_STEP):
        rows = slice(b * N_HEADS, (b + 1) * N_HEADS)
        of = lax.dot_general(e[rows], nv_ref[b].astype(BF16), (((1,), (1,)), ((), ())),
                             preferred_element_type=F32)
        of = jnp.where(own, of, 0.0)
        o = of[:, 0:HEAD_DIM]
        for g in range(1, N_KV_HEADS):
            o = o + of[:, g * HEAD_DIM:(g + 1) * HEAD_DIM]
        o_ref[b] = (o * inv[rows]).astype(BF16)


def _attention_sample(sink_col, q3, kvn_t, ck, cv):
    nb = N_SAMPLE // SEQS_PER_STEP
    cache_spec = pl.BlockSpec((SEQS_PER_STEP, KV_DIM, WINDOW), lambda i: (i, 0, 0))
    return pl.pallas_call(
        _attn_sample_kernel,
        grid=(nb,),
        in_specs=[
            pl.BlockSpec((N_HEADS, 1), lambda i: (0, 0)),
            pl.BlockSpec((SEQS_PER_STEP, N_HEADS, HEAD_DIM), lambda i: (i, 0, 0)),
            pl.BlockSpec((1, 2 * KV_DIM, SEQS_PER_STEP), lambda i: (i, 0, 0)),
            cache_spec, cache_spec,
        ],
        out_specs=[
            pl.BlockSpec((SEQS_PER_STEP, N_HEADS, HEAD_DIM), lambda i: (i, 0, 0)),
            cache_spec, cache_spec,
        ],
        out_shape=[
            jax.ShapeDtypeStruct((N_SAMPLE, N_HEADS, HEAD_DIM), BF16),
            jax.ShapeDtypeStruct((N_SAMPLE, KV_DIM, WINDOW), F32),
            jax.ShapeDtypeStruct((N_SAMPLE, KV_DIM, WINDOW), F32),
        ],
        compiler_params=_params(1),
        name="attn_sample",
    )(sink_col, q3, kvn_t, ck, cv)


POOL_PAD = 16


def _pool_kernel(xb_ref, wu0_ref, wu1_ref, sp_ref, wmix_ref, scale_ref, p_ref, spn_ref, plast_ref, ext):
    m = pl.program_id(0)

    @pl.when(m == 0)
    def _():
        ext[0:POOL_PAD, :] = jnp.zeros((POOL_PAD, POOL_WIDTH), F32)

    pos = m * TM + lax.broadcasted_iota(jnp.int32, (TM, 1), 0)
    for t, wu_ref in ((1, wu1_ref), (0, wu0_ref)):
        ext[POOL_PAD:POOL_PAD + TM, t * TN:(t + 1) * TN] = _dot(xb_ref[...], wu_ref[...])
    for g, w in reversed(list(enumerate(POOL_WINDOWS))):
        cols = slice(g * POOL_GROUP_DIM, (g + 1) * POOL_GROUP_DIM)
        u_g = ext[POOL_PAD:POOL_PAD + TM, cols]
        run = ext[:, cols]
        shift = 1
        while shift < w:
            run = run + pltpu.roll(run, shift, 0)
            shift *= 2
        tot = run[POOL_PAD:POOL_PAD + TM, :]
        inv_cnt = 1.0 / jnp.minimum(w, pos + 1).astype(F32)
        d = tot * inv_cnt - u_g
        y = _dot(d.astype(BF16), wmix_ref[g].astype(BF16))
        p_ref[:, cols] = (y * scale_ref[:, cols]).astype(BF16)

    @pl.when(m == NT - 1)
    def _():
        for g, w in enumerate(POOL_WINDOWS):
            cols = slice(g * POOL_GROUP_DIM, (g + 1) * POOL_GROUP_DIM)
            us = ext[POOL_PAD + SPLIT:POOL_PAD + TM, cols]
            tot = us
            for j in range(1, w):
                tot = tot + sp_ref[POOL_HIST - j, :, cols]
            d = tot * (1.0 / w) - us
            y = _dot(d.astype(BF16), wmix_ref[g].astype(BF16))
            p_ref[SPLIT:TM, cols] = (y * scale_ref[:, cols]).astype(BF16)
        spn_ref[0:POOL_HIST - 1] = sp_ref[1:POOL_HIST]
        spn_ref[POOL_HIST - 1] = ext[POOL_PAD + SPLIT:POOL_PAD + TM, :]
        plast_ref[...] = ext[SPLIT:POOL_PAD + SPLIT, :]

    ext[0:POOL_PAD, :] = ext[TM:TM + POOL_PAD, :]


def _pool(xb, wu0, wu1, sp_t, wmix, scale):
    full = lambda shape: pl.BlockSpec(shape, lambda m: (0,) * len(shape))
    return pl.pallas_call(
        _pool_kernel,
        grid=(NT,),
        in_specs=[
            pl.BlockSpec((TM, D_MODEL), lambda m: (m, 0)),
            _resident((D_MODEL, TN)), _resident((D_MODEL, TN)),
            full((POOL_HIST, N_SAMPLE, POOL_WIDTH)),
            full((len(POOL_WINDOWS), POOL_GROUP_DIM, POOL_GROUP_DIM)),
            full((1, POOL_WIDTH)),
        ],
        out_specs=[pl.BlockSpec((TM, POOL_WIDTH), lambda m: (m, 0)),
                   full((POOL_HIST, N_SAMPLE, POOL_WIDTH)),
                   full((POOL_PAD, POOL_WIDTH))],
        out_shape=[jax.ShapeDtypeStruct((ROWS, POOL_WIDTH), BF16),
                   jax.ShapeDtypeStruct((POOL_HIST, N_SAMPLE, POOL_WIDTH), F32),
                   jax.ShapeDtypeStruct((POOL_PAD, POOL_WIDTH), F32)],
        scratch_shapes=[pltpu.VMEM((POOL_PAD + TM, POOL_WIDTH), F32)],
        compiler_params=_params(1),
        name="pool_mix",
    )(xb, wu0, wu1, sp_t, wmix, scale)


_NC = D_MODEL // TN


TMH = TM // 2
NTH = ROWS // TMH
SPLIT_H = SEQ - (NTH - 1) * TMH
assert NTH * TMH == ROWS and SPLIT_H + N_SAMPLE == TMH and SPLIT_H % 8 == 0


def _resident(shape):
    return pl.BlockSpec(shape, lambda m: (0,) * len(shape), pipeline_mode=pl.Buffered(1))


def _merge_kernel(xb_ref, pool_ref, attn_ref, wp_ref, wa_ref, *refs):
    gate_w, o_ref = refs[:2 * _NC], refs[2 * _NC]
    for c in range(_NC):
        cols = slice(c * TN, (c + 1) * TN)
        gate_pool = jax.nn.sigmoid(_dot(xb_ref[...], gate_w[c][...]))
        gate_attn = jax.nn.sigmoid(_dot(xb_ref[...], gate_w[_NC + c][...]))
        a = _dot(pool_ref[...], wp_ref[:, cols])
        b = _dot(attn_ref[...], wa_ref[:, cols])
        o_ref[:, cols] = (gate_pool * a + gate_attn * b).astype(BF16)


def _merge(xb, pooled, attn, wp, wa, gate_w):
    row = lambda width: pl.BlockSpec((TMH, width), lambda m: (m, 0))
    return pl.pallas_call(
        _merge_kernel,
        grid=(NTH,),
        in_specs=[row(D_MODEL), row(POOL_WIDTH), row(D_MODEL),
                  _resident((POOL_WIDTH, D_MODEL)), _resident((D_MODEL, D_MODEL))]
        + [_resident((D_MODEL, TN))] * (2 * _NC),
        out_specs=row(D_MODEL),
        out_shape=jax.ShapeDtypeStruct((ROWS, D_MODEL), BF16),
        compiler_params=_params(1),
        name="branch_merge",
    )(xb, pooled, attn, wp, wa, *gate_w)


def _out_ln_kernel(mrg_ref, wo_ref, x_ref, xs_ref, lng_ref, lnb_ref, x1_ref, x1b_ref):
    m = pl.program_id(0)
    halves = ((0, 256), (256, TMH))

    def body(residual):
        accs = [_dot(mrg_ref[lo:hi, :], wo_ref[...]) for lo, hi in halves]
        for (lo, hi), acc in zip(halves, accs):
            x1 = _layer_norm(acc + ALPHA * residual(lo, hi), lng_ref[...], lnb_ref[...])
            x1_ref[lo:hi, :] = ALPHA * x1
            x1b_ref[lo:hi, :] = x1.astype(BF16)

    def last_tile_rows(lo, hi):
        if hi <= SPLIT_H:
            return x_ref[lo:hi, :]
        return jnp.concatenate([x_ref[lo:SPLIT_H, :], xs_ref[...]], axis=0)

    pl.when(m < NTH - 1)(lambda: body(lambda lo, hi: x_ref[lo:hi, :]))
    pl.when(m == NTH - 1)(lambda: body(last_tile_rows))


def _out_ln(merged, wo, x, xs, ln_g, ln_b):
    row = pl.BlockSpec((TMH, D_MODEL), lambda m: (m, 0))
    return pl.pallas_call(
        _out_ln_kernel,
        grid=(NTH,),
        in_specs=[row, _resident((D_MODEL, D_MODEL)), row, _resident((N_SAMPLE, D_MODEL)),
                  _resident((1, D_MODEL)), _resident((1, D_MODEL))],
        out_specs=[row, row],
        out_shape=[jax.ShapeDtypeStruct((ROWS, D_MODEL), F32), jax.ShapeDtypeStruct((ROWS, D_MODEL), BF16)],
        compiler_params=_params(1),
        name="out_proj_ln1",
    )(merged, wo, x, xs, ln_g, ln_b)


TF = 512
NF = D_FF // TF
CARRY = 8


def _ffn_kernel(x1b_ref, x1_hbm, sc_ref, wg_ref, wu_ref, wd_ref, cw_ref, cb_ref, lng_ref, lnb_ref,
                o_ref, os_ref, glast_ref, scn_ref,
                h0, h1, gext, carry, sem):
    m = pl.program_id(0)
    f = pl.program_id(1)
    last_m = m == NT - 1
    h_slots = (h0, h1)
    x1_copy = pltpu.make_async_copy(x1_hbm.at[pl.ds(m * TM, TM), :], o_ref, sem)

    def up_and_gate(slot, with_down):
        halves = ((0, HALF), (HALF, TM))
        gext[0:CARRY, :] = carry[f]
        for lo, hi in halves:
            gext[CARRY + lo:CARRY + hi, :] = _dot(x1b_ref[lo:hi, :], wg_ref[...])
        ups = [_dot(x1b_ref[lo:hi, :], wu_ref[...]) for lo, hi in halves]
        if with_down:
            o_ref[...] += _dot(h_slots[1 - slot][...], wd_ref[...])
        w0 = cw_ref[0:1, :]
        w1 = cw_ref[1:2, :]
        w2 = cw_ref[2:3, :]
        cb = cb_ref[...]

        def conv(lo, hi, older=None, newer=None):
            older = gext[CARRY - 2 + lo:CARRY - 2 + hi, :] if older is None else older
            newer = gext[CARRY - 1 + lo:CARRY - 1 + hi, :] if newer is None else newer
            return cb + w0 * older + w1 * newer + w2 * gext[CARRY + lo:CARRY + hi, :]

        for (lo, hi), up in zip(halves, ups):
            if hi <= SPLIT:
                gc = conv(lo, hi)
            else:
                older_s = jnp.where(last_m, sc_ref[:, 0, :], gext[CARRY - 2 + SPLIT:CARRY - 2 + TM, :])
                newer_s = jnp.where(last_m, sc_ref[:, 1, :], gext[CARRY - 1 + SPLIT:CARRY - 1 + TM, :])
                gc = jnp.concatenate([conv(lo, SPLIT), conv(SPLIT, TM, older_s, newer_s)], axis=0)
            h_slots[slot][lo:hi, :] = (jax.nn.gelu(gc, approximate=True) * up).astype(BF16)
        carry[f] = gext[TM:CARRY + TM, :]

        @pl.when(last_m)
        def _():
            glast_ref[...] = gext[SPLIT:CARRY + SPLIT, :]
            scn_ref[:, 0, :] = sc_ref[:, 1, :]
            scn_ref[:, 1, :] = gext[CARRY + SPLIT:CARRY + TM, :]

    @pl.when(f == 0)
    def _():
        @pl.when(m == 0)
        def _():
            carry[...] = jnp.zeros_like(carry)

        x1_copy.start()
        up_and_gate(0, False)
        x1_copy.wait()

    for parity in range(2):
        @pl.when((f > 0) & (f < NF) & (f % 2 == parity))
        def _():
            up_and_gate(parity, True)

    @pl.when(f == NF)
    def _():
        h_last = h_slots[(NF - 1) % 2]
        parts = ((0, 352), (352, 704), (704, TM))
        accs = [o_ref[lo:hi, :] + _dot(h_last[lo:hi, :], wd_ref[...]) for lo, hi in parts]
        for (lo, hi), acc in zip(parts, accs):
            o_ref[lo:hi, :] = _layer_norm(acc, lng_ref[...], lnb_ref[...])

        @pl.when(last_m)
        def _():
            os_ref[...] = o_ref[SPLIT:TM, :]


def _ffn(x1b, x1, sc, wg, wu, wd, conv_w, conv_b, ln_g, ln_b):
    full = lambda shape: pl.BlockSpec(shape, lambda m, f: (0, 0))
    g_tile = lambda f: jnp.minimum(f, NF - 1)
    d_tile = lambda f: jnp.maximum(f - 1, 0)
    only_last = lambda m, f: jnp.where(m == NT - 1, g_tile(f), 0)
    row = pl.BlockSpec((TM, D_MODEL), lambda m, f: (m, 0))
    return pl.pallas_call(
        _ffn_kernel,
        grid=(NT, NF + 1),
        in_specs=[
            row,
            pl.BlockSpec(memory_space=pl.ANY),
            pl.BlockSpec((N_SAMPLE, CONV_W - 1, TF), lambda m, f: (0, 0, only_last(m, f))),
            pl.BlockSpec((D_MODEL, TF), lambda m, f: (0, g_tile(f))),
            pl.BlockSpec((D_MODEL, TF), lambda m, f: (0, g_tile(f))),
            pl.BlockSpec((TF, D_MODEL), lambda m, f: (d_tile(f), 0)),
            pl.BlockSpec((CONV_W, TF), lambda m, f: (0, g_tile(f))),
            pl.BlockSpec((1, TF), lambda m, f: (0, g_tile(f))),
            full((1, D_MODEL)), full((1, D_MODEL)),
        ],
        out_specs=[
            row,
            full((N_SAMPLE, D_MODEL)),
            pl.BlockSpec((CARRY, TF), lambda m, f: (0, only_last(m, f))),
            pl.BlockSpec((N_SAMPLE, CONV_W - 1, TF), lambda m, f: (0, 0, only_last(m, f))),
        ],
        out_shape=[
            jax.ShapeDtypeStruct((SEQ, D_MODEL), F32),
            jax.ShapeDtypeStruct((N_SAMPLE, D_MODEL), F32),
            jax.ShapeDtypeStruct((CARRY, D_FF), F32),
            jax.ShapeDtypeStruct((N_SAMPLE, CONV_W - 1, D_FF), F32),
        ],
        scratch_shapes=[
            pltpu.VMEM((TM, TF), BF16), pltpu.VMEM((TM, TF), BF16),
            pltpu.VMEM((CARRY + TM, TF), F32),
            pltpu.VMEM((NF, CARRY, TF), F32),
            pltpu.SemaphoreType.DMA(()),
        ],
        compiler_params=_params(2),
        name="convffn_ln2",
    )(x1b, x1, sc, wg, wu, wd, conv_w, conv_b, ln_g, ln_b)


def _rope_tables(pos):
    half = HEAD_DIM // 2
    inv = ROPE_THETA ** (-jnp.arange(half, dtype=F32) / half)
    lane = jnp.arange(LANES)
    inv_lanes = inv[lane % half]
    sign = jnp.where((lane % HEAD_DIM) < half, -1.0, 1.0).astype(F32)
    ang = pos.astype(F32)[:, None] * inv_lanes[None, :]
    return jnp.cos(ang), jnp.sin(ang) * sign[None, :]


def kernel(x_prompt, x_sample, cache_k, cache_v, state_pool, state_conv, w_in, attn_sinks, w_pool_mix,
           pool_scale, w_attn_branch, w_pool_branch, w_out, ln1_g, ln1_b, w_up, w_gate, conv_w, conv_b,
           w_down, ln2_g, ln2_b):
    x = x_prompt.reshape(SEQ, D_MODEL)
    xs = x_sample.reshape(N_SAMPLE, D_MODEL)
    pos = jnp.concatenate([jnp.arange(SEQ), jnp.full((N_SAMPLE,), PAST_LEN)])
    cos, sin = _rope_tables(pos)

    q, kv, xb = _in_proj(x, xs, w_in[0], cos, sin)

    to_dsw = lambda c: jnp.transpose(c[0], (0, 2, 3, 1)).reshape(N_SAMPLE, KV_DIM, WINDOW)
    steps = N_SAMPLE // SEQS_PER_STEP
    kvn_t = jnp.transpose(kv[SEQ:].reshape(steps, SEQS_PER_STEP, 2 * KV_DIM), (0, 2, 1))
    sinks = attn_sinks[0] * LOG2E
    attn_s, new_k, new_v = _attention_sample(
        sinks.reshape(N_HEADS, 1), q[SEQ:].reshape(N_SAMPLE, N_HEADS, HEAD_DIM), kvn_t,
        to_dsw(cache_k), to_dsw(cache_v))
    whole = lambda w: (w, 0, w.shape[1])
    late_tiles = [(w_in[0], j, TN) for j in range(_N_U0, _N_G0 + 2 * _NC)]
    attn, (wp_b, wa_b, wo_b, wg_b, wu_b, wd_b, wpool0, wpool1, *gate_w) = _attention(
        sinks, q, kv, attn_s.reshape(N_SAMPLE, D_MODEL),
        [whole(w_pool_branch[0]), whole(w_attn_branch[0]), whole(w_out[0]),
         whole(w_gate[0]), whole(w_up[0]), whole(w_down[0])] + late_tiles)

    state_pool_t = jnp.transpose(state_pool[0], (1, 0, 2))
    pooled, new_pool_t, pool_last = _pool(xb, wpool0, wpool1, state_pool_t, w_pool_mix[0], pool_scale)

    merged = _merge(xb, pooled, attn, wp_b, wa_b, gate_w)
    x1, x1b = _out_ln(merged, wo_b, x, xs, ln1_g, ln1_b)

    y, y_s, g_last, new_conv_s = _ffn(x1b, x1, state_conv[0], wg_b, wu_b, wd_b,
                                      conv_w[0], conv_b, ln2_g, ln2_b)

    from_dsw = lambda c: jnp.transpose(
        c.reshape(N_SAMPLE, N_KV_HEADS, HEAD_DIM, WINDOW), (0, 3, 1, 2))[None]
    return (
        y.reshape(1, SEQ, D_MODEL),
        y_s.reshape(N_SAMPLE, 1, D_MODEL),
        kv[SEQ - WINDOW:SEQ, :KV_DIM].reshape(1, 1, WINDOW, N_KV_HEADS, HEAD_DIM),
        kv[SEQ - WINDOW:SEQ, KV_DIM:].reshape(1, 1, WINDOW, N_KV_HEADS, HEAD_DIM),
        pool_last[POOL_PAD - POOL_HIST:].reshape(1, 1, POOL_HIST, POOL_WIDTH),
        g_last[CARRY - (CONV_W - 1):].reshape(1, 1, CONV_W - 1, D_FF),
        from_dsw(new_k),
        from_dsw(new_v),
        jnp.transpose(new_pool_t, (1, 0, 2))[None],
        new_conv_s[None],
    )
```

```python
import jax
import jax.numpy as jnp
from jax import lax
from jax.experimental import pallas as pl
from jax.experimental.pallas import tpu as pltpu

F32 = jnp.float32
BF16 = jnp.bfloat16

D_MODEL = 2048
SEQ = 8192
N_SAMPLE = 128
PAST_LEN = 8192
HEAD_DIM = 64
N_HEADS = 32
N_KV_HEADS = 4
GROUP = N_HEADS // N_KV_HEADS
KV_DIM = N_KV_HEADS * HEAD_DIM
WINDOW = 128
ROPE_THETA = 10000.0
POOL_WINDOWS = (2, 4, 8, 16)
POOL_WIDTH = 1024
POOL_GROUP_DIM = 256
POOL_HIST = 15
D_FF = 5632
CONV_W = 3
LN_EPS = 1e-5
NEG_INF = -1e30
ALPHA = 2.0 ** 0.25
LOG2E = 1.4426950408889634

LANES = 128
VMEM_LIMIT = 60000 * 1024

ROWS = SEQ + N_SAMPLE
TM = 1040
NT = ROWS // TM
SPLIT = SEQ - (NT - 1) * TM
TN = 512
HALF = 528
HALVES = ((0, HALF), (HALF, TM))
THIRDS = ((0, 352), (352, 704), (704, TM))
assert NT * TM == ROWS and SPLIT + N_SAMPLE == TM and SPLIT % 16 == 0
assert all(lo % 16 == 0 for lo, _ in HALVES + THIRDS)

TMH = TM // 2
NTH = ROWS // TMH
SPLIT_H = SEQ - (NTH - 1) * TMH
HALVES_H = ((0, 256), (256, TMH))
assert NTH * TMH == ROWS and SPLIT_H + N_SAMPLE == TMH and SPLIT_H % 8 == 0


def _params(ndim):
    return pltpu.CompilerParams(dimension_semantics=("arbitrary",) * ndim,
                                vmem_limit_bytes=VMEM_LIMIT)


def _resident(shape):
    return pl.BlockSpec(shape, lambda m: (0,) * len(shape), pipeline_mode=pl.Buffered(1))


def _dot(a, b):
    return jnp.dot(a, b, preferred_element_type=F32)


def _layer_norm(z, g, b):
    mu = jnp.mean(z, axis=-1, keepdims=True)
    d = z - mu
    var = jnp.mean(d * d, axis=-1, keepdims=True)
    return d * lax.rsqrt(var + LN_EPS) * g + b


def _rope(x, cos, sin_signed):
    lane = lax.broadcasted_iota(jnp.int32, (1, LANES), 1)
    low_half = (lane % HEAD_DIM) < (HEAD_DIM // 2)
    outs = []
    for c in range(x.shape[1] // LANES):
        xc = x[:, c * LANES:(c + 1) * LANES]
        up = pltpu.roll(xc, LANES - HEAD_DIM // 2, 1)
        down = pltpu.roll(xc, HEAD_DIM // 2, 1)
        outs.append(xc * cos + jnp.where(low_half, up, down) * sin_signed)
    return jnp.concatenate(outs, axis=1)


_NQ = D_MODEL // TN
_N_KV = _NQ
_N_U0 = _N_KV + 1
_N_G0 = _N_U0 + POOL_WIDTH // TN


def _in_proj_kernel(x_ref, xs_ref, w_ref, cos_ref, sin_ref, q_ref, kv_ref, xb, wres):
    m = pl.program_id(0)
    n = pl.program_id(1)

    @pl.when(n == 0)
    def _():
        @pl.when(m < NT - 1)
        def _():
            xb[...] = x_ref[...].astype(BF16)

        @pl.when(m == NT - 1)
        def _():
            xb[0:SPLIT, :] = x_ref[0:SPLIT, :].astype(BF16)
            xb[SPLIT:TM, :] = xs_ref[...].astype(BF16)

    @pl.when(m == 0)
    def _():
        wres[n] = w_ref[...].astype(BF16)

    def project(epilogue, parts):
        accs = [_dot(xb[lo:hi, :], wres[n]) for lo, hi in parts]
        for (lo, hi), acc in zip(parts, accs):
            epilogue(acc, lo, hi)

    def q_epilogue(acc, lo, hi):
        roped = _rope(acc, cos_ref[lo:hi, :], sin_ref[lo:hi, :])
        q_ref[lo:hi, :] = (roped * (HEAD_DIM ** -0.5 * LOG2E)).astype(BF16)

    def kv_epilogue(acc, lo, hi):
        kv_ref[lo:hi, :KV_DIM] = _rope(acc[:, :KV_DIM], cos_ref[lo:hi, :], sin_ref[lo:hi, :])
        kv_ref[lo:hi, KV_DIM:] = acc[:, KV_DIM:]

    pl.when(n < _NQ)(lambda: project(q_epilogue, THIRDS))
    pl.when(n == _N_KV)(lambda: project(kv_epilogue, HALVES))


def _in_proj(x, xs, w_in, cos, sin):
    nn = _N_U0

    return pl.pallas_call(
        _in_proj_kernel,
        grid=(NT, nn),
        in_specs=[
            pl.BlockSpec((TM, D_MODEL), lambda m, n: (m, 0)),
            pl.BlockSpec((N_SAMPLE, D_MODEL), lambda m, n: (0, 0)),
            pl.BlockSpec((D_MODEL, TN), lambda m, n: (0, jnp.where(m == 0, n, nn - 1))),
            pl.BlockSpec((TM, LANES), lambda m, n: (m, 0)),
            pl.BlockSpec((TM, LANES), lambda m, n: (m, 0)),
        ],
        out_specs=[
            pl.BlockSpec((TM, TN), lambda m, n: (m, jnp.minimum(n, _NQ - 1))),
            pl.BlockSpec((TM, TN), lambda m, n: (m, 0)),
            pl.BlockSpec((TM, D_MODEL), lambda m, n: (m, 0)),
        ],
        out_shape=[
            jax.ShapeDtypeStruct((ROWS, D_MODEL), BF16),
            jax.ShapeDtypeStruct((ROWS, 2 * KV_DIM), F32),
            jax.ShapeDtypeStruct((ROWS, D_MODEL), BF16),
        ],
        scratch_shapes=[pltpu.VMEM((nn, D_MODEL, TN), BF16)],
        compiler_params=_params(2),
        name="in_proj",
    )(x, xs, w_in, cos, sin)


PAIRS = GROUP // 2
Q_BLOCKS = 2


def _attn_kernel(fill_ref, q_ref, kp_ref, kc_ref, vp_ref, vc_ref, os_ref, *refs):
    n_w = (len(refs) - 1) // 2
    w_refs, o_ref, wb_refs = refs[:n_w], refs[n_w], refs[n_w + 1:]
    i = pl.program_id(0)
    last = pl.num_programs(0) - 1

    @pl.when(i < last)
    def _():
        for w_ref, wb_ref in zip(w_refs, wb_refs):
            wb_ref[...] = w_ref[...].astype(BF16)
        keys = (kp_ref[...],) + tuple(kc_ref[b * WINDOW:(b + 1) * WINDOW, :] for b in range(Q_BLOCKS))
        vals = (vp_ref[...],) + tuple(vc_ref[b * WINDOW:(b + 1) * WINDOW, :] for b in range(Q_BLOCKS))
        for b in range(Q_BLOCKS):
            _attn_block(Q_BLOCKS * i + b, fill_ref, q_ref, o_ref, b * WINDOW,
                        keys[b], keys[b + 1], vals[b], vals[b + 1])

    @pl.when(i == last)
    def _():
        o_ref[0:N_SAMPLE, :] = os_ref[...]


def _attn_block(i, fill_ref, q_ref, o_ref, row0, k_prev, k_cur, v_prev, v_cur):
    q_rows = slice(row0, row0 + WINDOW)
    kk = jnp.concatenate([k_prev, k_cur], axis=0)
    vv = jnp.concatenate([v_prev, v_cur], axis=0)
    key = lax.broadcasted_iota(jnp.int32, (2 * WINDOW, 1), 0)
    vv = jnp.where(key == 0, 0.0, vv)
    low = lax.broadcasted_iota(jnp.int32, (1, LANES), 1) < HEAD_DIM
    rows = PAIRS * WINDOW
    r = lax.broadcasted_iota(jnp.int32, (rows, 2 * WINDOW), 0) % WINDOW
    c = lax.broadcasted_iota(jnp.int32, (rows, 2 * WINDOW), 1)
    visible = (c > r) & (c <= r + WINDOW) & ((i > 0) | (c >= WINDOW))

    for pair in range(N_KV_HEADS // 2):
        k_pair = kk[:, pair * LANES:(pair + 1) * LANES]
        v_pair = vv[:, pair * LANES:(pair + 1) * LANES]
        k_swap = pltpu.roll(k_pair, HEAD_DIM, 1)
        v_swap = pltpu.roll(v_pair, HEAD_DIM, 1)
        for second in range(2):
            g = 2 * pair + second
            k_in_low, k_in_high = (k_swap, k_pair) if second else (k_pair, k_swap)
            v_in_low, v_in_high = (v_swap, v_pair) if second else (v_pair, v_swap)
            x = jnp.concatenate(
                [q_ref[q_rows, (g * PAIRS + t) * LANES:(g * PAIRS + t + 1) * LANES] for t in range(PAIRS)],
                axis=0)

            def half(k_pad, v_pad, fill):
                s = lax.dot_general(x, k_pad.astype(BF16), (((1,), (1,)), ((), ())),
                                    preferred_element_type=F32)
                s = jnp.where(visible, s, fill)
                e = jnp.exp2(s - jnp.max(s, axis=-1, keepdims=True)).astype(BF16)
                return _dot(e, v_pad.astype(BF16))

            a = half(jnp.where(low, k_in_low, 0.0), jnp.where(low, v_in_low, 1.0), fill_ref[2 * g])
            b = half(jnp.where(low, 0.0, k_in_high), jnp.where(low, 1.0, v_in_high), fill_ref[2 * g + 1])
            den = pltpu.roll(jnp.where(low, b, a), HEAD_DIM, 1)
            out = (jnp.where(low, a, b) / den).astype(BF16)
            for t in range(PAIRS):
                o_ref[q_rows, (g * PAIRS + t) * LANES:(g * PAIRS + t + 1) * LANES] = \
                    out[t * WINDOW:(t + 1) * WINDOW]


def _cast_chunk_rows(rows, steps):
    assert rows % 16 == 0
    units = rows // 16
    n = max(d for d in range(1, steps + 1) if units % d == 0)
    return rows // n


def _attention(sinks, q, kv, attn_s, weights):
    nb = SEQ // (Q_BLOCKS * WINDOW)
    cur = lambda i: jnp.minimum(i, nb - 1)
    prev = lambda i: jnp.clip(Q_BLOCKS * i - 1, 0, Q_BLOCKS * nb - 1)
    per_chunk = jnp.transpose(sinks.reshape(N_KV_HEADS, PAIRS, 2), (0, 2, 1))
    per_row = jnp.broadcast_to(per_chunk.reshape(2 * N_KV_HEADS, PAIRS, 1),
                               (2 * N_KV_HEADS, PAIRS, WINDOW)).reshape(2 * N_KV_HEADS, PAIRS * WINDOW)
    first_col = jnp.arange(2 * WINDOW)[None, None, :] == 0
    fill = jnp.where(first_col, per_row[:, :, None], NEG_INF).astype(F32)
    w_in_specs, w_out_specs, w_out_shapes = [], [], []
    for w, col, width in weights:
        chunk = _cast_chunk_rows(w.shape[0], nb)
        n = w.shape[0] // chunk
        w_in_specs.append(pl.BlockSpec((chunk, width), lambda i, n=n, col=col: (jnp.minimum(i, n - 1), col)))
        w_out_specs.append(pl.BlockSpec((chunk, width), lambda i, n=n: (jnp.minimum(i, n - 1), 0)))
        w_out_shapes.append(jax.ShapeDtypeStruct((w.shape[0], width), BF16))
    outs = pl.pallas_call(
        _attn_kernel,
        grid=(nb + 1,),
        in_specs=[
            _resident(fill.shape),
            pl.BlockSpec((Q_BLOCKS * WINDOW, D_MODEL), lambda i: (cur(i), 0)),
            pl.BlockSpec((WINDOW, KV_DIM), lambda i: (prev(i), 0)),
            pl.BlockSpec((Q_BLOCKS * WINDOW, KV_DIM), lambda i: (cur(i), 0)),
            pl.BlockSpec((WINDOW, KV_DIM), lambda i: (prev(i), 1)),
            pl.BlockSpec((Q_BLOCKS * WINDOW, KV_DIM), lambda i: (cur(i), 1)),
            pl.BlockSpec((N_SAMPLE, D_MODEL), lambda i: (0, 0)),
        ] + w_in_specs,
        out_specs=[pl.BlockSpec((Q_BLOCKS * WINDOW, D_MODEL), lambda i: (i, 0))] + w_out_specs,
        out_shape=[jax.ShapeDtypeStruct((ROWS, D_MODEL), BF16)] + w_out_shapes,
        compiler_params=_params(1),
        name="attn_prompt",
    )(fill, q, kv, kv, kv, kv, attn_s, *[w for w, _, _ in weights])
    return outs[0], outs[1:]


SEQS_PER_STEP = 8


def _attn_sample_kernel(sink_ref, q_ref, kvn_ref, ck_ref, cv_ref, o_ref, nk_ref, nv_ref):
    head = lax.broadcasted_iota(jnp.int32, (N_HEADS, KV_DIM), 0) // GROUP
    col_group = lax.broadcasted_iota(jnp.int32, (N_HEADS, KV_DIM), 1) // HEAD_DIM
    own = head == col_group
    newest = lax.broadcasted_iota(jnp.int32, (1, WINDOW), 1) == WINDOW - 1
    scores = []
    for b in range(SEQS_PER_STEP):
        nk_ref[b] = jnp.where(newest, kvn_ref[0, 0:KV_DIM, b:b + 1], pltpu.roll(ck_ref[b], WINDOW - 1, 1))
        nv_ref[b] = jnp.where(newest, kvn_ref[0, KV_DIM:2 * KV_DIM, b:b + 1],
                              pltpu.roll(cv_ref[b], WINDOW - 1, 1))
        qb = q_ref[b]
        qe = jnp.where(own, jnp.concatenate([qb] * N_KV_HEADS, axis=1), jnp.zeros((), BF16))
        scores.append(_dot(qe, nk_ref[b].astype(BF16)))
    s = jnp.concatenate(scores, axis=0)
    sink = jnp.concatenate([sink_ref[...]] * SEQS_PER_STEP, axis=0)
    mx = jnp.maximum(jnp.max(s, axis=-1, keepdims=True), sink)
    e = jnp.exp2(s - mx)
    inv = 1.0 / (jnp.sum(e, axis=-1, keepdims=True) + jnp.exp2(sink - mx))
    e = e.astype(BF16)
    for b in range(SEQS_PER_STEP):
        rows = slice(b * N_HEADS, (b + 1) * N_HEADS)
        of = lax.dot_general(e[rows], nv_ref[b].astype(BF16), (((1,), (1,)), ((), ())),
                             preferred_element_type=F32)
        of = jnp.where(own, of, 0.0)
        o = of[:, 0:HEAD_DIM]
        for g in range(1, N_KV_HEADS):
            o = o + of[:, g * HEAD_DIM:(g + 1) * HEAD_DIM]
        o_ref[b] = (o * inv[rows]).astype(BF16)


def _attention_sample(sink_col, q3, kvn_t, ck, cv):
    nb = N_SAMPLE // SEQS_PER_STEP
    cache_spec = pl.BlockSpec((SEQS_PER_STEP, KV_DIM, WINDOW), lambda i: (i, 0, 0))
    return pl.pallas_call(
        _attn_sample_kernel,
        grid=(nb,),
        in_specs=[
            pl.BlockSpec((N_HEADS, 1), lambda i: (0, 0)),
            pl.BlockSpec((SEQS_PER_STEP, N_HEADS, HEAD_DIM), lambda i: (i, 0, 0)),
            pl.BlockSpec((1, 2 * KV_DIM, SEQS_PER_STEP), lambda i: (i, 0, 0)),
            cache_spec, cache_spec,
        ],
        out_specs=[
            pl.BlockSpec((SEQS_PER_STEP, N_HEADS, HEAD_DIM), lambda i: (i, 0, 0)),
            cache_spec, cache_spec,
        ],
        out_shape=[
            jax.ShapeDtypeStruct((N_SAMPLE, N_HEADS, HEAD_DIM), BF16),
            jax.ShapeDtypeStruct((N_SAMPLE, KV_DIM, WINDOW), F32),
            jax.ShapeDtypeStruct((N_SAMPLE, KV_DIM, WINDOW), F32),
        ],
        compiler_params=_params(1),
        name="attn_sample",
    )(sink_col, q3, kvn_t, ck, cv)


POOL_PAD = 16


def _pool_kernel(xb_ref, wu0_ref, wu1_ref, sp_ref, wmix_ref, scale_ref, p_ref, spn_ref, plast_ref, ext):
    m = pl.program_id(0)

    @pl.when(m == 0)
    def _():
        ext[0:POOL_PAD, :] = jnp.zeros((POOL_PAD, POOL_WIDTH), F32)

    pos = m * TM + lax.broadcasted_iota(jnp.int32, (TM, 1), 0)
    for t, wu_ref in ((1, wu1_ref), (0, wu0_ref)):
        ext[POOL_PAD:POOL_PAD + TM, t * TN:(t + 1) * TN] = _dot(xb_ref[...], wu_ref[...])
    for g, w in reversed(list(enumerate(POOL_WINDOWS))):
        cols = slice(g * POOL_GROUP_DIM, (g + 1) * POOL_GROUP_DIM)
        u_g = ext[POOL_PAD:POOL_PAD + TM, cols]
        run = ext[:, cols]
        shift = 1
        while shift < w:
            run = run + pltpu.roll(run, shift, 0)
            shift *= 2
        tot = run[POOL_PAD:POOL_PAD + TM, :]
        inv_cnt = 1.0 / jnp.minimum(w, pos + 1).astype(F32)
        d = tot * inv_cnt - u_g
        y = _dot(d.astype(BF16), wmix_ref[g].astype(BF16))
        p_ref[:, cols] = (y * scale_ref[:, cols]).astype(BF16)

    @pl.when(m == NT - 1)
    def _():
        for g, w in enumerate(POOL_WINDOWS):
            cols = slice(g * POOL_GROUP_DIM, (g + 1) * POOL_GROUP_DIM)
            us = ext[POOL_PAD + SPLIT:POOL_PAD + TM, cols]
            tot = us
            for j in range(1, w):
                tot = tot + sp_ref[POOL_HIST - j, :, cols]
            d = tot * (1.0 / w) - us
            y = _dot(d.astype(BF16), wmix_ref[g].astype(BF16))
            p_ref[SPLIT:TM, cols] = (y * scale_ref[:, cols]).astype(BF16)
        spn_ref[0:POOL_HIST - 1] = sp_ref[1:POOL_HIST]
        spn_ref[POOL_HIST - 1] = ext[POOL_PAD + SPLIT:POOL_PAD + TM, :]
        plast_ref[...] = ext[SPLIT:POOL_PAD + SPLIT, :]

    ext[0:POOL_PAD, :] = ext[TM:TM + POOL_PAD, :]


def _pool(xb, wu0, wu1, sp_t, wmix, scale):
    full = lambda shape: pl.BlockSpec(shape, lambda m: (0,) * len(shape))
    return pl.pallas_call(
        _pool_kernel,
        grid=(NT,),
        in_specs=[
            pl.BlockSpec((TM, D_MODEL), lambda m: (m, 0)),
            _resident((D_MODEL, TN)), _resident((D_MODEL, TN)),
            full((POOL_HIST, N_SAMPLE, POOL_WIDTH)),
            full((len(POOL_WINDOWS), POOL_GROUP_DIM, POOL_GROUP_DIM)),
            full((1, POOL_WIDTH)),
        ],
        out_specs=[pl.BlockSpec((TM, POOL_WIDTH), lambda m: (m, 0)),
                   full((POOL_HIST, N_SAMPLE, POOL_WIDTH)),
                   full((POOL_PAD, POOL_WIDTH))],
        out_shape=[jax.ShapeDtypeStruct((ROWS, POOL_WIDTH), BF16),
                   jax.ShapeDtypeStruct((POOL_HIST, N_SAMPLE, POOL_WIDTH), F32),
                   jax.ShapeDtypeStruct((POOL_PAD, POOL_WIDTH), F32)],
        scratch_shapes=[pltpu.VMEM((POOL_PAD + TM, POOL_WIDTH), F32)],
        compiler_params=_params(1),
        name="pool_mix",
    )(xb, wu0, wu1, sp_t, wmix, scale)


_NC = D_MODEL // TN


def _merge_kernel(xb_ref, pool_ref, attn_ref, wp_ref, wa_ref, *refs):
    gate_w, o_ref = refs[:2 * _NC], refs[2 * _NC]
    for c in range(_NC):
        cols = slice(c * TN, (c + 1) * TN)
        gate_pool = jax.nn.sigmoid(_dot(xb_ref[...], gate_w[c][...]))
        gate_attn = jax.nn.sigmoid(_dot(xb_ref[...], gate_w[_NC + c][...]))
        a = _dot(pool_ref[...], wp_ref[:, cols])
        b = _dot(attn_ref[...], wa_ref[:, cols])
        o_ref[:, cols] = (gate_pool * a + gate_attn * b).astype(BF16)


def _merge(xb, pooled, attn, wp, wa, gate_w):
    row = lambda width: pl.BlockSpec((TMH, width), lambda m: (m, 0))
    return pl.pallas_call(
        _merge_kernel,
        grid=(NTH,),
        in_specs=[row(D_MODEL), row(POOL_WIDTH), row(D_MODEL),
                  _resident((POOL_WIDTH, D_MODEL)), _resident((D_MODEL, D_MODEL))]
        + [_resident((D_MODEL, TN))] * (2 * _NC),
        out_specs=row(D_MODEL),
        out_shape=jax.ShapeDtypeStruct((ROWS, D_MODEL), BF16),
        compiler_params=_params(1),
        name="branch_merge",
    )(xb, pooled, attn, wp, wa, *gate_w)


def _out_ln_kernel(mrg_ref, wo_ref, x_ref, xs_ref, lng_ref, lnb_ref, x1_ref, x1b_ref):
    m = pl.program_id(0)

    def body(residual):
        accs = [_dot(mrg_ref[lo:hi, :], wo_ref[...]) for lo, hi in HALVES_H]
        for (lo, hi), acc in zip(HALVES_H, accs):
            x1 = _layer_norm(acc + ALPHA * residual(lo, hi), lng_ref[...], lnb_ref[...])
            x1_ref[lo:hi, :] = ALPHA * x1
            x1b_ref[lo:hi, :] = x1.astype(BF16)

    def last_tile_rows(lo, hi):
        if hi <= SPLIT_H:
            return x_ref[lo:hi, :]
        return jnp.concatenate([x_ref[lo:SPLIT_H, :], xs_ref[...]], axis=0)

    pl.when(m < NTH - 1)(lambda: body(lambda lo, hi: x_ref[lo:hi, :]))
    pl.when(m == NTH - 1)(lambda: body(last_tile_rows))


def _out_ln(merged, wo, x, xs, ln_g, ln_b):
    row = pl.BlockSpec((TMH, D_MODEL), lambda m: (m, 0))
    return pl.pallas_call(
        _out_ln_kernel,
        grid=(NTH,),
        in_specs=[row, _resident((D_MODEL, D_MODEL)), row, _resident((N_SAMPLE, D_MODEL)),
                  _resident((1, D_MODEL)), _resident((1, D_MODEL))],
        out_specs=[row, row],
        out_shape=[jax.ShapeDtypeStruct((ROWS, D_MODEL), F32), jax.ShapeDtypeStruct((ROWS, D_MODEL), BF16)],
        compiler_params=_params(1),
        name="out_proj_ln1",
    )(merged, wo, x, xs, ln_g, ln_b)


TF = 512
NF = D_FF // TF
CARRY = 8


def _ffn_kernel(x1b_ref, x1_hbm, sc_ref, wg_ref, wu_ref, wd_ref, cw_ref, cb_ref, lng_ref, lnb_ref,
                o_ref, os_ref, glast_ref, scn_ref,
                h0, h1, gext, carry, sem):
    m = pl.program_id(0)
    f = pl.program_id(1)
    last_m = m == NT - 1
    h_slots = (h0, h1)
    x1_copy = pltpu.make_async_copy(x1_hbm.at[pl.ds(m * TM, TM), :], o_ref, sem)

    def up_and_gate(slot, with_down):
        gext[0:CARRY, :] = carry[f]
        for lo, hi in HALVES:
            gext[CARRY + lo:CARRY + hi, :] = _dot(x1b_ref[lo:hi, :], wg_ref[...])
        ups = [_dot(x1b_ref[lo:hi, :], wu_ref[...]) for lo, hi in HALVES]
        if with_down:
            o_ref[...] += _dot(h_slots[1 - slot][...], wd_ref[...])
        w0 = cw_ref[0:1, :]
        w1 = cw_ref[1:2, :]
        w2 = cw_ref[2:3, :]
        cb = cb_ref[...]

        def conv(lo, hi, older=None, newer=None):
            older = gext[CARRY - 2 + lo:CARRY - 2 + hi, :] if older is None else older
            newer = gext[CARRY - 1 + lo:CARRY - 1 + hi, :] if newer is None else newer
            return cb + w0 * older + w1 * newer + w2 * gext[CARRY + lo:CARRY + hi, :]

        for (lo, hi), up in zip(HALVES, ups):
            if hi <= SPLIT:
                gc = conv(lo, hi)
            else:
                older_s = jnp.where(last_m, sc_ref[:, 0, :], gext[CARRY - 2 + SPLIT:CARRY - 2 + TM, :])
                newer_s = jnp.where(last_m, sc_ref[:, 1, :], gext[CARRY - 1 + SPLIT:CARRY - 1 + TM, :])
                gc = jnp.concatenate([conv(lo, SPLIT), conv(SPLIT, TM, older_s, newer_s)], axis=0)
            h_slots[slot][lo:hi, :] = (jax.nn.gelu(gc, approximate=True) * up).astype(BF16)
        carry[f] = gext[TM:CARRY + TM, :]

        @pl.when(last_m)
        def _():
            glast_ref[...] = gext[SPLIT:CARRY + SPLIT, :]
            scn_ref[:, 0, :] = sc_ref[:, 1, :]
            scn_ref[:, 1, :] = gext[CARRY + SPLIT:CARRY + TM, :]

    @pl.when(f == 0)
    def _():
        @pl.when(m == 0)
        def _():
            carry[...] = jnp.zeros_like(carry)

        x1_copy.start()
        up_and_gate(0, False)
        x1_copy.wait()

    for parity in range(2):
        @pl.when((f > 0) & (f < NF) & (f % 2 == parity))
        def _():
            up_and_gate(parity, True)

    @pl.when(f == NF)
    def _():
        h_last = h_slots[(NF - 1) % 2]
        accs = [o_ref[lo:hi, :] + _dot(h_last[lo:hi, :], wd_ref[...]) for lo, hi in THIRDS]
        for (lo, hi), acc in zip(THIRDS, accs):
            o_ref[lo:hi, :] = _layer_norm(acc, lng_ref[...], lnb_ref[...])

        @pl.when(last_m)
        def _():
            os_ref[...] = o_ref[SPLIT:TM, :]


def _ffn(x1b, x1, sc, wg, wu, wd, conv_w, conv_b, ln_g, ln_b):
    full = lambda shape: pl.BlockSpec(shape, lambda m, f: (0, 0))
    g_tile = lambda f: jnp.minimum(f, NF - 1)
    d_tile = lambda f: jnp.maximum(f - 1, 0)
    only_last = lambda m, f: jnp.where(m == NT - 1, g_tile(f), 0)
    row = pl.BlockSpec((TM, D_MODEL), lambda m, f: (m, 0))
    return pl.pallas_call(
        _ffn_kernel,
        grid=(NT, NF + 1),
        in_specs=[
            row,
            pl.BlockSpec(memory_space=pl.ANY),
            pl.BlockSpec((N_SAMPLE, CONV_W - 1, TF), lambda m, f: (0, 0, only_last(m, f))),
            pl.BlockSpec((D_MODEL, TF), lambda m, f: (0, g_tile(f))),
            pl.BlockSpec((D_MODEL, TF), lambda m, f: (0, g_tile(f))),
            pl.BlockSpec((TF, D_MODEL), lambda m, f: (d_tile(f), 0)),
            pl.BlockSpec((CONV_W, TF), lambda m, f: (0, g_tile(f))),
            pl.BlockSpec((1, TF), lambda m, f: (0, g_tile(f))),
            full((1, D_MODEL)), full((1, D_MODEL)),
        ],
        out_specs=[
            row,
            full((N_SAMPLE, D_MODEL)),
            pl.BlockSpec((CARRY, TF), lambda m, f: (0, only_last(m, f))),
            pl.BlockSpec((N_SAMPLE, CONV_W - 1, TF), lambda m, f: (0, 0, only_last(m, f))),
        ],
        out_shape=[
            jax.ShapeDtypeStruct((SEQ, D_MODEL), F32),
            jax.ShapeDtypeStruct((N_SAMPLE, D_MODEL), F32),
            jax.ShapeDtypeStruct((CARRY, D_FF), F32),
            jax.ShapeDtypeStruct((N_SAMPLE, CONV_W - 1, D_FF), F32),
        ],
        scratch_shapes=[
            pltpu.VMEM((TM, TF), BF16), pltpu.VMEM((TM, TF), BF16),
            pltpu.VMEM((CARRY + TM, TF), F32),
            pltpu.VMEM((NF, CARRY, TF), F32),
            pltpu.SemaphoreType.DMA(()),
        ],
        compiler_params=_params(2),
        name="convffn_ln2",
    )(x1b, x1, sc, wg, wu, wd, conv_w, conv_b, ln_g, ln_b)


def _rope_tables(pos):
    half = HEAD_DIM // 2
    inv = ROPE_THETA ** (-jnp.arange(half, dtype=F32) / half)
    lane = jnp.arange(LANES)
    inv_lanes = inv[lane % half]
    sign = jnp.where((lane % HEAD_DIM) < half, -1.0, 1.0).astype(F32)
    ang = pos.astype(F32)[:, None] * inv_lanes[None, :]
    return jnp.cos(ang), jnp.sin(ang) * sign[None, :]


def kernel(x_prompt, x_sample, cache_k, cache_v, state_pool, state_conv, w_in, attn_sinks, w_pool_mix,
           pool_scale, w_attn_branch, w_pool_branch, w_out, ln1_g, ln1_b, w_up, w_gate, conv_w, conv_b,
           w_down, ln2_g, ln2_b):
    x = x_prompt.reshape(SEQ, D_MODEL)
    xs = x_sample.reshape(N_SAMPLE, D_MODEL)
    pos = jnp.concatenate([jnp.arange(SEQ), jnp.full((N_SAMPLE,), PAST_LEN)])
    cos, sin = _rope_tables(pos)

    q, kv, xb = _in_proj(x, xs, w_in[0], cos, sin)

    to_dsw = lambda c: jnp.transpose(c[0], (0, 2, 3, 1)).reshape(N_SAMPLE, KV_DIM, WINDOW)
    steps = N_SAMPLE // SEQS_PER_STEP
    kvn_t = jnp.transpose(kv[SEQ:].reshape(steps, SEQS_PER_STEP, 2 * KV_DIM), (0, 2, 1))
    sinks = attn_sinks[0] * LOG2E
    attn_s, new_k, new_v = _attention_sample(
        sinks.reshape(N_HEADS, 1), q[SEQ:].reshape(N_SAMPLE, N_HEADS, HEAD_DIM), kvn_t,
        to_dsw(cache_k), to_dsw(cache_v))
    whole = lambda w: (w, 0, w.shape[1])
    late_tiles = [(w_in[0], j, TN) for j in range(_N_U0, _N_G0 + 2 * _NC)]
    attn, (wp_b, wa_b, wo_b, wg_b, wu_b, wd_b, wpool0, wpool1, *gate_w) = _attention(
        sinks, q, kv, attn_s.reshape(N_SAMPLE, D_MODEL),
        [whole(w_pool_branch[0]), whole(w_attn_branch[0]), whole(w_out[0]),
         whole(w_gate[0]), whole(w_up[0]), whole(w_down[0])] + late_tiles)

    state_pool_t = jnp.transpose(state_pool[0], (1, 0, 2))
    pooled, new_pool_t, pool_last = _pool(xb, wpool0, wpool1, state_pool_t, w_pool_mix[0], pool_scale)

    merged = _merge(xb, pooled, attn, wp_b, wa_b, gate_w)
    x1, x1b = _out_ln(merged, wo_b, x, xs, ln1_g, ln1_b)

    y, y_s, g_last, new_conv_s = _ffn(x1b, x1, state_conv[0], wg_b, wu_b, wd_b,
                                      conv_w[0], conv_b, ln2_g, ln2_b)

    from_dsw = lambda c: jnp.transpose(
        c.reshape(N_SAMPLE, N_KV_HEADS, HEAD_DIM, WINDOW), (0, 3, 1, 2))[None]
    return (
        y.reshape(1, SEQ, D_MODEL),
        y_s.reshape(N_SAMPLE, 1, D_MODEL),
        kv[SEQ - WINDOW:SEQ, :KV_DIM].reshape(1, 1, WINDOW, N_KV_HEADS, HEAD_DIM),
        kv[SEQ - WINDOW:SEQ, KV_DIM:].reshape(1, 1, WINDOW, N_KV_HEADS, HEAD_DIM),
        pool_last[POOL_PAD - POOL_HIST:].reshape(1, 1, POOL_HIST, POOL_WIDTH),
        g_last[CARRY - (CONV_W - 1):].reshape(1, 1, CONV_W - 1, D_FF),
        from_dsw(new_k),
        from_dsw(new_v),
        jnp.transpose(new_pool_t, (1, 0, 2))[None],
        new_conv_s[None],
    )
```

```python
import jax
import jax.numpy as jnp
from jax import lax
from jax.experimental import pallas as pl
from jax.experimental.pallas import tpu as pltpu

F32 = jnp.float32
BF16 = jnp.bfloat16

D_MODEL = 2048
SEQ = 8192
N_SAMPLE = 128
PAST_LEN = 8192
HEAD_DIM = 64
N_HEADS = 32
N_KV_HEADS = 4
GROUP = N_HEADS // N_KV_HEADS
KV_DIM = N_KV_HEADS * HEAD_DIM
WINDOW = 128
ROPE_THETA = 10000.0
POOL_WINDOWS = (2, 4, 8, 16)
POOL_WIDTH = 1024
POOL_GROUP_DIM = 256
POOL_HIST = 15
D_FF = 5632
CONV_W = 3
LN_EPS = 1e-5
NEG_INF = -1e30
ALPHA = 2.0 ** 0.25
LOG2E = 1.4426950408889634

LANES = 128
VMEM_LIMIT = 60000 * 1024

ROWS = SEQ + N_SAMPLE
TM = 1040
NT = ROWS // TM
SPLIT = SEQ - (NT - 1) * TM
TN = 512
HALF = 528
HALVES = ((0, HALF), (HALF, TM))
THIRDS = ((0, 352), (352, 704), (704, TM))
assert NT * TM == ROWS and SPLIT + N_SAMPLE == TM and SPLIT % 16 == 0
assert all(lo % 16 == 0 for lo, _ in HALVES + THIRDS)

TMH = TM // 2
NTH = ROWS // TMH
SPLIT_H = SEQ - (NTH - 1) * TMH
HALVES_H = ((0, 256), (256, TMH))
assert NTH * TMH == ROWS and SPLIT_H + N_SAMPLE == TMH and SPLIT_H % 8 == 0


def _params(ndim):
    return pltpu.CompilerParams(dimension_semantics=("arbitrary",) * ndim,
                                vmem_limit_bytes=VMEM_LIMIT)


def _resident(shape):
    return pl.BlockSpec(shape, lambda m: (0,) * len(shape), pipeline_mode=pl.Buffered(1))


def _dot(a, b):
    return jnp.dot(a, b, preferred_element_type=F32)


def _layer_norm(z, g, b):
    mu = jnp.mean(z, axis=-1, keepdims=True)
    d = z - mu
    var = jnp.mean(d * d, axis=-1, keepdims=True)
    return d * lax.rsqrt(var + LN_EPS) * g + b


def _rope(x, cos, sin_signed):
    lane = lax.broadcasted_iota(jnp.int32, (1, LANES), 1)
    low_half = (lane % HEAD_DIM) < (HEAD_DIM // 2)
    outs = []
    for c in range(x.shape[1] // LANES):
        xc = x[:, c * LANES:(c + 1) * LANES]
        up = pltpu.roll(xc, LANES - HEAD_DIM // 2, 1)
        down = pltpu.roll(xc, HEAD_DIM // 2, 1)
        outs.append(xc * cos + jnp.where(low_half, up, down) * sin_signed)
    return jnp.concatenate(outs, axis=1)


_NQ = D_MODEL // TN
_N_KV = _NQ
_N_U0 = _N_KV + 1
_N_G0 = _N_U0 + POOL_WIDTH // TN


def _in_proj_kernel(x_ref, xs_ref, w_ref, cos_ref, sin_ref, q_ref, kv_ref, xb, wres):
    m = pl.program_id(0)
    n = pl.program_id(1)

    @pl.when(n == 0)
    def _():
        @pl.when(m < NT - 1)
        def _():
            xb[...] = x_ref[...].astype(BF16)

        @pl.when(m == NT - 1)
        def _():
            xb[0:SPLIT, :] = x_ref[0:SPLIT, :].astype(BF16)
            xb[SPLIT:TM, :] = xs_ref[...].astype(BF16)

    @pl.when(m == 0)
    def _():
        wres[n] = w_ref[...].astype(BF16)

    def project(epilogue, parts):
        accs = [_dot(xb[lo:hi, :], wres[n]) for lo, hi in parts]
        for (lo, hi), acc in zip(parts, accs):
            epilogue(acc, lo, hi)

    def q_epilogue(acc, lo, hi):
        roped = _rope(acc, cos_ref[lo:hi, :], sin_ref[lo:hi, :])
        q_ref[lo:hi, :] = (roped * (HEAD_DIM ** -0.5 * LOG2E)).astype(BF16)

    def kv_epilogue(acc, lo, hi):
        kv_ref[lo:hi, :KV_DIM] = _rope(acc[:, :KV_DIM], cos_ref[lo:hi, :], sin_ref[lo:hi, :])
        kv_ref[lo:hi, KV_DIM:] = acc[:, KV_DIM:]

    pl.when(n < _NQ)(lambda: project(q_epilogue, THIRDS))
    pl.when(n == _N_KV)(lambda: project(kv_epilogue, HALVES))


def _in_proj(x, xs, w_in, cos, sin):
    nn = _N_U0

    return pl.pallas_call(
        _in_proj_kernel,
        grid=(NT, nn),
        in_specs=[
            pl.BlockSpec((TM, D_MODEL), lambda m, n: (m, 0)),
            pl.BlockSpec((N_SAMPLE, D_MODEL), lambda m, n: (0, 0)),
            pl.BlockSpec((D_MODEL, TN), lambda m, n: (0, jnp.where(m == 0, n, nn - 1))),
            pl.BlockSpec((TM, LANES), lambda m, n: (m, 0)),
            pl.BlockSpec((TM, LANES), lambda m, n: (m, 0)),
        ],
        out_specs=[
            pl.BlockSpec((TM, TN), lambda m, n: (m, jnp.minimum(n, _NQ - 1))),
            pl.BlockSpec((TM, TN), lambda m, n: (m, 0)),
            pl.BlockSpec((TM, D_MODEL), lambda m, n: (m, 0)),
        ],
        out_shape=[
            jax.ShapeDtypeStruct((ROWS, D_MODEL), BF16),
            jax.ShapeDtypeStruct((ROWS, 2 * KV_DIM), F32),
            jax.ShapeDtypeStruct((ROWS, D_MODEL), BF16),
        ],
        scratch_shapes=[pltpu.VMEM((nn, D_MODEL, TN), BF16)],
        compiler_params=_params(2),
        name="in_proj",
    )(x, xs, w_in, cos, sin)


PAIRS = GROUP // 2
Q_BLOCKS = 2


def _attn_kernel(fill_ref, q_ref, kp_ref, kc_ref, vp_ref, vc_ref, os_ref, *refs):
    n_w = (len(refs) - 1) // 2
    w_refs, o_ref, wb_refs = refs[:n_w], refs[n_w], refs[n_w + 1:]
    i = pl.program_id(0)
    last = pl.num_programs(0) - 1

    @pl.when(i < last)
    def _():
        for w_ref, wb_ref in zip(w_refs, wb_refs):
            wb_ref[...] = w_ref[...].astype(BF16)
        keys = (kp_ref[...],) + tuple(kc_ref[b * WINDOW:(b + 1) * WINDOW, :] for b in range(Q_BLOCKS))
        vals = (vp_ref[...],) + tuple(vc_ref[b * WINDOW:(b + 1) * WINDOW, :] for b in range(Q_BLOCKS))
        for b in range(Q_BLOCKS):
            _attn_block(Q_BLOCKS * i + b, fill_ref, q_ref, o_ref, b * WINDOW,
                        keys[b], keys[b + 1], vals[b], vals[b + 1])

    @pl.when(i == last)
    def _():
        o_ref[0:N_SAMPLE, :] = os_ref[...]


def _attn_block(i, fill_ref, q_ref, o_ref, row0, k_prev, k_cur, v_prev, v_cur):
    q_rows = slice(row0, row0 + WINDOW)
    kk = jnp.concatenate([k_prev, k_cur], axis=0)
    vv = jnp.concatenate([v_prev, v_cur], axis=0)
    key = lax.broadcasted_iota(jnp.int32, (2 * WINDOW, 1), 0)
    vv = jnp.where(key == 0, 0.0, vv)
    low = lax.broadcasted_iota(jnp.int32, (1, LANES), 1) < HEAD_DIM
    rows = PAIRS * WINDOW
    r = lax.broadcasted_iota(jnp.int32, (rows, 2 * WINDOW), 0) % WINDOW
    c = lax.broadcasted_iota(jnp.int32, (rows, 2 * WINDOW), 1)
    visible = (c > r) & (c <= r + WINDOW) & ((i > 0) | (c >= WINDOW))

    for pair in range(N_KV_HEADS // 2):
        k_pair = kk[:, pair * LANES:(pair + 1) * LANES]
        v_pair = vv[:, pair * LANES:(pair + 1) * LANES]
        k_swap = pltpu.roll(k_pair, HEAD_DIM, 1)
        v_swap = pltpu.roll(v_pair, HEAD_DIM, 1)
        for second in range(2):
            g = 2 * pair + second
            k_in_low, k_in_high = (k_swap, k_pair) if second else (k_pair, k_swap)
            v_in_low, v_in_high = (v_swap, v_pair) if second else (v_pair, v_swap)
            x = jnp.concatenate(
                [q_ref[q_rows, (g * PAIRS + t) * LANES:(g * PAIRS + t + 1) * LANES] for t in range(PAIRS)],
                axis=0)

            def half(k_pad, v_pad, fill):
                s = lax.dot_general(x, k_pad.astype(BF16), (((1,), (1,)), ((), ())),
                                    preferred_element_type=F32)
                s = jnp.where(visible, s, fill)
                e = jnp.exp2(s - jnp.max(s, axis=-1, keepdims=True)).astype(BF16)
                return _dot(e, v_pad.astype(BF16))

            a = half(jnp.where(low, k_in_low, 0.0), jnp.where(low, v_in_low, 1.0), fill_ref[2 * g])
            b = half(jnp.where(low, 0.0, k_in_high), jnp.where(low, 1.0, v_in_high), fill_ref[2 * g + 1])
            den = pltpu.roll(jnp.where(low, b, a), HEAD_DIM, 1)
            out = (jnp.where(low, a, b) / den).astype(BF16)
            for t in range(PAIRS):
                o_ref[q_rows, (g * PAIRS + t) * LANES:(g * PAIRS + t + 1) * LANES] = \
                    out[t * WINDOW:(t + 1) * WINDOW]


def _cast_chunk_rows(rows, steps):
    assert rows % 16 == 0
    units = rows // 16
    n = max(d for d in range(1, steps + 1) if units % d == 0)
    return rows // n


def _attention(sinks, q, kv, attn_s, weights):
    nb = SEQ // (Q_BLOCKS * WINDOW)
    cur = lambda i: jnp.minimum(i, nb - 1)
    prev = lambda i: jnp.clip(Q_BLOCKS * i - 1, 0, Q_BLOCKS * nb - 1)
    per_chunk = jnp.transpose(sinks.reshape(N_KV_HEADS, PAIRS, 2), (0, 2, 1))
    per_row = jnp.broadcast_to(per_chunk.reshape(2 * N_KV_HEADS, PAIRS, 1),
                               (2 * N_KV_HEADS, PAIRS, WINDOW)).reshape(2 * N_KV_HEADS, PAIRS * WINDOW)
    first_col = jnp.arange(2 * WINDOW)[None, None, :] == 0
    fill = jnp.where(first_col, per_row[:, :, None], NEG_INF).astype(F32)
    w_in_specs, w_out_specs, w_out_shapes = [], [], []
    for w, col, width in weights:
        chunk = _cast_chunk_rows(w.shape[0], nb)
        n = w.shape[0] // chunk
        w_in_specs.append(pl.BlockSpec((chunk, width), lambda i, n=n, col=col: (jnp.minimum(i, n - 1), col)))
        w_out_specs.append(pl.BlockSpec((chunk, width), lambda i, n=n: (jnp.minimum(i, n - 1), 0)))
        w_out_shapes.append(jax.ShapeDtypeStruct((w.shape[0], width), BF16))
    outs = pl.pallas_call(
        _attn_kernel,
        grid=(nb + 1,),
        in_specs=[
            _resident(fill.shape),
            pl.BlockSpec((Q_BLOCKS * WINDOW, D_MODEL), lambda i: (cur(i), 0)),
            pl.BlockSpec((WINDOW, KV_DIM), lambda i: (prev(i), 0)),
            pl.BlockSpec((Q_BLOCKS * WINDOW, KV_DIM), lambda i: (cur(i), 0)),
            pl.BlockSpec((WINDOW, KV_DIM), lambda i: (prev(i), 1)),
            pl.BlockSpec((Q_BLOCKS * WINDOW, KV_DIM), lambda i: (cur(i), 1)),
            pl.BlockSpec((N_SAMPLE, D_MODEL), lambda i: (0, 0)),
        ] + w_in_specs,
        out_specs=[pl.BlockSpec((Q_BLOCKS * WINDOW, D_MODEL), lambda i: (i, 0))] + w_out_specs,
        out_shape=[jax.ShapeDtypeStruct((ROWS, D_MODEL), BF16)] + w_out_shapes,
        compiler_params=_params(1),
        name="attn_prompt",
    )(fill, q, kv, kv, kv, kv, attn_s, *[w for w, _, _ in weights])
    return outs[0], outs[1:]


SEQS_PER_STEP = 8


def _attn_sample_kernel(sink_ref, q_ref, kvn_ref, ck_ref, cv_ref, o_ref, nk_ref, nv_ref):
    head = lax.broadcasted_iota(jnp.int32, (N_HEADS, KV_DIM), 0) // GROUP
    col_group = lax.broadcasted_iota(jnp.int32, (N_HEADS, KV_DIM), 1) // HEAD_DIM
    own = head == col_group
    newest = lax.broadcasted_iota(jnp.int32, (1, WINDOW), 1) == WINDOW - 1
    scores = []
    for b in range(SEQS_PER_STEP):
        nk_ref[b] = jnp.where(newest, kvn_ref[0, 0:KV_DIM, b:b + 1], pltpu.roll(ck_ref[b], WINDOW - 1, 1))
        nv_ref[b] = jnp.where(newest, kvn_ref[0, KV_DIM:2 * KV_DIM, b:b + 1],
                              pltpu.roll(cv_ref[b], WINDOW - 1, 1))
        qb = q_ref[b]
        qe = jnp.where(own, jnp.concatenate([qb] * N_KV_HEADS, axis=1), jnp.zeros((), BF16))
        scores.append(_dot(qe, nk_ref[b].astype(BF16)))
    s = jnp.concatenate(scores, axis=0)
    sink = jnp.concatenate([sink_ref[...]] * SEQS_PER_STEP, axis=0)
    mx = jnp.maximum(jnp.max(s, axis=-1, keepdims=True), sink)
    e = jnp.exp2(s - mx)
    inv = 1.0 / (jnp.sum(e, axis=-1, keepdims=True) + jnp.exp2(sink - mx))
    e = e.astype(BF16)
    for b in range(SEQS_PER_STEP):
        rows = slice(b * N_HEADS, (b + 1) * N_HEADS)
        of = lax.dot_general(e[rows], nv_ref[b].astype(BF16), (((1,), (1,)), ((), ())),
                             preferred_element_type=F32)
        of = jnp.where(own, of, 0.0)
        o = of[:, 0:HEAD_DIM]
        for g in range(1, N_KV_HEADS):
            o = o + of[:, g * HEAD_DIM:(g + 1) * HEAD_DIM]
        o_ref[b] = (o * inv[rows]).astype(BF16)


def _attention_sample(sink_col, q3, kvn_t, ck, cv):
    nb = N_SAMPLE // SEQS_PER_STEP
    cache_spec = pl.BlockSpec((SEQS_PER_STEP, KV_DIM, WINDOW), lambda i: (i, 0, 0))
    return pl.pallas_call(
        _attn_sample_kernel,
        grid=(nb,),
        in_specs=[
            pl.BlockSpec((N_HEADS, 1), lambda i: (0, 0)),
            pl.BlockSpec((SEQS_PER_STEP, N_HEADS, HEAD_DIM), lambda i: (i, 0, 0)),
            pl.BlockSpec((1, 2 * KV_DIM, SEQS_PER_STEP), lambda i: (i, 0, 0)),
            cache_spec, cache_spec,
        ],
        out_specs=[
            pl.BlockSpec((SEQS_PER_STEP, N_HEADS, HEAD_DIM), lambda i: (i, 0, 0)),
            cache_spec, cache_spec,
        ],
        out_shape=[
            jax.ShapeDtypeStruct((N_SAMPLE, N_HEADS, HEAD_DIM), BF16),
            jax.ShapeDtypeStruct((N_SAMPLE, KV_DIM, WINDOW), F32),
            jax.ShapeDtypeStruct((N_SAMPLE, KV_DIM, WINDOW), F32),
        ],
        compiler_params=_params(1),
        name="attn_sample",
    )(sink_col, q3, kvn_t, ck, cv)


POOL_PAD = 16


def _pool_kernel(xb_ref, wu0_ref, wu1_ref, sp_ref, wmix_ref, scale_ref, p_ref, spn_ref, plast_ref, ext):
    m = pl.program_id(0)

    @pl.when(m == 0)
    def _():
        ext[0:POOL_PAD, :] = jnp.zeros((POOL_PAD, POOL_WIDTH), F32)

    pos = m * TM + lax.broadcasted_iota(jnp.int32, (TM, 1), 0)
    for t, wu_ref in ((1, wu1_ref), (0, wu0_ref)):
        ext[POOL_PAD:POOL_PAD + TM, t * TN:(t + 1) * TN] = _dot(xb_ref[...], wu_ref[...])
    for g, w in reversed(list(enumerate(POOL_WINDOWS))):
        cols = slice(g * POOL_GROUP_DIM, (g + 1) * POOL_GROUP_DIM)
        u_g = ext[POOL_PAD:POOL_PAD + TM, cols]
        run = ext[:, cols]
        shift = 1
        while shift < w:
            run = run + pltpu.roll(run, shift, 0)
            shift *= 2
        tot = run[POOL_PAD:POOL_PAD + TM, :]
        inv_cnt = 1.0 / jnp.minimum(w, pos + 1).astype(F32)
        d = tot * inv_cnt - u_g
        y = _dot(d.astype(BF16), wmix_ref[g].astype(BF16))
        p_ref[:, cols] = (y * scale_ref[:, cols]).astype(BF16)

    @pl.when(m == NT - 1)
    def _():
        for g, w in enumerate(POOL_WINDOWS):
            cols = slice(g * POOL_GROUP_DIM, (g + 1) * POOL_GROUP_DIM)
            us = ext[POOL_PAD + SPLIT:POOL_PAD + TM, cols]
            tot = us
            for j in range(1, w):
                tot = tot + sp_ref[POOL_HIST - j, :, cols]
            d = tot * (1.0 / w) - us
            y = _dot(d.astype(BF16), wmix_ref[g].astype(BF16))
            p_ref[SPLIT:TM, cols] = (y * scale_ref[:, cols]).astype(BF16)
        spn_ref[0:POOL_HIST - 1] = sp_ref[1:POOL_HIST]
        spn_ref[POOL_HIST - 1] = ext[POOL_PAD + SPLIT:POOL_PAD + TM, :]
        plast_ref[...] = ext[SPLIT:POOL_PAD + SPLIT, :]

    ext[0:POOL_PAD, :] = ext[TM:TM + POOL_PAD, :]


def _pool(xb, wu0, wu1, sp_t, wmix, scale):
    full = lambda shape: pl.BlockSpec(shape, lambda m: (0,) * len(shape))
    return pl.pallas_call(
        _pool_kernel,
        grid=(NT,),
        in_specs=[
            pl.BlockSpec((TM, D_MODEL), lambda m: (m, 0)),
            _resident((D_MODEL, TN)), _resident((D_MODEL, TN)),
            full((POOL_HIST, N_SAMPLE, POOL_WIDTH)),
            full((len(POOL_WINDOWS), POOL_GROUP_DIM, POOL_GROUP_DIM)),
            full((1, POOL_WIDTH)),
        ],
        out_specs=[pl.BlockSpec((TM, POOL_WIDTH), lambda m: (m, 0)),
                   full((POOL_HIST, N_SAMPLE, POOL_WIDTH)),
                   full((POOL_PAD, POOL_WIDTH))],
        out_shape=[jax.ShapeDtypeStruct((ROWS, POOL_WIDTH), BF16),
                   jax.ShapeDtypeStruct((POOL_HIST, N_SAMPLE, POOL_WIDTH), F32),
                   jax.ShapeDtypeStruct((POOL_PAD, POOL_WIDTH), F32)],
        scratch_shapes=[pltpu.VMEM((POOL_PAD + TM, POOL_WIDTH), F32)],
        compiler_params=_params(1),
        name="pool_mix",
    )(xb, wu0, wu1, sp_t, wmix, scale)


_NC = D_MODEL // TN


def _merge_kernel(xb_ref, pool_ref, attn_ref, wp_ref, wa_ref, *refs):
    gate_w, o_ref = refs[:2 * _NC], refs[2 * _NC]
    for c in range(_NC):
        cols = slice(c * TN, (c + 1) * TN)
        gate_pool = jax.nn.sigmoid(_dot(xb_ref[...], gate_w[c][...]))
        gate_attn = jax.nn.sigmoid(_dot(xb_ref[...], gate_w[_NC + c][...]))
        a = _dot(pool_ref[...], wp_ref[:, cols])
        b = _dot(attn_ref[...], wa_ref[:, cols])
        o_ref[:, cols] = (gate_pool * a + gate_attn * b).astype(BF16)


def _merge(xb, pooled, attn, wp, wa, gate_w):
    row = lambda width: pl.BlockSpec((TMH, width), lambda m: (m, 0))
    return pl.pallas_call(
        _merge_kernel,
        grid=(NTH,),
        in_specs=[row(D_MODEL), row(POOL_WIDTH), row(D_MODEL),
                  _resident((POOL_WIDTH, D_MODEL)), _resident((D_MODEL, D_MODEL))]
        + [_resident((D_MODEL, TN))] * (2 * _NC),
        out_specs=row(D_MODEL),
        out_shape=jax.ShapeDtypeStruct((ROWS, D_MODEL), BF16),
        compiler_params=_params(1),
        name="branch_merge",
    )(xb, pooled, attn, wp, wa, *gate_w)


def _out_ln_kernel(mrg_ref, wo_ref, x_ref, xs_ref, lng_ref, lnb_ref, x1_ref, x1b_ref):
    m = pl.program_id(0)

    def body(residual):
        accs = [_dot(mrg_ref[lo:hi, :], wo_ref[...]) for lo, hi in HALVES_H]
        for (lo, hi), acc in zip(HALVES_H, accs):
            x1 = _layer_norm(acc + ALPHA * residual(lo, hi), lng_ref[...], lnb_ref[...])
            x1_ref[lo:hi, :] = ALPHA * x1
            x1b_ref[lo:hi, :] = x1.astype(BF16)

    def last_tile_rows(lo, hi):
        if hi <= SPLIT_H:
            return x_ref[lo:hi, :]
        return jnp.concatenate([x_ref[lo:SPLIT_H, :], xs_ref[...]], axis=0)

    pl.when(m < NTH - 1)(lambda: body(lambda lo, hi: x_ref[lo:hi, :]))
    pl.when(m == NTH - 1)(lambda: body(last_tile_rows))


def _out_ln(merged, wo, x, xs, ln_g, ln_b):
    row = pl.BlockSpec((TMH, D_MODEL), lambda m: (m, 0))
    return pl.pallas_call(
        _out_ln_kernel,
        grid=(NTH,),
        in_specs=[row, _resident((D_MODEL, D_MODEL)), row, _resident((N_SAMPLE, D_MODEL)),
                  _resident((1, D_MODEL)), _resident((1, D_MODEL))],
        out_specs=[row, row],
        out_shape=[jax.ShapeDtypeStruct((ROWS, D_MODEL), F32), jax.ShapeDtypeStruct((ROWS, D_MODEL), BF16)],
        compiler_params=_params(1),
        name="out_proj_ln1",
    )(merged, wo, x, xs, ln_g, ln_b)


TF = 512
NF = D_FF // TF
CARRY = 8


def _ffn_kernel(x1b_ref, x1_hbm, sc_ref, wg_ref, wu_ref, wd_ref, cw_ref, cb_ref, lng_ref, lnb_ref,
                o_ref, os_ref, glast_ref, scn_ref,
                h0, h1, gext, carry, sem):
    m = pl.program_id(0)
    f = pl.program_id(1)
    last_m = m == NT - 1
    h_slots = (h0, h1)
    x1_copy = pltpu.make_async_copy(x1_hbm.at[pl.ds(m * TM, TM), :], o_ref, sem)

    def up_and_gate(slot, with_down, residual_arrives=False):
        gext[0:CARRY, :] = carry[f]
        for lo, hi in HALVES:
            gext[CARRY + lo:CARRY + hi, :] = _dot(x1b_ref[lo:hi, :], wg_ref[...])
        ups = [_dot(x1b_ref[lo:hi, :], wu_ref[...]) for lo, hi in HALVES]
        if residual_arrives:
            x1_copy.wait()
        if with_down:
            o_ref[...] += _dot(h_slots[1 - slot][...], wd_ref[...])
        w0 = cw_ref[0:1, :]
        w1 = cw_ref[1:2, :]
        w2 = cw_ref[2:3, :]
        cb = cb_ref[...]

        def conv(lo, hi, older=None, newer=None):
            older = gext[CARRY - 2 + lo:CARRY - 2 + hi, :] if older is None else older
            newer = gext[CARRY - 1 + lo:CARRY - 1 + hi, :] if newer is None else newer
            return cb + w0 * older + w1 * newer + w2 * gext[CARRY + lo:CARRY + hi, :]

        for (lo, hi), up in zip(HALVES, ups):
            if hi <= SPLIT:
                gc = conv(lo, hi)
            else:
                older_s = jnp.where(last_m, sc_ref[:, 0, :], gext[CARRY - 2 + SPLIT:CARRY - 2 + TM, :])
                newer_s = jnp.where(last_m, sc_ref[:, 1, :], gext[CARRY - 1 + SPLIT:CARRY - 1 + TM, :])
                gc = jnp.concatenate([conv(lo, SPLIT), conv(SPLIT, TM, older_s, newer_s)], axis=0)
            h_slots[slot][lo:hi, :] = (jax.nn.gelu(gc, approximate=True) * up).astype(BF16)
        carry[f] = gext[TM:CARRY + TM, :]

        @pl.when(last_m)
        def _():
            glast_ref[...] = gext[SPLIT:CARRY + SPLIT, :]
            scn_ref[:, 0, :] = sc_ref[:, 1, :]
            scn_ref[:, 1, :] = gext[CARRY + SPLIT:CARRY + TM, :]

    @pl.when(f == 0)
    def _():
        @pl.when(m == 0)
        def _():
            carry[...] = jnp.zeros_like(carry)

        x1_copy.start()
        up_and_gate(0, False)

    @pl.when(f == 1)
    def _():
        up_and_gate(1, True, residual_arrives=True)

    for parity in range(2):
        @pl.when((f > 1) & (f < NF) & (f % 2 == parity))
        def _():
            up_and_gate(parity, True)

    @pl.when(f == NF)
    def _():
        h_last = h_slots[(NF - 1) % 2]
        accs = [o_ref[lo:hi, :] + _dot(h_last[lo:hi, :], wd_ref[...]) for lo, hi in THIRDS]
        for (lo, hi), acc in zip(THIRDS, accs):
            o_ref[lo:hi, :] = _layer_norm(acc, lng_ref[...], lnb_ref[...])

        @pl.when(last_m)
        def _():
            os_ref[...] = o_ref[SPLIT:TM, :]


def _ffn(x1b, x1, sc, wg, wu, wd, conv_w, conv_b, ln_g, ln_b):
    full = lambda shape: pl.BlockSpec(shape, lambda m, f: (0, 0))
    g_tile = lambda f: jnp.minimum(f, NF - 1)
    d_tile = lambda f: jnp.maximum(f - 1, 0)
    only_last = lambda m, f: jnp.where(m == NT - 1, g_tile(f), 0)
    row = pl.BlockSpec((TM, D_MODEL), lambda m, f: (m, 0))
    return pl.pallas_call(
        _ffn_kernel,
        grid=(NT, NF + 1),
        in_specs=[
            row,
            pl.BlockSpec(memory_space=pl.ANY),
            pl.BlockSpec((N_SAMPLE, CONV_W - 1, TF), lambda m, f: (0, 0, only_last(m, f))),
            pl.BlockSpec((D_MODEL, TF), lambda m, f: (0, g_tile(f))),
            pl.BlockSpec((D_MODEL, TF), lambda m, f: (0, g_tile(f))),
            pl.BlockSpec((TF, D_MODEL), lambda m, f: (d_tile(f), 0)),
            pl.BlockSpec((CONV_W, TF), lambda m, f: (0, g_tile(f))),
            pl.BlockSpec((1, TF), lambda m, f: (0, g_tile(f))),
            full((1, D_MODEL)), full((1, D_MODEL)),
        ],
        out_specs=[
            row,
            full((N_SAMPLE, D_MODEL)),
            pl.BlockSpec((CARRY, TF), lambda m, f: (0, only_last(m, f))),
            pl.BlockSpec((N_SAMPLE, CONV_W - 1, TF), lambda m, f: (0, 0, only_last(m, f))),
        ],
        out_shape=[
            jax.ShapeDtypeStruct((SEQ, D_MODEL), F32),
            jax.ShapeDtypeStruct((N_SAMPLE, D_MODEL), F32),
            jax.ShapeDtypeStruct((CARRY, D_FF), F32),
            jax.ShapeDtypeStruct((N_SAMPLE, CONV_W - 1, D_FF), F32),
        ],
        scratch_shapes=[
            pltpu.VMEM((TM, TF), BF16), pltpu.VMEM((TM, TF), BF16),
            pltpu.VMEM((CARRY + TM, TF), F32),
            pltpu.VMEM((NF, CARRY, TF), F32),
            pltpu.SemaphoreType.DMA(()),
        ],
        compiler_params=_params(2),
        name="convffn_ln2",
    )(x1b, x1, sc, wg, wu, wd, conv_w, conv_b, ln_g, ln_b)


def _rope_tables(pos):
    half = HEAD_DIM // 2
    inv = ROPE_THETA ** (-jnp.arange(half, dtype=F32) / half)
    lane = jnp.arange(LANES)
    inv_lanes = inv[lane % half]
    sign = jnp.where((lane % HEAD_DIM) < half, -1.0, 1.0).astype(F32)
    ang = pos.astype(F32)[:, None] * inv_lanes[None, :]
    return jnp.cos(ang), jnp.sin(ang) * sign[None, :]


def kernel(x_prompt, x_sample, cache_k, cache_v, state_pool, state_conv, w_in, attn_sinks, w_pool_mix,
           pool_scale, w_attn_branch, w_pool_branch, w_out, ln1_g, ln1_b, w_up, w_gate, conv_w, conv_b,
           w_down, ln2_g, ln2_b):
    x = x_prompt.reshape(SEQ, D_MODEL)
    xs = x_sample.reshape(N_SAMPLE, D_MODEL)
    pos = jnp.concatenate([jnp.arange(SEQ), jnp.full((N_SAMPLE,), PAST_LEN)])
    cos, sin = _rope_tables(pos)

    q, kv, xb = _in_proj(x, xs, w_in[0], cos, sin)

    to_dsw = lambda c: jnp.transpose(c[0], (0, 2, 3, 1)).reshape(N_SAMPLE, KV_DIM, WINDOW)
    steps = N_SAMPLE // SEQS_PER_STEP
    kvn_t = jnp.transpose(kv[SEQ:].reshape(steps, SEQS_PER_STEP, 2 * KV_DIM), (0, 2, 1))
    sinks = attn_sinks[0] * LOG2E
    attn_s, new_k, new_v = _attention_sample(
        sinks.reshape(N_HEADS, 1), q[SEQ:].reshape(N_SAMPLE, N_HEADS, HEAD_DIM), kvn_t,
        to_dsw(cache_k), to_dsw(cache_v))
    whole = lambda w: (w, 0, w.shape[1])
    late_tiles = [(w_in[0], j, TN) for j in range(_N_U0, _N_G0 + 2 * _NC)]
    attn, (wp_b, wa_b, wo_b, wg_b, wu_b, wd_b, wpool0, wpool1, *gate_w) = _attention(
        sinks, q, kv, attn_s.reshape(N_SAMPLE, D_MODEL),
        [whole(w_pool_branch[0]), whole(w_attn_branch[0]), whole(w_out[0]),
         whole(w_gate[0]), whole(w_up[0]), whole(w_down[0])] + late_tiles)

    state_pool_t = jnp.transpose(state_pool[0], (1, 0, 2))
    pooled, new_pool_t, pool_last = _pool(xb, wpool0, wpool1, state_pool_t, w_pool_mix[0], pool_scale)

    merged = _merge(xb, pooled, attn, wp_b, wa_b, gate_w)
    x1, x1b = _out_ln(merged, wo_b, x, xs, ln1_g, ln1_b)

    y, y_s, g_last, new_conv_s = _ffn(x1b, x1, state_conv[0], wg_b, wu_b, wd_b,
                                      conv_w[0], conv_b, ln2_g, ln2_b)

    from_dsw = lambda c: jnp.transpose(
        c.reshape(N_SAMPLE, N_KV_HEADS, HEAD_DIM, WINDOW), (0, 3, 1, 2))[None]
    return (
        y.reshape(1, SEQ, D_MODEL),
        y_s.reshape(N_SAMPLE, 1, D_MODEL),
        kv[SEQ - WINDOW:SEQ, :KV_DIM].reshape(1, 1, WINDOW, N_KV_HEADS, HEAD_DIM),
        kv[SEQ - WINDOW:SEQ, KV_DIM:].reshape(1, 1, WINDOW, N_KV_HEADS, HEAD_DIM),
        pool_last[POOL_PAD - POOL_HIST:].reshape(1, 1, POOL_HIST, POOL_WIDTH),
        g_last[CARRY - (CONV_W - 1):].reshape(1, 1, CONV_W - 1, D_FF),
        from_dsw(new_k),
        from_dsw(new_v),
        jnp.transpose(new_pool_t, (1, 0, 2))[None],
        new_conv_s[None],
    )
```

```python
import jax
import jax.numpy as jnp
from jax import lax
from jax.experimental import pallas as pl
from jax.experimental.pallas import tpu as pltpu

F32 = jnp.float32
BF16 = jnp.bfloat16

D_MODEL = 2048
SEQ = 8192
N_SAMPLE = 128
PAST_LEN = 8192
HEAD_DIM = 64
N_HEADS = 32
N_KV_HEADS = 4
GROUP = N_HEADS // N_KV_HEADS
KV_DIM = N_KV_HEADS * HEAD_DIM
WINDOW = 128
ROPE_THETA = 10000.0
POOL_WINDOWS = (2, 4, 8, 16)
POOL_WIDTH = 1024
POOL_GROUP_DIM = 256
POOL_HIST = 15
D_FF = 5632
CONV_W = 3
LN_EPS = 1e-5
NEG_INF = -1e30
ALPHA = 2.0 ** 0.25
LOG2E = 1.4426950408889634

LANES = 128
VMEM_LIMIT = 60000 * 1024

ROWS = SEQ + N_SAMPLE
TM = 1040
NT = ROWS // TM
SPLIT = SEQ - (NT - 1) * TM
TN = 512
HALF = 528
HALVES = ((0, HALF), (HALF, TM))
THIRDS = ((0, 352), (352, 704), (704, TM))
assert NT * TM == ROWS and SPLIT + N_SAMPLE == TM and SPLIT % 16 == 0
assert all(lo % 16 == 0 for lo, _ in HALVES + THIRDS)

TMH = TM // 2
NTH = ROWS // TMH
SPLIT_H = SEQ - (NTH - 1) * TMH
HALVES_H = ((0, 256), (256, TMH))
assert NTH * TMH == ROWS and SPLIT_H + N_SAMPLE == TMH and SPLIT_H % 8 == 0


def _params(ndim):
    return pltpu.CompilerParams(dimension_semantics=("arbitrary",) * ndim,
                                vmem_limit_bytes=VMEM_LIMIT)


def _resident(shape):
    return pl.BlockSpec(shape, lambda m: (0,) * len(shape), pipeline_mode=pl.Buffered(1))


def _dot(a, b):
    return jnp.dot(a, b, preferred_element_type=F32)


def _layer_norm(z, g, b):
    mu = jnp.mean(z, axis=-1, keepdims=True)
    d = z - mu
    var = jnp.mean(d * d, axis=-1, keepdims=True)
    return d * lax.rsqrt(var + LN_EPS) * g + b


def _rope(x, cos, sin_signed):
    lane = lax.broadcasted_iota(jnp.int32, (1, LANES), 1)
    low_half = (lane % HEAD_DIM) < (HEAD_DIM // 2)
    outs = []
    for c in range(x.shape[1] // LANES):
        xc = x[:, c * LANES:(c + 1) * LANES]
        up = pltpu.roll(xc, LANES - HEAD_DIM // 2, 1)
        down = pltpu.roll(xc, HEAD_DIM // 2, 1)
        outs.append(xc * cos + jnp.where(low_half, up, down) * sin_signed)
    return jnp.concatenate(outs, axis=1)


_NQ = D_MODEL // TN
_N_KV = _NQ
_N_U0 = _N_KV + 1
_N_G0 = _N_U0 + POOL_WIDTH // TN


def _in_proj_kernel(x_ref, xs_ref, w_ref, cos_ref, sin_ref, q_ref, kv_ref, xb, wres):
    m = pl.program_id(0)
    n = pl.program_id(1)

    @pl.when(n == 0)
    def _():
        @pl.when(m < NT - 1)
        def _():
            xb[...] = x_ref[...].astype(BF16)

        @pl.when(m == NT - 1)
        def _():
            xb[0:SPLIT, :] = x_ref[0:SPLIT, :].astype(BF16)
            xb[SPLIT:TM, :] = xs_ref[...].astype(BF16)

    @pl.when(m == 0)
    def _():
        wres[n] = w_ref[...].astype(BF16)

    def project(epilogue, parts):
        accs = [_dot(xb[lo:hi, :], wres[n]) for lo, hi in parts]
        for (lo, hi), acc in zip(parts, accs):
            epilogue(acc, lo, hi)

    def q_epilogue(acc, lo, hi):
        roped = _rope(acc, cos_ref[lo:hi, :], sin_ref[lo:hi, :])
        q_ref[lo:hi, :] = (roped * (HEAD_DIM ** -0.5 * LOG2E)).astype(BF16)

    def kv_epilogue(acc, lo, hi):
        kv_ref[lo:hi, :KV_DIM] = _rope(acc[:, :KV_DIM], cos_ref[lo:hi, :], sin_ref[lo:hi, :])
        kv_ref[lo:hi, KV_DIM:] = acc[:, KV_DIM:]

    pl.when(n < _NQ)(lambda: project(q_epilogue, THIRDS))
    pl.when(n == _N_KV)(lambda: project(kv_epilogue, HALVES))


def _in_proj(x, xs, w_in, cos, sin):
    nn = _N_U0

    return pl.pallas_call(
        _in_proj_kernel,
        grid=(NT, nn),
        in_specs=[
            pl.BlockSpec((TM, D_MODEL), lambda m, n: (m, 0)),
            pl.BlockSpec((N_SAMPLE, D_MODEL), lambda m, n: (0, 0)),
            pl.BlockSpec((D_MODEL, TN), lambda m, n: (0, jnp.where(m == 0, n, nn - 1))),
            pl.BlockSpec((TM, LANES), lambda m, n: (m, 0)),
            pl.BlockSpec((TM, LANES), lambda m, n: (m, 0)),
        ],
        out_specs=[
            pl.BlockSpec((TM, TN), lambda m, n: (m, jnp.minimum(n, _NQ - 1))),
            pl.BlockSpec((TM, TN), lambda m, n: (m, 0)),
            pl.BlockSpec((TM, D_MODEL), lambda m, n: (m, 0)),
        ],
        out_shape=[
            jax.ShapeDtypeStruct((ROWS, D_MODEL), BF16),
            jax.ShapeDtypeStruct((ROWS, 2 * KV_DIM), F32),
            jax.ShapeDtypeStruct((ROWS, D_MODEL), BF16),
        ],
        scratch_shapes=[pltpu.VMEM((nn, D_MODEL, TN), BF16)],
        compiler_params=_params(2),
        name="in_proj",
    )(x, xs, w_in, cos, sin)


PAIRS = GROUP // 2
Q_BLOCKS = 2


def _attn_kernel(fill_ref, q_ref, kp_ref, kc_ref, vp_ref, vc_ref, os_ref, *refs):
    n_w = (len(refs) - 1) // 2
    w_refs, o_ref, wb_refs = refs[:n_w], refs[n_w], refs[n_w + 1:]
    i = pl.program_id(0)
    last = pl.num_programs(0) - 1

    @pl.when(i < last)
    def _():
        for w_ref, wb_ref in zip(w_refs, wb_refs):
            wb_ref[...] = w_ref[...].astype(BF16)
        keys = (kp_ref[...],) + tuple(kc_ref[b * WINDOW:(b + 1) * WINDOW, :] for b in range(Q_BLOCKS))
        vals = (vp_ref[...],) + tuple(vc_ref[b * WINDOW:(b + 1) * WINDOW, :] for b in range(Q_BLOCKS))
        for b in range(Q_BLOCKS):
            _attn_block(Q_BLOCKS * i + b, fill_ref, q_ref, o_ref, b * WINDOW,
                        keys[b], keys[b + 1], vals[b], vals[b + 1])

    @pl.when(i == last)
    def _():
        o_ref[0:N_SAMPLE, :] = os_ref[...]


def _attn_block(i, fill_ref, q_ref, o_ref, row0, k_prev, k_cur, v_prev, v_cur):
    q_rows = slice(row0, row0 + WINDOW)
    kk = jnp.concatenate([k_prev, k_cur], axis=0)
    vv = jnp.concatenate([v_prev, v_cur], axis=0)
    key = lax.broadcasted_iota(jnp.int32, (2 * WINDOW, 1), 0)
    vv = jnp.where(key == 0, 0.0, vv)
    low = lax.broadcasted_iota(jnp.int32, (1, LANES), 1) < HEAD_DIM
    rows = PAIRS * WINDOW
    r = lax.broadcasted_iota(jnp.int32, (rows, 2 * WINDOW), 0) % WINDOW
    c = lax.broadcasted_iota(jnp.int32, (rows, 2 * WINDOW), 1)
    visible = (c > r) & (c <= r + WINDOW) & ((i > 0) | (c >= WINDOW))

    for pair in range(N_KV_HEADS // 2):
        k_pair = kk[:, pair * LANES:(pair + 1) * LANES]
        v_pair = vv[:, pair * LANES:(pair + 1) * LANES]
        k_swap = pltpu.roll(k_pair, HEAD_DIM, 1)
        v_swap = pltpu.roll(v_pair, HEAD_DIM, 1)
        for second in range(2):
            g = 2 * pair + second
            k_in_low, k_in_high = (k_swap, k_pair) if second else (k_pair, k_swap)
            v_in_low, v_in_high = (v_swap, v_pair) if second else (v_pair, v_swap)
            x = jnp.concatenate(
                [q_ref[q_rows, (g * PAIRS + t) * LANES:(g * PAIRS + t + 1) * LANES] for t in range(PAIRS)],
                axis=0)

            def half(k_pad, v_pad, fill):
                s = lax.dot_general(x, k_pad.astype(BF16), (((1,), (1,)), ((), ())),
                                    preferred_element_type=F32)
                s = jnp.where(visible, s, fill)
                e = jnp.exp2(s - jnp.max(s, axis=-1, keepdims=True)).astype(BF16)
                return _dot(e, v_pad.astype(BF16))

            a = half(jnp.where(low, k_in_low, 0.0), jnp.where(low, v_in_low, 1.0), fill_ref[2 * g])
            b = half(jnp.where(low, 0.0, k_in_high), jnp.where(low, 1.0, v_in_high), fill_ref[2 * g + 1])
            den = pltpu.roll(jnp.where(low, b, a), HEAD_DIM, 1)
            out = (jnp.where(low, a, b) / den).astype(BF16)
            for t in range(PAIRS):
                o_ref[q_rows, (g * PAIRS + t) * LANES:(g * PAIRS + t + 1) * LANES] = \
                    out[t * WINDOW:(t + 1) * WINDOW]


def _cast_chunk_rows(rows, steps):
    assert rows % 16 == 0
    units = rows // 16
    n = max(d for d in range(1, steps + 1) if units % d == 0)
    return rows // n


def _attention(sinks, q, kv, attn_s, weights):
    nb = SEQ // (Q_BLOCKS * WINDOW)
    cur = lambda i: jnp.minimum(i, nb - 1)
    prev = lambda i: jnp.clip(Q_BLOCKS * i - 1, 0, Q_BLOCKS * nb - 1)
    per_chunk = jnp.transpose(sinks.reshape(N_KV_HEADS, PAIRS, 2), (0, 2, 1))
    per_row = jnp.broadcast_to(per_chunk.reshape(2 * N_KV_HEADS, PAIRS, 1),
                               (2 * N_KV_HEADS, PAIRS, WINDOW)).reshape(2 * N_KV_HEADS, PAIRS * WINDOW)
    first_col = jnp.arange(2 * WINDOW)[None, None, :] == 0
    fill = jnp.where(first_col, per_row[:, :, None], NEG_INF).astype(F32)
    w_in_specs, w_out_specs, w_out_shapes = [], [], []
    for w, col, width in weights:
        chunk = _cast_chunk_rows(w.shape[0], nb)
        n = w.shape[0] // chunk
        w_in_specs.append(pl.BlockSpec((chunk, width), lambda i, n=n, col=col: (jnp.minimum(i, n - 1), col)))
        w_out_specs.append(pl.BlockSpec((chunk, width), lambda i, n=n: (jnp.minimum(i, n - 1), 0)))
        w_out_shapes.append(jax.ShapeDtypeStruct((w.shape[0], width), BF16))
    outs = pl.pallas_call(
        _attn_kernel,
        grid=(nb + 1,),
        in_specs=[
            _resident(fill.shape),
            pl.BlockSpec((Q_BLOCKS * WINDOW, D_MODEL), lambda i: (cur(i), 0)),
            pl.BlockSpec((WINDOW, KV_DIM), lambda i: (prev(i), 0)),
            pl.BlockSpec((Q_BLOCKS * WINDOW, KV_DIM), lambda i: (cur(i), 0)),
            pl.BlockSpec((WINDOW, KV_DIM), lambda i: (prev(i), 1)),
            pl.BlockSpec((Q_BLOCKS * WINDOW, KV_DIM), lambda i: (cur(i), 1)),
            pl.BlockSpec((N_SAMPLE, D_MODEL), lambda i: (0, 0)),
        ] + w_in_specs,
        out_specs=[pl.BlockSpec((Q_BLOCKS * WINDOW, D_MODEL), lambda i: (i, 0))] + w_out_specs,
        out_shape=[jax.ShapeDtypeStruct((ROWS, D_MODEL), BF16)] + w_out_shapes,
        compiler_params=_params(1),
        name="attn_prompt",
    )(fill, q, kv, kv, kv, kv, attn_s, *[w for w, _, _ in weights])
    return outs[0], outs[1:]


SEQS_PER_STEP = 8


def _attn_sample_kernel(sink_ref, q_ref, kvn_ref, ck_ref, cv_ref, o_ref, nk_ref, nv_ref):
    head = lax.broadcasted_iota(jnp.int32, (N_HEADS, KV_DIM), 0) // GROUP
    col_group = lax.broadcasted_iota(jnp.int32, (N_HEADS, KV_DIM), 1) // HEAD_DIM
    own = head == col_group
    newest = lax.broadcasted_iota(jnp.int32, (1, WINDOW), 1) == WINDOW - 1
    scores = []
    for b in range(SEQS_PER_STEP):
        nk_ref[b] = jnp.where(newest, kvn_ref[0, 0:KV_DIM, b:b + 1], pltpu.roll(ck_ref[b], WINDOW - 1, 1))
        nv_ref[b] = jnp.where(newest, kvn_ref[0, KV_DIM:2 * KV_DIM, b:b + 1],
                              pltpu.roll(cv_ref[b], WINDOW - 1, 1))
        qb = q_ref[b]
        qe = jnp.where(own, jnp.concatenate([qb] * N_KV_HEADS, axis=1), jnp.zeros((), BF16))
        scores.append(_dot(qe, nk_ref[b].astype(BF16)))
    s = jnp.concatenate(scores, axis=0)
    sink = jnp.concatenate([sink_ref[...]] * SEQS_PER_STEP, axis=0)
    mx = jnp.maximum(jnp.max(s, axis=-1, keepdims=True), sink)
    e = jnp.exp2(s - mx)
    inv = 1.0 / (jnp.sum(e, axis=-1, keepdims=True) + jnp.exp2(sink - mx))
    e = e.astype(BF16)
    for b in range(SEQS_PER_STEP):
        rows = slice(b * N_HEADS, (b + 1) * N_HEADS)
        of = lax.dot_general(e[rows], nv_ref[b].astype(BF16), (((1,), (1,)), ((), ())),
                             preferred_element_type=F32)
        of = jnp.where(own, of, 0.0)
        o = of[:, 0:HEAD_DIM]
        for g in range(1, N_KV_HEADS):
            o = o + of[:, g * HEAD_DIM:(g + 1) * HEAD_DIM]
        o_ref[b] = (o * inv[rows]).astype(BF16)


def _attention_sample(sink_col, q3, kvn_t, ck, cv):
    nb = N_SAMPLE // SEQS_PER_STEP
    cache_spec = pl.BlockSpec((SEQS_PER_STEP, KV_DIM, WINDOW), lambda i: (i, 0, 0))
    return pl.pallas_call(
        _attn_sample_kernel,
        grid=(nb,),
        in_specs=[
            pl.BlockSpec((N_HEADS, 1), lambda i: (0, 0)),
            pl.BlockSpec((SEQS_PER_STEP, N_HEADS, HEAD_DIM), lambda i: (i, 0, 0)),
            pl.BlockSpec((1, 2 * KV_DIM, SEQS_PER_STEP), lambda i: (i, 0, 0)),
            cache_spec, cache_spec,
        ],
        out_specs=[
            pl.BlockSpec((SEQS_PER_STEP, N_HEADS, HEAD_DIM), lambda i: (i, 0, 0)),
            cache_spec, cache_spec,
        ],
        out_shape=[
            jax.ShapeDtypeStruct((N_SAMPLE, N_HEADS, HEAD_DIM), BF16),
            jax.ShapeDtypeStruct((N_SAMPLE, KV_DIM, WINDOW), F32),
            jax.ShapeDtypeStruct((N_SAMPLE, KV_DIM, WINDOW), F32),
        ],
        compiler_params=_params(1),
        name="attn_sample",
    )(sink_col, q3, kvn_t, ck, cv)


POOL_PAD = 16


def _pool_kernel(xb_ref, wu0_ref, wu1_ref, sp_ref, wmix_ref, scale_ref, p_ref, spn_ref, plast_ref, ext):
    m = pl.program_id(0)

    @pl.when(m == 0)
    def _():
        ext[0:POOL_PAD, :] = jnp.zeros((POOL_PAD, POOL_WIDTH), F32)

    pos = m * TM + lax.broadcasted_iota(jnp.int32, (TM, 1), 0)
    for t, wu_ref in ((1, wu1_ref), (0, wu0_ref)):
        ext[POOL_PAD:POOL_PAD + TM, t * TN:(t + 1) * TN] = _dot(xb_ref[...], wu_ref[...])
    for g, w in reversed(list(enumerate(POOL_WINDOWS))):
        cols = slice(g * POOL_GROUP_DIM, (g + 1) * POOL_GROUP_DIM)
        u_g = ext[POOL_PAD:POOL_PAD + TM, cols]
        run = ext[:, cols]
        shift = 1
        while shift < w:
            run = run + pltpu.roll(run, shift, 0)
            shift *= 2
        tot = run[POOL_PAD:POOL_PAD + TM, :]
        inv_cnt = 1.0 / jnp.minimum(w, pos + 1).astype(F32)
        d = tot * inv_cnt - u_g
        y = _dot(d.astype(BF16), wmix_ref[g].astype(BF16))
        p_ref[:, cols] = (y * scale_ref[:, cols]).astype(BF16)

    @pl.when(m == NT - 1)
    def _():
        for g, w in enumerate(POOL_WINDOWS):
            cols = slice(g * POOL_GROUP_DIM, (g + 1) * POOL_GROUP_DIM)
            us = ext[POOL_PAD + SPLIT:POOL_PAD + TM, cols]
            tot = us
            for j in range(1, w):
                tot = tot + sp_ref[POOL_HIST - j, :, cols]
            d = tot * (1.0 / w) - us
            y = _dot(d.astype(BF16), wmix_ref[g].astype(BF16))
            p_ref[SPLIT:TM, cols] = (y * scale_ref[:, cols]).astype(BF16)
        spn_ref[0:POOL_HIST - 1] = sp_ref[1:POOL_HIST]
        spn_ref[POOL_HIST - 1] = ext[POOL_PAD + SPLIT:POOL_PAD + TM, :]
        plast_ref[...] = ext[SPLIT:POOL_PAD + SPLIT, :]

    ext[0:POOL_PAD, :] = ext[TM:TM + POOL_PAD, :]


def _pool(xb, wu0, wu1, sp_t, wmix, scale):
    full = lambda shape: pl.BlockSpec(shape, lambda m: (0,) * len(shape))
    return pl.pallas_call(
        _pool_kernel,
        grid=(NT,),
        in_specs=[
            pl.BlockSpec((TM, D_MODEL), lambda m: (m, 0)),
            _resident((D_MODEL, TN)), _resident((D_MODEL, TN)),
            full((POOL_HIST, N_SAMPLE, POOL_WIDTH)),
            full((len(POOL_WINDOWS), POOL_GROUP_DIM, POOL_GROUP_DIM)),
            full((1, POOL_WIDTH)),
        ],
        out_specs=[pl.BlockSpec((TM, POOL_WIDTH), lambda m: (m, 0)),
                   full((POOL_HIST, N_SAMPLE, POOL_WIDTH)),
                   full((POOL_PAD, POOL_WIDTH))],
        out_shape=[jax.ShapeDtypeStruct((ROWS, POOL_WIDTH), BF16),
                   jax.ShapeDtypeStruct((POOL_HIST, N_SAMPLE, POOL_WIDTH), F32),
                   jax.ShapeDtypeStruct((POOL_PAD, POOL_WIDTH), F32)],
        scratch_shapes=[pltpu.VMEM((POOL_PAD + TM, POOL_WIDTH), F32)],
        compiler_params=_params(1),
        name="pool_mix",
    )(xb, wu0, wu1, sp_t, wmix, scale)


_NC = D_MODEL // TN


def _merge_kernel(xb_ref, pool_ref, attn_ref, wp_ref, wa_ref, *refs):
    gate_w, o_ref = refs[:2 * _NC], refs[2 * _NC]
    for c in range(_NC):
        cols = slice(c * TN, (c + 1) * TN)
        gate_pool = jax.nn.sigmoid(_dot(xb_ref[...], gate_w[c][...]))
        gate_attn = jax.nn.sigmoid(_dot(xb_ref[...], gate_w[_NC + c][...]))
        a = _dot(pool_ref[...], wp_ref[:, cols])
        b = _dot(attn_ref[...], wa_ref[:, cols])
        o_ref[:, cols] = (gate_pool * a + gate_attn * b).astype(BF16)


def _merge(xb, pooled, attn, wp, wa, gate_w):
    row = lambda width: pl.BlockSpec((TMH, width), lambda m: (m, 0))
    return pl.pallas_call(
        _merge_kernel,
        grid=(NTH,),
        in_specs=[row(D_MODEL), row(POOL_WIDTH), row(D_MODEL),
                  _resident((POOL_WIDTH, D_MODEL)), _resident((D_MODEL, D_MODEL))]
        + [_resident((D_MODEL, TN))] * (2 * _NC),
        out_specs=row(D_MODEL),
        out_shape=jax.ShapeDtypeStruct((ROWS, D_MODEL), BF16),
        compiler_params=_params(1),
        name="branch_merge",
    )(xb, pooled, attn, wp, wa, *gate_w)


def _out_ln_kernel(mrg_ref, wo_ref, x_ref, xs_ref, lng_ref, lnb_ref, x1_ref, x1b_ref):
    m = pl.program_id(0)

    def body(residual):
        accs = [_dot(mrg_ref[lo:hi, :], wo_ref[...]) for lo, hi in HALVES_H]
        for (lo, hi), acc in zip(HALVES_H, accs):
            x1 = _layer_norm(acc + ALPHA * residual(lo, hi), lng_ref[...], lnb_ref[...])
            x1_ref[lo:hi, :] = ALPHA * x1
            x1b_ref[lo:hi, :] = x1.astype(BF16)

    def last_tile_rows(lo, hi):
        if hi <= SPLIT_H:
            return x_ref[lo:hi, :]
        return jnp.concatenate([x_ref[lo:SPLIT_H, :], xs_ref[...]], axis=0)

    pl.when(m < NTH - 1)(lambda: body(lambda lo, hi: x_ref[lo:hi, :]))
    pl.when(m == NTH - 1)(lambda: body(last_tile_rows))


def _out_ln(merged, wo, x, xs, ln_g, ln_b):
    row = pl.BlockSpec((TMH, D_MODEL), lambda m: (m, 0))
    return pl.pallas_call(
        _out_ln_kernel,
        grid=(NTH,),
        in_specs=[row, _resident((D_MODEL, D_MODEL)), row, _resident((N_SAMPLE, D_MODEL)),
                  _resident((1, D_MODEL)), _resident((1, D_MODEL))],
        out_specs=[row, row],
        out_shape=[jax.ShapeDtypeStruct((ROWS, D_MODEL), F32), jax.ShapeDtypeStruct((ROWS, D_MODEL), BF16)],
        compiler_params=_params(1),
        name="out_proj_ln1",
    )(merged, wo, x, xs, ln_g, ln_b)


TF = 512
NF = D_FF // TF
CARRY = 8


def _ffn_kernel(x1b_ref, x1_hbm, sc_ref, wg_ref, wu_ref, wd_ref, cw_ref, cb_ref, lng_ref, lnb_ref,
                o_ref, os_ref, glast_ref, scn_ref,
                h0, h1, gext, carry, sem):
    m = pl.program_id(0)
    f = pl.program_id(1)
    last_m = m == NT - 1
    h_slots = (h0, h1)
    x1_copy = pltpu.make_async_copy(x1_hbm.at[pl.ds(m * TM, TM), :], o_ref, sem)

    def up_and_gate(slot, with_down):
        gext[0:CARRY, :] = carry[f]
        for lo, hi in HALVES:
            gext[CARRY + lo:CARRY + hi, :] = _dot(x1b_ref[lo:hi, :], wg_ref[...])
        ups = [_dot(x1b_ref[lo:hi, :], wu_ref[...]) for lo, hi in HALVES]
        if with_down:
            o_ref[...] += _dot(h_slots[1 - slot][...], wd_ref[...])
        w0 = cw_ref[0:1, :]
        w1 = cw_ref[1:2, :]
        w2 = cw_ref[2:3, :]
        cb = cb_ref[...]

        def conv(lo, hi, older=None, newer=None):
            older = gext[CARRY - 2 + lo:CARRY - 2 + hi, :] if older is None else older
            newer = gext[CARRY - 1 + lo:CARRY - 1 + hi, :] if newer is None else newer
            return cb + w0 * older + w1 * newer + w2 * gext[CARRY + lo:CARRY + hi, :]

        for (lo, hi), up in zip(HALVES, ups):
            if hi <= SPLIT:
                gc = conv(lo, hi)
            else:
                older_s = jnp.where(last_m, sc_ref[:, 0, :], gext[CARRY - 2 + SPLIT:CARRY - 2 + TM, :])
                newer_s = jnp.where(last_m, sc_ref[:, 1, :], gext[CARRY - 1 + SPLIT:CARRY - 1 + TM, :])
                gc = jnp.concatenate([conv(lo, SPLIT), conv(SPLIT, TM, older_s, newer_s)], axis=0)
            h_slots[slot][lo:hi, :] = (jax.nn.gelu(gc, approximate=True) * up).astype(BF16)
        carry[f] = gext[TM:CARRY + TM, :]

        @pl.when(last_m)
        def _():
            glast_ref[...] = gext[SPLIT:CARRY + SPLIT, :]
            scn_ref[:, 0, :] = sc_ref[:, 1, :]
            scn_ref[:, 1, :] = gext[CARRY + SPLIT:CARRY + TM, :]

    @pl.when(f == 0)
    def _():
        @pl.when(m == 0)
        def _():
            carry[...] = jnp.zeros_like(carry)

        x1_copy.start()
        up_and_gate(0, False)

    @pl.when(f == 1)
    def _():
        x1_copy.wait()
        up_and_gate(1, True)

    for parity in range(2):
        @pl.when((f > 1) & (f < NF) & (f % 2 == parity))
        def _():
            up_and_gate(parity, True)

    @pl.when(f == NF)
    def _():
        h_last = h_slots[(NF - 1) % 2]
        accs = [o_ref[lo:hi, :] + _dot(h_last[lo:hi, :], wd_ref[...]) for lo, hi in THIRDS]
        for (lo, hi), acc in zip(THIRDS, accs):
            o_ref[lo:hi, :] = _layer_norm(acc, lng_ref[...], lnb_ref[...])

        @pl.when(last_m)
        def _():
            os_ref[...] = o_ref[SPLIT:TM, :]


def _ffn(x1b, x1, sc, wg, wu, wd, conv_w, conv_b, ln_g, ln_b):
    full = lambda shape: pl.BlockSpec(shape, lambda m, f: (0, 0))
    g_tile = lambda f: jnp.minimum(f, NF - 1)
    d_tile = lambda f: jnp.maximum(f - 1, 0)
    only_last = lambda m, f: jnp.where(m == NT - 1, g_tile(f), 0)
    row = pl.BlockSpec((TM, D_MODEL), lambda m, f: (m, 0))
    return pl.pallas_call(
        _ffn_kernel,
        grid=(NT, NF + 1),
        in_specs=[
            row,
            pl.BlockSpec(memory_space=pl.ANY),
            pl.BlockSpec((N_SAMPLE, CONV_W - 1, TF), lambda m, f: (0, 0, only_last(m, f))),
            pl.BlockSpec((D_MODEL, TF), lambda m, f: (0, g_tile(f))),
            pl.BlockSpec((D_MODEL, TF), lambda m, f: (0, g_tile(f))),
            pl.BlockSpec((TF, D_MODEL), lambda m, f: (d_tile(f), 0)),
            pl.BlockSpec((CONV_W, TF), lambda m, f: (0, g_tile(f))),
            pl.BlockSpec((1, TF), lambda m, f: (0, g_tile(f))),
            full((1, D_MODEL)), full((1, D_MODEL)),
        ],
        out_specs=[
            row,
            full((N_SAMPLE, D_MODEL)),
            pl.BlockSpec((CARRY, TF), lambda m, f: (0, only_last(m, f))),
            pl.BlockSpec((N_SAMPLE, CONV_W - 1, TF), lambda m, f: (0, 0, only_last(m, f))),
        ],
        out_shape=[
            jax.ShapeDtypeStruct((SEQ, D_MODEL), F32),
            jax.ShapeDtypeStruct((N_SAMPLE, D_MODEL), F32),
            jax.ShapeDtypeStruct((CARRY, D_FF), F32),
            jax.ShapeDtypeStruct((N_SAMPLE, CONV_W - 1, D_FF), F32),
        ],
        scratch_shapes=[
            pltpu.VMEM((TM, TF), BF16), pltpu.VMEM((TM, TF), BF16),
            pltpu.VMEM((CARRY + TM, TF), F32),
            pltpu.VMEM((NF, CARRY, TF), F32),
            pltpu.SemaphoreType.DMA(()),
        ],
        compiler_params=_params(2),
        name="convffn_ln2",
    )(x1b, x1, sc, wg, wu, wd, conv_w, conv_b, ln_g, ln_b)


def _rope_tables(pos):
    half = HEAD_DIM // 2
    inv = ROPE_THETA ** (-jnp.arange(half, dtype=F32) / half)
    lane = jnp.arange(LANES)
    inv_lanes = inv[lane % half]
    sign = jnp.where((lane % HEAD_DIM) < half, -1.0, 1.0).astype(F32)
    ang = pos.astype(F32)[:, None] * inv_lanes[None, :]
    return jnp.cos(ang), jnp.sin(ang) * sign[None, :]


def kernel(x_prompt, x_sample, cache_k, cache_v, state_pool, state_conv, w_in, attn_sinks, w_pool_mix,
           pool_scale, w_attn_branch, w_pool_branch, w_out, ln1_g, ln1_b, w_up, w_gate, conv_w, conv_b,
           w_down, ln2_g, ln2_b):
    x = x_prompt.reshape(SEQ, D_MODEL)
    xs = x_sample.reshape(N_SAMPLE, D_MODEL)
    pos = jnp.concatenate([jnp.arange(SEQ), jnp.full((N_SAMPLE,), PAST_LEN)])
    cos, sin = _rope_tables(pos)

    q, kv, xb = _in_proj(x, xs, w_in[0], cos, sin)

    to_dsw = lambda c: jnp.transpose(c[0], (0, 2, 3, 1)).reshape(N_SAMPLE, KV_DIM, WINDOW)
    steps = N_SAMPLE // SEQS_PER_STEP
    kvn_t = jnp.transpose(kv[SEQ:].reshape(steps, SEQS_PER_STEP, 2 * KV_DIM), (0, 2, 1))
    sinks = attn_sinks[0] * LOG2E
    attn_s, new_k, new_v = _attention_sample(
        sinks.reshape(N_HEADS, 1), q[SEQ:].reshape(N_SAMPLE, N_HEADS, HEAD_DIM), kvn_t,
        to_dsw(cache_k), to_dsw(cache_v))
    whole = lambda w: (w, 0, w.shape[1])
    late_tiles = [(w_in[0], j, TN) for j in range(_N_U0, _N_G0 + 2 * _NC)]
    attn, (wp_b, wa_b, wo_b, wg_b, wu_b, wd_b, wpool0, wpool1, *gate_w) = _attention(
        sinks, q, kv, attn_s.reshape(N_SAMPLE, D_MODEL),
        [whole(w_pool_branch[0]), whole(w_attn_branch[0]), whole(w_out[0]),
         whole(w_gate[0]), whole(w_up[0]), whole(w_down[0])] + late_tiles)

    state_pool_t = jnp.transpose(state_pool[0], (1, 0, 2))
    pooled, new_pool_t, pool_last = _pool(xb, wpool0, wpool1, state_pool_t, w_pool_mix[0], pool_scale)

    merged = _merge(xb, pooled, attn, wp_b, wa_b, gate_w)
    x1, x1b = _out_ln(merged, wo_b, x, xs, ln1_g, ln1_b)

    y, y_s, g_last, new_conv_s = _ffn(x1b, x1, state_conv[0], wg_b, wu_b, wd_b,
                                      conv_w[0], conv_b, ln2_g, ln2_b)

    from_dsw = lambda c: jnp.transpose(
        c.reshape(N_SAMPLE, N_KV_HEADS, HEAD_DIM, WINDOW), (0, 3, 1, 2))[None]
    return (
        y.reshape(1, SEQ, D_MODEL),
        y_s.reshape(N_SAMPLE, 1, D_MODEL),
        kv[SEQ - WINDOW:SEQ, :KV_DIM].reshape(1, 1, WINDOW, N_KV_HEADS, HEAD_DIM),
        kv[SEQ - WINDOW:SEQ, KV_DIM:].reshape(1, 1, WINDOW, N_KV_HEADS, HEAD_DIM),
        pool_last[POOL_PAD - POOL_HIST:].reshape(1, 1, POOL_HIST, POOL_WIDTH),
        g_last[CARRY - (CONV_W - 1):].reshape(1, 1, CONV_W - 1, D_FF),
        from_dsw(new_k),
        from_dsw(new_v),
        jnp.transpose(new_pool_t, (1, 0, 2))[None],
        new_conv_s[None],
    )
```

```python
import jax
import jax.numpy as jnp
from jax import lax
from jax.experimental import pallas as pl
from jax.experimental.pallas import tpu as pltpu

F32 = jnp.float32
BF16 = jnp.bfloat16

D_MODEL = 2048
SEQ = 8192
N_SAMPLE = 128
PAST_LEN = 8192
HEAD_DIM = 64
N_HEADS = 32
N_KV_HEADS = 4
GROUP = N_HEADS // N_KV_HEADS
KV_DIM = N_KV_HEADS * HEAD_DIM
WINDOW = 128
ROPE_THETA = 10000.0
POOL_WINDOWS = (2, 4, 8, 16)
POOL_WIDTH = 1024
POOL_GROUP_DIM = 256
POOL_HIST = 15
D_FF = 5632
CONV_W = 3
LN_EPS = 1e-5
NEG_INF = -1e30
ALPHA = 2.0 ** 0.25
LOG2E = 1.4426950408889634

LANES = 128
VMEM_LIMIT = 60000 * 1024

ROWS = SEQ + N_SAMPLE
TM = 1040
NT = ROWS // TM
SPLIT = SEQ - (NT - 1) * TM
TN = 512
HALF = 528
HALVES = ((0, HALF), (HALF, TM))
THIRDS = ((0, 352), (352, 704), (704, TM))
assert NT * TM == ROWS and SPLIT + N_SAMPLE == TM and SPLIT % 16 == 0
assert all(lo % 16 == 0 for lo, _ in HALVES + THIRDS)

TMH = TM // 2
NTH = ROWS // TMH
SPLIT_H = SEQ - (NTH - 1) * TMH
HALVES_H = ((0, 256), (256, TMH))
assert NTH * TMH == ROWS and SPLIT_H + N_SAMPLE == TMH and SPLIT_H % 8 == 0


def _params(ndim):
    return pltpu.CompilerParams(dimension_semantics=("arbitrary",) * ndim,
                                vmem_limit_bytes=VMEM_LIMIT)


def _resident(shape):
    return pl.BlockSpec(shape, lambda m: (0,) * len(shape), pipeline_mode=pl.Buffered(1))


def _dot(a, b):
    return jnp.dot(a, b, preferred_element_type=F32)


def _layer_norm(z, g, b):
    mu = jnp.mean(z, axis=-1, keepdims=True)
    d = z - mu
    var = jnp.mean(d * d, axis=-1, keepdims=True)
    return d * lax.rsqrt(var + LN_EPS) * g + b


def _rope(x, cos, sin_signed):
    lane = lax.broadcasted_iota(jnp.int32, (1, LANES), 1)
    low_half = (lane % HEAD_DIM) < (HEAD_DIM // 2)
    outs = []
    for c in range(x.shape[1] // LANES):
        xc = x[:, c * LANES:(c + 1) * LANES]
        up = pltpu.roll(xc, LANES - HEAD_DIM // 2, 1)
        down = pltpu.roll(xc, HEAD_DIM // 2, 1)
        outs.append(xc * cos + jnp.where(low_half, up, down) * sin_signed)
    return jnp.concatenate(outs, axis=1)


_NQ = D_MODEL // TN
_N_KV = _NQ
_N_U0 = _N_KV + 1
_N_G0 = _N_U0 + POOL_WIDTH // TN


def _in_proj_kernel(x_ref, xs_ref, w_ref, cos_ref, sin_ref, q_ref, kv_ref, xb, wres):
    m = pl.program_id(0)
    n = pl.program_id(1)

    @pl.when(n == 0)
    def _():
        @pl.when(m < NT - 1)
        def _():
            xb[...] = x_ref[...].astype(BF16)

        @pl.when(m == NT - 1)
        def _():
            xb[0:SPLIT, :] = x_ref[0:SPLIT, :].astype(BF16)
            xb[SPLIT:TM, :] = xs_ref[...].astype(BF16)

    @pl.when(m == 0)
    def _():
        wres[n] = w_ref[...].astype(BF16)

    def project(epilogue, parts):
        accs = [_dot(xb[lo:hi, :], wres[n]) for lo, hi in parts]
        for (lo, hi), acc in zip(parts, accs):
            epilogue(acc, lo, hi)

    def q_epilogue(acc, lo, hi):
        roped = _rope(acc, cos_ref[lo:hi, :], sin_ref[lo:hi, :])
        q_ref[lo:hi, :] = (roped * (HEAD_DIM ** -0.5 * LOG2E)).astype(BF16)

    def kv_epilogue(acc, lo, hi):
        kv_ref[lo:hi, :KV_DIM] = _rope(acc[:, :KV_DIM], cos_ref[lo:hi, :], sin_ref[lo:hi, :])
        kv_ref[lo:hi, KV_DIM:] = acc[:, KV_DIM:]

    pl.when(n < _NQ)(lambda: project(q_epilogue, THIRDS))
    pl.when(n == _N_KV)(lambda: project(kv_epilogue, HALVES))


def _in_proj(x, xs, w_in, cos, sin):
    nn = _N_U0

    return pl.pallas_call(
        _in_proj_kernel,
        grid=(NT, nn),
        in_specs=[
            pl.BlockSpec((TM, D_MODEL), lambda m, n: (m, 0)),
            pl.BlockSpec((N_SAMPLE, D_MODEL), lambda m, n: (0, 0)),
            pl.BlockSpec((D_MODEL, TN), lambda m, n: (0, jnp.where(m == 0, n, nn - 1))),
            pl.BlockSpec((TM, LANES), lambda m, n: (m, 0)),
            pl.BlockSpec((TM, LANES), lambda m, n: (m, 0)),
        ],
        out_specs=[
            pl.BlockSpec((TM, TN), lambda m, n: (m, jnp.minimum(n, _NQ - 1))),
            pl.BlockSpec((TM, TN), lambda m, n: (m, 0)),
            pl.BlockSpec((TM, D_MODEL), lambda m, n: (m, 0)),
        ],
        out_shape=[
            jax.ShapeDtypeStruct((ROWS, D_MODEL), BF16),
            jax.ShapeDtypeStruct((ROWS, 2 * KV_DIM), F32),
            jax.ShapeDtypeStruct((ROWS, D_MODEL), BF16),
        ],
        scratch_shapes=[pltpu.VMEM((nn, D_MODEL, TN), BF16)],
        compiler_params=_params(2),
        name="in_proj",
    )(x, xs, w_in, cos, sin)


PAIRS = GROUP // 2
Q_BLOCKS = 2


def _attn_kernel(fill_ref, q_ref, kp_ref, kc_ref, vp_ref, vc_ref, os_ref, *refs):
    n_w = (len(refs) - 1) // 2
    w_refs, o_ref, wb_refs = refs[:n_w], refs[n_w], refs[n_w + 1:]
    i = pl.program_id(0)
    last = pl.num_programs(0) - 1

    @pl.when(i < last)
    def _():
        for w_ref, wb_ref in zip(w_refs, wb_refs):
            wb_ref[...] = w_ref[...].astype(BF16)
        keys = (kp_ref[...],) + tuple(kc_ref[b * WINDOW:(b + 1) * WINDOW, :] for b in range(Q_BLOCKS))
        vals = (vp_ref[...],) + tuple(vc_ref[b * WINDOW:(b + 1) * WINDOW, :] for b in range(Q_BLOCKS))
        for b in range(Q_BLOCKS):
            _attn_block(Q_BLOCKS * i + b, fill_ref, q_ref, o_ref, b * WINDOW,
                        keys[b], keys[b + 1], vals[b], vals[b + 1])

    @pl.when(i == last)
    def _():
        o_ref[0:N_SAMPLE, :] = os_ref[...]


def _attn_block(i, fill_ref, q_ref, o_ref, row0, k_prev, k_cur, v_prev, v_cur):
    q_rows = slice(row0, row0 + WINDOW)
    kk = jnp.concatenate([k_prev, k_cur], axis=0)
    vv = jnp.concatenate([v_prev, v_cur], axis=0)
    key = lax.broadcasted_iota(jnp.int32, (2 * WINDOW, 1), 0)
    vv = jnp.where(key == 0, 0.0, vv)
    low = lax.broadcasted_iota(jnp.int32, (1, LANES), 1) < HEAD_DIM
    rows = PAIRS * WINDOW
    r = lax.broadcasted_iota(jnp.int32, (rows, 2 * WINDOW), 0) % WINDOW
    c = lax.broadcasted_iota(jnp.int32, (rows, 2 * WINDOW), 1)
    visible = (c > r) & (c <= r + WINDOW) & ((i > 0) | (c >= WINDOW))

    for pair in range(N_KV_HEADS // 2):
        k_pair = kk[:, pair * LANES:(pair + 1) * LANES]
        v_pair = vv[:, pair * LANES:(pair + 1) * LANES]
        k_swap = pltpu.roll(k_pair, HEAD_DIM, 1)
        v_swap = pltpu.roll(v_pair, HEAD_DIM, 1)
        for second in range(2):
            g = 2 * pair + second
            k_in_low, k_in_high = (k_swap, k_pair) if second else (k_pair, k_swap)
            v_in_low, v_in_high = (v_swap, v_pair) if second else (v_pair, v_swap)
            x = jnp.concatenate(
                [q_ref[q_rows, (g * PAIRS + t) * LANES:(g * PAIRS + t + 1) * LANES] for t in range(PAIRS)],
                axis=0)

            def half(k_pad, v_pad, fill):
                s = lax.dot_general(x, k_pad.astype(BF16), (((1,), (1,)), ((), ())),
                                    preferred_element_type=F32)
                s = jnp.where(visible, s, fill)
                e = jnp.exp2(s - jnp.max(s, axis=-1, keepdims=True)).astype(BF16)
                return _dot(e, v_pad.astype(BF16))

            a = half(jnp.where(low, k_in_low, 0.0), jnp.where(low, v_in_low, 1.0), fill_ref[2 * g])
            b = half(jnp.where(low, 0.0, k_in_high), jnp.where(low, 1.0, v_in_high), fill_ref[2 * g + 1])
            den = pltpu.roll(jnp.where(low, b, a), HEAD_DIM, 1)
            out = (jnp.where(low, a, b) / den).astype(BF16)
            for t in range(PAIRS):
                o_ref[q_rows, (g * PAIRS + t) * LANES:(g * PAIRS + t + 1) * LANES] = \
                    out[t * WINDOW:(t + 1) * WINDOW]


def _cast_chunk_rows(rows, steps):
    assert rows % 16 == 0
    units = rows // 16
    n = max(d for d in range(1, steps + 1) if units % d == 0)
    return rows // n


def _attention(sinks, q, kv, attn_s, weights):
    nb = SEQ // (Q_BLOCKS * WINDOW)
    cur = lambda i: jnp.minimum(i, nb - 1)
    prev = lambda i: jnp.clip(Q_BLOCKS * i - 1, 0, Q_BLOCKS * nb - 1)
    per_chunk = jnp.transpose(sinks.reshape(N_KV_HEADS, PAIRS, 2), (0, 2, 1))
    per_row = jnp.broadcast_to(per_chunk.reshape(2 * N_KV_HEADS, PAIRS, 1),
                               (2 * N_KV_HEADS, PAIRS, WINDOW)).reshape(2 * N_KV_HEADS, PAIRS * WINDOW)
    first_col = jnp.arange(2 * WINDOW)[None, None, :] == 0
    fill = jnp.where(first_col, per_row[:, :, None], NEG_INF).astype(F32)
    w_in_specs, w_out_specs, w_out_shapes = [], [], []
    for w, col, width in weights:
        chunk = _cast_chunk_rows(w.shape[0], nb)
        n = w.shape[0] // chunk
        w_in_specs.append(pl.BlockSpec((chunk, width), lambda i, n=n, col=col: (jnp.minimum(i, n - 1), col)))
        w_out_specs.append(pl.BlockSpec((chunk, width), lambda i, n=n: (jnp.minimum(i, n - 1), 0)))
        w_out_shapes.append(jax.ShapeDtypeStruct((w.shape[0], width), BF16))
    outs = pl.pallas_call(
        _attn_kernel,
        grid=(nb + 1,),
        in_specs=[
            _resident(fill.shape),
            pl.BlockSpec((Q_BLOCKS * WINDOW, D_MODEL), lambda i: (cur(i), 0)),
            pl.BlockSpec((WINDOW, KV_DIM), lambda i: (prev(i), 0)),
            pl.BlockSpec((Q_BLOCKS * WINDOW, KV_DIM), lambda i: (cur(i), 0)),
            pl.BlockSpec((WINDOW, KV_DIM), lambda i: (prev(i), 1)),
            pl.BlockSpec((Q_BLOCKS * WINDOW, KV_DIM), lambda i: (cur(i), 1)),
            pl.BlockSpec((N_SAMPLE, D_MODEL), lambda i: (0, 0)),
        ] + w_in_specs,
        out_specs=[pl.BlockSpec((Q_BLOCKS * WINDOW, D_MODEL), lambda i: (i, 0))] + w_out_specs,
        out_shape=[jax.ShapeDtypeStruct((ROWS, D_MODEL), BF16)] + w_out_shapes,
        compiler_params=_params(1),
        name="attn_prompt",
    )(fill, q, kv, kv, kv, kv, attn_s, *[w for w, _, _ in weights])
    return outs[0], outs[1:]


SEQS_PER_STEP = 8


def _attn_sample_kernel(sink_ref, q_ref, kvn_ref, ck_ref, cv_ref, o_ref, nk_ref, nv_ref):
    head = lax.broadcasted_iota(jnp.int32, (N_HEADS, KV_DIM), 0) // GROUP
    col_group = lax.broadcasted_iota(jnp.int32, (N_HEADS, KV_DIM), 1) // HEAD_DIM
    own = head == col_group
    newest = lax.broadcasted_iota(jnp.int32, (1, WINDOW), 1) == WINDOW - 1
    scores = []
    for b in range(SEQS_PER_STEP):
        nk_ref[b] = jnp.where(newest, kvn_ref[0, 0:KV_DIM, b:b + 1], pltpu.roll(ck_ref[b], WINDOW - 1, 1))
        nv_ref[b] = jnp.where(newest, kvn_ref[0, KV_DIM:2 * KV_DIM, b:b + 1],
                              pltpu.roll(cv_ref[b], WINDOW - 1, 1))
        qb = q_ref[b]
        qe = jnp.where(own, jnp.concatenate([qb] * N_KV_HEADS, axis=1), jnp.zeros((), BF16))
        scores.append(_dot(qe, nk_ref[b].astype(BF16)))
    s = jnp.concatenate(scores, axis=0)
    sink = jnp.concatenate([sink_ref[...]] * SEQS_PER_STEP, axis=0)
    mx = jnp.maximum(jnp.max(s, axis=-1, keepdims=True), sink)
    e = jnp.exp2(s - mx)
    inv = 1.0 / (jnp.sum(e, axis=-1, keepdims=True) + jnp.exp2(sink - mx))
    e = e.astype(BF16)
    for b in range(SEQS_PER_STEP):
        rows = slice(b * N_HEADS, (b + 1) * N_HEADS)
        of = lax.dot_general(e[rows], nv_ref[b].astype(BF16), (((1,), (1,)), ((), ())),
                             preferred_element_type=F32)
        of = jnp.where(own, of, 0.0)
        o = of[:, 0:HEAD_DIM]
        for g in range(1, N_KV_HEADS):
            o = o + of[:, g * HEAD_DIM:(g + 1) * HEAD_DIM]
        o_ref[b] = (o * inv[rows]).astype(BF16)


def _attention_sample(sink_col, q3, kvn_t, ck, cv):
    nb = N_SAMPLE // SEQS_PER_STEP
    cache_spec = pl.BlockSpec((SEQS_PER_STEP, KV_DIM, WINDOW), lambda i: (i, 0, 0))
    return pl.pallas_call(
        _attn_sample_kernel,
        grid=(nb,),
        in_specs=[
            pl.BlockSpec((N_HEADS, 1), lambda i: (0, 0)),
            pl.BlockSpec((SEQS_PER_STEP, N_HEADS, HEAD_DIM), lambda i: (i, 0, 0)),
            pl.BlockSpec((1, 2 * KV_DIM, SEQS_PER_STEP), lambda i: (i, 0, 0)),
            cache_spec, cache_spec,
        ],
        out_specs=[
            pl.BlockSpec((SEQS_PER_STEP, N_HEADS, HEAD_DIM), lambda i: (i, 0, 0)),
            cache_spec, cache_spec,
        ],
        out_shape=[
            jax.ShapeDtypeStruct((N_SAMPLE, N_HEADS, HEAD_DIM), BF16),
            jax.ShapeDtypeStruct((N_SAMPLE, KV_DIM, WINDOW), F32),
            jax.ShapeDtypeStruct((N_SAMPLE, KV_DIM, WINDOW), F32),
        ],
        compiler_params=_params(1),
        name="attn_sample",
    )(sink_col, q3, kvn_t, ck, cv)


POOL_PAD = 16


def _pool_kernel(xb_ref, wu0_ref, wu1_ref, sp_ref, wmix_ref, scale_ref, p_ref, spn_ref, plast_ref, ext):
    m = pl.program_id(0)

    @pl.when(m == 0)
    def _():
        ext[0:POOL_PAD, :] = jnp.zeros((POOL_PAD, POOL_WIDTH), F32)

    pos = m * TM + lax.broadcasted_iota(jnp.int32, (TM, 1), 0)
    for t, wu_ref in ((1, wu1_ref), (0, wu0_ref)):
        ext[POOL_PAD:POOL_PAD + TM, t * TN:(t + 1) * TN] = _dot(xb_ref[...], wu_ref[...])
    for g, w in reversed(list(enumerate(POOL_WINDOWS))):
        cols = slice(g * POOL_GROUP_DIM, (g + 1) * POOL_GROUP_DIM)
        u_g = ext[POOL_PAD:POOL_PAD + TM, cols]
        run = ext[:, cols]
        shift = 1
        while shift < w:
            run = run + pltpu.roll(run, shift, 0)
            shift *= 2
        tot = run[POOL_PAD:POOL_PAD + TM, :]
        inv_cnt = 1.0 / jnp.minimum(w, pos + 1).astype(F32)
        d = tot * inv_cnt - u_g
        y = _dot(d.astype(BF16), wmix_ref[g].astype(BF16))
        p_ref[:, cols] = (y * scale_ref[:, cols]).astype(BF16)

    @pl.when(m == NT - 1)
    def _():
        for g, w in enumerate(POOL_WINDOWS):
            cols = slice(g * POOL_GROUP_DIM, (g + 1) * POOL_GROUP_DIM)
            us = ext[POOL_PAD + SPLIT:POOL_PAD + TM, cols]
            tot = us
            for j in range(1, w):
                tot = tot + sp_ref[POOL_HIST - j, :, cols]
            d = tot * (1.0 / w) - us
            y = _dot(d.astype(BF16), wmix_ref[g].astype(BF16))
            p_ref[SPLIT:TM, cols] = (y * scale_ref[:, cols]).astype(BF16)
        spn_ref[0:POOL_HIST - 1] = sp_ref[1:POOL_HIST]
        spn_ref[POOL_HIST - 1] = ext[POOL_PAD + SPLIT:POOL_PAD + TM, :]
        plast_ref[...] = ext[SPLIT:POOL_PAD + SPLIT, :]

    ext[0:POOL_PAD, :] = ext[TM:TM + POOL_PAD, :]


def _pool(xb, wu0, wu1, sp_t, wmix, scale):
    full = lambda shape: pl.BlockSpec(shape, lambda m: (0,) * len(shape))
    return pl.pallas_call(
        _pool_kernel,
        grid=(NT,),
        in_specs=[
            pl.BlockSpec((TM, D_MODEL), lambda m: (m, 0)),
            _resident((D_MODEL, TN)), _resident((D_MODEL, TN)),
            full((POOL_HIST, N_SAMPLE, POOL_WIDTH)),
            full((len(POOL_WINDOWS), POOL_GROUP_DIM, POOL_GROUP_DIM)),
            full((1, POOL_WIDTH)),
        ],
        out_specs=[pl.BlockSpec((TM, POOL_WIDTH), lambda m: (m, 0)),
                   full((POOL_HIST, N_SAMPLE, POOL_WIDTH)),
                   full((POOL_PAD, POOL_WIDTH))],
        out_shape=[jax.ShapeDtypeStruct((ROWS, POOL_WIDTH), BF16),
                   jax.ShapeDtypeStruct((POOL_HIST, N_SAMPLE, POOL_WIDTH), F32),
                   jax.ShapeDtypeStruct((POOL_PAD, POOL_WIDTH), F32)],
        scratch_shapes=[pltpu.VMEM((POOL_PAD + TM, POOL_WIDTH), F32)],
        compiler_params=_params(1),
        name="pool_mix",
    )(xb, wu0, wu1, sp_t, wmix, scale)


_NC = D_MODEL // TN


def _merge_kernel(xb_ref, pool_ref, attn_ref, wp_ref, wa_ref, *refs):
    gate_w, o_ref = refs[:2 * _NC], refs[2 * _NC]
    for c in range(_NC):
        cols = slice(c * TN, (c + 1) * TN)
        gate_pool = jax.nn.sigmoid(_dot(xb_ref[...], gate_w[c][...]))
        gate_attn = jax.nn.sigmoid(_dot(xb_ref[...], gate_w[_NC + c][...]))
        a = _dot(pool_ref[...], wp_ref[:, cols])
        b = _dot(attn_ref[...], wa_ref[:, cols])
        o_ref[:, cols] = (gate_pool * a + gate_attn * b).astype(BF16)


def _merge(xb, pooled, attn, wp, wa, gate_w):
    row = lambda width: pl.BlockSpec((TMH, width), lambda m: (m, 0))
    return pl.pallas_call(
        _merge_kernel,
        grid=(NTH,),
        in_specs=[row(D_MODEL), row(POOL_WIDTH), row(D_MODEL),
                  _resident((POOL_WIDTH, D_MODEL)), _resident((D_MODEL, D_MODEL))]
        + [_resident((D_MODEL, TN))] * (2 * _NC),
        out_specs=row(D_MODEL),
        out_shape=jax.ShapeDtypeStruct((ROWS, D_MODEL), BF16),
        compiler_params=_params(1),
        name="branch_merge",
    )(xb, pooled, attn, wp, wa, *gate_w)


def _out_ln_kernel(mrg_ref, wo_ref, x_ref, xs_ref, lng_ref, lnb_ref, x1_ref, x1b_ref):
    m = pl.program_id(0)

    def body(residual):
        accs = [_dot(mrg_ref[lo:hi, :], wo_ref[...]) for lo, hi in HALVES_H]
        for (lo, hi), acc in zip(HALVES_H, accs):
            x1 = _layer_norm(acc + ALPHA * residual(lo, hi), lng_ref[...], lnb_ref[...])
            x1_ref[lo:hi, :] = ALPHA * x1
            x1b_ref[lo:hi, :] = x1.astype(BF16)

    def last_tile_rows(lo, hi):
        if hi <= SPLIT_H:
            return x_ref[lo:hi, :]
        return jnp.concatenate([x_ref[lo:SPLIT_H, :], xs_ref[...]], axis=0)

    pl.when(m < NTH - 1)(lambda: body(lambda lo, hi: x_ref[lo:hi, :]))
    pl.when(m == NTH - 1)(lambda: body(last_tile_rows))


def _out_ln(merged, wo, x, xs, ln_g, ln_b):
    row = pl.BlockSpec((TMH, D_MODEL), lambda m: (m, 0))
    return pl.pallas_call(
        _out_ln_kernel,
        grid=(NTH,),
        in_specs=[row, _resident((D_MODEL, D_MODEL)), row, _resident((N_SAMPLE, D_MODEL)),
                  _resident((1, D_MODEL)), _resident((1, D_MODEL))],
        out_specs=[row, row],
        out_shape=[jax.ShapeDtypeStruct((ROWS, D_MODEL), F32), jax.ShapeDtypeStruct((ROWS, D_MODEL), BF16)],
        compiler_params=_params(1),
        name="out_proj_ln1",
    )(merged, wo, x, xs, ln_g, ln_b)


TF = 512
NF = D_FF // TF
W_BUFS = 3
CARRY = 8


def _ffn_kernel(x1b_ref, x1_hbm, sc_ref, wg_hbm, wu_hbm, wd_hbm, cw_ref, cb_ref, lng_ref, lnb_ref,
                o_ref, os_ref, glast_ref, scn_ref,
                h0, h1, gext, carry, sem, wgb, wub, wdb, wsem):
    m = pl.program_id(0)
    f = pl.program_id(1)
    last_m = m == NT - 1
    h_slots = (h0, h1)
    x1_copy = pltpu.make_async_copy(x1_hbm.at[pl.ds(m * TM, TM), :], o_ref, sem)

    n_uses = NT * NF
    use_g = m * NF + f
    use_d = m * NF + f - 1

    def gate_up_copies(use):
        cols = pl.ds(pl.multiple_of((use % NF) * TF, TF), TF)
        slot = use % W_BUFS
        return (pltpu.make_async_copy(wg_hbm.at[:, cols], wgb.at[slot], wsem.at[0, slot]),
                pltpu.make_async_copy(wu_hbm.at[:, cols], wub.at[slot], wsem.at[1, slot]))

    def down_copy(use):
        rows = pl.ds(pl.multiple_of((use % NF) * TF, TF), TF)
        slot = use % W_BUFS
        return pltpu.make_async_copy(wd_hbm.at[rows, :], wdb.at[slot], wsem.at[2, slot])

    @pl.when((m == 0) & (f == 0))
    def _():
        for use in range(W_BUFS - 1):
            for c in gate_up_copies(use):
                c.start()
            down_copy(use).start()

    def acquire_gate_up():
        for c in gate_up_copies(use_g):
            c.wait()

        @pl.when(use_g + W_BUFS - 1 < n_uses)
        def _():
            for c in gate_up_copies(use_g + W_BUFS - 1):
                c.start()
        return wgb[use_g % W_BUFS], wub[use_g % W_BUFS]

    def acquire_down():
        down_copy(use_d).wait()

        @pl.when(use_d + W_BUFS - 1 < n_uses)
        def _():
            down_copy(use_d + W_BUFS - 1).start()
        return wdb[use_d % W_BUFS]

    def up_and_gate(slot, with_down):
        wg, wu = acquire_gate_up()
        wd = acquire_down() if with_down else None
        gext[0:CARRY, :] = carry[f]
        for lo, hi in HALVES:
            gext[CARRY + lo:CARRY + hi, :] = _dot(x1b_ref[lo:hi, :], wg)
        ups = [_dot(x1b_ref[lo:hi, :], wu) for lo, hi in HALVES]
        if with_down:
            o_ref[...] += _dot(h_slots[1 - slot][...], wd)
        w0 = cw_ref[0:1, :]
        w1 = cw_ref[1:2, :]
        w2 = cw_ref[2:3, :]
        cb = cb_ref[...]

        def conv(lo, hi, older=None, newer=None):
            older = gext[CARRY - 2 + lo:CARRY - 2 + hi, :] if older is None else older
            newer = gext[CARRY - 1 + lo:CARRY - 1 + hi, :] if newer is None else newer
            return cb + w0 * older + w1 * newer + w2 * gext[CARRY + lo:CARRY + hi, :]

        for (lo, hi), up in zip(HALVES, ups):
            if hi <= SPLIT:
                gc = conv(lo, hi)
            else:
                older_s = jnp.where(last_m, sc_ref[:, 0, :], gext[CARRY - 2 + SPLIT:CARRY - 2 + TM, :])
                newer_s = jnp.where(last_m, sc_ref[:, 1, :], gext[CARRY - 1 + SPLIT:CARRY - 1 + TM, :])
                gc = jnp.concatenate([conv(lo, SPLIT), conv(SPLIT, TM, older_s, newer_s)], axis=0)
            h_slots[slot][lo:hi, :] = (jax.nn.gelu(gc, approximate=True) * up).astype(BF16)
        carry[f] = gext[TM:CARRY + TM, :]

        @pl.when(last_m)
        def _():
            glast_ref[...] = gext[SPLIT:CARRY + SPLIT, :]
            scn_ref[:, 0, :] = sc_ref[:, 1, :]
            scn_ref[:, 1, :] = gext[CARRY + SPLIT:CARRY + TM, :]

    @pl.when(f == 0)
    def _():
        @pl.when(m == 0)
        def _():
            carry[...] = jnp.zeros_like(carry)

        x1_copy.start()
        up_and_gate(0, False)
        x1_copy.wait()

    for parity in range(2):
        @pl.when((f > 0) & (f < NF) & (f % 2 == parity))
        def _():
            up_and_gate(parity, True)

    @pl.when(f == NF)
    def _():
        h_last = h_slots[(NF - 1) % 2]
        wd = acquire_down()
        accs = [o_ref[lo:hi, :] + _dot(h_last[lo:hi, :], wd) for lo, hi in THIRDS]
        for (lo, hi), acc in zip(THIRDS, accs):
            o_ref[lo:hi, :] = _layer_norm(acc, lng_ref[...], lnb_ref[...])

        @pl.when(last_m)
        def _():
            os_ref[...] = o_ref[SPLIT:TM, :]


def _ffn(x1b, x1, sc, wg, wu, wd, conv_w, conv_b, ln_g, ln_b):
    full = lambda shape: pl.BlockSpec(shape, lambda m, f: (0, 0))
    g_tile = lambda f: jnp.minimum(f, NF - 1)
    d_tile = lambda f: jnp.maximum(f - 1, 0)
    only_last = lambda m, f: jnp.where(m == NT - 1, g_tile(f), 0)
    row = pl.BlockSpec((TM, D_MODEL), lambda m, f: (m, 0))
    return pl.pallas_call(
        _ffn_kernel,
        grid=(NT, NF + 1),
        in_specs=[
            row,
            pl.BlockSpec(memory_space=pl.ANY),
            pl.BlockSpec((N_SAMPLE, CONV_W - 1, TF), lambda m, f: (0, 0, only_last(m, f))),
            pl.BlockSpec(memory_space=pl.ANY),
            pl.BlockSpec(memory_space=pl.ANY),
            pl.BlockSpec(memory_space=pl.ANY),
            pl.BlockSpec((CONV_W, TF), lambda m, f: (0, g_tile(f))),
            pl.BlockSpec((1, TF), lambda m, f: (0, g_tile(f))),
            full((1, D_MODEL)), full((1, D_MODEL)),
        ],
        out_specs=[
            row,
            full((N_SAMPLE, D_MODEL)),
            pl.BlockSpec((CARRY, TF), lambda m, f: (0, only_last(m, f))),
            pl.BlockSpec((N_SAMPLE, CONV_W - 1, TF), lambda m, f: (0, 0, only_last(m, f))),
        ],
        out_shape=[
            jax.ShapeDtypeStruct((SEQ, D_MODEL), F32),
            jax.ShapeDtypeStruct((N_SAMPLE, D_MODEL), F32),
            jax.ShapeDtypeStruct((CARRY, D_FF), F32),
            jax.ShapeDtypeStruct((N_SAMPLE, CONV_W - 1, D_FF), F32),
        ],
        scratch_shapes=[
            pltpu.VMEM((TM, TF), BF16), pltpu.VMEM((TM, TF), BF16),
            pltpu.VMEM((CARRY + TM, TF), F32),
            pltpu.VMEM((NF, CARRY, TF), F32),
            pltpu.SemaphoreType.DMA(()),
            pltpu.VMEM((W_BUFS, D_MODEL, TF), BF16), pltpu.VMEM((W_BUFS, D_MODEL, TF), BF16),
            pltpu.VMEM((W_BUFS, TF, D_MODEL), BF16),
            pltpu.SemaphoreType.DMA((3, W_BUFS)),
        ],
        compiler_params=_params(2),
        name="convffn_ln2",
    )(x1b, x1, sc, wg, wu, wd, conv_w, conv_b, ln_g, ln_b)


def _rope_tables(pos):
    half = HEAD_DIM // 2
    inv = ROPE_THETA ** (-jnp.arange(half, dtype=F32) / half)
    lane = jnp.arange(LANES)
    inv_lanes = inv[lane % half]
    sign = jnp.where((lane % HEAD_DIM) < half, -1.0, 1.0).astype(F32)
    ang = pos.astype(F32)[:, None] * inv_lanes[None, :]
    return jnp.cos(ang), jnp.sin(ang) * sign[None, :]


def kernel(x_prompt, x_sample, cache_k, cache_v, state_pool, state_conv, w_in, attn_sinks, w_pool_mix,
           pool_scale, w_attn_branch, w_pool_branch, w_out, ln1_g, ln1_b, w_up, w_gate, conv_w, conv_b,
           w_down, ln2_g, ln2_b):
    x = x_prompt.reshape(SEQ, D_MODEL)
    xs = x_sample.reshape(N_SAMPLE, D_MODEL)
    pos = jnp.concatenate([jnp.arange(SEQ), jnp.full((N_SAMPLE,), PAST_LEN)])
    cos, sin = _rope_tables(pos)

    q, kv, xb = _in_proj(x, xs, w_in[0], cos, sin)

    to_dsw = lambda c: jnp.transpose(c[0], (0, 2, 3, 1)).reshape(N_SAMPLE, KV_DIM, WINDOW)
    steps = N_SAMPLE // SEQS_PER_STEP
    kvn_t = jnp.transpose(kv[SEQ:].reshape(steps, SEQS_PER_STEP, 2 * KV_DIM), (0, 2, 1))
    sinks = attn_sinks[0] * LOG2E
    attn_s, new_k, new_v = _attention_sample(
        sinks.reshape(N_HEADS, 1), q[SEQ:].reshape(N_SAMPLE, N_HEADS, HEAD_DIM), kvn_t,
        to_dsw(cache_k), to_dsw(cache_v))
    whole = lambda w: (w, 0, w.shape[1])
    late_tiles = [(w_in[0], j, TN) for j in range(_N_U0, _N_G0 + 2 * _NC)]
    attn, (wp_b, wa_b, wo_b, wg_b, wu_b, wd_b, wpool0, wpool1, *gate_w) = _attention(
        sinks, q, kv, attn_s.reshape(N_SAMPLE, D_MODEL),
        [whole(w_pool_branch[0]), whole(w_attn_branch[0]), whole(w_out[0]),
         whole(w_gate[0]), whole(w_up[0]), whole(w_down[0])] + late_tiles)

    state_pool_t = jnp.transpose(state_pool[0], (1, 0, 2))
    pooled, new_pool_t, pool_last = _pool(xb, wpool0, wpool1, state_pool_t, w_pool_mix[0], pool_scale)

    merged = _merge(xb, pooled, attn, wp_b, wa_b, gate_w)
    x1, x1b = _out_ln(merged, wo_b, x, xs, ln1_g, ln1_b)

    y, y_s, g_last, new_conv_s = _ffn(x1b, x1, state_conv[0], wg_b, wu_b, wd_b,
                                      conv_w[0], conv_b, ln2_g, ln2_b)

    from_dsw = lambda c: jnp.transpose(
        c.reshape(N_SAMPLE, N_KV_HEADS, HEAD_DIM, WINDOW), (0, 3, 1, 2))[None]
    return (
        y.reshape(1, SEQ, D_MODEL),
        y_s.reshape(N_SAMPLE, 1, D_MODEL),
        kv[SEQ - WINDOW:SEQ, :KV_DIM].reshape(1, 1, WINDOW, N_KV_HEADS, HEAD_DIM),
        kv[SEQ - WINDOW:SEQ, KV_DIM:].reshape(1, 1, WINDOW, N_KV_HEADS, HEAD_DIM),
        pool_last[POOL_PAD - POOL_HIST:].reshape(1, 1, POOL_HIST, POOL_WIDTH),
        g_last[CARRY - (CONV_W - 1):].reshape(1, 1, CONV_W - 1, D_FF),
        from_dsw(new_k),
        from_dsw(new_v),
        jnp.transpose(new_pool_t, (1, 0, 2))[None],
        new_conv_s[None],
    )
```
